```python
import jax, jax.numpy as jnp
from jax import lax
import numpy as np

D_MODEL = 1024
BATCH = 8
SEQ = 2048
DEPTH = 1
DEC_BATCH = 128
DEC_SEQ = 4
PAST_LEN = 16384
PAGE_SIZE = 128

D_MIX = D_MODEL
W_RWKV = D_MIX // 2
N_R = 64
H_R = W_RWKV // N_R
LORA_W = D_MODEL // 16
LORA_A = D_MODEL // 16
LORA_G = D_MODEL // 8
R_PROJ = 3 * W_RWKV + LORA_W + LORA_A + LORA_G
GN_EPS = 64e-5
W_GDN = D_MIX - W_RWKV
D_G = 128
H_G = W_GDN // D_G
GDN_CONV_DIM = 3 * W_GDN
CONV_W = 4
G_PROJ = GDN_CONV_DIM + W_GDN + 2 * H_G
P_TOT = R_PROJ + G_PROJ
GDN_CHUNK = 64
N_MEM = 256
H_X = 4
D_X = D_MODEL // H_X
N_EXPERTS = 32
TOP_K = 4
D_FF = D_MODEL
SWIGLU_LIMIT = 7.0
SWIGLU_ALPHA = 1.702
MOE_BLOCK = 128
RMS_EPS = 1e-6

kernel_name = 'hymba_rwkv7_gdn_memx_moe_step'

F32 = jnp.float32


def rmsnorm(x, g, eps=RMS_EPS):
    xf = x.astype(F32)
    y = xf * lax.rsqrt(jnp.mean(xf * xf, axis=-1, keepdims=True) + eps)
    return (y * g.astype(F32)).astype(x.dtype)


def l2norm(x, eps=1e-6):
    xf = x.astype(F32)
    return xf * lax.rsqrt(jnp.sum(xf * xf, axis=-1, keepdims=True) + eps)


def rwkv7_recurrence(r, wl, k, v, kk, a, s0):
    def step(s, inp):
        r_t, wl_t, k_t, v_t, kk_t, a_t = inp
        sa = jnp.einsum('bhvk,bhk->bhv', s, -kk_t)
        s = (s * jnp.exp(wl_t)[:, :, None, :]
             + sa[..., None] * (kk_t * a_t)[:, :, None, :]
             + v_t[..., None] * k_t[:, :, None, :])
        return s, jnp.einsum('bhvk,bhk->bhv', s, r_t)
    xs = tuple(jnp.moveaxis(t, 1, 0) for t in (r, wl, k, v, kk, a))
    s, ys = lax.scan(step, s0, xs)
    return jnp.moveaxis(ys, 0, 1), s


def gated_delta_chunked(q, k, v, g, beta, s0, chunk):
    b, t, h, _ = q.shape
    dv = v.shape[-1]
    pad = (-t) % chunk
    n = (t + pad) // chunk

    def blocks(x):
        x = jnp.pad(x, [(0, 0), (0, pad)] + [(0, 0)] * (x.ndim - 2))
        x = x.reshape((b, n, chunk) + x.shape[2:])
        return jnp.moveaxis(x, 3, 1)

    q, k, v, g, beta = (blocks(x) for x in (q, k, v, g, beta))
    gc = jnp.cumsum(g, axis=-1)
    tril = jnp.tril(jnp.ones((chunk, chunk), bool))
    strict = jnp.tril(jnp.ones((chunk, chunk), bool), -1)
    diff = gc[..., :, None] - gc[..., None, :]
    decay = jnp.where(tril, jnp.exp(jnp.where(tril, diff, 0.0)), 0.0)
    k_beta = k * beta[..., None]
    lmat = jnp.where(strict, jnp.einsum('bhnik,bhnjk->bhnij', k_beta, k) * decay, 0.0)
    eye = jnp.eye(chunk, dtype=q.dtype)
    tmat = lax.linalg.triangular_solve(eye + lmat, jnp.broadcast_to(eye, lmat.shape),
                                       left_side=True, lower=True, unit_diagonal=True)
    u = jnp.einsum('bhnij,bhnjv->bhniv', tmat, v * beta[..., None])
    w = jnp.einsum('bhnij,bhnjk->bhnik', tmat, k_beta * jnp.exp(gc)[..., None])
    attn = jnp.where(tril, jnp.einsum('bhnik,bhnjk->bhnij', q, k) * decay, 0.0)
    g_last = gc[..., -1]
    k_tail = k * jnp.exp(g_last[..., None] - gc)[..., None]
    q_dec = q * jnp.exp(gc)[..., None]

    def step(s, inp):
        u_i, w_i, qd_i, a_i, kt_i, gl_i = inp
        v_new = u_i - jnp.einsum('bhck,bhkv->bhcv', w_i, s)
        o = jnp.einsum('bhck,bhkv->bhcv', qd_i, s) + jnp.einsum('bhcd,bhdv->bhcv', a_i, v_new)
        s = s * jnp.exp(gl_i)[..., None, None] + jnp.einsum('bhck,bhcv->bhkv', kt_i, v_new)
        return s, o

    xs = tuple(jnp.moveaxis(x, 2, 0) for x in (u, w, q_dec, attn, k_tail, g_last))
    s, o = lax.scan(step, s0, xs)
    o = jnp.transpose(o, (1, 0, 3, 2, 4)).reshape(b, n * chunk, h, dv)[:, :t]
    return o, s


def rwkv7_group(pr, shift_prev, s0, lp):
    b, t, _ = pr.shape
    prev = jnp.concatenate([shift_prev[:, None, :].astype(pr.dtype), pr[:, :-1]], axis=1)
    xm = (pr + (prev - pr) * lp['mu_shift']).astype(F32)
    r = xm[..., :W_RWKV]
    k = xm[..., W_RWKV:2 * W_RWKV]
    v = xm[..., 2 * W_RWKV:3 * W_RWKV]
    o = 3 * W_RWKV
    w_lo = xm[..., o:o + LORA_W]
    a_lo = xm[..., o + LORA_W:o + LORA_W + LORA_A]
    g_lo = xm[..., o + LORA_W + LORA_A:]
    logw = -jax.nn.softplus(-(lp['w0'] + jnp.tanh(w_lo) @ lp['w2_decay'])) - 0.5
    wl = -jnp.exp(logw)
    a = jax.nn.sigmoid(lp['a0'] + a_lo @ lp['a2_iclr'])
    g = jax.nn.sigmoid(g_lo) @ lp['g2_gate']
    hs = lambda z: z.reshape(b, t, H_R, N_R)
    kk = l2norm(hs(k * lp['k_k']))
    k = k * (1.0 + (a - 1.0) * lp['k_a'])
    r_h, k_h, v_h = hs(r), hs(k), hs(v)
    y, s = rwkv7_recurrence(r_h, hs(wl), k_h, v_h, kk, hs(a), s0.astype(F32))
    mean = jnp.mean(y, axis=-1, keepdims=True)
    var = jnp.mean(jnp.square(y - mean), axis=-1, keepdims=True)
    yn = ((y - mean) * lax.rsqrt(var + GN_EPS)).reshape(b, t, W_RWKV) * lp['lnx_w'] + lp['lnx_b']
    bonus = jnp.sum(r_h * k_h * lp['r_k'], axis=-1, keepdims=True) * v_h
    out = (yn + bonus.reshape(b, t, W_RWKV)) * g
    return out.astype(pr.dtype), pr[:, -1], s


def gdn_group(pg, conv_buf, s0, lp):
    b, t, _ = pg.shape
    raw = pg[..., :GDN_CONV_DIM]
    z = pg[..., GDN_CONV_DIM:GDN_CONV_DIM + W_GDN].astype(F32)
    bb = pg[..., GDN_CONV_DIM + W_GDN:GDN_CONV_DIM + W_GDN + H_G].astype(F32)
    aa = pg[..., GDN_CONV_DIM + W_GDN + H_G:].astype(F32)
    xp = jnp.concatenate([conv_buf.astype(raw.dtype), raw], axis=1)
    cw = lp['conv_w']
    conv = xp[:, 0:t] * cw[0]
    for j in range(1, CONV_W):
        conv = conv + xp[:, j:j + t] * cw[j]
    qkv = jax.nn.silu(conv.astype(F32))
    hs = lambda zz: zz.reshape(b, t, H_G, D_G)
    q = l2norm(hs(qkv[..., :W_GDN])) * (D_G ** -0.5)
    k = l2norm(hs(qkv[..., W_GDN:2 * W_GDN]))
    v = hs(qkv[..., 2 * W_GDN:])
    beta = jax.nn.sigmoid(bb)
    g = -jnp.exp(lp['a_log'].astype(F32)) * jax.nn.softplus(aa + lp['dt_bias'])
    o, s = gated_delta_chunked(q, k, v, g, beta, s0.astype(F32), min(GDN_CHUNK, t))
    o = o * lax.rsqrt(jnp.mean(o * o, axis=-1, keepdims=True) + RMS_EPS) * lp['gdn_norm_w']
    o = o * jax.nn.silu(hs(z))
    return o.reshape(b, t, W_GDN).astype(pg.dtype), xp[:, -(CONV_W - 1):], s


def parallel_mixer(h, shift_prev, s_r, conv_buf, s_g, lp):
    proj = h @ lp['w_in']
    y_r, shift_new, s_r_new = rwkv7_group(proj[..., :R_PROJ], shift_prev, s_r, lp)
    y_g, conv_new, s_g_new = gdn_group(proj[..., R_PROJ:], conv_buf, s_g, lp)
    y = jnp.concatenate([y_r, y_g], axis=-1) @ lp['w_out']
    return y, shift_new, s_r_new, conv_new, s_g_new


def memory_kv(mem, g, wk, wv):
    b, m, _ = mem.shape
    hm = rmsnorm(mem, g)
    return (hm @ wk).reshape(b, m, H_X, D_X), (hm @ wv).reshape(b, m, H_X, D_X)


def memory_cross_attention(h, mk, mv, wq, wo):
    b, t, _ = h.shape
    q = (h @ wq).reshape(b, t, H_X, D_X)
    s = jnp.einsum('bthd,bmhd->bhtm', q, mk.astype(q.dtype)).astype(F32) * (D_X ** -0.5)
    p = jax.nn.softmax(s, axis=-1).astype(h.dtype)
    o = jnp.einsum('bhtm,bmhd->bthd', p, mv.astype(h.dtype)).reshape(b, t, H_X * D_X)
    return o @ wo


def routed_ffn(h, lp):
    n_tok, d = h.shape
    logits = h.astype(F32) @ lp['router_w'].astype(F32) + lp['router_b'].astype(F32)
    top_v, top_i = lax.top_k(logits, TOP_K)
    gates = jax.nn.softmax(top_v, axis=-1)
    n_assign = n_tok * TOP_K
    flat_e = top_i.reshape(n_assign)
    flat_tok = jnp.arange(n_assign, dtype=jnp.int32) // TOP_K
    order = jnp.argsort(flat_e)
    sorted_e = flat_e[order]
    counts = jnp.bincount(flat_e, length=N_EXPERTS)
    padded = (counts + MOE_BLOCK - 1) // MOE_BLOCK * MOE_BLOCK
    starts = jnp.cumsum(counts) - counts
    pends = jnp.cumsum(padded)
    pstarts = pends - padded
    dest = pstarts[sorted_e] + jnp.arange(n_assign, dtype=jnp.int32) - starts[sorted_e]
    n_blocks = -(-(n_assign + N_EXPERTS * (MOE_BLOCK - 1)) // MOE_BLOCK)
    m_rows = n_blocks * MOE_BLOCK
    tok_sorted = flat_tok[order]
    row_tok = jnp.full((m_rows,), n_tok, jnp.int32).at[dest].set(tok_sorted)
    block_e = jnp.clip(jnp.searchsorted(pends, jnp.arange(n_blocks, dtype=jnp.int32) * MOE_BLOCK,
                                        side='right'), 0, N_EXPERTS - 1)
    h_pad = jnp.concatenate([h, jnp.zeros((1, d), h.dtype)], axis=0)
    xb = h_pad[row_tok].reshape(n_blocks, MOE_BLOCK, d)
    w1, b1, w2, b2 = lp['w1_e'], lp['b1_e'], lp['w2_e'], lp['b2_e']

    def expert_block(args):
        xe, e = args
        hc = xe @ w1[e] + b1[e]
        hg = jnp.minimum(hc[:, :D_FF], SWIGLU_LIMIT)
        hl = jnp.clip(hc[:, D_FF:], -SWIGLU_LIMIT, SWIGLU_LIMIT)
        act = hg * jax.nn.sigmoid(SWIGLU_ALPHA * hg) * (hl + 1.0)
        return act @ w2[e] + b2[e]

    yb = lax.map(expert_block, (xb, block_e)).reshape(m_rows, d)
    contrib = yb[dest] * gates.reshape(n_assign)[order][:, None].astype(yb.dtype)
    return jax.ops.segment_sum(contrib, tok_sorted, num_segments=n_tok).astype(h.dtype)


def decoder_layer(x, mk, mv, shift_prev, s_r, conv_buf, s_g, lp):
    y, shift_new, s_r_new, conv_new, s_g_new = parallel_mixer(
        rmsnorm(x, lp['norm_mix']), shift_prev, s_r, conv_buf, s_g, lp)
    x = x + y
    x = x + memory_cross_attention(rmsnorm(x, lp['norm_cross']), mk, mv, lp['wq_x'], lp['wo_x'])
    b, t, d = x.shape
    x = x + routed_ffn(rmsnorm(x, lp['norm_ffn']).reshape(b * t, d), lp).reshape(b, t, d)
    return x, shift_new, s_r_new, conv_new, s_g_new


def setup_inputs(seed: int = 0) -> dict:
    key = jax.random.key(seed)
    ks = iter(jax.random.split(key, 64))
    nrm = lambda shape, scale: scale * jax.random.normal(next(ks), shape, F32)
    uni = lambda shape, lo, hi: jax.random.uniform(next(ks), shape, F32, lo, hi)
    L = DEPTH
    dt = jnp.exp(uni((L, H_G), float(np.log(1e-3)), float(np.log(1e-1))))
    return {
        'x_prompt': nrm((BATCH, SEQ, D_MODEL), 1.0),
        'x_sample': nrm((DEC_BATCH, DEC_SEQ, D_MODEL), 1.0),
        'mem_prompt': nrm((BATCH, N_MEM, D_MODEL), 1.0),
        'state_rwkv': nrm((L, DEC_BATCH, H_R, N_R, N_R), 0.5),
        'state_rwkv_shift': nrm((L, DEC_BATCH, R_PROJ), 1.0),
        'state_gdn': nrm((L, DEC_BATCH, H_G, D_G, D_G), 0.1),
        'state_gdn_conv': nrm((L, DEC_BATCH, CONV_W - 1, GDN_CONV_DIM), 1.0),
        'cache_mem_k': nrm((L, DEC_BATCH, N_MEM, H_X, D_X), 1.0),
        'cache_mem_v': nrm((L, DEC_BATCH, N_MEM, H_X, D_X), 1.0),
        'norm_mix': 1.0 + nrm((L, D_MODEL), 0.02),
        'w_in': nrm((L, D_MODEL, P_TOT), D_MODEL ** -0.5),
        'mu_shift': uni((L, R_PROJ), 0.0, 1.0),
        'w0': uni((L, W_RWKV), -6.0, -0.5),
        'w2_decay': nrm((L, LORA_W, W_RWKV), 0.1),
        'a0': nrm((L, W_RWKV), 0.5),
        'a2_iclr': nrm((L, LORA_A, W_RWKV), 0.1),
        'g2_gate': nrm((L, LORA_G, W_RWKV), LORA_G ** -0.5),
        'k_k': 0.85 + nrm((L, W_RWKV), 0.05),
        'k_a': 1.0 + nrm((L, W_RWKV), 0.05),
        'r_k': nrm((L, H_R, N_R), 0.1),
        'lnx_w': 1.0 + nrm((L, W_RWKV), 0.02),
        'lnx_b': nrm((L, W_RWKV), 0.02),
        'conv_w': nrm((L, CONV_W, GDN_CONV_DIM), CONV_W ** -0.5),
        'a_log': jnp.log(uni((L, H_G), 1.0, 16.0)),
        'dt_bias': dt + jnp.log(-jnp.expm1(-dt)),
        'gdn_norm_w': 1.0 + nrm((L, D_G), 0.02),
        'w_out': nrm((L, D_MIX, D_MODEL), D_MIX ** -0.5),
        'norm_cross': 1.0 + nrm((L, D_MODEL), 0.02),
        'norm_mem': 1.0 + nrm((L, D_MODEL), 0.02),
        'wq_x': nrm((L, D_MODEL, D_MODEL), D_MODEL ** -0.5),
        'wk_x': nrm((L, D_MODEL, D_MODEL), D_MODEL ** -0.5),
        'wv_x': nrm((L, D_MODEL, D_MODEL), D_MODEL ** -0.5),
        'wo_x': nrm((L, D_MODEL, D_MODEL), D_MODEL ** -0.5),
        'norm_ffn': 1.0 + nrm((L, D_MODEL), 0.02),
        'router_w': nrm((L, D_MODEL, N_EXPERTS), D_MODEL ** -0.5),
        'router_b': nrm((L, N_EXPERTS), 0.01),
        'w1_e': nrm((L, N_EXPERTS, D_MODEL, 2 * D_FF), D_MODEL ** -0.5),
        'b1_e': nrm((L, N_EXPERTS, 2 * D_FF), 0.01),
        'w2_e': nrm((L, N_EXPERTS, D_FF, D_MODEL), D_FF ** -0.5),
        'b2_e': nrm((L, N_EXPERTS, D_MODEL), 0.01),
        'final_norm': 1.0 + nrm((D_MODEL,), 0.02),
    }


def reference(x_prompt, x_sample, mem_prompt, state_rwkv, state_rwkv_shift, state_gdn, state_gdn_conv,
              cache_mem_k, cache_mem_v, norm_mix, w_in, mu_shift, w0, w2_decay, a0, a2_iclr, g2_gate,
              k_k, k_a, r_k, lnx_w, lnx_b, conv_w, a_log, dt_bias, gdn_norm_w, w_out,
              norm_cross, norm_mem, wq_x, wk_x, wv_x, wo_x,
              norm_ffn, router_w, router_b, w1_e, b1_e, w2_e, b2_e, final_norm):
    xp, xs = x_prompt, x_sample
    bp = x_prompt.shape[0]
    dt = x_prompt.dtype
    p_sr, p_sh, p_sg, p_cb, p_mk, p_mv = [], [], [], [], [], []
    s_sr, s_sh, s_sg, s_cb = [], [], [], []
    for l in range(DEPTH):
        lp = {
            'norm_mix': norm_mix[l], 'w_in': w_in[l], 'mu_shift': mu_shift[l], 'w0': w0[l],
            'w2_decay': w2_decay[l], 'a0': a0[l], 'a2_iclr': a2_iclr[l], 'g2_gate': g2_gate[l],
            'k_k': k_k[l], 'k_a': k_a[l], 'r_k': r_k[l], 'lnx_w': lnx_w[l], 'lnx_b': lnx_b[l],
            'conv_w': conv_w[l], 'a_log': a_log[l], 'dt_bias': dt_bias[l], 'gdn_norm_w': gdn_norm_w[l],
            'w_out': w_out[l], 'norm_cross': norm_cross[l], 'wq_x': wq_x[l], 'wo_x': wo_x[l],
            'norm_ffn': norm_ffn[l], 'router_w': router_w[l], 'router_b': router_b[l],
            'w1_e': w1_e[l], 'b1_e': b1_e[l], 'w2_e': w2_e[l], 'b2_e': b2_e[l],
        }
        mk, mv = memory_kv(mem_prompt, norm_mem[l], wk_x[l], wv_x[l])
        xp, sh, sr, cb, sg = decoder_layer(
            xp, mk, mv,
            jnp.zeros((bp, R_PROJ), dt), jnp.zeros((bp, H_R, N_R, N_R), F32),
            jnp.zeros((bp, CONV_W - 1, GDN_CONV_DIM), dt), jnp.zeros((bp, H_G, D_G, D_G), F32), lp)
        p_sr.append(sr.astype(dt)); p_sh.append(sh.astype(dt)); p_sg.append(sg.astype(dt))
        p_cb.append(cb.astype(dt)); p_mk.append(mk.astype(dt)); p_mv.append(mv.astype(dt))
        xs, sh, sr, cb, sg = decoder_layer(
            xs, cache_mem_k[l], cache_mem_v[l], state_rwkv_shift[l], state_rwkv[l],
            state_gdn_conv[l], state_gdn[l], lp)
        s_sr.append(sr.astype(dt)); s_sh.append(sh.astype(dt)); s_sg.append(sg.astype(dt))
        s_cb.append(cb.astype(dt))
    y_prompt = rmsnorm(xp, final_norm)
    y_sample = rmsnorm(xs, final_norm)
    return (y_prompt, y_sample,
            jnp.stack(p_sr), jnp.stack(p_sh), jnp.stack(p_sg), jnp.stack(p_cb), jnp.stack(p_mk), jnp.stack(p_mv),
            jnp.stack(s_sr), jnp.stack(s_sh), jnp.stack(s_sg), jnp.stack(s_cb))
```

```python
import functools

import jax
import jax.numpy as jnp
from jax import lax
from jax.experimental import pallas as pl
from jax.experimental.pallas import tpu as pltpu

F32 = jnp.float32
BF16 = jnp.bfloat16
HIGHEST = lax.Precision.HIGHEST
DEFAULT = lax.Precision.DEFAULT

D_MODEL = 1024
W_RWKV = 512
N_R = 64
H_R = W_RWKV // N_R
LORA_W = 64
LORA_A = 64
LORA_G = 128
R_PROJ = 3 * W_RWKV + LORA_W + LORA_A + LORA_G
GN_EPS = 64e-5
W_GDN = 512
D_G = 128
H_G = W_GDN // D_G
GDN_CONV_DIM = 3 * W_GDN
CONV_W = 4
G_PROJ = GDN_CONV_DIM + W_GDN + 2 * H_G
LANES = 128
SUBLANES = 8
G_PROJ_PAD = GDN_CONV_DIM + W_GDN + LANES
MIX_CHUNK = 64
N_MEM = 256
H_X = 4
D_X = D_MODEL // H_X
N_EXPERTS = 32
TOP_K = 4
D_FF = D_MODEL
SWIGLU_LIMIT = 7.0
SWIGLU_ALPHA = 1.702
MOE_ROWS = 256
RMS_EPS = 1e-6
L2_EPS = 1e-6
NEG_BIG = -1e30
VMEM_LIMIT = 56 * 1024 * 1024


def _dot(a, b, precision=DEFAULT):
    return jnp.dot(a, b, preferred_element_type=F32, precision=precision)


def _dot_nt(a, b, precision=DEFAULT):
    return lax.dot_general(a, b, (((1,), (1,)), ((), ())), preferred_element_type=F32, precision=precision)


def _dot_tn(a, b, precision=DEFAULT):
    return lax.dot_general(a, b, (((0,), (0,)), ((), ())), preferred_element_type=F32, precision=precision)


def _sigmoid(x):
    return 1.0 / (1.0 + jnp.exp(-x))


def _softplus(x):
    return jnp.maximum(x, 0.0) + jnp.log(1.0 + jnp.exp(-jnp.abs(x)))


def _rmsnorm(x, g):
    return x * lax.rsqrt(jnp.mean(x * x, axis=-1, keepdims=True) + RMS_EPS) * g


def _tri_masks(c):
    row = lax.broadcasted_iota(jnp.int32, (c, c), 0)
    col = lax.broadcasted_iota(jnp.int32, (c, c), 1)
    return col <= row, col < row, (col == row).astype(F32)


def _unit_lower_inverse(m, eye, c):
    t = eye + m
    p = m
    covered = 2
    while covered < c:
        p = _dot(p, p, HIGHEST)
        t = t + _dot(t, p, HIGHEST)
        covered *= 2
    return t


def _norm_proj_body(x_ref, g_ref, *refs, n_out):
    w_refs, o_refs = refs[:n_out], refs[n_out:]
    hb = _rmsnorm(x_ref[...], g_ref[...]).astype(BF16)
    for w_ref, o_ref in zip(w_refs, o_refs):
        o_ref[...] = _dot(hb, w_ref[...]).astype(o_ref.dtype)


def _norm_proj(x, g, ws, out_dtypes, tm):
    n, d = x.shape
    assert n % tm == 0
    in_specs = [pl.BlockSpec((tm, d), lambda i: (i, 0)), pl.BlockSpec((1, d), lambda i: (0, 0))]
    in_specs += [pl.BlockSpec(w.shape, lambda i: (0, 0)) for w in ws]
    return pl.pallas_call(
        functools.partial(_norm_proj_body, n_out=len(ws)),
        grid=(n // tm,),
        in_specs=in_specs,
        out_specs=[pl.BlockSpec((tm, w.shape[1]), lambda i: (i, 0)) for w in ws],
        out_shape=[jax.ShapeDtypeStruct((n, w.shape[1]), dt) for w, dt in zip(ws, out_dtypes)],
        compiler_params=pltpu.CompilerParams(dimension_semantics=("parallel",), vmem_limit_bytes=VMEM_LIMIT),
        name="norm_proj",
    )(x, g, *ws)


def _rwkv_body(pr_ref, shift_ref, s0_ref, mu_ref, w0_ref, w2_ref, a0_ref, a2_ref, g2_ref, kk_ref, ka_ref,
               rk_ref, lnw_ref, lnb_ref, y_ref, sout_ref, s_scr, prev_scr, *, chunk, t_valid, n_chunks):
    c = pl.program_id(1)

    @pl.when(c == 0)
    def _init():
        s_scr[...] = s0_ref[0]
        prev_scr[...] = shift_ref[0]

    pr = pr_ref[0]
    rows = lax.broadcasted_iota(jnp.int32, (chunk, 1), 0)
    prev = jnp.where(rows == 0, prev_scr[...], pltpu.roll(pr, 1, 0))
    prev_scr[...] = pr[chunk - 1:chunk, :]
    xm = pr + (prev - pr) * mu_ref[...]
    r = xm[:, :W_RWKV]
    k = xm[:, W_RWKV:2 * W_RWKV]
    v = xm[:, 2 * W_RWKV:3 * W_RWKV]
    lo = xm[:, 3 * W_RWKV:3 * W_RWKV + LORA_W + LORA_A]
    g_lo = xm[:, 3 * W_RWKV + LORA_W + LORA_A:]
    logw = -_softplus(-(w0_ref[...] + _dot(jnp.tanh(lo).astype(BF16), w2_ref[...]))) - 0.5
    wl = -jnp.exp(logw)
    a = _sigmoid(a0_ref[...] + _dot(lo.astype(BF16), a2_ref[...]))
    g = _dot(_sigmoid(g_lo).astype(BF16), g2_ref[...])
    kkv = k * kk_ref[...]
    k = k * (1.0 + (a - 1.0) * ka_ref[...])
    if t_valid < chunk:
        valid = rows < t_valid
        wl = jnp.where(valid, wl, 0.0)
        kkv = jnp.where(valid, kkv, 0.0)
        k = jnp.where(valid, k, 0.0)
        v = jnp.where(valid, v, 0.0)

    incl, strict, eye = _tri_masks(chunk)
    cum = _dot(incl.astype(F32), wl, HIGHEST)
    cum_last = cum[chunk - 1:chunk, :]
    w_incl = jnp.exp(cum)
    w_prev = jnp.exp(cum - wl)
    w_inv = jnp.exp(-cum)
    w_tail = jnp.exp(cum_last - cum)
    w_last = jnp.exp(cum_last)

    for h in range(H_R):
        sl = slice(h * N_R, (h + 1) * N_R)
        kk_h = kkv[:, sl]
        kk_h = kk_h * lax.rsqrt(jnp.sum(kk_h * kk_h, axis=-1, keepdims=True) + L2_EPS)
        r_h, k_h, v_h, a_h = r[:, sl], k[:, sl], v[:, sl], a[:, sl]
        a_hat = -(kk_h * w_prev[:, sl])
        kka = kk_h * a_h
        b_hat = kka * w_inv[:, sl]
        k_hat = k_h * w_inv[:, sl]
        r_hat = r_h * w_incl[:, sl]
        m_ab = jnp.where(strict, _dot_nt(a_hat, b_hat, HIGHEST), 0.0)
        m_ak = jnp.where(strict, _dot_nt(a_hat, k_hat, HIGHEST), 0.0)
        a_rb = jnp.where(incl, _dot_nt(r_hat, b_hat, HIGHEST), 0.0)
        a_rk = jnp.where(incl, _dot_nt(r_hat, k_hat, HIGHEST), 0.0)
        t_inv = _unit_lower_inverse(m_ab, eye, chunk)
        w_hat = _dot(t_inv, a_hat, HIGHEST)
        u = _dot(t_inv, _dot(m_ak, v_h, HIGHEST), HIGHEST)
        s = s_scr[h]
        p = _dot_nt(w_hat, s, HIGHEST) + u
        y = _dot_nt(r_hat, s, HIGHEST) + _dot(a_rb, p, HIGHEST) + _dot(a_rk, v_h, HIGHEST)
        s_scr[h] = (s * w_last[:, sl] + _dot_tn(p, kka * w_tail[:, sl], HIGHEST)
                    + _dot_tn(v_h, k_h * w_tail[:, sl], HIGHEST))
        mean = jnp.mean(y, axis=-1, keepdims=True)
        yc = y - mean
        var = jnp.mean(yc * yc, axis=-1, keepdims=True)
        yn = yc * lax.rsqrt(var + GN_EPS) * lnw_ref[:, sl] + lnb_ref[:, sl]
        bonus = jnp.sum(r_h * k_h * rk_ref[:, sl], axis=-1, keepdims=True) * v_h
        y_ref[0, :, sl] = ((yn + bonus) * g[:, sl]).astype(y_ref.dtype)

    @pl.when(c == n_chunks - 1)
    def _fin():
        sout_ref[0] = s_scr[...]


def _rwkv_mix(pr, shift_prev, s0, p, chunk, t_valid):
    b, t, _ = pr.shape
    n_chunks = t // chunk
    const = lambda shape: pl.BlockSpec(shape, lambda i, j: (0,) * len(shape))
    return pl.pallas_call(
        functools.partial(_rwkv_body, chunk=chunk, t_valid=t_valid, n_chunks=n_chunks),
        grid=(b, n_chunks),
        in_specs=[
            pl.BlockSpec((1, chunk, R_PROJ), lambda i, j: (i, j, 0)),
            pl.BlockSpec((1, 1, R_PROJ), lambda i, j: (i, 0, 0)),
            pl.BlockSpec((1, H_R, N_R, N_R), lambda i, j: (i, 0, 0, 0)),
            const((1, R_PROJ)), const((1, W_RWKV)), const((LANES, W_RWKV)), const((1, W_RWKV)),
            const((LANES, W_RWKV)), const((LORA_G, W_RWKV)), const((1, W_RWKV)), const((1, W_RWKV)),
            const((1, W_RWKV)), const((1, W_RWKV)), const((1, W_RWKV)),
        ],
        out_specs=[
            pl.BlockSpec((1, chunk, W_RWKV), lambda i, j: (i, j, 0)),
            pl.BlockSpec((1, H_R, N_R, N_R), lambda i, j: (i, 0, 0, 0)),
        ],
        out_shape=[jax.ShapeDtypeStruct((b, t, W_RWKV), BF16), jax.ShapeDtypeStruct((b, H_R, N_R, N_R), F32)],
        scratch_shapes=[pltpu.VMEM((H_R, N_R, N_R), F32), pltpu.VMEM((1, R_PROJ), F32)],
        compiler_params=pltpu.CompilerParams(dimension_semantics=("arbitrary", "arbitrary")),
        name="rwkv_mix",
    )(pr, shift_prev[:, None, :], s0, p['mu'], p['w0'], p['w2'], p['a0'], p['a2'], p['g2'], p['k_k'], p['k_a'],
      p['r_k'], p['lnx_w'], p['lnx_b'])


def _gdn_body(pg_ref, cbuf_ref, s0_ref, cw_ref, alog_ref, dtb_ref, nw_ref, y_ref, sout_ref, s_scr, xp_scr,
              *, chunk, t_valid, n_chunks):
    c = pl.program_id(1)

    @pl.when(c == 0)
    def _init():
        s_scr[...] = s0_ref[0]
        xp_scr[0:SUBLANES, :] = cbuf_ref[0]

    xp_scr[SUBLANES:SUBLANES + chunk, :] = pg_ref[0, :, :GDN_CONV_DIM]
    base = SUBLANES - (CONV_W - 1)
    conv = xp_scr[base:base + chunk, :] * cw_ref[0:1, :]
    for j in range(1, CONV_W):
        conv = conv + xp_scr[base + j:base + j + chunk, :] * cw_ref[j:j + 1, :]
    xp_scr[0:SUBLANES, :] = xp_scr[chunk:chunk + SUBLANES, :]
    qkv = conv * _sigmoid(conv)
    z = pg_ref[0, :, GDN_CONV_DIM:GDN_CONV_DIM + W_GDN]
    ba = pg_ref[0, :, GDN_CONV_DIM + W_GDN:]
    beta_blk = _sigmoid(ba)
    g_blk = -jnp.exp(alog_ref[...]) * _softplus(ba + dtb_ref[...])
    if t_valid < chunk:
        valid = lax.broadcasted_iota(jnp.int32, (chunk, 1), 0) < t_valid
        beta_blk = jnp.where(valid, beta_blk, 0.0)
        g_blk = jnp.where(valid, g_blk, 0.0)

    incl, strict, eye = _tri_masks(chunk)
    gc_blk = _dot(incl.astype(F32), g_blk, HIGHEST)
    lane = lax.broadcasted_iota(jnp.int32, (chunk, LANES), 1)

    for h in range(H_G):
        sl = slice(h * D_G, (h + 1) * D_G)
        q = qkv[:, sl]
        q = q * lax.rsqrt(jnp.sum(q * q, axis=-1, keepdims=True) + L2_EPS) * (D_G ** -0.5)
        k = qkv[:, W_GDN + h * D_G:W_GDN + (h + 1) * D_G]
        k = k * lax.rsqrt(jnp.sum(k * k, axis=-1, keepdims=True) + L2_EPS)
        v = qkv[:, 2 * W_GDN + h * D_G:2 * W_GDN + (h + 1) * D_G]
        beta = beta_blk[:, h:h + 1]
        gcol = gc_blk[:, H_G + h:H_G + h + 1]
        grow = _dot_nt((lane == H_G + h).astype(F32), gc_blk, HIGHEST)
        g_last = gcol[chunk - 1:chunk, :]
        decay = jnp.where(incl, jnp.exp(jnp.where(incl, gcol - grow, 0.0)), 0.0)
        k_beta = k * beta
        lmat = jnp.where(strict, _dot_nt(k_beta, k, HIGHEST) * decay, 0.0)
        t_inv = _unit_lower_inverse(-lmat, eye, chunk)
        e_gc = jnp.exp(gcol)
        u = _dot(t_inv, v * beta, HIGHEST)
        w = _dot(t_inv, k_beta * e_gc, HIGHEST)
        attn = jnp.where(incl, _dot_nt(q, k, HIGHEST) * decay, 0.0)
        s = s_scr[h]
        v_new = u - _dot(w, s, HIGHEST)
        o = _dot(q * e_gc, s, HIGHEST) + _dot(attn, v_new, HIGHEST)
        s_scr[h] = s * jnp.exp(g_last) + _dot_tn(k * jnp.exp(g_last - gcol), v_new, HIGHEST)
        o = o * lax.rsqrt(jnp.mean(o * o, axis=-1, keepdims=True) + RMS_EPS) * nw_ref[...]
        z_h = z[:, sl]
        y_ref[0, :, sl] = (o * (z_h * _sigmoid(z_h))).astype(y_ref.dtype)

    @pl.when(c == n_chunks - 1)
    def _fin():
        sout_ref[0] = s_scr[...]


def _gdn_mix(pg, conv_buf, s0, p, chunk, t_valid):
    b, t, _ = pg.shape
    n_chunks = t // chunk
    cbuf = jnp.pad(conv_buf, ((0, 0), (SUBLANES - (CONV_W - 1), 0), (0, 0)))
    const = lambda shape: pl.BlockSpec(shape, lambda i, j: (0,) * len(shape))
    return pl.pallas_call(
        functools.partial(_gdn_body, chunk=chunk, t_valid=t_valid, n_chunks=n_chunks),
        grid=(b, n_chunks),
        in_specs=[
            pl.BlockSpec((1, chunk, G_PROJ_PAD), lambda i, j: (i, j, 0)),
            pl.BlockSpec((1, SUBLANES, GDN_CONV_DIM), lambda i, j: (i, 0, 0)),
            pl.BlockSpec((1, H_G, D_G, D_G), lambda i, j: (i, 0, 0, 0)),
            const((CONV_W, GDN_CONV_DIM)), const((1, LANES)), const((1, LANES)), const((1, D_G)),
        ],
        out_specs=[
            pl.BlockSpec((1, chunk, W_GDN), lambda i, j: (i, j, 0)),
            pl.BlockSpec((1, H_G, D_G, D_G), lambda i, j: (i, 0, 0, 0)),
        ],
        out_shape=[jax.ShapeDtypeStruct((b, t, W_GDN), BF16), jax.ShapeDtypeStruct((b, H_G, D_G, D_G), F32)],
        scratch_shapes=[pltpu.VMEM((H_G, D_G, D_G), F32), pltpu.VMEM((SUBLANES + chunk, GDN_CONV_DIM), F32)],
        compiler_params=pltpu.CompilerParams(dimension_semantics=("arbitrary", "arbitrary")),
        name="gdn_mix",
    )(pg, cbuf, s0, p['conv_w'], p['a_log'], p['dt_bias'], p['gdn_norm_w'])


def _out_q_body(x_ref, yr_ref, yg_ref, wor_ref, wog_ref, gn_ref, wq_ref, x1_ref, q_ref):
    x1 = x_ref[...] + _dot(yr_ref[...], wor_ref[...]) + _dot(yg_ref[...], wog_ref[...])
    x1_ref[...] = x1
    q_ref[...] = _dot(_rmsnorm(x1, gn_ref[...]).astype(BF16), wq_ref[...]).astype(q_ref.dtype)


def _out_q(x, yr, yg, p, tm):
    n, d = x.shape
    assert n % tm == 0
    row = lambda w: pl.BlockSpec((tm, w), lambda i: (i, 0))
    const = lambda shape: pl.BlockSpec(shape, lambda i: (0, 0))
    return pl.pallas_call(
        _out_q_body,
        grid=(n // tm,),
        in_specs=[row(d), row(W_RWKV), row(W_GDN), const((W_RWKV, d)), const((W_GDN, d)), const((1, d)),
                  const((d, d))],
        out_specs=[row(d), row(d)],
        out_shape=[jax.ShapeDtypeStruct((n, d), F32), jax.ShapeDtypeStruct((n, d), BF16)],
        compiler_params=pltpu.CompilerParams(dimension_semantics=("parallel",), vmem_limit_bytes=VMEM_LIMIT),
        name="out_q",
    )(x, yr, yg, p['w_out_r'], p['w_out_g'], p['norm_cross'], p['wq'])


def _attn_body(q_ref, x_ref, mk_ref, mv_ref, wo_ref, gn_ref, rw_ref, rb_ref, x2_ref, h_ref, ti_ref, gt_ref,
               *, bb):
    for i in range(bb):
        q = q_ref[i]
        heads = []
        for hh in range(H_X):
            sl = slice(hh * D_X, (hh + 1) * D_X)
            s = _dot_nt(q[:, sl], mk_ref[i, :, sl].astype(BF16)) * (D_X ** -0.5)
            e = jnp.exp(s - jnp.max(s, axis=-1, keepdims=True))
            prob = e / jnp.sum(e, axis=-1, keepdims=True)
            heads.append(_dot(prob.astype(BF16), mv_ref[i, :, sl].astype(BF16)))
        o = jnp.concatenate(heads, axis=-1).astype(BF16)
        x2 = x_ref[i] + _dot(o, wo_ref[...])
        x2_ref[i] = x2
        h = _rmsnorm(x2, gn_ref[...])
        h_ref[i] = h
        logits = _dot(h, rw_ref[...], HIGHEST) + rb_ref[...]
        lane = lax.broadcasted_iota(jnp.int32, logits.shape, 1)
        vals, idxs = [], []
        for _ in range(TOP_K):
            m = jnp.max(logits, axis=-1, keepdims=True)
            first = jnp.min(jnp.where(logits == m, lane, LANES), axis=-1, keepdims=True)
            vals.append(m)
            idxs.append(first)
            logits = jnp.where(lane == first, -jnp.inf, logits)
        es = [jnp.exp(vv - vals[0]) for vv in vals]
        den = es[0] + es[1] + es[2] + es[3]
        ti = jnp.zeros(lane.shape, jnp.int32)
        gt = jnp.zeros(lane.shape, F32)
        for j in range(TOP_K):
            ti = jnp.where(lane == j, idxs[j], ti)
            gt = jnp.where(lane == j, es[j] / den, gt)
        ti_ref[i] = ti
        gt_ref[i] = gt


def _attn_route(q, x1, mk, mv, p, bb, tq):
    b, t, d = x1.shape
    assert b % bb == 0 and t % tq == 0
    blk = lambda w: pl.BlockSpec((bb, tq, w), lambda i, j: (i, j, 0))
    mem = pl.BlockSpec((bb, N_MEM, d), lambda i, j: (i, 0, 0))
    const = lambda shape: pl.BlockSpec(shape, lambda i, j: (0, 0))
    return pl.pallas_call(
        functools.partial(_attn_body, bb=bb),
        grid=(b // bb, t // tq),
        in_specs=[blk(d), blk(d), mem, mem, const((d, d)), const((1, d)), const((d, LANES)), const((1, LANES))],
        out_specs=[blk(d), blk(d), blk(LANES), blk(LANES)],
        out_shape=[jax.ShapeDtypeStruct((b, t, d), F32), jax.ShapeDtypeStruct((b, t, d), F32),
                   jax.ShapeDtypeStruct((b, t, LANES), jnp.int32), jax.ShapeDtypeStruct((b, t, LANES), F32)],
        compiler_params=pltpu.CompilerParams(dimension_semantics=("parallel", "parallel"),
                                             vmem_limit_bytes=VMEM_LIMIT),
        name="attn_route",
    )(q, x1, mk, mv, p['wo'], p['norm_ffn'], p['router_w'], p['router_b'])


def _moe_body(be_ref, nv_ref, asg_ref, h_hbm, w1_ref, b1_ref, w2_ref, b2_ref, out_hbm,
              xbuf, ybuf, w1b, w2b, gsem, ssem, *, tm):
    i = pl.program_id(0)
    nv = nv_ref[i]

    def gather_copy(r):
        tok = lax.shift_right_logical(asg_ref[0, 0, r], 2)
        return pltpu.make_async_copy(h_hbm.at[pl.ds(tok, 1)], xbuf.at[pl.ds(r, 1)], gsem)

    def scatter_copy(r):
        return pltpu.make_async_copy(ybuf.at[pl.ds(r, 1)], out_hbm.at[pl.ds(asg_ref[0, 0, r], 1)], ssem)

    @pl.when(nv > 0)
    def _block():
        @pl.loop(0, tm)
        def _(r):
            gather_copy(r).start()

        @pl.when(jnp.logical_or(i == 0, be_ref[i] != be_ref[jnp.maximum(i - 1, 0)]))
        def _cast():
            w1b[...] = w1_ref[0].astype(BF16)
            w2b[...] = w2_ref[0].astype(BF16)

        @pl.loop(0, tm)
        def _(r):
            gather_copy(r).wait()

        hc = _dot(xbuf[...].astype(BF16), w1b[...]) + b1_ref[0]
        hg = jnp.minimum(hc[:, :D_FF], SWIGLU_LIMIT)
        hl = jnp.clip(hc[:, D_FF:], -SWIGLU_LIMIT, SWIGLU_LIMIT)
        act = hg * _sigmoid(SWIGLU_ALPHA * hg) * (hl + 1.0)
        ybuf[...] = _dot(act.astype(BF16), w2b[...]) + b2_ref[0]

        @pl.loop(0, nv)
        def _(r):
            scatter_copy(r).start()

        @pl.loop(0, nv)
        def _(r):
            scatter_copy(r).wait()


def _moe(h, asg_rows, block_e, block_nv, p, tm):
    n, d = h.shape
    n_blocks = block_e.shape[0]
    grid_spec = pltpu.PrefetchScalarGridSpec(
        num_scalar_prefetch=2,
        grid=(n_blocks,),
        in_specs=[
            pl.BlockSpec((1, 1, tm), lambda i, be, nv: (i, 0, 0), memory_space=pltpu.SMEM),
            pl.BlockSpec(memory_space=pl.ANY),
            pl.BlockSpec((1, d, 2 * D_FF), lambda i, be, nv: (be[i], 0, 0)),
            pl.BlockSpec((1, 1, 2 * D_FF), lambda i, be, nv: (be[i], 0, 0)),
            pl.BlockSpec((1, D_FF, d), lambda i, be, nv: (be[i], 0, 0)),
            pl.BlockSpec((1, 1, d), lambda i, be, nv: (be[i], 0, 0)),
        ],
        out_specs=pl.BlockSpec(memory_space=pl.ANY),
        scratch_shapes=[
            pltpu.VMEM((tm, d), F32), pltpu.VMEM((tm, d), F32),
            pltpu.VMEM((d, 2 * D_FF), BF16), pltpu.VMEM((D_FF, d), BF16),
            pltpu.SemaphoreType.DMA(()), pltpu.SemaphoreType.DMA(()),
        ],
    )
    return pl.pallas_call(
        functools.partial(_moe_body, tm=tm),
        grid_spec=grid_spec,
        out_shape=jax.ShapeDtypeStruct((n * TOP_K, d), F32),
        compiler_params=pltpu.CompilerParams(dimension_semantics=("arbitrary",), vmem_limit_bytes=VMEM_LIMIT),
        name="moe_experts",
    )(block_e, block_nv, asg_rows.reshape(n_blocks, 1, tm), h, p['w1_e'], p['b1_e'], p['w2_e'], p['b2_e'])


def _route_plan(top_i, tm):
    n = top_i.shape[0]
    na = n * TOP_K
    flat_e = top_i.reshape(na)
    order = jnp.argsort(flat_e).astype(jnp.int32)
    sorted_e = flat_e[order]
    counts = jnp.bincount(flat_e, length=N_EXPERTS).astype(jnp.int32)
    padded = (counts + tm - 1) // tm * tm
    starts = jnp.cumsum(counts) - counts
    pends = jnp.cumsum(padded)
    pstarts = pends - padded
    dest = pstarts[sorted_e] + jnp.arange(na, dtype=jnp.int32) - starts[sorted_e]
    n_blocks = -(-(na + N_EXPERTS * (tm - 1)) // tm)
    asg_rows = jnp.zeros((n_blocks * tm,), jnp.int32).at[dest].set(order)
    blk_start = jnp.arange(n_blocks, dtype=jnp.int32) * tm
    block_e = jnp.clip(jnp.searchsorted(pends, blk_start, side='right'), 0, N_EXPERTS - 1).astype(jnp.int32)
    block_nv = jnp.clip(pstarts[block_e] + counts[block_e] - blk_start, 0, tm).astype(jnp.int32)
    return asg_rows, block_e, block_nv


def _combine_body(x_ref, slot_ref, gt_ref, gn_ref, y_ref, *, final):
    x = x_ref[...]
    gt = gt_ref[...]
    d = x.shape[-1]
    for j in range(TOP_K):
        x = x + gt[:, j:j + 1] * slot_ref[:, j * d:(j + 1) * d]
    y_ref[...] = _rmsnorm(x, gn_ref[...]) if final else x


def _combine(x2, slots, gates, gn, tm, final):
    n, d = x2.shape
    assert n % tm == 0
    return pl.pallas_call(
        functools.partial(_combine_body, final=final),
        grid=(n // tm,),
        in_specs=[pl.BlockSpec((tm, d), lambda i: (i, 0)), pl.BlockSpec((tm, TOP_K * d), lambda i: (i, 0)),
                  pl.BlockSpec((tm, LANES), lambda i: (i, 0)), pl.BlockSpec((1, d), lambda i: (0, 0))],
        out_specs=pl.BlockSpec((tm, d), lambda i: (i, 0)),
        out_shape=jax.ShapeDtypeStruct((n, d), F32),
        compiler_params=pltpu.CompilerParams(dimension_semantics=("parallel",), vmem_limit_bytes=VMEM_LIMIT),
        name="combine",
    )(x2, slots, gates, gn)


def _layer_params(l, norm_mix, w_in, mu_shift, w0, w2_decay, a0, a2_iclr, g2_gate, k_k, k_a, r_k, lnx_w, lnx_b,
                  conv_w, a_log, dt_bias, gdn_norm_w, w_out, norm_cross, norm_mem, wq_x, wk_x, wv_x, wo_x,
                  norm_ffn, router_w, router_b, w1_e, b1_e, w2_e, b2_e):
    row = lambda z: z.reshape(1, -1).astype(F32)
    lane_pad = lambda z, at: jnp.zeros((1, LANES), F32).at[0, at:at + z.shape[0]].set(z)
    return {
        'norm_mix': row(norm_mix[l]),
        'w_in_r': w_in[l][:, :R_PROJ].astype(BF16),
        'w_in_g': jnp.pad(w_in[l][:, R_PROJ:], ((0, 0), (0, G_PROJ_PAD - G_PROJ))).astype(BF16),
        'mu': row(mu_shift[l]), 'w0': row(w0[l]), 'a0': row(a0[l]),
        'w2': jnp.pad(w2_decay[l], ((0, LORA_A), (0, 0))).astype(BF16),
        'a2': jnp.pad(a2_iclr[l], ((LORA_W, 0), (0, 0))).astype(BF16),
        'g2': g2_gate[l].astype(BF16),
        'k_k': row(k_k[l]), 'k_a': row(k_a[l]), 'r_k': row(r_k[l]), 'lnx_w': row(lnx_w[l]), 'lnx_b': row(lnx_b[l]),
        'conv_w': conv_w[l].astype(F32),
        'a_log': lane_pad(a_log[l], H_G), 'dt_bias': lane_pad(dt_bias[l], H_G),
        'gdn_norm_w': row(gdn_norm_w[l]),
        'w_out_r': w_out[l][:W_RWKV].astype(BF16), 'w_out_g': w_out[l][W_RWKV:].astype(BF16),
        'norm_cross': row(norm_cross[l]), 'norm_mem': row(norm_mem[l]),
        'wq': wq_x[l].astype(BF16), 'wk': wk_x[l].astype(BF16), 'wv': wv_x[l].astype(BF16),
        'wo': wo_x[l].astype(BF16),
        'norm_ffn': row(norm_ffn[l]),
        'router_w': jnp.pad(router_w[l].astype(F32), ((0, 0), (0, LANES - N_EXPERTS))),
        'router_b': jnp.full((1, LANES), NEG_BIG, F32).at[0, :N_EXPERTS].set(router_b[l].astype(F32)),
        'w1_e': w1_e[l], 'b1_e': b1_e[l][:, None, :], 'w2_e': w2_e[l], 'b2_e': b2_e[l][:, None, :],
    }


def _mix_and_attend(x, mk, mv, shift_prev, s_r, conv_buf, s_g, p, *, chunk, tm, bb, tq):
    b, t, d = x.shape
    assert t >= CONV_W - 1
    pr, pg = _norm_proj(x.reshape(b * t, d), p['norm_mix'], [p['w_in_r'], p['w_in_g']], [F32, F32], tm)
    pr = pr.reshape(b, t, R_PROJ)
    pg = pg.reshape(b, t, G_PROJ_PAD)
    shift_new = pr[:, t - 1]
    conv_new = pg[:, t - (CONV_W - 1):, :GDN_CONV_DIM]
    t_pad = -(-t // chunk) * chunk
    if t_pad != t:
        pr = jnp.pad(pr, ((0, 0), (0, t_pad - t), (0, 0)))
        pg = jnp.pad(pg, ((0, 0), (0, t_pad - t), (0, 0)))
    t_valid = chunk if t_pad == t else t
    y_r, s_r_new = _rwkv_mix(pr, shift_prev, s_r, p, chunk, t_valid)
    y_g, s_g_new = _gdn_mix(pg, conv_buf, s_g, p, chunk, t_valid)
    if t_pad != t:
        y_r, y_g = y_r[:, :t], y_g[:, :t]
    x1, q = _out_q(x.reshape(b * t, d), y_r.reshape(b * t, W_RWKV), y_g.reshape(b * t, W_GDN), p, tm)
    x2, h, top_i, gates = _attn_route(q.reshape(b, t, d), x1.reshape(b, t, d), mk, mv, p, bb, tq)
    return (x2.reshape(b * t, d), h.reshape(b * t, d), top_i.reshape(b * t, LANES), gates.reshape(b * t, LANES),
            shift_new, s_r_new, conv_new, s_g_new)


def kernel(x_prompt, x_sample, mem_prompt, state_rwkv, state_rwkv_shift, state_gdn, state_gdn_conv, cache_mem_k, cache_mem_v, norm_mix, w_in, mu_shift, w0, w2_decay, a0, a2_iclr, g2_gate, k_k, k_a, r_k, lnx_w, lnx_b, conv_w, a_log, dt_bias, gdn_norm_w, w_out, norm_cross, norm_mem, wq_x, wk_x, wv_x, wo_x, norm_ffn, router_w, router_b, w1_e, b1_e, w2_e, b2_e, final_norm):
    bp, tp, d = x_prompt.shape
    bs, ts, _ = x_sample.shape
    depth = w_in.shape[0]
    np_, ns = bp * tp, bs * ts
    xp, xs = x_prompt, x_sample
    outs = [[] for _ in range(10)]
    for l in range(depth):
        p = _layer_params(l, norm_mix, w_in, mu_shift, w0, w2_decay, a0, a2_iclr, g2_gate, k_k, k_a, r_k, lnx_w,
                          lnx_b, conv_w, a_log, dt_bias, gdn_norm_w, w_out, norm_cross, norm_mem, wq_x, wk_x, wv_x,
                          wo_x, norm_ffn, router_w, router_b, w1_e, b1_e, w2_e, b2_e)
        n_mem = mem_prompt.shape[1]
        mk, mv = _norm_proj(mem_prompt.reshape(bp * n_mem, d), p['norm_mem'], [p['wk'], p['wv']], [F32, F32], 256)
        mk, mv = mk.reshape(bp, n_mem, d), mv.reshape(bp, n_mem, d)
        res_p = _mix_and_attend(
            xp, mk, mv, jnp.zeros((bp, R_PROJ), F32), jnp.zeros((bp, H_R, N_R, N_R), F32),
            jnp.zeros((bp, CONV_W - 1, GDN_CONV_DIM), F32), jnp.zeros((bp, H_G, D_G, D_G), F32), p,
            chunk=MIX_CHUNK, tm=256, bb=1, tq=512)
        res_s = _mix_and_attend(
            xs, cache_mem_k[l].reshape(bs, n_mem, d), cache_mem_v[l].reshape(bs, n_mem, d), state_rwkv_shift[l],
            state_rwkv[l], state_gdn_conv[l], state_gdn[l], p, chunk=SUBLANES, tm=256, bb=8, tq=ts)
        x2 = jnp.concatenate([res_p[0], res_s[0]], axis=0)
        h = jnp.concatenate([res_p[1], res_s[1]], axis=0)
        top_i = jnp.concatenate([res_p[2], res_s[2]], axis=0)[:, :TOP_K]
        gates = jnp.concatenate([res_p[3], res_s[3]], axis=0)
        asg_rows, block_e, block_nv = _route_plan(top_i, MOE_ROWS)
        slots = _moe(h, asg_rows, block_e, block_nv, p, MOE_ROWS)
        gn = final_norm.reshape(1, d).astype(F32)
        x3 = _combine(x2, slots.reshape(np_ + ns, TOP_K * d), gates, gn, 256, final=(l == depth - 1))
        xp, xs = x3[:np_].reshape(bp, tp, d), x3[np_:].reshape(bs, ts, d)
        new = [res_p[5], res_p[4], res_p[7], res_p[6], mk.reshape(bp, n_mem, H_X, D_X),
               mv.reshape(bp, n_mem, H_X, D_X), res_s[5], res_s[4], res_s[7], res_s[6]]
        for acc, val in zip(outs, new):
            acc.append(val)
    return (xp, xs) + tuple(jnp.stack(o) for o in outs)
```

```python
import functools

import jax
import jax.numpy as jnp
from jax import lax
from jax.experimental import pallas as pl
from jax.experimental.pallas import tpu as pltpu

F32 = jnp.float32
BF16 = jnp.bfloat16
HIGHEST = lax.Precision.HIGHEST
DEFAULT = lax.Precision.DEFAULT

D_MODEL = 1024
W_RWKV = 512
N_R = 64
H_R = W_RWKV // N_R
LORA_W = 64
LORA_A = 64
LORA_G = 128
R_PROJ = 3 * W_RWKV + LORA_W + LORA_A + LORA_G
GN_EPS = 64e-5
W_GDN = 512
D_G = 128
H_G = W_GDN // D_G
GDN_CONV_DIM = 3 * W_GDN
CONV_W = 4
G_PROJ = GDN_CONV_DIM + W_GDN + 2 * H_G
LANES = 128
SUBLANES = 8
G_PROJ_PAD = GDN_CONV_DIM + W_GDN + LANES
MIX_CHUNK = 64
N_MEM = 256
H_X = 4
D_X = D_MODEL // H_X
N_EXPERTS = 32
TOP_K = 4
D_FF = D_MODEL
SWIGLU_LIMIT = 7.0
SWIGLU_ALPHA = 1.702
MOE_ROWS = 256
RMS_EPS = 1e-6
L2_EPS = 1e-6
NEG_BIG = -1e30
VMEM_LIMIT = 56 * 1024 * 1024


def _dot(a, b, precision=DEFAULT):
    return jnp.dot(a, b, preferred_element_type=F32, precision=precision)


def _dot_nt(a, b, precision=DEFAULT):
    return lax.dot_general(a, b, (((1,), (1,)), ((), ())), preferred_element_type=F32, precision=precision)


NN = (((1,), (0,)), ((), ()))
NT = (((1,), (1,)), ((), ()))
TN = (((0,), (0,)), ((), ()))
STATE_PASSES = 3


def _split2(x):
    hi = x.astype(BF16)
    return hi, (x - hi.astype(F32)).astype(BF16)


def _mm(a, b, dims, passes):
    dg = lambda x, y: lax.dot_general(x, y, dims, preferred_element_type=F32)
    if passes == 1:
        return dg(a.astype(BF16), b.astype(BF16))
    a_hi, a_lo = _split2(a)
    b_hi, b_lo = _split2(b)
    return dg(a_hi, b_hi) + (dg(a_hi, b_lo) + dg(a_lo, b_hi))


def _sel_mm(sel, x, dims):
    dg = lambda y: lax.dot_general(sel, y, dims, preferred_element_type=F32)
    hi = x.astype(BF16)
    r1 = x - hi.astype(F32)
    mid = r1.astype(BF16)
    lo = (r1 - mid.astype(F32)).astype(BF16)
    return dg(hi) + (dg(mid) + dg(lo))


def _sigmoid(x):
    return 1.0 / (1.0 + jnp.exp(-x))


def _softplus(x):
    return jnp.maximum(x, 0.0) + jnp.log(1.0 + jnp.exp(-jnp.abs(x)))


def _rmsnorm(x, g):
    return x * lax.rsqrt(jnp.mean(x * x, axis=-1, keepdims=True) + RMS_EPS) * g


def _tri_masks(c):
    row = lax.broadcasted_iota(jnp.int32, (c, c), 0)
    col = lax.broadcasted_iota(jnp.int32, (c, c), 1)
    return col <= row, col < row, (col == row).astype(F32)


def _unit_lower_inverses(ms, eye, c):
    ts = [eye + m for m in ms]
    ps = list(ms)
    covered = 2
    while covered < c:
        ps = [_mm(p, p, NN, 1) for p in ps]
        ts = [t + _mm(t, p, NN, 1) for t, p in zip(ts, ps)]
        covered *= 2
    return ts


def _norm_proj_body(x_ref, g_ref, *refs, n_out):
    w_refs, o_refs = refs[:n_out], refs[n_out:]
    hb = _rmsnorm(x_ref[...], g_ref[...]).astype(BF16)
    for w_ref, o_ref in zip(w_refs, o_refs):
        o_ref[...] = _dot(hb, w_ref[...]).astype(o_ref.dtype)


def _norm_proj(x, g, ws, out_dtypes, tm):
    n, d = x.shape
    assert n % tm == 0
    in_specs = [pl.BlockSpec((tm, d), lambda i: (i, 0)), pl.BlockSpec((1, d), lambda i: (0, 0))]
    in_specs += [pl.BlockSpec(w.shape, lambda i: (0, 0)) for w in ws]
    return pl.pallas_call(
        functools.partial(_norm_proj_body, n_out=len(ws)),
        grid=(n // tm,),
        in_specs=in_specs,
        out_specs=[pl.BlockSpec((tm, w.shape[1]), lambda i: (i, 0)) for w in ws],
        out_shape=[jax.ShapeDtypeStruct((n, w.shape[1]), dt) for w, dt in zip(ws, out_dtypes)],
        compiler_params=pltpu.CompilerParams(dimension_semantics=("parallel",), vmem_limit_bytes=VMEM_LIMIT),
        name="norm_proj",
    )(x, g, *ws)


def _rwkv_body(pr_ref, shift_ref, s0_ref, mu_ref, w0_ref, w2_ref, a0_ref, a2_ref, g2_ref, kk_ref, ka_ref,
               rk_ref, lnw_ref, lnb_ref, y_ref, sout_ref, s_scr, prev_scr, *, chunk, t_valid, n_chunks):
    c = pl.program_id(1)

    @pl.when(c == 0)
    def _init():
        s_scr[...] = s0_ref[0]
        prev_scr[...] = shift_ref[0]

    pr = pr_ref[0]
    rows = lax.broadcasted_iota(jnp.int32, (chunk, 1), 0)
    prev = jnp.where(rows == 0, prev_scr[...], pltpu.roll(pr, 1, 0))
    prev_scr[...] = pr[chunk - 1:chunk, :]
    xm = pr + (prev - pr) * mu_ref[...]
    r = xm[:, :W_RWKV]
    k = xm[:, W_RWKV:2 * W_RWKV]
    v = xm[:, 2 * W_RWKV:3 * W_RWKV]
    lo = xm[:, 3 * W_RWKV:3 * W_RWKV + LORA_W + LORA_A]
    g_lo = xm[:, 3 * W_RWKV + LORA_W + LORA_A:]
    logw = -_softplus(-(w0_ref[...] + _dot(jnp.tanh(lo).astype(BF16), w2_ref[...]))) - 0.5
    wl = -jnp.exp(logw)
    a = _sigmoid(a0_ref[...] + _dot(lo.astype(BF16), a2_ref[...]))
    g = _dot(_sigmoid(g_lo).astype(BF16), g2_ref[...])
    kkv = k * kk_ref[...]
    k = k * (1.0 + (a - 1.0) * ka_ref[...])
    if t_valid < chunk:
        valid = rows < t_valid
        wl = jnp.where(valid, wl, 0.0)
        kkv = jnp.where(valid, kkv, 0.0)
        k = jnp.where(valid, k, 0.0)
        v = jnp.where(valid, v, 0.0)

    incl, strict, eye = _tri_masks(chunk)
    cum = _sel_mm(incl.astype(BF16), wl, NN)
    cum_last = cum[chunk - 1:chunk, :]
    w_incl = jnp.exp(cum)
    w_prev = jnp.exp(cum - wl)
    w_inv = jnp.exp(-cum)
    w_tail = jnp.exp(cum_last - cum)
    w_last = jnp.exp(cum_last)

    heads = range(H_R)
    sls = [slice(h * N_R, (h + 1) * N_R) for h in heads]
    kks = [kkv[:, sl] for sl in sls]
    kks = [x * lax.rsqrt(jnp.sum(x * x, axis=-1, keepdims=True) + L2_EPS) for x in kks]
    a_hat = [-(kks[h] * w_prev[:, sls[h]]) for h in heads]
    kka = [kks[h] * a[:, sls[h]] for h in heads]
    b_hat = [kka[h] * w_inv[:, sls[h]] for h in heads]
    k_hat = [k[:, sls[h]] * w_inv[:, sls[h]] for h in heads]
    r_hat = [r[:, sls[h]] * w_incl[:, sls[h]] for h in heads]
    m_ab = [jnp.where(strict, _mm(a_hat[h], b_hat[h], NT, 1), 0.0) for h in heads]
    m_ak = [jnp.where(strict, _mm(a_hat[h], k_hat[h], NT, 1), 0.0) for h in heads]
    a_rb = [jnp.where(incl, _mm(r_hat[h], b_hat[h], NT, 1), 0.0) for h in heads]
    a_rk = [jnp.where(incl, _mm(r_hat[h], k_hat[h], NT, 1), 0.0) for h in heads]
    t_inv = _unit_lower_inverses(m_ab, eye, chunk)
    w_hat = [_mm(t_inv[h], a_hat[h], NN, 1) for h in heads]
    mv = [_mm(m_ak[h], v[:, sls[h]], NN, 1) for h in heads]
    u = [_mm(t_inv[h], mv[h], NN, 1) for h in heads]
    ss = [s_scr[h] for h in heads]
    p = [_mm(w_hat[h], ss[h], NT, STATE_PASSES) + u[h] for h in heads]
    ys = [_mm(r_hat[h], ss[h], NT, 1) + _mm(a_rb[h], p[h], NN, 1) + _mm(a_rk[h], v[:, sls[h]], NN, 1)
          for h in heads]
    for h in heads:
        sl = sls[h]
        s_scr[h] = (ss[h] * w_last[:, sl] + _mm(p[h], kka[h] * w_tail[:, sl], TN, STATE_PASSES)
                    + _mm(v[:, sl], k[:, sl] * w_tail[:, sl], TN, STATE_PASSES))
    for h in heads:
        sl = sls[h]
        y = ys[h]
        mean = jnp.mean(y, axis=-1, keepdims=True)
        yc = y - mean
        var = jnp.mean(yc * yc, axis=-1, keepdims=True)
        yn = yc * lax.rsqrt(var + GN_EPS) * lnw_ref[:, sl] + lnb_ref[:, sl]
        bonus = jnp.sum(r[:, sl] * k[:, sl] * rk_ref[:, sl], axis=-1, keepdims=True) * v[:, sl]
        y_ref[0, :, sl] = ((yn + bonus) * g[:, sl]).astype(y_ref.dtype)

    @pl.when(c == n_chunks - 1)
    def _fin():
        sout_ref[0] = s_scr[...]


def _rwkv_mix(pr, shift_prev, s0, p, chunk, t_valid):
    b, t, _ = pr.shape
    n_chunks = t // chunk
    const = lambda shape: pl.BlockSpec(shape, lambda i, j: (0,) * len(shape))
    return pl.pallas_call(
        functools.partial(_rwkv_body, chunk=chunk, t_valid=t_valid, n_chunks=n_chunks),
        grid=(b, n_chunks),
        in_specs=[
            pl.BlockSpec((1, chunk, R_PROJ), lambda i, j: (i, j, 0)),
            pl.BlockSpec((1, 1, R_PROJ), lambda i, j: (i, 0, 0)),
            pl.BlockSpec((1, H_R, N_R, N_R), lambda i, j: (i, 0, 0, 0)),
            const((1, R_PROJ)), const((1, W_RWKV)), const((LANES, W_RWKV)), const((1, W_RWKV)),
            const((LANES, W_RWKV)), const((LORA_G, W_RWKV)), const((1, W_RWKV)), const((1, W_RWKV)),
            const((1, W_RWKV)), const((1, W_RWKV)), const((1, W_RWKV)),
        ],
        out_specs=[
            pl.BlockSpec((1, chunk, W_RWKV), lambda i, j: (i, j, 0)),
            pl.BlockSpec((1, H_R, N_R, N_R), lambda i, j: (i, 0, 0, 0)),
        ],
        out_shape=[jax.ShapeDtypeStruct((b, t, W_RWKV), BF16), jax.ShapeDtypeStruct((b, H_R, N_R, N_R), F32)],
        scratch_shapes=[pltpu.VMEM((H_R, N_R, N_R), F32), pltpu.VMEM((1, R_PROJ), F32)],
        compiler_params=pltpu.CompilerParams(dimension_semantics=("arbitrary", "arbitrary")),
        name="rwkv_mix",
    )(pr, shift_prev[:, None, :], s0, p['mu'], p['w0'], p['w2'], p['a0'], p['a2'], p['g2'], p['k_k'], p['k_a'],
      p['r_k'], p['lnx_w'], p['lnx_b'])


def _gdn_body(pg_ref, cbuf_ref, s0_ref, cw_ref, alog_ref, dtb_ref, nw_ref, y_ref, sout_ref, s_scr, xp_scr,
              *, chunk, t_valid, n_chunks):
    c = pl.program_id(1)

    @pl.when(c == 0)
    def _init():
        s_scr[...] = s0_ref[0]
        xp_scr[0:SUBLANES, :] = cbuf_ref[0]

    xp_scr[SUBLANES:SUBLANES + chunk, :] = pg_ref[0, :, :GDN_CONV_DIM]
    base = SUBLANES - (CONV_W - 1)
    conv = xp_scr[base:base + chunk, :] * cw_ref[0:1, :]
    for j in range(1, CONV_W):
        conv = conv + xp_scr[base + j:base + j + chunk, :] * cw_ref[j:j + 1, :]
    xp_scr[0:SUBLANES, :] = xp_scr[chunk:chunk + SUBLANES, :]
    qkv = conv * _sigmoid(conv)
    z = pg_ref[0, :, GDN_CONV_DIM:GDN_CONV_DIM + W_GDN]
    ba = pg_ref[0, :, GDN_CONV_DIM + W_GDN:]
    beta_blk = _sigmoid(ba)
    g_blk = -jnp.exp(alog_ref[...]) * _softplus(ba + dtb_ref[...])
    if t_valid < chunk:
        valid = lax.broadcasted_iota(jnp.int32, (chunk, 1), 0) < t_valid
        beta_blk = jnp.where(valid, beta_blk, 0.0)
        g_blk = jnp.where(valid, g_blk, 0.0)

    incl, strict, eye = _tri_masks(chunk)
    gc_blk = _sel_mm(incl.astype(BF16), g_blk, NN)
    lane = lax.broadcasted_iota(jnp.int32, (chunk, LANES), 1)

    heads = range(H_G)
    sls = [slice(h * D_G, (h + 1) * D_G) for h in heads]
    qs = [qkv[:, sl] for sl in sls]
    qs = [x * lax.rsqrt(jnp.sum(x * x, axis=-1, keepdims=True) + L2_EPS) * (D_G ** -0.5) for x in qs]
    ks = [qkv[:, W_GDN + h * D_G:W_GDN + (h + 1) * D_G] for h in heads]
    ks = [x * lax.rsqrt(jnp.sum(x * x, axis=-1, keepdims=True) + L2_EPS) for x in ks]
    vs = [qkv[:, 2 * W_GDN + h * D_G:2 * W_GDN + (h + 1) * D_G] for h in heads]
    betas = [beta_blk[:, h:h + 1] for h in heads]
    gcols = [gc_blk[:, H_G + h:H_G + h + 1] for h in heads]
    grows = [_sel_mm((lane == H_G + h).astype(BF16), gc_blk, NT) for h in heads]
    g_last = [gcols[h][chunk - 1:chunk, :] for h in heads]
    decay = [jnp.where(incl, jnp.exp(jnp.where(incl, gcols[h] - grows[h], 0.0)), 0.0) for h in heads]
    k_beta = [ks[h] * betas[h] for h in heads]
    lmat = [jnp.where(strict, _mm(k_beta[h], ks[h], NT, 1) * decay[h], 0.0) for h in heads]
    attn = [jnp.where(incl, _mm(qs[h], ks[h], NT, 1) * decay[h], 0.0) for h in heads]
    t_inv = _unit_lower_inverses([-x for x in lmat], eye, chunk)
    e_gc = [jnp.exp(gcols[h]) for h in heads]
    u = [_mm(t_inv[h], vs[h] * betas[h], NN, 1) for h in heads]
    w = [_mm(t_inv[h], k_beta[h] * e_gc[h], NN, 1) for h in heads]
    ss = [s_scr[h] for h in heads]
    v_new = [u[h] - _mm(w[h], ss[h], NN, STATE_PASSES) for h in heads]
    os_ = [_mm(qs[h] * e_gc[h], ss[h], NN, 1) + _mm(attn[h], v_new[h], NN, 1) for h in heads]
    for h in heads:
        s_scr[h] = (ss[h] * jnp.exp(g_last[h])
                    + _mm(ks[h] * jnp.exp(g_last[h] - gcols[h]), v_new[h], TN, STATE_PASSES))
    for h in heads:
        o = os_[h]
        o = o * lax.rsqrt(jnp.mean(o * o, axis=-1, keepdims=True) + RMS_EPS) * nw_ref[...]
        z_h = z[:, sls[h]]
        y_ref[0, :, sls[h]] = (o * (z_h * _sigmoid(z_h))).astype(y_ref.dtype)

    @pl.when(c == n_chunks - 1)
    def _fin():
        sout_ref[0] = s_scr[...]


def _gdn_mix(pg, conv_buf, s0, p, chunk, t_valid):
    b, t, _ = pg.shape
    n_chunks = t // chunk
    cbuf = jnp.pad(conv_buf, ((0, 0), (SUBLANES - (CONV_W - 1), 0), (0, 0)))
    const = lambda shape: pl.BlockSpec(shape, lambda i, j: (0,) * len(shape))
    return pl.pallas_call(
        functools.partial(_gdn_body, chunk=chunk, t_valid=t_valid, n_chunks=n_chunks),
        grid=(b, n_chunks),
        in_specs=[
            pl.BlockSpec((1, chunk, G_PROJ_PAD), lambda i, j: (i, j, 0)),
            pl.BlockSpec((1, SUBLANES, GDN_CONV_DIM), lambda i, j: (i, 0, 0)),
            pl.BlockSpec((1, H_G, D_G, D_G), lambda i, j: (i, 0, 0, 0)),
            const((CONV_W, GDN_CONV_DIM)), const((1, LANES)), const((1, LANES)), const((1, D_G)),
        ],
        out_specs=[
            pl.BlockSpec((1, chunk, W_GDN), lambda i, j: (i, j, 0)),
            pl.BlockSpec((1, H_G, D_G, D_G), lambda i, j: (i, 0, 0, 0)),
        ],
        out_shape=[jax.ShapeDtypeStruct((b, t, W_GDN), BF16), jax.ShapeDtypeStruct((b, H_G, D_G, D_G), F32)],
        scratch_shapes=[pltpu.VMEM((H_G, D_G, D_G), F32), pltpu.VMEM((SUBLANES + chunk, GDN_CONV_DIM), F32)],
        compiler_params=pltpu.CompilerParams(dimension_semantics=("arbitrary", "arbitrary")),
        name="gdn_mix",
    )(pg, cbuf, s0, p['conv_w'], p['a_log'], p['dt_bias'], p['gdn_norm_w'])


def _out_q_body(x_ref, yr_ref, yg_ref, wor_ref, wog_ref, gn_ref, wq_ref, x1_ref, q_ref):
    x1 = x_ref[...] + _dot(yr_ref[...], wor_ref[...]) + _dot(yg_ref[...], wog_ref[...])
    x1_ref[...] = x1
    q_ref[...] = _dot(_rmsnorm(x1, gn_ref[...]).astype(BF16), wq_ref[...]).astype(q_ref.dtype)


def _out_q(x, yr, yg, p, tm):
    n, d = x.shape
    assert n % tm == 0
    row = lambda w: pl.BlockSpec((tm, w), lambda i: (i, 0))
    const = lambda shape: pl.BlockSpec(shape, lambda i: (0, 0))
    return pl.pallas_call(
        _out_q_body,
        grid=(n // tm,),
        in_specs=[row(d), row(W_RWKV), row(W_GDN), const((W_RWKV, d)), const((W_GDN, d)), const((1, d)),
                  const((d, d))],
        out_specs=[row(d), row(d)],
        out_shape=[jax.ShapeDtypeStruct((n, d), F32), jax.ShapeDtypeStruct((n, d), BF16)],
        compiler_params=pltpu.CompilerParams(dimension_semantics=("parallel",), vmem_limit_bytes=VMEM_LIMIT),
        name="out_q",
    )(x, yr, yg, p['w_out_r'], p['w_out_g'], p['norm_cross'], p['wq'])


def _attn_body(q_ref, x_ref, mk_ref, mv_ref, wo_ref, gn_ref, rw_ref, rb_ref, x2_ref, h_ref, ti_ref, gt_ref,
               *, bb):
    for i in range(bb):
        q = q_ref[i]
        heads = []
        for hh in range(H_X):
            sl = slice(hh * D_X, (hh + 1) * D_X)
            s = _dot_nt(q[:, sl], mk_ref[i, :, sl].astype(BF16)) * (D_X ** -0.5)
            e = jnp.exp(s - jnp.max(s, axis=-1, keepdims=True))
            prob = e / jnp.sum(e, axis=-1, keepdims=True)
            heads.append(_dot(prob.astype(BF16), mv_ref[i, :, sl].astype(BF16)))
        o = jnp.concatenate(heads, axis=-1).astype(BF16)
        x2 = x_ref[i] + _dot(o, wo_ref[...])
        x2_ref[i] = x2
        h = _rmsnorm(x2, gn_ref[...])
        h_ref[i] = h
        logits = _dot(h, rw_ref[...], HIGHEST) + rb_ref[...]
        lane = lax.broadcasted_iota(jnp.int32, logits.shape, 1)
        vals, idxs = [], []
        for _ in range(TOP_K):
            m = jnp.max(logits, axis=-1, keepdims=True)
            first = jnp.min(jnp.where(logits == m, lane, LANES), axis=-1, keepdims=True)
            vals.append(m)
            idxs.append(first)
            logits = jnp.where(lane == first, -jnp.inf, logits)
        es = [jnp.exp(vv - vals[0]) for vv in vals]
        den = es[0] + es[1] + es[2] + es[3]
        ti = jnp.zeros(lane.shape, jnp.int32)
        gt = jnp.zeros(lane.shape, F32)
        for j in range(TOP_K):
            ti = jnp.where(lane == j, idxs[j], ti)
            gt = jnp.where(lane == j, es[j] / den, gt)
        ti_ref[i] = ti
        gt_ref[i] = gt


def _attn_route(q, x1, mk, mv, p, bb, tq):
    b, t, d = x1.shape
    assert b % bb == 0 and t % tq == 0
    blk = lambda w: pl.BlockSpec((bb, tq, w), lambda i, j: (i, j, 0))
    mem = pl.BlockSpec((bb, N_MEM, d), lambda i, j: (i, 0, 0))
    const = lambda shape: pl.BlockSpec(shape, lambda i, j: (0, 0))
    return pl.pallas_call(
        functools.partial(_attn_body, bb=bb),
        grid=(b // bb, t // tq),
        in_specs=[blk(d), blk(d), mem, mem, const((d, d)), const((1, d)), const((d, LANES)), const((1, LANES))],
        out_specs=[blk(d), blk(d), blk(LANES), blk(LANES)],
        out_shape=[jax.ShapeDtypeStruct((b, t, d), F32), jax.ShapeDtypeStruct((b, t, d), F32),
                   jax.ShapeDtypeStruct((b, t, LANES), jnp.int32), jax.ShapeDtypeStruct((b, t, LANES), F32)],
        compiler_params=pltpu.CompilerParams(dimension_semantics=("parallel", "parallel"),
                                             vmem_limit_bytes=VMEM_LIMIT),
        name="attn_route",
    )(q, x1, mk, mv, p['wo'], p['norm_ffn'], p['router_w'], p['router_b'])


def _moe_body(be_ref, nv_ref, asg_ref, h_hbm, w1_ref, b1_ref, w2_ref, b2_ref, out_hbm,
              xbuf, ybuf, w1b, w2b, gsem, ssem, *, tm):
    i = pl.program_id(0)
    nv = nv_ref[i]

    def gather_copy(r):
        tok = lax.shift_right_logical(asg_ref[0, 0, r], 2)
        return pltpu.make_async_copy(h_hbm.at[pl.ds(tok, 1)], xbuf.at[pl.ds(r, 1)], gsem)

    def scatter_copy(r):
        return pltpu.make_async_copy(ybuf.at[pl.ds(r, 1)], out_hbm.at[pl.ds(asg_ref[0, 0, r], 1)], ssem)

    @pl.when(nv > 0)
    def _block():
        @pl.loop(0, tm)
        def _(r):
            gather_copy(r).start()

        @pl.when(jnp.logical_or(i == 0, be_ref[i] != be_ref[jnp.maximum(i - 1, 0)]))
        def _cast():
            w1b[...] = w1_ref[0].astype(BF16)
            w2b[...] = w2_ref[0].astype(BF16)

        @pl.loop(0, tm)
        def _(r):
            gather_copy(r).wait()

        hc = _dot(xbuf[...].astype(BF16), w1b[...]) + b1_ref[0]
        hg = jnp.minimum(hc[:, :D_FF], SWIGLU_LIMIT)
        hl = jnp.clip(hc[:, D_FF:], -SWIGLU_LIMIT, SWIGLU_LIMIT)
        act = hg * _sigmoid(SWIGLU_ALPHA * hg) * (hl + 1.0)
        ybuf[...] = _dot(act.astype(BF16), w2b[...]) + b2_ref[0]

        @pl.loop(0, nv)
        def _(r):
            scatter_copy(r).start()

        @pl.loop(0, nv)
        def _(r):
            scatter_copy(r).wait()


def _moe(h, asg_rows, block_e, block_nv, p, tm):
    n, d = h.shape
    n_blocks = block_e.shape[0]
    grid_spec = pltpu.PrefetchScalarGridSpec(
        num_scalar_prefetch=2,
        grid=(n_blocks,),
        in_specs=[
            pl.BlockSpec((1, 1, tm), lambda i, be, nv: (i, 0, 0), memory_space=pltpu.SMEM),
            pl.BlockSpec(memory_space=pl.ANY),
            pl.BlockSpec((1, d, 2 * D_FF), lambda i, be, nv: (be[i], 0, 0)),
            pl.BlockSpec((1, 1, 2 * D_FF), lambda i, be, nv: (be[i], 0, 0)),
            pl.BlockSpec((1, D_FF, d), lambda i, be, nv: (be[i], 0, 0)),
            pl.BlockSpec((1, 1, d), lambda i, be, nv: (be[i], 0, 0)),
        ],
        out_specs=pl.BlockSpec(memory_space=pl.ANY),
        scratch_shapes=[
            pltpu.VMEM((tm, d), F32), pltpu.VMEM((tm, d), F32),
            pltpu.VMEM((d, 2 * D_FF), BF16), pltpu.VMEM((D_FF, d), BF16),
            pltpu.SemaphoreType.DMA(()), pltpu.SemaphoreType.DMA(()),
        ],
    )
    return pl.pallas_call(
        functools.partial(_moe_body, tm=tm),
        grid_spec=grid_spec,
        out_shape=jax.ShapeDtypeStruct((n * TOP_K, d), F32),
        compiler_params=pltpu.CompilerParams(dimension_semantics=("arbitrary",), vmem_limit_bytes=VMEM_LIMIT),
        name="moe_experts",
    )(block_e, block_nv, asg_rows.reshape(n_blocks, 1, tm), h, p['w1_e'], p['b1_e'], p['w2_e'], p['b2_e'])


def _route_plan(top_i, tm):
    n = top_i.shape[0]
    na = n * TOP_K
    flat_e = top_i.reshape(na)
    order = jnp.argsort(flat_e).astype(jnp.int32)
    sorted_e = flat_e[order]
    counts = jnp.bincount(flat_e, length=N_EXPERTS).astype(jnp.int32)
    padded = (counts + tm - 1) // tm * tm
    starts = jnp.cumsum(counts) - counts
    pends = jnp.cumsum(padded)
    pstarts = pends - padded
    dest = pstarts[sorted_e] + jnp.arange(na, dtype=jnp.int32) - starts[sorted_e]
    n_blocks = -(-(na + N_EXPERTS * (tm - 1)) // tm)
    asg_rows = jnp.zeros((n_blocks * tm,), jnp.int32).at[dest].set(order)
    blk_start = jnp.arange(n_blocks, dtype=jnp.int32) * tm
    block_e = jnp.clip(jnp.searchsorted(pends, blk_start, side='right'), 0, N_EXPERTS - 1).astype(jnp.int32)
    block_nv = jnp.clip(pstarts[block_e] + counts[block_e] - blk_start, 0, tm).astype(jnp.int32)
    return asg_rows, block_e, block_nv


def _combine_body(x_ref, slot_ref, gt_ref, gn_ref, y_ref, *, final):
    x = x_ref[...]
    gt = gt_ref[...]
    d = x.shape[-1]
    for j in range(TOP_K):
        x = x + gt[:, j:j + 1] * slot_ref[:, j * d:(j + 1) * d]
    y_ref[...] = _rmsnorm(x, gn_ref[...]) if final else x


def _combine(x2, slots, gates, gn, tm, final):
    n, d = x2.shape
    assert n % tm == 0
    return pl.pallas_call(
        functools.partial(_combine_body, final=final),
        grid=(n // tm,),
        in_specs=[pl.BlockSpec((tm, d), lambda i: (i, 0)), pl.BlockSpec((tm, TOP_K * d), lambda i: (i, 0)),
                  pl.BlockSpec((tm, LANES), lambda i: (i, 0)), pl.BlockSpec((1, d), lambda i: (0, 0))],
        out_specs=pl.BlockSpec((tm, d), lambda i: (i, 0)),
        out_shape=jax.ShapeDtypeStruct((n, d), F32),
        compiler_params=pltpu.CompilerParams(dimension_semantics=("parallel",), vmem_limit_bytes=VMEM_LIMIT),
        name="combine",
    )(x2, slots, gates, gn)


def _layer_params(l, norm_mix, w_in, mu_shift, w0, w2_decay, a0, a2_iclr, g2_gate, k_k, k_a, r_k, lnx_w, lnx_b,
                  conv_w, a_log, dt_bias, gdn_norm_w, w_out, norm_cross, norm_mem, wq_x, wk_x, wv_x, wo_x,
                  norm_ffn, router_w, router_b, w1_e, b1_e, w2_e, b2_e):
    row = lambda z: z.reshape(1, -1).astype(F32)
    lane_pad = lambda z, at: jnp.zeros((1, LANES), F32).at[0, at:at + z.shape[0]].set(z)
    return {
        'norm_mix': row(norm_mix[l]),
        'w_in_r': w_in[l][:, :R_PROJ].astype(BF16),
        'w_in_g': jnp.pad(w_in[l][:, R_PROJ:], ((0, 0), (0, G_PROJ_PAD - G_PROJ))).astype(BF16),
        'mu': row(mu_shift[l]), 'w0': row(w0[l]), 'a0': row(a0[l]),
        'w2': jnp.pad(w2_decay[l], ((0, LORA_A), (0, 0))).astype(BF16),
        'a2': jnp.pad(a2_iclr[l], ((LORA_W, 0), (0, 0))).astype(BF16),
        'g2': g2_gate[l].astype(BF16),
        'k_k': row(k_k[l]), 'k_a': row(k_a[l]), 'r_k': row(r_k[l]), 'lnx_w': row(lnx_w[l]), 'lnx_b': row(lnx_b[l]),
        'conv_w': conv_w[l].astype(F32),
        'a_log': lane_pad(a_log[l], H_G), 'dt_bias': lane_pad(dt_bias[l], H_G),
        'gdn_norm_w': row(gdn_norm_w[l]),
        'w_out_r': w_out[l][:W_RWKV].astype(BF16), 'w_out_g': w_out[l][W_RWKV:].astype(BF16),
        'norm_cross': row(norm_cross[l]), 'norm_mem': row(norm_mem[l]),
        'wq': wq_x[l].astype(BF16), 'wk': wk_x[l].astype(BF16), 'wv': wv_x[l].astype(BF16),
        'wo': wo_x[l].astype(BF16),
        'norm_ffn': row(norm_ffn[l]),
        'router_w': jnp.pad(router_w[l].astype(F32), ((0, 0), (0, LANES - N_EXPERTS))),
        'router_b': jnp.full((1, LANES), NEG_BIG, F32).at[0, :N_EXPERTS].set(router_b[l].astype(F32)),
        'w1_e': w1_e[l], 'b1_e': b1_e[l][:, None, :], 'w2_e': w2_e[l], 'b2_e': b2_e[l][:, None, :],
    }


def _mix_and_attend(x, mk, mv, shift_prev, s_r, conv_buf, s_g, p, *, chunk, tm, bb, tq):
    b, t, d = x.shape
    assert t >= CONV_W - 1
    pr, pg = _norm_proj(x.reshape(b * t, d), p['norm_mix'], [p['w_in_r'], p['w_in_g']], [F32, F32], tm)
    pr = pr.reshape(b, t, R_PROJ)
    pg = pg.reshape(b, t, G_PROJ_PAD)
    shift_new = pr[:, t - 1]
    conv_new = pg[:, t - (CONV_W - 1):, :GDN_CONV_DIM]
    t_pad = -(-t // chunk) * chunk
    if t_pad != t:
        pr = jnp.pad(pr, ((0, 0), (0, t_pad - t), (0, 0)))
        pg = jnp.pad(pg, ((0, 0), (0, t_pad - t), (0, 0)))
    t_valid = chunk if t_pad == t else t
    y_r, s_r_new = _rwkv_mix(pr, shift_prev, s_r, p, chunk, t_valid)
    y_g, s_g_new = _gdn_mix(pg, conv_buf, s_g, p, chunk, t_valid)
    if t_pad != t:
        y_r, y_g = y_r[:, :t], y_g[:, :t]
    x1, q = _out_q(x.reshape(b * t, d), y_r.reshape(b * t, W_RWKV), y_g.reshape(b * t, W_GDN), p, tm)
    x2, h, top_i, gates = _attn_route(q.reshape(b, t, d), x1.reshape(b, t, d), mk, mv, p, bb, tq)
    return (x2.reshape(b * t, d), h.reshape(b * t, d), top_i.reshape(b * t, LANES), gates.reshape(b * t, LANES),
            shift_new, s_r_new, conv_new, s_g_new)


def kernel(x_prompt, x_sample, mem_prompt, state_rwkv, state_rwkv_shift, state_gdn, state_gdn_conv, cache_mem_k, cache_mem_v, norm_mix, w_in, mu_shift, w0, w2_decay, a0, a2_iclr, g2_gate, k_k, k_a, r_k, lnx_w, lnx_b, conv_w, a_log, dt_bias, gdn_norm_w, w_out, norm_cross, norm_mem, wq_x, wk_x, wv_x, wo_x, norm_ffn, router_w, router_b, w1_e, b1_e, w2_e, b2_e, final_norm):
    bp, tp, d = x_prompt.shape
    bs, ts, _ = x_sample.shape
    depth = w_in.shape[0]
    np_, ns = bp * tp, bs * ts
    xp, xs = x_prompt, x_sample
    outs = [[] for _ in range(10)]
    for l in range(depth):
        p = _layer_params(l, norm_mix, w_in, mu_shift, w0, w2_decay, a0, a2_iclr, g2_gate, k_k, k_a, r_k, lnx_w,
                          lnx_b, conv_w, a_log, dt_bias, gdn_norm_w, w_out, norm_cross, norm_mem, wq_x, wk_x, wv_x,
                          wo_x, norm_ffn, router_w, router_b, w1_e, b1_e, w2_e, b2_e)
        n_mem = mem_prompt.shape[1]
        mk, mv = _norm_proj(mem_prompt.reshape(bp * n_mem, d), p['norm_mem'], [p['wk'], p['wv']], [F32, F32], 256)
        mk, mv = mk.reshape(bp, n_mem, d), mv.reshape(bp, n_mem, d)
        res_p = _mix_and_attend(
            xp, mk, mv, jnp.zeros((bp, R_PROJ), F32), jnp.zeros((bp, H_R, N_R, N_R), F32),
            jnp.zeros((bp, CONV_W - 1, GDN_CONV_DIM), F32), jnp.zeros((bp, H_G, D_G, D_G), F32), p,
            chunk=MIX_CHUNK, tm=256, bb=1, tq=512)
        res_s = _mix_and_attend(
            xs, cache_mem_k[l].reshape(bs, n_mem, d), cache_mem_v[l].reshape(bs, n_mem, d), state_rwkv_shift[l],
            state_rwkv[l], state_gdn_conv[l], state_gdn[l], p, chunk=SUBLANES, tm=256, bb=8, tq=ts)
        x2 = jnp.concatenate([res_p[0], res_s[0]], axis=0)
        h = jnp.concatenate([res_p[1], res_s[1]], axis=0)
        top_i = jnp.concatenate([res_p[2], res_s[2]], axis=0)[:, :TOP_K]
        gates = jnp.concatenate([res_p[3], res_s[3]], axis=0)
        asg_rows, block_e, block_nv = _route_plan(top_i, MOE_ROWS)
        slots = _moe(h, asg_rows, block_e, block_nv, p, MOE_ROWS)
        gn = final_norm.reshape(1, d).astype(F32)
        x3 = _combine(x2, slots.reshape(np_ + ns, TOP_K * d), gates, gn, 256, final=(l == depth - 1))
        xp, xs = x3[:np_].reshape(bp, tp, d), x3[np_:].reshape(bs, ts, d)
        new = [res_p[5], res_p[4], res_p[7], res_p[6], mk.reshape(bp, n_mem, H_X, D_X),
               mv.reshape(bp, n_mem, H_X, D_X), res_s[5], res_s[4], res_s[7], res_s[6]]
        for acc, val in zip(outs, new):
            acc.append(val)
    return (xp, xs) + tuple(jnp.stack(o) for o in outs)
```

```python
import functools

import jax
import jax.numpy as jnp
from jax import lax
from jax.experimental import pallas as pl
from jax.experimental.pallas import tpu as pltpu

F32 = jnp.float32
BF16 = jnp.bfloat16
HIGHEST = lax.Precision.HIGHEST
DEFAULT = lax.Precision.DEFAULT

D_MODEL = 1024
W_RWKV = 512
N_R = 64
H_R = W_RWKV // N_R
LORA_W = 64
LORA_A = 64
LORA_G = 128
R_PROJ = 3 * W_RWKV + LORA_W + LORA_A + LORA_G
GN_EPS = 64e-5
W_GDN = 512
D_G = 128
H_G = W_GDN // D_G
GDN_CONV_DIM = 3 * W_GDN
CONV_W = 4
G_PROJ = GDN_CONV_DIM + W_GDN + 2 * H_G
LANES = 128
SUBLANES = 8
G_PROJ_PAD = GDN_CONV_DIM + W_GDN + LANES
MIX_CHUNK = 64
N_MEM = 256
H_X = 4
D_X = D_MODEL // H_X
N_EXPERTS = 32
TOP_K = 4
D_FF = D_MODEL
SWIGLU_LIMIT = 7.0
SWIGLU_ALPHA = 1.702
MOE_ROWS = 256
RMS_EPS = 1e-6
L2_EPS = 1e-6
NEG_BIG = -1e30
VMEM_LIMIT = 56 * 1024 * 1024


def _dot(a, b, precision=DEFAULT):
    return jnp.dot(a, b, preferred_element_type=F32, precision=precision)


def _dot_nt(a, b, precision=DEFAULT):
    return lax.dot_general(a, b, (((1,), (1,)), ((), ())), preferred_element_type=F32, precision=precision)


NN = (((1,), (0,)), ((), ()))
NT = (((1,), (1,)), ((), ()))
TN = (((0,), (0,)), ((), ()))
STATE_PASSES = 3


def _split2(x):
    hi = x.astype(BF16)
    return hi, (x - hi.astype(F32)).astype(BF16)


def _mm(a, b, dims, passes):
    dg = lambda x, y: lax.dot_general(x, y, dims, preferred_element_type=F32)
    if passes == 1:
        return dg(a.astype(BF16), b.astype(BF16))
    a_hi, a_lo = _split2(a)
    b_hi, b_lo = _split2(b)
    return dg(a_hi, b_hi) + (dg(a_hi, b_lo) + dg(a_lo, b_hi))


def _sel_mm(sel, x, dims):
    dg = lambda y: lax.dot_general(sel, y, dims, preferred_element_type=F32)
    hi = x.astype(BF16)
    r1 = x - hi.astype(F32)
    mid = r1.astype(BF16)
    lo = (r1 - mid.astype(F32)).astype(BF16)
    return dg(hi) + (dg(mid) + dg(lo))


def _sigmoid(x):
    return 1.0 / (1.0 + jnp.exp(-x))


def _softplus(x):
    return jnp.maximum(x, 0.0) + jnp.log(1.0 + jnp.exp(-jnp.abs(x)))


def _rmsnorm(x, g):
    return x * lax.rsqrt(jnp.mean(x * x, axis=-1, keepdims=True) + RMS_EPS) * g


def _tri_masks(c):
    row = lax.broadcasted_iota(jnp.int32, (c, c), 0)
    col = lax.broadcasted_iota(jnp.int32, (c, c), 1)
    return col <= row, col < row, (col == row).astype(F32)


def _unit_lower_inverses(ms, eye, c):
    ts = [eye + m for m in ms]
    ps = list(ms)
    covered = 2
    while covered < c:
        ps = [_mm(p, p, NN, 1) for p in ps]
        ts = [t + _mm(t, p, NN, 1) for t, p in zip(ts, ps)]
        covered *= 2
    return ts


def _norm_proj_body(x_ref, g_ref, *refs, n_out):
    w_refs, o_refs = refs[:n_out], refs[n_out:]
    hb = _rmsnorm(x_ref[...], g_ref[...]).astype(BF16)
    for w_ref, o_ref in zip(w_refs, o_refs):
        o_ref[...] = _dot(hb, w_ref[...]).astype(o_ref.dtype)


def _norm_proj(x, g, ws, out_dtypes, tm):
    n, d = x.shape
    assert n % tm == 0
    in_specs = [pl.BlockSpec((tm, d), lambda i: (i, 0)), pl.BlockSpec((1, d), lambda i: (0, 0))]
    in_specs += [pl.BlockSpec(w.shape, lambda i: (0, 0)) for w in ws]
    return pl.pallas_call(
        functools.partial(_norm_proj_body, n_out=len(ws)),
        grid=(n // tm,),
        in_specs=in_specs,
        out_specs=[pl.BlockSpec((tm, w.shape[1]), lambda i: (i, 0)) for w in ws],
        out_shape=[jax.ShapeDtypeStruct((n, w.shape[1]), dt) for w, dt in zip(ws, out_dtypes)],
        compiler_params=pltpu.CompilerParams(dimension_semantics=("parallel",), vmem_limit_bytes=VMEM_LIMIT),
        name="norm_proj",
    )(x, g, *ws)


def _rwkv_body(pr_ref, shift_ref, s0_ref, mu_ref, w0_ref, w2_ref, a0_ref, a2_ref, g2_ref, kk_ref, ka_ref,
               rk_ref, lnw_ref, lnb_ref, y_ref, sout_ref, s_scr, prev_scr, *, chunk, t_valid, n_chunks):
    c = pl.program_id(1)

    @pl.when(c == 0)
    def _init():
        s_scr[...] = s0_ref[0]
        prev_scr[...] = shift_ref[0]

    pr = pr_ref[0]
    rows = lax.broadcasted_iota(jnp.int32, (chunk, 1), 0)
    prev = jnp.where(rows == 0, prev_scr[...], pltpu.roll(pr, 1, 0))
    prev_scr[...] = pr[chunk - 1:chunk, :]
    xm = pr + (prev - pr) * mu_ref[...]
    r = xm[:, :W_RWKV]
    k = xm[:, W_RWKV:2 * W_RWKV]
    v = xm[:, 2 * W_RWKV:3 * W_RWKV]
    lo = xm[:, 3 * W_RWKV:3 * W_RWKV + LORA_W + LORA_A]
    g_lo = xm[:, 3 * W_RWKV + LORA_W + LORA_A:]
    logw = -_softplus(-(w0_ref[...] + _dot(jnp.tanh(lo).astype(BF16), w2_ref[...]))) - 0.5
    wl = -jnp.exp(logw)
    a = _sigmoid(a0_ref[...] + _dot(lo.astype(BF16), a2_ref[...]))
    g = _dot(_sigmoid(g_lo).astype(BF16), g2_ref[...])
    kkv = k * kk_ref[...]
    k = k * (1.0 + (a - 1.0) * ka_ref[...])
    if t_valid < chunk:
        valid = rows < t_valid
        wl = jnp.where(valid, wl, 0.0)
        kkv = jnp.where(valid, kkv, 0.0)
        k = jnp.where(valid, k, 0.0)
        v = jnp.where(valid, v, 0.0)

    incl, strict, eye = _tri_masks(chunk)
    cum = _sel_mm(incl.astype(BF16), wl, NN)
    cum_last = cum[chunk - 1:chunk, :]
    w_incl = jnp.exp(cum)
    w_prev = jnp.exp(cum - wl)
    w_inv = jnp.exp(-cum)
    w_tail = jnp.exp(cum_last - cum)
    w_last = jnp.exp(cum_last)

    heads = range(H_R)
    sls = [slice(h * N_R, (h + 1) * N_R) for h in heads]
    kks = [kkv[:, sl] for sl in sls]
    kks = [x * lax.rsqrt(jnp.sum(x * x, axis=-1, keepdims=True) + L2_EPS) for x in kks]
    a_hat = [-(kks[h] * w_prev[:, sls[h]]) for h in heads]
    kka = [kks[h] * a[:, sls[h]] for h in heads]
    b_hat = [kka[h] * w_inv[:, sls[h]] for h in heads]
    k_hat = [k[:, sls[h]] * w_inv[:, sls[h]] for h in heads]
    r_hat = [r[:, sls[h]] * w_incl[:, sls[h]] for h in heads]
    m_ab = [jnp.where(strict, _mm(a_hat[h], b_hat[h], NT, 1), 0.0) for h in heads]
    m_ak = [jnp.where(strict, _mm(a_hat[h], k_hat[h], NT, 1), 0.0) for h in heads]
    a_rb = [jnp.where(incl, _mm(r_hat[h], b_hat[h], NT, 1), 0.0) for h in heads]
    a_rk = [jnp.where(incl, _mm(r_hat[h], k_hat[h], NT, 1), 0.0) for h in heads]
    t_inv = _unit_lower_inverses(m_ab, eye, chunk)
    w_hat = [_mm(t_inv[h], a_hat[h], NN, 1) for h in heads]
    mv = [_mm(m_ak[h], v[:, sls[h]], NN, 1) for h in heads]
    u = [_mm(t_inv[h], mv[h], NN, 1) for h in heads]
    ss = [s_scr[h] for h in heads]
    p = [_mm(w_hat[h], ss[h], NT, STATE_PASSES) + u[h] for h in heads]
    ys = [_mm(r_hat[h], ss[h], NT, 1) + _mm(a_rb[h], p[h], NN, 1) + _mm(a_rk[h], v[:, sls[h]], NN, 1)
          for h in heads]
    for h in heads:
        sl = sls[h]
        s_scr[h] = (ss[h] * w_last[:, sl] + _mm(p[h], kka[h] * w_tail[:, sl], TN, STATE_PASSES)
                    + _mm(v[:, sl], k[:, sl] * w_tail[:, sl], TN, STATE_PASSES))
    for h in heads:
        sl = sls[h]
        y = ys[h]
        mean = jnp.mean(y, axis=-1, keepdims=True)
        yc = y - mean
        var = jnp.mean(yc * yc, axis=-1, keepdims=True)
        yn = yc * lax.rsqrt(var + GN_EPS) * lnw_ref[:, sl] + lnb_ref[:, sl]
        bonus = jnp.sum(r[:, sl] * k[:, sl] * rk_ref[:, sl], axis=-1, keepdims=True) * v[:, sl]
        y_ref[0, :, sl] = ((yn + bonus) * g[:, sl]).astype(y_ref.dtype)

    @pl.when(c == n_chunks - 1)
    def _fin():
        sout_ref[0] = s_scr[...]


def _rwkv_mix(pr, shift_prev, s0, p, chunk, t_valid):
    b, t, _ = pr.shape
    n_chunks = t // chunk
    const = lambda shape: pl.BlockSpec(shape, lambda i, j: (0,) * len(shape))
    return pl.pallas_call(
        functools.partial(_rwkv_body, chunk=chunk, t_valid=t_valid, n_chunks=n_chunks),
        grid=(b, n_chunks),
        in_specs=[
            pl.BlockSpec((1, chunk, R_PROJ), lambda i, j: (i, j, 0)),
            pl.BlockSpec((1, 1, R_PROJ), lambda i, j: (i, 0, 0)),
            pl.BlockSpec((1, H_R, N_R, N_R), lambda i, j: (i, 0, 0, 0)),
            const((1, R_PROJ)), const((1, W_RWKV)), const((LANES, W_RWKV)), const((1, W_RWKV)),
            const((LANES, W_RWKV)), const((LORA_G, W_RWKV)), const((1, W_RWKV)), const((1, W_RWKV)),
            const((1, W_RWKV)), const((1, W_RWKV)), const((1, W_RWKV)),
        ],
        out_specs=[
            pl.BlockSpec((1, chunk, W_RWKV), lambda i, j: (i, j, 0)),
            pl.BlockSpec((1, H_R, N_R, N_R), lambda i, j: (i, 0, 0, 0)),
        ],
        out_shape=[jax.ShapeDtypeStruct((b, t, W_RWKV), BF16), jax.ShapeDtypeStruct((b, H_R, N_R, N_R), F32)],
        scratch_shapes=[pltpu.VMEM((H_R, N_R, N_R), F32), pltpu.VMEM((1, R_PROJ), F32)],
        compiler_params=pltpu.CompilerParams(dimension_semantics=("arbitrary", "arbitrary")),
        name="rwkv_mix",
    )(pr, shift_prev[:, None, :], s0, p['mu'], p['w0'], p['w2'], p['a0'], p['a2'], p['g2'], p['k_k'], p['k_a'],
      p['r_k'], p['lnx_w'], p['lnx_b'])


def _gdn_body(pg_ref, cbuf_ref, s0_ref, cw_ref, alog_ref, dtb_ref, nw_ref, y_ref, sout_ref, s_scr, xp_scr,
              *, chunk, t_valid, n_chunks):
    c = pl.program_id(1)

    @pl.when(c == 0)
    def _init():
        s_scr[...] = s0_ref[0]
        xp_scr[0:SUBLANES, :] = cbuf_ref[0]

    xp_scr[SUBLANES:SUBLANES + chunk, :] = pg_ref[0, :, :GDN_CONV_DIM]
    base = SUBLANES - (CONV_W - 1)
    conv = xp_scr[base:base + chunk, :] * cw_ref[0:1, :]
    for j in range(1, CONV_W):
        conv = conv + xp_scr[base + j:base + j + chunk, :] * cw_ref[j:j + 1, :]
    xp_scr[0:SUBLANES, :] = xp_scr[chunk:chunk + SUBLANES, :]
    qkv = conv * _sigmoid(conv)
    z = pg_ref[0, :, GDN_CONV_DIM:GDN_CONV_DIM + W_GDN]
    ba = pg_ref[0, :, GDN_CONV_DIM + W_GDN:]
    beta_blk = _sigmoid(ba)
    g_blk = -jnp.exp(alog_ref[...]) * _softplus(ba + dtb_ref[...])
    if t_valid < chunk:
        valid = lax.broadcasted_iota(jnp.int32, (chunk, 1), 0) < t_valid
        beta_blk = jnp.where(valid, beta_blk, 0.0)
        g_blk = jnp.where(valid, g_blk, 0.0)

    incl, strict, eye = _tri_masks(chunk)
    gc_blk = _sel_mm(incl.astype(BF16), g_blk, NN)
    lane = lax.broadcasted_iota(jnp.int32, (chunk, LANES), 1)

    heads = range(H_G)
    sls = [slice(h * D_G, (h + 1) * D_G) for h in heads]
    qs = [qkv[:, sl] for sl in sls]
    qs = [x * lax.rsqrt(jnp.sum(x * x, axis=-1, keepdims=True) + L2_EPS) * (D_G ** -0.5) for x in qs]
    ks = [qkv[:, W_GDN + h * D_G:W_GDN + (h + 1) * D_G] for h in heads]
    ks = [x * lax.rsqrt(jnp.sum(x * x, axis=-1, keepdims=True) + L2_EPS) for x in ks]
    vs = [qkv[:, 2 * W_GDN + h * D_G:2 * W_GDN + (h + 1) * D_G] for h in heads]
    betas = [beta_blk[:, h:h + 1] for h in heads]
    gcols = [gc_blk[:, H_G + h:H_G + h + 1] for h in heads]
    grows = [_sel_mm((lane == H_G + h).astype(BF16), gc_blk, NT) for h in heads]
    g_last = [gcols[h][chunk - 1:chunk, :] for h in heads]
    decay = [jnp.where(incl, jnp.exp(jnp.where(incl, gcols[h] - grows[h], 0.0)), 0.0) for h in heads]
    k_beta = [ks[h] * betas[h] for h in heads]
    lmat = [jnp.where(strict, _mm(k_beta[h], ks[h], NT, 1) * decay[h], 0.0) for h in heads]
    attn = [jnp.where(incl, _mm(qs[h], ks[h], NT, 1) * decay[h], 0.0) for h in heads]
    t_inv = _unit_lower_inverses([-x for x in lmat], eye, chunk)
    e_gc = [jnp.exp(gcols[h]) for h in heads]
    u = [_mm(t_inv[h], vs[h] * betas[h], NN, 1) for h in heads]
    w = [_mm(t_inv[h], k_beta[h] * e_gc[h], NN, 1) for h in heads]
    ss = [s_scr[h] for h in heads]
    v_new = [u[h] - _mm(w[h], ss[h], NN, STATE_PASSES) for h in heads]
    os_ = [_mm(qs[h] * e_gc[h], ss[h], NN, 1) + _mm(attn[h], v_new[h], NN, 1) for h in heads]
    for h in heads:
        s_scr[h] = (ss[h] * jnp.exp(g_last[h])
                    + _mm(ks[h] * jnp.exp(g_last[h] - gcols[h]), v_new[h], TN, STATE_PASSES))
    for h in heads:
        o = os_[h]
        o = o * lax.rsqrt(jnp.mean(o * o, axis=-1, keepdims=True) + RMS_EPS) * nw_ref[...]
        z_h = z[:, sls[h]]
        y_ref[0, :, sls[h]] = (o * (z_h * _sigmoid(z_h))).astype(y_ref.dtype)

    @pl.when(c == n_chunks - 1)
    def _fin():
        sout_ref[0] = s_scr[...]


def _gdn_mix(pg, conv_buf, s0, p, chunk, t_valid):
    b, t, _ = pg.shape
    n_chunks = t // chunk
    cbuf = jnp.pad(conv_buf, ((0, 0), (SUBLANES - (CONV_W - 1), 0), (0, 0)))
    const = lambda shape: pl.BlockSpec(shape, lambda i, j: (0,) * len(shape))
    return pl.pallas_call(
        functools.partial(_gdn_body, chunk=chunk, t_valid=t_valid, n_chunks=n_chunks),
        grid=(b, n_chunks),
        in_specs=[
            pl.BlockSpec((1, chunk, G_PROJ_PAD), lambda i, j: (i, j, 0)),
            pl.BlockSpec((1, SUBLANES, GDN_CONV_DIM), lambda i, j: (i, 0, 0)),
            pl.BlockSpec((1, H_G, D_G, D_G), lambda i, j: (i, 0, 0, 0)),
            const((CONV_W, GDN_CONV_DIM)), const((1, LANES)), const((1, LANES)), const((1, D_G)),
        ],
        out_specs=[
            pl.BlockSpec((1, chunk, W_GDN), lambda i, j: (i, j, 0)),
            pl.BlockSpec((1, H_G, D_G, D_G), lambda i, j: (i, 0, 0, 0)),
        ],
        out_shape=[jax.ShapeDtypeStruct((b, t, W_GDN), BF16), jax.ShapeDtypeStruct((b, H_G, D_G, D_G), F32)],
        scratch_shapes=[pltpu.VMEM((H_G, D_G, D_G), F32), pltpu.VMEM((SUBLANES + chunk, GDN_CONV_DIM), F32)],
        compiler_params=pltpu.CompilerParams(dimension_semantics=("arbitrary", "arbitrary")),
        name="gdn_mix",
    )(pg, cbuf, s0, p['conv_w'], p['a_log'], p['dt_bias'], p['gdn_norm_w'])


def _out_q_body(x_ref, yr_ref, yg_ref, wor_ref, wog_ref, gn_ref, wq_ref, x1_ref, q_ref):
    x1 = x_ref[...] + _dot(yr_ref[...], wor_ref[...]) + _dot(yg_ref[...], wog_ref[...])
    x1_ref[...] = x1
    q_ref[...] = _dot(_rmsnorm(x1, gn_ref[...]).astype(BF16), wq_ref[...]).astype(q_ref.dtype)


def _out_q(x, yr, yg, p, tm):
    n, d = x.shape
    assert n % tm == 0
    row = lambda w: pl.BlockSpec((tm, w), lambda i: (i, 0))
    const = lambda shape: pl.BlockSpec(shape, lambda i: (0, 0))
    return pl.pallas_call(
        _out_q_body,
        grid=(n // tm,),
        in_specs=[row(d), row(W_RWKV), row(W_GDN), const((W_RWKV, d)), const((W_GDN, d)), const((1, d)),
                  const((d, d))],
        out_specs=[row(d), row(d)],
        out_shape=[jax.ShapeDtypeStruct((n, d), F32), jax.ShapeDtypeStruct((n, d), BF16)],
        compiler_params=pltpu.CompilerParams(dimension_semantics=("parallel",), vmem_limit_bytes=VMEM_LIMIT),
        name="out_q",
    )(x, yr, yg, p['w_out_r'], p['w_out_g'], p['norm_cross'], p['wq'])


def _attn_body(q_ref, x_ref, mk_ref, mv_ref, wo_ref, gn_ref, rw_ref, rb_ref, x2_ref, h_ref, ti_ref, gt_ref,
               *, bb, tq, mem4d):
    for i in range(bb):
        q = q_ref[i]
        heads = []
        for hh in range(H_X):
            sl = slice(hh * D_X, (hh + 1) * D_X)
            mk_h = mk_ref[i, :, hh, :] if mem4d else mk_ref[i, :, sl]
            mv_h = mv_ref[i, :, hh, :] if mem4d else mv_ref[i, :, sl]
            s = _dot_nt(q[:, sl], mk_h.astype(BF16)) * (D_X ** -0.5)
            e = jnp.exp(s - jnp.max(s, axis=-1, keepdims=True))
            prob = e / jnp.sum(e, axis=-1, keepdims=True)
            heads.append(_dot(prob.astype(BF16), mv_h.astype(BF16)))
        o = jnp.concatenate(heads, axis=-1).astype(BF16)
        x2 = x_ref[i] + _dot(o, wo_ref[...])
        x2_ref[i] = x2
        h = _rmsnorm(x2, gn_ref[...])
        h_ref[i * tq:(i + 1) * tq, :] = h
        logits = _dot(h, rw_ref[...], HIGHEST) + rb_ref[...]
        lane = lax.broadcasted_iota(jnp.int32, logits.shape, 1)
        vals, idxs = [], []
        for _ in range(TOP_K):
            m = jnp.max(logits, axis=-1, keepdims=True)
            first = jnp.min(jnp.where(logits == m, lane, LANES), axis=-1, keepdims=True)
            vals.append(m)
            idxs.append(first)
            logits = jnp.where(lane == first, -jnp.inf, logits)
        es = [jnp.exp(vv - vals[0]) for vv in vals]
        den = es[0] + es[1] + es[2] + es[3]
        ti = jnp.zeros(lane.shape, jnp.int32)
        gt = jnp.zeros(lane.shape, F32)
        for j in range(TOP_K):
            ti = jnp.where(lane == j, idxs[j], ti)
            gt = jnp.where(lane == j, es[j] / den, gt)
        ti_ref[i] = ti
        gt_ref[i] = gt


def _attn_route(q, x1, mk, mv, p, bb, tq):
    b, t, d = x1.shape
    rows = bb * tq
    assert b % bb == 0 and t % tq == 0 and (bb == 1 or tq == t)
    n_tq = t // tq
    blk = lambda w: pl.BlockSpec((bb, tq, w), lambda i, j: (i, j, 0))
    mem4d = mk.ndim == 4
    mem = (pl.BlockSpec((bb,) + mk.shape[1:], lambda i, j: (i, 0, 0, 0)) if mem4d
           else pl.BlockSpec((bb, N_MEM, d), lambda i, j: (i, 0, 0)))
    const = lambda shape: pl.BlockSpec(shape, lambda i, j: (0, 0))
    return pl.pallas_call(
        functools.partial(_attn_body, bb=bb, tq=tq, mem4d=mem4d),
        grid=(b // bb, n_tq),
        in_specs=[blk(d), blk(d), mem, mem, const((d, d)), const((1, d)), const((d, LANES)), const((1, LANES))],
        out_specs=[blk(d), pl.BlockSpec((rows, d), lambda i, j: (i * n_tq + j, 0)), blk(LANES), blk(LANES)],
        out_shape=[jax.ShapeDtypeStruct((b, t, d), F32), jax.ShapeDtypeStruct((b * t, d), F32),
                   jax.ShapeDtypeStruct((b, t, LANES), jnp.int32), jax.ShapeDtypeStruct((b, t, LANES), F32)],
        compiler_params=pltpu.CompilerParams(dimension_semantics=("parallel", "parallel"),
                                             vmem_limit_bytes=VMEM_LIMIT),
        name="attn_route",
    )(q, x1, mk, mv, p['wo'], p['norm_ffn'], p['router_w'], p['router_b'])


def _moe_body(be_ref, gnext_ref, sprev_ref, h_hbm, w1_ref, b1_ref, w2_ref, b2_ref, out_hbm,
              xb0, xb1, yb0, yb1, w1b, w2b, gsem, ssem, *, tm, n_steps):
    k = pl.program_id(0)
    xbufs, ybufs = (xb0, xb1), (yb0, yb1)

    def gather_all(slot):
        return pltpu.make_async_copy(h_hbm.at[pl.ds(0, tm)], xbufs[slot], gsem.at[slot])

    def scatter_all(slot):
        return pltpu.make_async_copy(ybufs[slot], out_hbm.at[pl.ds(0, tm)], ssem.at[slot])

    @pl.when(k == 0)
    def _prologue():
        yb0[...] = jnp.zeros(yb0.shape, F32)
        yb1[...] = jnp.zeros(yb1.shape, F32)
        for r in range(tm):
            pltpu.make_async_copy(h_hbm.at[pl.ds(0, 1)], xb0.at[pl.ds(r, 1)], gsem.at[0]).start()

    @pl.when(jnp.logical_or(k == 0, be_ref[k] != be_ref[jnp.maximum(k - 1, 0)]))
    def _cast():
        w1b[...] = w1_ref[0].astype(BF16)
        w2b[...] = w2_ref[0].astype(BF16)

    def phase(cur):
        nxt = 1 - cur
        gather_all(cur).wait()

        @pl.when(k >= 1)
        def _():
            scatter_all(cur).wait()

        for r in range(tm):
            pltpu.make_async_copy(h_hbm.at[pl.ds(gnext_ref[0, 0, r], 1)], xbufs[nxt].at[pl.ds(r, 1)],
                                  gsem.at[nxt]).start()
            pltpu.make_async_copy(ybufs[nxt].at[pl.ds(r, 1)], out_hbm.at[pl.ds(sprev_ref[0, 0, r], 1)],
                                  ssem.at[nxt]).start()
        hc = _dot(xbufs[cur][...].astype(BF16), w1b[...]) + b1_ref[0]
        hg = jnp.minimum(hc[:, :D_FF], SWIGLU_LIMIT)
        hl = jnp.clip(hc[:, D_FF:], -SWIGLU_LIMIT, SWIGLU_LIMIT)
        act = hg * _sigmoid(SWIGLU_ALPHA * hg) * (hl + 1.0)
        ybufs[cur][...] = _dot(act.astype(BF16), w2b[...]) + b2_ref[0]

        @pl.when(k == n_steps - 1)
        def _epilogue():
            gather_all(nxt).wait()
            scatter_all(nxt).wait()

    for parity in range(2):
        pl.when(lax.rem(k, 2) == parity)(functools.partial(phase, parity))


def _moe(h, gtok, sdst, block_e, p, tm):
    n, d = h.shape
    n_steps = block_e.shape[0]
    grid_spec = pltpu.PrefetchScalarGridSpec(
        num_scalar_prefetch=1,
        grid=(n_steps,),
        in_specs=[
            pl.BlockSpec((1, 1, tm), lambda k, be: (jnp.minimum(k + 1, n_steps - 1), 0, 0),
                         memory_space=pltpu.SMEM),
            pl.BlockSpec((1, 1, tm), lambda k, be: (jnp.maximum(k - 1, 0), 0, 0), memory_space=pltpu.SMEM),
            pl.BlockSpec(memory_space=pl.ANY),
            pl.BlockSpec((1, d, 2 * D_FF), lambda k, be: (be[k], 0, 0)),
            pl.BlockSpec((1, 1, 2 * D_FF), lambda k, be: (be[k], 0, 0)),
            pl.BlockSpec((1, D_FF, d), lambda k, be: (be[k], 0, 0)),
            pl.BlockSpec((1, 1, d), lambda k, be: (be[k], 0, 0)),
        ],
        out_specs=pl.BlockSpec(memory_space=pl.ANY),
        scratch_shapes=[
            pltpu.VMEM((tm, d), F32), pltpu.VMEM((tm, d), F32), pltpu.VMEM((tm, d), F32), pltpu.VMEM((tm, d), F32),
            pltpu.VMEM((d, 2 * D_FF), BF16), pltpu.VMEM((D_FF, d), BF16),
            pltpu.SemaphoreType.DMA((2,)), pltpu.SemaphoreType.DMA((2,)),
        ],
    )
    return pl.pallas_call(
        functools.partial(_moe_body, tm=tm, n_steps=n_steps),
        grid_spec=grid_spec,
        out_shape=jax.ShapeDtypeStruct((n * TOP_K + tm, d), F32),
        compiler_params=pltpu.CompilerParams(dimension_semantics=("arbitrary",), vmem_limit_bytes=VMEM_LIMIT),
        name="moe_experts",
    )(block_e, gtok.reshape(n_steps, 1, tm), sdst.reshape(n_steps, 1, tm), h, p['w1_e'], p['b1_e'], p['w2_e'],
      p['b2_e'])


def _route_plan(top_i, tm):
    n = top_i.shape[0]
    na = n * TOP_K
    flat_e = top_i.reshape(na)
    order = jnp.argsort(flat_e).astype(jnp.int32)
    counts = jnp.sum((flat_e[:, None] == jnp.arange(N_EXPERTS, dtype=jnp.int32)[None, :]).astype(jnp.int32), axis=0)
    padded = (counts + tm - 1) // tm * tm
    starts = jnp.cumsum(counts) - counts
    pends = jnp.cumsum(padded)
    pstarts = pends - padded
    n_steps = -(-(na + N_EXPERTS * (tm - 1)) // tm) + 2
    blk_start = (jnp.arange(n_steps, dtype=jnp.int32) - 1) * tm
    block_e = jnp.sum((blk_start[:, None] >= pends[None, :]).astype(jnp.int32), axis=1)
    block_e = jnp.minimum(block_e, N_EXPERTS - 1)
    block_e = block_e.at[0].set(block_e[1])
    lane = jnp.arange(tm, dtype=jnp.int32)[None, :]
    local = blk_start[:, None] + lane - pstarts[block_e][:, None]
    valid = (local >= 0) & (local < counts[block_e][:, None]) & (blk_start[:, None] >= 0)
    asg = order[jnp.clip(starts[block_e][:, None] + local, 0, na - 1)]
    gtok = jnp.where(valid, asg // TOP_K, 0)
    sdst = jnp.where(valid, (asg % TOP_K) * n + asg // TOP_K, TOP_K * n + lane)
    return gtok, sdst, block_e


def _combine_body(x_ref, gt_ref, gn_ref, *refs, final):
    slot_refs, y_ref = refs[:TOP_K], refs[TOP_K]
    x = x_ref[...]
    gt = gt_ref[...]
    for j in range(TOP_K):
        x = x + gt[:, j:j + 1] * slot_refs[j][...]
    y_ref[...] = _rmsnorm(x, gn_ref[...]) if final else x


def _combine(x2, gates, slots, gn, tm, n_all, row0, final):
    n, d = x2.shape
    assert n % tm == 0 and row0 % tm == 0 and n_all % tm == 0
    slot_spec = lambda j: pl.BlockSpec((tm, d), lambda i: ((j * n_all + row0) // tm + i, 0))
    return pl.pallas_call(
        functools.partial(_combine_body, final=final),
        grid=(n // tm,),
        in_specs=[pl.BlockSpec((tm, d), lambda i: (i, 0)), pl.BlockSpec((tm, LANES), lambda i: (i, 0)),
                  pl.BlockSpec((1, d), lambda i: (0, 0))] + [slot_spec(j) for j in range(TOP_K)],
        out_specs=pl.BlockSpec((tm, d), lambda i: (i, 0)),
        out_shape=jax.ShapeDtypeStruct((n, d), F32),
        compiler_params=pltpu.CompilerParams(dimension_semantics=("parallel",), vmem_limit_bytes=VMEM_LIMIT),
        name="combine",
    )(x2, gates, gn, *([slots] * TOP_K))


def _layer_params(l, norm_mix, w_in, mu_shift, w0, w2_decay, a0, a2_iclr, g2_gate, k_k, k_a, r_k, lnx_w, lnx_b,
                  conv_w, a_log, dt_bias, gdn_norm_w, w_out, norm_cross, norm_mem, wq_x, wk_x, wv_x, wo_x,
                  norm_ffn, router_w, router_b, w1_e, b1_e, w2_e, b2_e):
    row = lambda z: z.reshape(1, -1).astype(F32)
    lane_pad = lambda z, at: jnp.zeros((1, LANES), F32).at[0, at:at + z.shape[0]].set(z)
    return {
        'norm_mix': row(norm_mix[l]),
        'w_in_r': w_in[l][:, :R_PROJ].astype(BF16),
        'w_in_g': jnp.pad(w_in[l][:, R_PROJ:], ((0, 0), (0, G_PROJ_PAD - G_PROJ))).astype(BF16),
        'mu': row(mu_shift[l]), 'w0': row(w0[l]), 'a0': row(a0[l]),
        'w2': jnp.pad(w2_decay[l], ((0, LORA_A), (0, 0))).astype(BF16),
        'a2': jnp.pad(a2_iclr[l], ((LORA_W, 0), (0, 0))).astype(BF16),
        'g2': g2_gate[l].astype(BF16),
        'k_k': row(k_k[l]), 'k_a': row(k_a[l]), 'r_k': row(r_k[l]), 'lnx_w': row(lnx_w[l]), 'lnx_b': row(lnx_b[l]),
        'conv_w': conv_w[l].astype(F32),
        'a_log': lane_pad(a_log[l], H_G), 'dt_bias': lane_pad(dt_bias[l], H_G),
        'gdn_norm_w': row(gdn_norm_w[l]),
        'w_out_r': w_out[l][:W_RWKV].astype(BF16), 'w_out_g': w_out[l][W_RWKV:].astype(BF16),
        'norm_cross': row(norm_cross[l]), 'norm_mem': row(norm_mem[l]),
        'wq': wq_x[l].astype(BF16), 'wk': wk_x[l].astype(BF16), 'wv': wv_x[l].astype(BF16),
        'wo': wo_x[l].astype(BF16),
        'norm_ffn': row(norm_ffn[l]),
        'router_w': jnp.pad(router_w[l].astype(F32), ((0, 0), (0, LANES - N_EXPERTS))),
        'router_b': jnp.full((1, LANES), NEG_BIG, F32).at[0, :N_EXPERTS].set(router_b[l].astype(F32)),
        'w1_e': w1_e[l], 'b1_e': b1_e[l][:, None, :], 'w2_e': w2_e[l], 'b2_e': b2_e[l][:, None, :],
    }


def _mix_and_attend(x, mk, mv, shift_prev, s_r, conv_buf, s_g, p, *, chunk, tm, bb, tq):
    b, t, d = x.shape
    assert t >= CONV_W - 1
    pr, pg = _norm_proj(x.reshape(b * t, d), p['norm_mix'], [p['w_in_r'], p['w_in_g']], [F32, F32], tm)
    pr = pr.reshape(b, t, R_PROJ)
    pg = pg.reshape(b, t, G_PROJ_PAD)
    shift_new = pr[:, t - 1]
    conv_new = pg[:, t - (CONV_W - 1):, :GDN_CONV_DIM]
    t_pad = -(-t // chunk) * chunk
    if t_pad != t:
        pr = jnp.pad(pr, ((0, 0), (0, t_pad - t), (0, 0)))
        pg = jnp.pad(pg, ((0, 0), (0, t_pad - t), (0, 0)))
    t_valid = chunk if t_pad == t else t
    y_r, s_r_new = _rwkv_mix(pr, shift_prev, s_r, p, chunk, t_valid)
    y_g, s_g_new = _gdn_mix(pg, conv_buf, s_g, p, chunk, t_valid)
    if t_pad != t:
        y_r, y_g = y_r[:, :t], y_g[:, :t]
    x1, q = _out_q(x.reshape(b * t, d), y_r.reshape(b * t, W_RWKV), y_g.reshape(b * t, W_GDN), p, tm)
    x2, h, top_i, gates = _attn_route(q.reshape(b, t, d), x1.reshape(b, t, d), mk, mv, p, bb, tq)
    return (x2.reshape(b * t, d), h, top_i.reshape(b * t, LANES), gates.reshape(b * t, LANES),
            shift_new, s_r_new, conv_new, s_g_new)


def kernel(x_prompt, x_sample, mem_prompt, state_rwkv, state_rwkv_shift, state_gdn, state_gdn_conv, cache_mem_k, cache_mem_v, norm_mix, w_in, mu_shift, w0, w2_decay, a0, a2_iclr, g2_gate, k_k, k_a, r_k, lnx_w, lnx_b, conv_w, a_log, dt_bias, gdn_norm_w, w_out, norm_cross, norm_mem, wq_x, wk_x, wv_x, wo_x, norm_ffn, router_w, router_b, w1_e, b1_e, w2_e, b2_e, final_norm):
    bp, tp, d = x_prompt.shape
    bs, ts, _ = x_sample.shape
    depth = w_in.shape[0]
    np_, ns = bp * tp, bs * ts
    xp, xs = x_prompt, x_sample
    outs = [[] for _ in range(10)]
    for l in range(depth):
        p = _layer_params(l, norm_mix, w_in, mu_shift, w0, w2_decay, a0, a2_iclr, g2_gate, k_k, k_a, r_k, lnx_w,
                          lnx_b, conv_w, a_log, dt_bias, gdn_norm_w, w_out, norm_cross, norm_mem, wq_x, wk_x, wv_x,
                          wo_x, norm_ffn, router_w, router_b, w1_e, b1_e, w2_e, b2_e)
        n_mem = mem_prompt.shape[1]
        mk, mv = _norm_proj(mem_prompt.reshape(bp * n_mem, d), p['norm_mem'], [p['wk'], p['wv']], [F32, F32], 256)
        mk, mv = mk.reshape(bp, n_mem, d), mv.reshape(bp, n_mem, d)
        res_p = _mix_and_attend(
            xp, mk, mv, jnp.zeros((bp, R_PROJ), F32), jnp.zeros((bp, H_R, N_R, N_R), F32),
            jnp.zeros((bp, CONV_W - 1, GDN_CONV_DIM), F32), jnp.zeros((bp, H_G, D_G, D_G), F32), p,
            chunk=MIX_CHUNK, tm=256, bb=1, tq=512)
        res_s = _mix_and_attend(
            xs, cache_mem_k[l], cache_mem_v[l], state_rwkv_shift[l], state_rwkv[l], state_gdn_conv[l],
            state_gdn[l], p, chunk=SUBLANES, tm=256, bb=2, tq=ts)
        h = jnp.concatenate([res_p[1], res_s[1]], axis=0)
        top_i = jnp.concatenate([res_p[2], res_s[2]], axis=0)[:, :TOP_K]
        gtok, sdst, block_e = _route_plan(top_i, MOE_ROWS)
        slots = _moe(h, gtok, sdst, block_e, p, MOE_ROWS)
        gn = final_norm.reshape(1, d).astype(F32)
        last = l == depth - 1
        xp = _combine(res_p[0], res_p[3], slots, gn, 256, np_ + ns, 0, last).reshape(bp, tp, d)
        xs = _combine(res_s[0], res_s[3], slots, gn, 256, np_ + ns, np_, last).reshape(bs, ts, d)
        new = [res_p[5], res_p[4], res_p[7], res_p[6], mk.reshape(bp, n_mem, H_X, D_X),
               mv.reshape(bp, n_mem, H_X, D_X), res_s[5], res_s[4], res_s[7], res_s[6]]
        for acc, val in zip(outs, new):
            acc.append(val)
    return (xp, xs) + tuple(jnp.stack(o) for o in outs)
```

```python
import functools

import jax
import jax.numpy as jnp
from jax import lax
from jax.experimental import pallas as pl
from jax.experimental.pallas import tpu as pltpu

F32 = jnp.float32
BF16 = jnp.bfloat16
HIGHEST = lax.Precision.HIGHEST
DEFAULT = lax.Precision.DEFAULT

D_MODEL = 1024
W_RWKV = 512
N_R = 64
H_R = W_RWKV // N_R
LORA_W = 64
LORA_A = 64
LORA_G = 128
R_PROJ = 3 * W_RWKV + LORA_W + LORA_A + LORA_G
GN_EPS = 64e-5
W_GDN = 512
D_G = 128
H_G = W_GDN // D_G
GDN_CONV_DIM = 3 * W_GDN
CONV_W = 4
G_PROJ = GDN_CONV_DIM + W_GDN + 2 * H_G
LANES = 128
SUBLANES = 8
G_PROJ_PAD = GDN_CONV_DIM + W_GDN + LANES
MIX_CHUNK = 64
N_MEM = 256
H_X = 4
D_X = D_MODEL // H_X
N_EXPERTS = 32
TOP_K = 4
D_FF = D_MODEL
SWIGLU_LIMIT = 7.0
SWIGLU_ALPHA = 1.702
MOE_ROWS = 256
RMS_EPS = 1e-6
L2_EPS = 1e-6
NEG_BIG = -1e30
VMEM_LIMIT = 56 * 1024 * 1024


def _dot(a, b, precision=DEFAULT):
    return jnp.dot(a, b, preferred_element_type=F32, precision=precision)


def _dot_nt(a, b, precision=DEFAULT):
    return lax.dot_general(a, b, (((1,), (1,)), ((), ())), preferred_element_type=F32, precision=precision)


NN = (((1,), (0,)), ((), ()))
NT = (((1,), (1,)), ((), ()))
TN = (((0,), (0,)), ((), ()))
STATE_PASSES = 3


def _split2(x):
    hi = x.astype(BF16)
    return hi, (x - hi.astype(F32)).astype(BF16)


def _mm(a, b, dims, passes):
    dg = lambda x, y: lax.dot_general(x, y, dims, preferred_element_type=F32)
    if passes == 1:
        return dg(a.astype(BF16), b.astype(BF16))
    a_hi, a_lo = _split2(a)
    b_hi, b_lo = _split2(b)
    return dg(a_hi, b_hi) + (dg(a_hi, b_lo) + dg(a_lo, b_hi))


def _sel_mm(sel, x, dims):
    dg = lambda y: lax.dot_general(sel, y, dims, preferred_element_type=F32)
    hi = x.astype(BF16)
    r1 = x - hi.astype(F32)
    mid = r1.astype(BF16)
    lo = (r1 - mid.astype(F32)).astype(BF16)
    return dg(hi) + (dg(mid) + dg(lo))


def _sigmoid(x):
    return 1.0 / (1.0 + jnp.exp(-x))


def _softplus(x):
    return jnp.maximum(x, 0.0) + jnp.log(1.0 + jnp.exp(-jnp.abs(x)))


def _rmsnorm(x, g):
    return x * lax.rsqrt(jnp.mean(x * x, axis=-1, keepdims=True) + RMS_EPS) * g


def _tri_masks(c):
    row = lax.broadcasted_iota(jnp.int32, (c, c), 0)
    col = lax.broadcasted_iota(jnp.int32, (c, c), 1)
    return col <= row, col < row, (col == row).astype(F32)


def _unit_lower_inverses(ms, eye, c):
    ts = [eye + m for m in ms]
    ps = list(ms)
    covered = 2
    while covered < c:
        ps = [_mm(p, p, NN, 1) for p in ps]
        ts = [t + _mm(t, p, NN, 1) for t, p in zip(ts, ps)]
        covered *= 2
    return ts


def _norm_proj_body(x_ref, g_ref, *refs, n_out):
    w_refs, o_refs = refs[:n_out], refs[n_out:]
    hb = _rmsnorm(x_ref[...], g_ref[...]).astype(BF16)
    for w_ref, o_ref in zip(w_refs, o_refs):
        o_ref[...] = _dot(hb, w_ref[...]).astype(o_ref.dtype)


def _norm_proj(x, g, ws, out_dtypes, tm):
    n, d = x.shape
    assert n % tm == 0
    in_specs = [pl.BlockSpec((tm, d), lambda i: (i, 0)), pl.BlockSpec((1, d), lambda i: (0, 0))]
    in_specs += [pl.BlockSpec(w.shape, lambda i: (0, 0)) for w in ws]
    return pl.pallas_call(
        functools.partial(_norm_proj_body, n_out=len(ws)),
        grid=(n // tm,),
        in_specs=in_specs,
        out_specs=[pl.BlockSpec((tm, w.shape[1]), lambda i: (i, 0)) for w in ws],
        out_shape=[jax.ShapeDtypeStruct((n, w.shape[1]), dt) for w, dt in zip(ws, out_dtypes)],
        compiler_params=pltpu.CompilerParams(dimension_semantics=("parallel",), vmem_limit_bytes=VMEM_LIMIT),
        name="norm_proj",
    )(x, g, *ws)


def _rwkv_body(pr_ref, shift_ref, s0_ref, mu_ref, w0_ref, w2_ref, a0_ref, a2_ref, g2_ref, kk_ref, ka_ref,
               rk_ref, lnw_ref, lnb_ref, y_ref, sout_ref, s_scr, prev_scr, *, bb, chunk, t_valid, n_chunks):
    c = pl.program_id(1)

    @pl.when(c == 0)
    def _init():
        s_scr[...] = s0_ref[...]
        prev_scr[...] = shift_ref[...]

    incl, strict, eye = _tri_masks(chunk)
    incl_bf = incl.astype(BF16)
    rows = lax.broadcasted_iota(jnp.int32, (chunk, 1), 0)
    sls = [slice(h * N_R, (h + 1) * N_R) for h in range(H_R)]

    def per_batch(bi):
        pr = pr_ref[bi]
        prev = jnp.where(rows == 0, prev_scr[bi], pltpu.roll(pr, 1, 0))
        prev_scr[bi] = pr[chunk - 1:chunk, :]
        xm = pr + (prev - pr) * mu_ref[...]
        r = xm[:, :W_RWKV]
        k = xm[:, W_RWKV:2 * W_RWKV]
        v = xm[:, 2 * W_RWKV:3 * W_RWKV]
        lo = xm[:, 3 * W_RWKV:3 * W_RWKV + LORA_W + LORA_A]
        g_lo = xm[:, 3 * W_RWKV + LORA_W + LORA_A:]
        logw = -_softplus(-(w0_ref[...] + _dot(jnp.tanh(lo).astype(BF16), w2_ref[...]))) - 0.5
        wl = -jnp.exp(logw)
        a = _sigmoid(a0_ref[...] + _dot(lo.astype(BF16), a2_ref[...]))
        g = _dot(_sigmoid(g_lo).astype(BF16), g2_ref[...])
        kkv = k * kk_ref[...]
        k = k * (1.0 + (a - 1.0) * ka_ref[...])
        if t_valid < chunk:
            valid = rows < t_valid
            wl = jnp.where(valid, wl, 0.0)
            kkv = jnp.where(valid, kkv, 0.0)
            k = jnp.where(valid, k, 0.0)
            v = jnp.where(valid, v, 0.0)
        cum = _sel_mm(incl_bf, wl, NN)
        cum_last = cum[chunk - 1:chunk, :]
        return dict(r=r, k=k, v=v, a=a, g=g, kkv=kkv, w_incl=jnp.exp(cum), w_prev=jnp.exp(cum - wl),
                    w_inv=jnp.exp(-cum), w_tail=jnp.exp(cum_last - cum), w_last=jnp.exp(cum_last))

    pre = [per_batch(bi) for bi in range(bb)]
    chains = [(bi, h) for bi in range(bb) for h in range(H_R)]
    col = lambda name: [pre[bi][name][:, sls[h]] for bi, h in chains]
    n = range(len(chains))
    r_, k_, v_, a_ = col('r'), col('k'), col('v'), col('a')
    w_incl, w_prev, w_inv, w_tail, w_last = col('w_incl'), col('w_prev'), col('w_inv'), col('w_tail'), col('w_last')
    kks = [x * lax.rsqrt(jnp.sum(x * x, axis=-1, keepdims=True) + L2_EPS) for x in col('kkv')]
    a_hat = [-(kks[i] * w_prev[i]) for i in n]
    kka = [kks[i] * a_[i] for i in n]
    b_hat = [kka[i] * w_inv[i] for i in n]
    k_hat = [k_[i] * w_inv[i] for i in n]
    r_hat = [r_[i] * w_incl[i] for i in n]
    m_ab = [jnp.where(strict, _mm(a_hat[i], b_hat[i], NT, 1), 0.0) for i in n]
    m_ak = [jnp.where(strict, _mm(a_hat[i], k_hat[i], NT, 1), 0.0) for i in n]
    a_rb = [jnp.where(incl, _mm(r_hat[i], b_hat[i], NT, 1), 0.0) for i in n]
    a_rk = [jnp.where(incl, _mm(r_hat[i], k_hat[i], NT, 1), 0.0) for i in n]
    t_inv = _unit_lower_inverses(m_ab, eye, chunk)
    w_hat = [_mm(t_inv[i], a_hat[i], NN, 1) for i in n]
    mv = [_mm(m_ak[i], v_[i], NN, 1) for i in n]
    u = [_mm(t_inv[i], mv[i], NN, 1) for i in n]
    ss = [s_scr[bi, h] for bi, h in chains]
    p = [_mm(w_hat[i], ss[i], NT, STATE_PASSES) + u[i] for i in n]
    ys = [_mm(r_hat[i], ss[i], NT, 1) + _mm(a_rb[i], p[i], NN, 1) + _mm(a_rk[i], v_[i], NN, 1) for i in n]
    for i, (bi, h) in enumerate(chains):
        s_scr[bi, h] = (ss[i] * w_last[i] + _mm(p[i], kka[i] * w_tail[i], TN, STATE_PASSES)
                        + _mm(v_[i], k_[i] * w_tail[i], TN, STATE_PASSES))
    for i, (bi, h) in enumerate(chains):
        sl = sls[h]
        y = ys[i]
        mean = jnp.mean(y, axis=-1, keepdims=True)
        yc = y - mean
        var = jnp.mean(yc * yc, axis=-1, keepdims=True)
        yn = yc * lax.rsqrt(var + GN_EPS) * lnw_ref[:, sl] + lnb_ref[:, sl]
        bonus = jnp.sum(r_[i] * k_[i] * rk_ref[:, sl], axis=-1, keepdims=True) * v_[i]
        y_ref[bi, :, sl] = ((yn + bonus) * pre[bi]['g'][:, sl]).astype(y_ref.dtype)

    @pl.when(c == n_chunks - 1)
    def _fin():
        sout_ref[...] = s_scr[...]


def _rwkv_mix(pr, shift_prev, s0, p, chunk, t_valid, bb):
    b, t, _ = pr.shape
    assert b % bb == 0 and t % chunk == 0
    n_chunks = t // chunk
    const = lambda shape: pl.BlockSpec(shape, lambda i, j: (0,) * len(shape))
    return pl.pallas_call(
        functools.partial(_rwkv_body, bb=bb, chunk=chunk, t_valid=t_valid, n_chunks=n_chunks),
        grid=(b // bb, n_chunks),
        in_specs=[
            pl.BlockSpec((bb, chunk, R_PROJ), lambda i, j: (i, j, 0)),
            pl.BlockSpec((bb, 1, R_PROJ), lambda i, j: (i, 0, 0)),
            pl.BlockSpec((bb, H_R, N_R, N_R), lambda i, j: (i, 0, 0, 0)),
            const((1, R_PROJ)), const((1, W_RWKV)), const((LANES, W_RWKV)), const((1, W_RWKV)),
            const((LANES, W_RWKV)), const((LORA_G, W_RWKV)), const((1, W_RWKV)), const((1, W_RWKV)),
            const((1, W_RWKV)), const((1, W_RWKV)), const((1, W_RWKV)),
        ],
        out_specs=[
            pl.BlockSpec((bb, chunk, W_RWKV), lambda i, j: (i, j, 0)),
            pl.BlockSpec((bb, H_R, N_R, N_R), lambda i, j: (i, 0, 0, 0)),
        ],
        out_shape=[jax.ShapeDtypeStruct((b, t, W_RWKV), BF16), jax.ShapeDtypeStruct((b, H_R, N_R, N_R), F32)],
        scratch_shapes=[pltpu.VMEM((bb, H_R, N_R, N_R), F32), pltpu.VMEM((bb, 1, R_PROJ), F32)],
        compiler_params=pltpu.CompilerParams(dimension_semantics=("arbitrary", "arbitrary"),
                                             vmem_limit_bytes=VMEM_LIMIT),
        name="rwkv_mix",
    )(pr, shift_prev[:, None, :], s0, p['mu'], p['w0'], p['w2'], p['a0'], p['a2'], p['g2'], p['k_k'], p['k_a'],
      p['r_k'], p['lnx_w'], p['lnx_b'])


def _gdn_body(pg_ref, cbuf_ref, s0_ref, cw_ref, alog_ref, dtb_ref, nw_ref, y_ref, sout_ref, s_scr, xp_scr,
              *, bb, chunk, t_valid, n_chunks):
    c = pl.program_id(1)

    @pl.when(c == 0)
    def _init():
        s_scr[...] = s0_ref[...]
        xp_scr[:, 0:SUBLANES, :] = cbuf_ref[...]

    incl, strict, eye = _tri_masks(chunk)
    incl_bf = incl.astype(BF16)
    lane = lax.broadcasted_iota(jnp.int32, (chunk, LANES), 1)
    sls = [slice(h * D_G, (h + 1) * D_G) for h in range(H_G)]

    def per_batch(bi):
        xp_scr[bi, SUBLANES:SUBLANES + chunk, :] = pg_ref[bi, :, :GDN_CONV_DIM]
        base = SUBLANES - (CONV_W - 1)
        conv = xp_scr[bi, base:base + chunk, :] * cw_ref[0:1, :]
        for j in range(1, CONV_W):
            conv = conv + xp_scr[bi, base + j:base + j + chunk, :] * cw_ref[j:j + 1, :]
        xp_scr[bi, 0:SUBLANES, :] = xp_scr[bi, chunk:chunk + SUBLANES, :]
        qkv = conv * _sigmoid(conv)
        z = pg_ref[bi, :, GDN_CONV_DIM:GDN_CONV_DIM + W_GDN]
        ba = pg_ref[bi, :, GDN_CONV_DIM + W_GDN:]
        beta_blk = _sigmoid(ba)
        g_blk = -jnp.exp(alog_ref[...]) * _softplus(ba + dtb_ref[...])
        if t_valid < chunk:
            valid = lax.broadcasted_iota(jnp.int32, (chunk, 1), 0) < t_valid
            beta_blk = jnp.where(valid, beta_blk, 0.0)
            g_blk = jnp.where(valid, g_blk, 0.0)
        gc_blk = _sel_mm(incl_bf, g_blk, NN)
        return dict(qkv=qkv, z=z, beta_blk=beta_blk, gc_blk=gc_blk)

    pre = [per_batch(bi) for bi in range(bb)]
    chains = [(bi, h) for bi in range(bb) for h in range(H_G)]
    n = range(len(chains))
    qs = [pre[bi]['qkv'][:, sls[h]] for bi, h in chains]
    qs = [x * lax.rsqrt(jnp.sum(x * x, axis=-1, keepdims=True) + L2_EPS) * (D_G ** -0.5) for x in qs]
    ks = [pre[bi]['qkv'][:, W_GDN + h * D_G:W_GDN + (h + 1) * D_G] for bi, h in chains]
    ks = [x * lax.rsqrt(jnp.sum(x * x, axis=-1, keepdims=True) + L2_EPS) for x in ks]
    vs = [pre[bi]['qkv'][:, 2 * W_GDN + h * D_G:2 * W_GDN + (h + 1) * D_G] for bi, h in chains]
    betas = [pre[bi]['beta_blk'][:, h:h + 1] for bi, h in chains]
    gcols = [pre[bi]['gc_blk'][:, H_G + h:H_G + h + 1] for bi, h in chains]
    grows = [_sel_mm((lane == H_G + h).astype(BF16), pre[bi]['gc_blk'], NT) for bi, h in chains]
    g_last = [gcols[i][chunk - 1:chunk, :] for i in n]
    decay = [jnp.where(incl, jnp.exp(jnp.where(incl, gcols[i] - grows[i], 0.0)), 0.0) for i in n]
    k_beta = [ks[i] * betas[i] for i in n]
    lmat = [jnp.where(strict, _mm(k_beta[i], ks[i], NT, 1) * decay[i], 0.0) for i in n]
    attn = [jnp.where(incl, _mm(qs[i], ks[i], NT, 1) * decay[i], 0.0) for i in n]
    t_inv = _unit_lower_inverses([-x for x in lmat], eye, chunk)
    e_gc = [jnp.exp(gcols[i]) for i in n]
    u = [_mm(t_inv[i], vs[i] * betas[i], NN, 1) for i in n]
    w = [_mm(t_inv[i], k_beta[i] * e_gc[i], NN, 1) for i in n]
    ss = [s_scr[bi, h] for bi, h in chains]
    v_new = [u[i] - _mm(w[i], ss[i], NN, STATE_PASSES) for i in n]
    os_ = [_mm(qs[i] * e_gc[i], ss[i], NN, 1) + _mm(attn[i], v_new[i], NN, 1) for i in n]
    for i, (bi, h) in enumerate(chains):
        s_scr[bi, h] = (ss[i] * jnp.exp(g_last[i])
                        + _mm(ks[i] * jnp.exp(g_last[i] - gcols[i]), v_new[i], TN, STATE_PASSES))
    for i, (bi, h) in enumerate(chains):
        o = os_[i]
        o = o * lax.rsqrt(jnp.mean(o * o, axis=-1, keepdims=True) + RMS_EPS) * nw_ref[...]
        z_h = pre[bi]['z'][:, sls[h]]
        y_ref[bi, :, sls[h]] = (o * (z_h * _sigmoid(z_h))).astype(y_ref.dtype)

    @pl.when(c == n_chunks - 1)
    def _fin():
        sout_ref[...] = s_scr[...]


def _gdn_mix(pg, conv_buf, s0, p, chunk, t_valid, bb):
    b, t, _ = pg.shape
    assert b % bb == 0 and t % chunk == 0
    n_chunks = t // chunk
    cbuf = jnp.pad(conv_buf, ((0, 0), (SUBLANES - (CONV_W - 1), 0), (0, 0)))
    const = lambda shape: pl.BlockSpec(shape, lambda i, j: (0,) * len(shape))
    return pl.pallas_call(
        functools.partial(_gdn_body, bb=bb, chunk=chunk, t_valid=t_valid, n_chunks=n_chunks),
        grid=(b // bb, n_chunks),
        in_specs=[
            pl.BlockSpec((bb, chunk, G_PROJ_PAD), lambda i, j: (i, j, 0)),
            pl.BlockSpec((bb, SUBLANES, GDN_CONV_DIM), lambda i, j: (i, 0, 0)),
            pl.BlockSpec((bb, H_G, D_G, D_G), lambda i, j: (i, 0, 0, 0)),
            const((CONV_W, GDN_CONV_DIM)), const((1, LANES)), const((1, LANES)), const((1, D_G)),
        ],
        out_specs=[
            pl.BlockSpec((bb, chunk, W_GDN), lambda i, j: (i, j, 0)),
            pl.BlockSpec((bb, H_G, D_G, D_G), lambda i, j: (i, 0, 0, 0)),
        ],
        out_shape=[jax.ShapeDtypeStruct((b, t, W_GDN), BF16), jax.ShapeDtypeStruct((b, H_G, D_G, D_G), F32)],
        scratch_shapes=[pltpu.VMEM((bb, H_G, D_G, D_G), F32),
                        pltpu.VMEM((bb, SUBLANES + chunk, GDN_CONV_DIM), F32)],
        compiler_params=pltpu.CompilerParams(dimension_semantics=("arbitrary", "arbitrary"),
                                             vmem_limit_bytes=VMEM_LIMIT),
        name="gdn_mix",
    )(pg, cbuf, s0, p['conv_w'], p['a_log'], p['dt_bias'], p['gdn_norm_w'])


def _out_q_body(x_ref, yr_ref, yg_ref, wor_ref, wog_ref, gn_ref, wq_ref, x1_ref, q_ref):
    x1 = x_ref[...] + _dot(yr_ref[...], wor_ref[...]) + _dot(yg_ref[...], wog_ref[...])
    x1_ref[...] = x1
    q_ref[...] = _dot(_rmsnorm(x1, gn_ref[...]).astype(BF16), wq_ref[...]).astype(q_ref.dtype)


def _out_q(x, yr, yg, p, tm):
    n, d = x.shape
    assert n % tm == 0
    row = lambda w: pl.BlockSpec((tm, w), lambda i: (i, 0))
    const = lambda shape: pl.BlockSpec(shape, lambda i: (0, 0))
    return pl.pallas_call(
        _out_q_body,
        grid=(n // tm,),
        in_specs=[row(d), row(W_RWKV), row(W_GDN), const((W_RWKV, d)), const((W_GDN, d)), const((1, d)),
                  const((d, d))],
        out_specs=[row(d), row(d)],
        out_shape=[jax.ShapeDtypeStruct((n, d), F32), jax.ShapeDtypeStruct((n, d), BF16)],
        compiler_params=pltpu.CompilerParams(dimension_semantics=("parallel",), vmem_limit_bytes=VMEM_LIMIT),
        name="out_q",
    )(x, yr, yg, p['w_out_r'], p['w_out_g'], p['norm_cross'], p['wq'])


def _attn_body(q_ref, x_ref, mk_ref, mv_ref, wo_ref, gn_ref, rw_ref, rb_ref, x2_ref, h_ref, ti_ref, gt_ref,
               *, bb, tq, mem4d):
    for i in range(bb):
        q = q_ref[i]
        heads = []
        for hh in range(H_X):
            sl = slice(hh * D_X, (hh + 1) * D_X)
            mk_h = mk_ref[i, :, hh, :] if mem4d else mk_ref[i, :, sl]
            mv_h = mv_ref[i, :, hh, :] if mem4d else mv_ref[i, :, sl]
            s = _dot_nt(q[:, sl], mk_h.astype(BF16)) * (D_X ** -0.5)
            e = jnp.exp(s - jnp.max(s, axis=-1, keepdims=True))
            prob = e / jnp.sum(e, axis=-1, keepdims=True)
            heads.append(_dot(prob.astype(BF16), mv_h.astype(BF16)))
        o = jnp.concatenate(heads, axis=-1).astype(BF16)
        x2 = x_ref[i] + _dot(o, wo_ref[...])
        x2_ref[i] = x2
        h = _rmsnorm(x2, gn_ref[...])
        for sub in range(SUBLANES):
            h_ref[pl.ds(i * tq * SUBLANES + sub, tq, stride=SUBLANES), :] = h[:, sub * LANES:(sub + 1) * LANES]
        logits = _dot(h, rw_ref[...], HIGHEST) + rb_ref[...]
        lane = lax.broadcasted_iota(jnp.int32, logits.shape, 1)
        vals, idxs = [], []
        for _ in range(TOP_K):
            m = jnp.max(logits, axis=-1, keepdims=True)
            first = jnp.min(jnp.where(logits == m, lane, LANES), axis=-1, keepdims=True)
            vals.append(m)
            idxs.append(first)
            logits = jnp.where(lane == first, -jnp.inf, logits)
        es = [jnp.exp(vv - vals[0]) for vv in vals]
        den = es[0] + es[1] + es[2] + es[3]
        ti = jnp.zeros(lane.shape, jnp.int32)
        gt = jnp.zeros(lane.shape, F32)
        for j in range(TOP_K):
            ti = jnp.where(lane == j, idxs[j], ti)
            gt = jnp.where(lane == j, es[j] / den, gt)
        ti_ref[i] = ti
        gt_ref[i] = gt


def _attn_route(q, x1, mk, mv, p, bb, tq):
    b, t, d = x1.shape
    assert d == SUBLANES * LANES
    rows = bb * tq
    assert b % bb == 0 and t % tq == 0 and (bb == 1 or tq == t)
    n_tq = t // tq
    blk = lambda w: pl.BlockSpec((bb, tq, w), lambda i, j: (i, j, 0))
    mem4d = mk.ndim == 4
    mem = (pl.BlockSpec((bb,) + mk.shape[1:], lambda i, j: (i, 0, 0, 0)) if mem4d
           else pl.BlockSpec((bb, N_MEM, d), lambda i, j: (i, 0, 0)))
    const = lambda shape: pl.BlockSpec(shape, lambda i, j: (0, 0))
    return pl.pallas_call(
        functools.partial(_attn_body, bb=bb, tq=tq, mem4d=mem4d),
        grid=(b // bb, n_tq),
        in_specs=[blk(d), blk(d), mem, mem, const((d, d)), const((1, d)), const((d, LANES)), const((1, LANES))],
        out_specs=[blk(d), pl.BlockSpec((rows * SUBLANES, LANES), lambda i, j: (i * n_tq + j, 0)), blk(LANES),
                   blk(LANES)],
        out_shape=[jax.ShapeDtypeStruct((b, t, d), F32), jax.ShapeDtypeStruct((b * t * SUBLANES, LANES), F32),
                   jax.ShapeDtypeStruct((b, t, LANES), jnp.int32), jax.ShapeDtypeStruct((b, t, LANES), F32)],
        compiler_params=pltpu.CompilerParams(dimension_semantics=("parallel", "parallel"),
                                             vmem_limit_bytes=VMEM_LIMIT),
        name="attn_route",
    )(q, x1, mk, mv, p['wo'], p['norm_ffn'], p['router_w'], p['router_b'])


def _moe_body(be_ref, gnext_ref, sprev_ref, h_hbm, w1_ref, b1_ref, w2_ref, b2_ref, out_hbm,
              xb0, xb1, yb0, yb1, w1b, w2b, gsem, ssem, *, tm, n_steps):
    k = pl.program_id(0)
    xbufs, ybufs = (xb0, xb1), (yb0, yb1)

    def row(ref, at):
        return ref.at[pl.ds(at, SUBLANES)]

    def gather_all(slot):
        return pltpu.make_async_copy(h_hbm.at[pl.ds(0, tm * SUBLANES)], xbufs[slot], gsem.at[slot])

    def scatter_all(slot):
        return pltpu.make_async_copy(ybufs[slot], out_hbm.at[pl.ds(0, tm * SUBLANES)], ssem.at[slot])

    @pl.when(k == 0)
    def _prologue():
        yb0[...] = jnp.zeros(yb0.shape, F32)
        yb1[...] = jnp.zeros(yb1.shape, F32)
        for r in range(tm):
            pltpu.make_async_copy(row(h_hbm, 0), row(xb0, r * SUBLANES), gsem.at[0]).start()

    @pl.when(jnp.logical_or(k == 0, be_ref[k] != be_ref[jnp.maximum(k - 1, 0)]))
    def _cast():
        w1b[...] = w1_ref[0].astype(BF16)
        w2b[...] = w2_ref[0].astype(BF16)

    def phase(cur):
        nxt = 1 - cur
        gather_all(cur).wait()

        @pl.when(k >= 1)
        def _():
            scatter_all(cur).wait()

        for r in range(tm):
            src = pl.multiple_of(gnext_ref[0, 0, r], SUBLANES)
            dst = pl.multiple_of(sprev_ref[0, 0, r], SUBLANES)
            pltpu.make_async_copy(row(h_hbm, src), row(xbufs[nxt], r * SUBLANES), gsem.at[nxt]).start()
            pltpu.make_async_copy(row(ybufs[nxt], r * SUBLANES), row(out_hbm, dst), ssem.at[nxt]).start()
        x = jnp.concatenate([xbufs[cur][pl.ds(sub, tm, stride=SUBLANES), :] for sub in range(SUBLANES)], axis=-1)
        hc = _dot(x.astype(BF16), w1b[...]) + b1_ref[0]
        hg = jnp.minimum(hc[:, :D_FF], SWIGLU_LIMIT)
        hl = jnp.clip(hc[:, D_FF:], -SWIGLU_LIMIT, SWIGLU_LIMIT)
        act = hg * _sigmoid(SWIGLU_ALPHA * hg) * (hl + 1.0)
        y = _dot(act.astype(BF16), w2b[...]) + b2_ref[0]
        for sub in range(SUBLANES):
            ybufs[cur][pl.ds(sub, tm, stride=SUBLANES), :] = y[:, sub * LANES:(sub + 1) * LANES]

        @pl.when(k == n_steps - 1)
        def _epilogue():
            gather_all(nxt).wait()
            scatter_all(nxt).wait()

    for parity in range(2):
        pl.when(lax.rem(k, 2) == parity)(functools.partial(phase, parity))


def _moe(h, gtok, sdst, block_e, p, tm):
    n, d = h.shape[0] // SUBLANES, D_MODEL
    n_steps = block_e.shape[0]
    grid_spec = pltpu.PrefetchScalarGridSpec(
        num_scalar_prefetch=1,
        grid=(n_steps,),
        in_specs=[
            pl.BlockSpec((1, 1, tm), lambda k, be: (jnp.minimum(k + 1, n_steps - 1), 0, 0),
                         memory_space=pltpu.SMEM),
            pl.BlockSpec((1, 1, tm), lambda k, be: (jnp.maximum(k - 1, 0), 0, 0), memory_space=pltpu.SMEM),
            pl.BlockSpec(memory_space=pl.ANY),
            pl.BlockSpec((1, d, 2 * D_FF), lambda k, be: (be[k], 0, 0)),
            pl.BlockSpec((1, 1, 2 * D_FF), lambda k, be: (be[k], 0, 0)),
            pl.BlockSpec((1, D_FF, d), lambda k, be: (be[k], 0, 0)),
            pl.BlockSpec((1, 1, d), lambda k, be: (be[k], 0, 0)),
        ],
        out_specs=pl.BlockSpec(memory_space=pl.ANY),
        scratch_shapes=[
            pltpu.VMEM((tm * SUBLANES, LANES), F32), pltpu.VMEM((tm * SUBLANES, LANES), F32),
            pltpu.VMEM((tm * SUBLANES, LANES), F32), pltpu.VMEM((tm * SUBLANES, LANES), F32),
            pltpu.VMEM((d, 2 * D_FF), BF16), pltpu.VMEM((D_FF, d), BF16),
            pltpu.SemaphoreType.DMA((2,)), pltpu.SemaphoreType.DMA((2,)),
        ],
    )
    return pl.pallas_call(
        functools.partial(_moe_body, tm=tm, n_steps=n_steps),
        grid_spec=grid_spec,
        out_shape=jax.ShapeDtypeStruct(((n * TOP_K + tm) * SUBLANES, LANES), F32),
        compiler_params=pltpu.CompilerParams(dimension_semantics=("arbitrary",), vmem_limit_bytes=VMEM_LIMIT),
        name="moe_experts",
    )(block_e, gtok.reshape(n_steps, 1, tm), sdst.reshape(n_steps, 1, tm), h, p['w1_e'], p['b1_e'], p['w2_e'],
      p['b2_e'])


def _route_plan(top_i, tm):
    n = top_i.shape[0]
    na = n * TOP_K
    flat_e = top_i.reshape(na)
    order = jnp.argsort(flat_e).astype(jnp.int32)
    counts = jnp.sum((flat_e[:, None] == jnp.arange(N_EXPERTS, dtype=jnp.int32)[None, :]).astype(jnp.int32), axis=0)
    padded = (counts + tm - 1) // tm * tm
    starts = jnp.cumsum(counts) - counts
    pends = jnp.cumsum(padded)
    pstarts = pends - padded
    n_steps = -(-(na + N_EXPERTS * (tm - 1)) // tm) + 2
    blk_start = (jnp.arange(n_steps, dtype=jnp.int32) - 1) * tm
    block_e = jnp.sum((blk_start[:, None] >= pends[None, :]).astype(jnp.int32), axis=1)
    block_e = jnp.minimum(block_e, N_EXPERTS - 1)
    block_e = block_e.at[0].set(block_e[1])
    lane = jnp.arange(tm, dtype=jnp.int32)[None, :]
    local = blk_start[:, None] + lane - pstarts[block_e][:, None]
    valid = (local >= 0) & (local < counts[block_e][:, None]) & (blk_start[:, None] >= 0)
    asg = order[jnp.clip(starts[block_e][:, None] + local, 0, na - 1)]
    gtok = jnp.where(valid, asg // TOP_K, 0)
    sdst = jnp.where(valid, (asg % TOP_K) * n + asg // TOP_K, TOP_K * n + lane)
    return gtok * SUBLANES, sdst * SUBLANES, block_e


def _combine_body(x_ref, gt_ref, gn_ref, *refs, final):
    slot_refs, y_ref = refs[:TOP_K], refs[TOP_K]
    x = x_ref[...]
    gt = gt_ref[...]
    tm = x.shape[0]
    for j in range(TOP_K):
        slot = jnp.concatenate([slot_refs[j][pl.ds(sub, tm, stride=SUBLANES), :] for sub in range(SUBLANES)],
                               axis=-1)
        x = x + gt[:, j:j + 1] * slot
    y_ref[...] = _rmsnorm(x, gn_ref[...]) if final else x


def _combine(x2, gates, slots, gn, tm, n_all, row0, final):
    n, d = x2.shape
    assert n % tm == 0 and row0 % tm == 0 and n_all % tm == 0
    slot_spec = lambda j: pl.BlockSpec((tm * SUBLANES, LANES), lambda i: ((j * n_all + row0) // tm + i, 0))
    return pl.pallas_call(
        functools.partial(_combine_body, final=final),
        grid=(n // tm,),
        in_specs=[pl.BlockSpec((tm, d), lambda i: (i, 0)), pl.BlockSpec((tm, LANES), lambda i: (i, 0)),
                  pl.BlockSpec((1, d), lambda i: (0, 0))] + [slot_spec(j) for j in range(TOP_K)],
        out_specs=pl.BlockSpec((tm, d), lambda i: (i, 0)),
        out_shape=jax.ShapeDtypeStruct((n, d), F32),
        compiler_params=pltpu.CompilerParams(dimension_semantics=("parallel",), vmem_limit_bytes=VMEM_LIMIT),
        name="combine",
    )(x2, gates, gn, *([slots] * TOP_K))


def _layer_params(l, norm_mix, w_in, mu_shift, w0, w2_decay, a0, a2_iclr, g2_gate, k_k, k_a, r_k, lnx_w, lnx_b,
                  conv_w, a_log, dt_bias, gdn_norm_w, w_out, norm_cross, norm_mem, wq_x, wk_x, wv_x, wo_x,
                  norm_ffn, router_w, router_b, w1_e, b1_e, w2_e, b2_e):
    row = lambda z: z.reshape(1, -1).astype(F32)
    lane_pad = lambda z, at: jnp.zeros((1, LANES), F32).at[0, at:at + z.shape[0]].set(z)
    return {
        'norm_mix': row(norm_mix[l]),
        'w_in_r': w_in[l][:, :R_PROJ].astype(BF16),
        'w_in_g': jnp.pad(w_in[l][:, R_PROJ:], ((0, 0), (0, G_PROJ_PAD - G_PROJ))).astype(BF16),
        'mu': row(mu_shift[l]), 'w0': row(w0[l]), 'a0': row(a0[l]),
        'w2': jnp.pad(w2_decay[l], ((0, LORA_A), (0, 0))).astype(BF16),
        'a2': jnp.pad(a2_iclr[l], ((LORA_W, 0), (0, 0))).astype(BF16),
        'g2': g2_gate[l].astype(BF16),
        'k_k': row(k_k[l]), 'k_a': row(k_a[l]), 'r_k': row(r_k[l]), 'lnx_w': row(lnx_w[l]), 'lnx_b': row(lnx_b[l]),
        'conv_w': conv_w[l].astype(F32),
        'a_log': lane_pad(a_log[l], H_G), 'dt_bias': lane_pad(dt_bias[l], H_G),
        'gdn_norm_w': row(gdn_norm_w[l]),
        'w_out_r': w_out[l][:W_RWKV].astype(BF16), 'w_out_g': w_out[l][W_RWKV:].astype(BF16),
        'norm_cross': row(norm_cross[l]), 'norm_mem': row(norm_mem[l]),
        'wq': wq_x[l].astype(BF16), 'wk': wk_x[l].astype(BF16), 'wv': wv_x[l].astype(BF16),
        'wo': wo_x[l].astype(BF16),
        'norm_ffn': row(norm_ffn[l]),
        'router_w': jnp.pad(router_w[l].astype(F32), ((0, 0), (0, LANES - N_EXPERTS))),
        'router_b': jnp.full((1, LANES), NEG_BIG, F32).at[0, :N_EXPERTS].set(router_b[l].astype(F32)),
        'w1_e': w1_e[l], 'b1_e': b1_e[l][:, None, :], 'w2_e': w2_e[l], 'b2_e': b2_e[l][:, None, :],
    }


def _mix_and_attend(x, mk, mv, shift_prev, s_r, conv_buf, s_g, p, *, chunk, mix_bb, tm, bb, tq):
    b, t, d = x.shape
    assert t >= CONV_W - 1
    pr, pg = _norm_proj(x.reshape(b * t, d), p['norm_mix'], [p['w_in_r'], p['w_in_g']], [F32, F32], tm)
    pr = pr.reshape(b, t, R_PROJ)
    pg = pg.reshape(b, t, G_PROJ_PAD)
    shift_new = pr[:, t - 1]
    conv_new = pg[:, t - (CONV_W - 1):, :GDN_CONV_DIM]
    t_pad = -(-t // chunk) * chunk
    if t_pad != t:
        pr = jnp.pad(pr, ((0, 0), (0, t_pad - t), (0, 0)))
        pg = jnp.pad(pg, ((0, 0), (0, t_pad - t), (0, 0)))
    t_valid = chunk if t_pad == t else t
    y_r, s_r_new = _rwkv_mix(pr, shift_prev, s_r, p, chunk, t_valid, mix_bb)
    y_g, s_g_new = _gdn_mix(pg, conv_buf, s_g, p, chunk, t_valid, mix_bb)
    if t_pad != t:
        y_r, y_g = y_r[:, :t], y_g[:, :t]
    x1, q = _out_q(x.reshape(b * t, d), y_r.reshape(b * t, W_RWKV), y_g.reshape(b * t, W_GDN), p, tm)
    x2, h, top_i, gates = _attn_route(q.reshape(b, t, d), x1.reshape(b, t, d), mk, mv, p, bb, tq)
    return (x2.reshape(b * t, d), h, top_i.reshape(b * t, LANES), gates.reshape(b * t, LANES),
            shift_new, s_r_new, conv_new, s_g_new)


def kernel(x_prompt, x_sample, mem_prompt, state_rwkv, state_rwkv_shift, state_gdn, state_gdn_conv, cache_mem_k, cache_mem_v, norm_mix, w_in, mu_shift, w0, w2_decay, a0, a2_iclr, g2_gate, k_k, k_a, r_k, lnx_w, lnx_b, conv_w, a_log, dt_bias, gdn_norm_w, w_out, norm_cross, norm_mem, wq_x, wk_x, wv_x, wo_x, norm_ffn, router_w, router_b, w1_e, b1_e, w2_e, b2_e, final_norm):
    bp, tp, d = x_prompt.shape
    bs, ts, _ = x_sample.shape
    depth = w_in.shape[0]
    np_, ns = bp * tp, bs * ts
    xp, xs = x_prompt, x_sample
    outs = [[] for _ in range(10)]
    for l in range(depth):
        p = _layer_params(l, norm_mix, w_in, mu_shift, w0, w2_decay, a0, a2_iclr, g2_gate, k_k, k_a, r_k, lnx_w,
                          lnx_b, conv_w, a_log, dt_bias, gdn_norm_w, w_out, norm_cross, norm_mem, wq_x, wk_x, wv_x,
                          wo_x, norm_ffn, router_w, router_b, w1_e, b1_e, w2_e, b2_e)
        n_mem = mem_prompt.shape[1]
        mk, mv = _norm_proj(mem_prompt.reshape(bp * n_mem, d), p['norm_mem'], [p['wk'], p['wv']], [F32, F32], 256)
        mk, mv = mk.reshape(bp, n_mem, d), mv.reshape(bp, n_mem, d)
        res_p = _mix_and_attend(
            xp, mk, mv, jnp.zeros((bp, R_PROJ), F32), jnp.zeros((bp, H_R, N_R, N_R), F32),
            jnp.zeros((bp, CONV_W - 1, GDN_CONV_DIM), F32), jnp.zeros((bp, H_G, D_G, D_G), F32), p,
            chunk=MIX_CHUNK, mix_bb=2, tm=256, bb=1, tq=512)
        res_s = _mix_and_attend(
            xs, cache_mem_k[l], cache_mem_v[l], state_rwkv_shift[l], state_rwkv[l], state_gdn_conv[l],
            state_gdn[l], p, chunk=SUBLANES, mix_bb=4, tm=256, bb=2, tq=ts)
        h = jnp.concatenate([res_p[1], res_s[1]], axis=0)
        top_i = jnp.concatenate([res_p[2], res_s[2]], axis=0)[:, :TOP_K]
        gtok, sdst, block_e = _route_plan(top_i, MOE_ROWS)
        slots = _moe(h, gtok, sdst, block_e, p, MOE_ROWS)
        gn = final_norm.reshape(1, d).astype(F32)
        last = l == depth - 1
        xp = _combine(res_p[0], res_p[3], slots, gn, 256, np_ + ns, 0, last).reshape(bp, tp, d)
        xs = _combine(res_s[0], res_s[3], slots, gn, 256, np_ + ns, np_, last).reshape(bs, ts, d)
        new = [res_p[5], res_p[4], res_p[7], res_p[6], mk.reshape(bp, n_mem, H_X, D_X),
               mv.reshape(bp, n_mem, H_X, D_X), res_s[5], res_s[4], res_s[7], res_s[6]]
        for acc, val in zip(outs, new):
            acc.append(val)
    return (xp, xs) + tuple(jnp.stack(o) for o in outs)
```

```python
import functools

import jax
import jax.numpy as jnp
from jax import lax
from jax.experimental import pallas as pl
from jax.experimental.pallas import tpu as pltpu

F32 = jnp.float32
BF16 = jnp.bfloat16
DEFAULT = lax.Precision.DEFAULT

D_MODEL = 1024
W_RWKV = 512
N_R = 64
H_R = W_RWKV // N_R
LORA_W = 64
LORA_A = 64
LORA_G = 128
R_PROJ = 3 * W_RWKV + LORA_W + LORA_A + LORA_G
GN_EPS = 64e-5
W_GDN = 512
D_G = 128
H_G = W_GDN // D_G
GDN_CONV_DIM = 3 * W_GDN
CONV_W = 4
G_PROJ = GDN_CONV_DIM + W_GDN + 2 * H_G
LANES = 128
SUBLANES = 8
G_PROJ_PAD = GDN_CONV_DIM + W_GDN + LANES
MIX_CHUNK = 64
N_MEM = 256
H_X = 4
D_X = D_MODEL // H_X
N_EXPERTS = 32
TOP_K = 4
D_FF = D_MODEL
SWIGLU_LIMIT = 7.0
SWIGLU_ALPHA = 1.702
MOE_ROWS = 256
RMS_EPS = 1e-6
L2_EPS = 1e-6
NEG_BIG = -1e30
VMEM_LIMIT = 56 * 1024 * 1024


def _dot(a, b, precision=DEFAULT):
    return jnp.dot(a, b, preferred_element_type=F32, precision=precision)


def _dot_nt(a, b, precision=DEFAULT):
    return lax.dot_general(a, b, (((1,), (1,)), ((), ())), preferred_element_type=F32, precision=precision)


NN = (((1,), (0,)), ((), ()))
NT = (((1,), (1,)), ((), ()))
TN = (((0,), (0,)), ((), ()))


def _split2(x):
    hi = x.astype(BF16)
    return hi, (x - hi.astype(F32)).astype(BF16)


def _mm(a, b, dims):
    return lax.dot_general(a.astype(BF16), b.astype(BF16), dims, preferred_element_type=F32)


def _sel_mm(sel, x, dims):
    dg = lambda y: lax.dot_general(sel, y, dims, preferred_element_type=F32)
    hi = x.astype(BF16)
    r1 = x - hi.astype(F32)
    mid = r1.astype(BF16)
    lo = (r1 - mid.astype(F32)).astype(BF16)
    return dg(hi) + (dg(mid) + dg(lo))


def _sigmoid(x):
    return 1.0 / (1.0 + jnp.exp(-x))


def _softplus(x):
    return jnp.maximum(x, 0.0) + jnp.log(1.0 + jnp.exp(-jnp.abs(x)))


def _rmsnorm(x, g):
    return x * lax.rsqrt(jnp.mean(x * x, axis=-1, keepdims=True) + RMS_EPS) * g


def _tri_masks(c):
    row = lax.broadcasted_iota(jnp.int32, (c, c), 0)
    col = lax.broadcasted_iota(jnp.int32, (c, c), 1)
    return col <= row, col < row, (col == row).astype(F32)


def _unit_lower_inverses(ms, eye, c):
    ts = [eye + m for m in ms]
    ps = list(ms)
    covered = 2
    while covered < c:
        ps = [_mm(p, p, NN) for p in ps]
        ts = [t + _mm(t, p, NN) for t, p in zip(ts, ps)]
        covered *= 2
    return ts


def _norm_proj_body(x_ref, g_ref, *refs, n_out):
    w_refs, o_refs = refs[:n_out], refs[n_out:]
    hb = _rmsnorm(x_ref[...], g_ref[...]).astype(BF16)
    for w_ref, o_ref in zip(w_refs, o_refs):
        o_ref[...] = _dot(hb, w_ref[...]).astype(o_ref.dtype)


def _norm_proj(x, g, ws, out_dtypes, tm):
    n, d = x.shape
    assert n % tm == 0
    in_specs = [pl.BlockSpec((tm, d), lambda i: (i, 0)), pl.BlockSpec((1, d), lambda i: (0, 0))]
    in_specs += [pl.BlockSpec(w.shape, lambda i: (0, 0)) for w in ws]
    return pl.pallas_call(
        functools.partial(_norm_proj_body, n_out=len(ws)),
        grid=(n // tm,),
        in_specs=in_specs,
        out_specs=[pl.BlockSpec((tm, w.shape[1]), lambda i: (i, 0)) for w in ws],
        out_shape=[jax.ShapeDtypeStruct((n, w.shape[1]), dt) for w, dt in zip(ws, out_dtypes)],
        compiler_params=pltpu.CompilerParams(dimension_semantics=("parallel",), vmem_limit_bytes=VMEM_LIMIT),
        name="norm_proj",
    )(x, g, *ws)


def _rwkv_body(pr_ref, shift_ref, s0_ref, mu_ref, w0_ref, w2_ref, a0_ref, a2_ref, g2_ref, kk_ref, ka_ref,
               rk_ref, lnw_ref, lnb_ref, y_ref, sout_ref, s_scr, prev_scr, *, bb, chunk, t_valid, n_chunks):
    c = pl.program_id(1)

    @pl.when(c == 0)
    def _init():
        s_scr[...] = s0_ref[...]
        prev_scr[...] = shift_ref[...]

    incl, strict, eye = _tri_masks(chunk)
    incl_bf = incl.astype(BF16)
    rows = lax.broadcasted_iota(jnp.int32, (chunk, 1), 0)
    sls = [slice(h * N_R, (h + 1) * N_R) for h in range(H_R)]

    def per_batch(bi):
        pr = pr_ref[bi]
        prev = jnp.where(rows == 0, prev_scr[bi], pltpu.roll(pr, 1, 0))
        prev_scr[bi] = pr[chunk - 1:chunk, :]
        xm = pr + (prev - pr) * mu_ref[...]
        r = xm[:, :W_RWKV]
        k = xm[:, W_RWKV:2 * W_RWKV]
        v = xm[:, 2 * W_RWKV:3 * W_RWKV]
        lo = xm[:, 3 * W_RWKV:3 * W_RWKV + LORA_W + LORA_A]
        g_lo = xm[:, 3 * W_RWKV + LORA_W + LORA_A:]
        logw = -_softplus(-(w0_ref[...] + _dot(jnp.tanh(lo).astype(BF16), w2_ref[...]))) - 0.5
        wl = -jnp.exp(logw)
        a = _sigmoid(a0_ref[...] + _dot(lo.astype(BF16), a2_ref[...]))
        g = _dot(_sigmoid(g_lo).astype(BF16), g2_ref[...])
        kkv = k * kk_ref[...]
        k = k * (1.0 + (a - 1.0) * ka_ref[...])
        if t_valid < chunk:
            valid = rows < t_valid
            wl = jnp.where(valid, wl, 0.0)
            kkv = jnp.where(valid, kkv, 0.0)
            k = jnp.where(valid, k, 0.0)
            v = jnp.where(valid, v, 0.0)
        cum = _sel_mm(incl_bf, wl, NN)
        cum_last = cum[chunk - 1:chunk, :]
        return dict(r=r, k=k, v=v, a=a, g=g, kkv=kkv, w_incl=jnp.exp(cum), w_prev=jnp.exp(cum - wl),
                    w_inv=jnp.exp(-cum), w_tail=jnp.exp(cum_last - cum), w_last=jnp.exp(cum_last))

    pre = [per_batch(bi) for bi in range(bb)]
    chains = [(bi, h) for bi in range(bb) for h in range(H_R)]
    col = lambda name: [pre[bi][name][:, sls[h]] for bi, h in chains]
    n = range(len(chains))
    r_, k_, v_, a_ = col('r'), col('k'), col('v'), col('a')
    w_incl, w_prev, w_inv, w_tail, w_last = col('w_incl'), col('w_prev'), col('w_inv'), col('w_tail'), col('w_last')
    kks = [x * lax.rsqrt(jnp.sum(x * x, axis=-1, keepdims=True) + L2_EPS) for x in col('kkv')]
    a_hat = [-(kks[i] * w_prev[i]) for i in n]
    kka = [kks[i] * a_[i] for i in n]
    b_hat = [kka[i] * w_inv[i] for i in n]
    k_hat = [k_[i] * w_inv[i] for i in n]
    r_hat = [r_[i] * w_incl[i] for i in n]
    cross = [_mm(jnp.concatenate([a_hat[i], r_hat[i]], axis=0), jnp.concatenate([b_hat[i], k_hat[i]], axis=0), NT)
             for i in n]
    row2 = lax.broadcasted_iota(jnp.int32, (chunk, 2 * chunk), 0)
    col2 = lax.broadcasted_iota(jnp.int32, (chunk, 2 * chunk), 1)
    col2 = jnp.where(col2 >= chunk, col2 - chunk, col2)
    strict2, incl2 = col2 < row2, col2 <= row2
    m_top = [jnp.where(strict2, x[:chunk], 0.0) for x in cross]
    a_bot = [jnp.where(incl2, x[chunk:], 0.0) for x in cross]
    t_inv = _unit_lower_inverses([x[:, :chunk] for x in m_top], eye, chunk)
    w_hat = [_mm(t_inv[i], a_hat[i], NN) for i in n]
    mv = [_mm(m_top[i], jnp.concatenate([jnp.zeros_like(v_[i]), v_[i]], axis=0), NN) for i in n]
    u = [_mm(t_inv[i], mv[i], NN) for i in n]
    ss = [s_scr[bi, h] for bi, h in chains]
    p = [_mm(w_hat[i], ss[i], NT) + u[i] for i in n]
    pv = [jnp.concatenate([p[i], v_[i]], axis=0) for i in n]
    ys = [_mm(r_hat[i], ss[i], NT) + _mm(a_bot[i], pv[i], NN) for i in n]
    for i, (bi, h) in enumerate(chains):
        tails = jnp.concatenate([kka[i] * w_tail[i], k_[i] * w_tail[i]], axis=0)
        s_scr[bi, h] = ss[i] * w_last[i] + _mm(pv[i], tails, TN)
    for i, (bi, h) in enumerate(chains):
        sl = sls[h]
        y = ys[i]
        mean = jnp.mean(y, axis=-1, keepdims=True)
        yc = y - mean
        var = jnp.mean(yc * yc, axis=-1, keepdims=True)
        yn = yc * lax.rsqrt(var + GN_EPS) * lnw_ref[:, sl] + lnb_ref[:, sl]
        bonus = jnp.sum(r_[i] * k_[i] * rk_ref[:, sl], axis=-1, keepdims=True) * v_[i]
        y_ref[bi, :, sl] = ((yn + bonus) * pre[bi]['g'][:, sl]).astype(y_ref.dtype)

    @pl.when(c == n_chunks - 1)
    def _fin():
        sout_ref[...] = s_scr[...]


def _rwkv_mix(pr, shift_prev, s0, p, chunk, t_valid, bb):
    b, t, _ = pr.shape
    assert b % bb == 0 and t % chunk == 0
    n_chunks = t // chunk
    const = lambda shape: pl.BlockSpec(shape, lambda i, j: (0,) * len(shape))
    return pl.pallas_call(
        functools.partial(_rwkv_body, bb=bb, chunk=chunk, t_valid=t_valid, n_chunks=n_chunks),
        grid=(b // bb, n_chunks),
        in_specs=[
            pl.BlockSpec((bb, chunk, R_PROJ), lambda i, j: (i, j, 0)),
            pl.BlockSpec((bb, 1, R_PROJ), lambda i, j: (i, 0, 0)),
            pl.BlockSpec((bb, H_R, N_R, N_R), lambda i, j: (i, 0, 0, 0)),
            const((1, R_PROJ)), const((1, W_RWKV)), const((LANES, W_RWKV)), const((1, W_RWKV)),
            const((LANES, W_RWKV)), const((LORA_G, W_RWKV)), const((1, W_RWKV)), const((1, W_RWKV)),
            const((1, W_RWKV)), const((1, W_RWKV)), const((1, W_RWKV)),
        ],
        out_specs=[
            pl.BlockSpec((bb, chunk, W_RWKV), lambda i, j: (i, j, 0)),
            pl.BlockSpec((bb, H_R, N_R, N_R), lambda i, j: (i, 0, 0, 0)),
        ],
        out_shape=[jax.ShapeDtypeStruct((b, t, W_RWKV), BF16), jax.ShapeDtypeStruct((b, H_R, N_R, N_R), F32)],
        scratch_shapes=[pltpu.VMEM((bb, H_R, N_R, N_R), F32), pltpu.VMEM((bb, 1, R_PROJ), F32)],
        compiler_params=pltpu.CompilerParams(dimension_semantics=("arbitrary", "arbitrary"),
                                             vmem_limit_bytes=VMEM_LIMIT),
        name="rwkv_mix",
    )(pr, shift_prev[:, None, :], s0, p['mu'], p['w0'], p['w2'], p['a0'], p['a2'], p['g2'], p['k_k'], p['k_a'],
      p['r_k'], p['lnx_w'], p['lnx_b'])


def _gdn_body(pg_ref, cbuf_ref, s0_ref, cw_ref, alog_ref, dtb_ref, nw_ref, y_ref, sout_ref, s_scr, xp_scr,
              *, bb, chunk, t_valid, n_chunks):
    c = pl.program_id(1)

    @pl.when(c == 0)
    def _init():
        s_scr[...] = s0_ref[...]
        xp_scr[:, 0:SUBLANES, :] = cbuf_ref[...]

    incl, strict, eye = _tri_masks(chunk)
    incl_bf = incl.astype(BF16)
    lane = lax.broadcasted_iota(jnp.int32, (chunk, LANES), 1)
    sls = [slice(h * D_G, (h + 1) * D_G) for h in range(H_G)]

    def per_batch(bi):
        xp_scr[bi, SUBLANES:SUBLANES + chunk, :] = pg_ref[bi, :, :GDN_CONV_DIM]
        base = SUBLANES - (CONV_W - 1)
        conv = xp_scr[bi, base:base + chunk, :] * cw_ref[0:1, :]
        for j in range(1, CONV_W):
            conv = conv + xp_scr[bi, base + j:base + j + chunk, :] * cw_ref[j:j + 1, :]
        xp_scr[bi, 0:SUBLANES, :] = xp_scr[bi, chunk:chunk + SUBLANES, :]
        qkv = conv * _sigmoid(conv)
        z = pg_ref[bi, :, GDN_CONV_DIM:GDN_CONV_DIM + W_GDN]
        ba = pg_ref[bi, :, GDN_CONV_DIM + W_GDN:]
        beta_blk = _sigmoid(ba)
        g_blk = -jnp.exp(alog_ref[...]) * _softplus(ba + dtb_ref[...])
        if t_valid < chunk:
            valid = lax.broadcasted_iota(jnp.int32, (chunk, 1), 0) < t_valid
            beta_blk = jnp.where(valid, beta_blk, 0.0)
            g_blk = jnp.where(valid, g_blk, 0.0)
        gc_blk = _sel_mm(incl_bf, g_blk, NN)
        return dict(qkv=qkv, z=z, beta_blk=beta_blk, gc_blk=gc_blk)

    pre = [per_batch(bi) for bi in range(bb)]
    chains = [(bi, h) for bi in range(bb) for h in range(H_G)]
    n = range(len(chains))
    qs = [pre[bi]['qkv'][:, sls[h]] for bi, h in chains]
    qs = [x * lax.rsqrt(jnp.sum(x * x, axis=-1, keepdims=True) + L2_EPS) * (D_G ** -0.5) for x in qs]
    ks = [pre[bi]['qkv'][:, W_GDN + h * D_G:W_GDN + (h + 1) * D_G] for bi, h in chains]
    ks = [x * lax.rsqrt(jnp.sum(x * x, axis=-1, keepdims=True) + L2_EPS) for x in ks]
    vs = [pre[bi]['qkv'][:, 2 * W_GDN + h * D_G:2 * W_GDN + (h + 1) * D_G] for bi, h in chains]
    betas = [pre[bi]['beta_blk'][:, h:h + 1] for bi, h in chains]
    gcols = [pre[bi]['gc_blk'][:, H_G + h:H_G + h + 1] for bi, h in chains]
    grows = [_sel_mm((lane == H_G + h).astype(BF16), pre[bi]['gc_blk'], NT) for bi, h in chains]
    g_last = [gcols[i][chunk - 1:chunk, :] for i in n]
    decay = [jnp.where(incl, jnp.exp(jnp.where(incl, gcols[i] - grows[i], 0.0)), 0.0) for i in n]
    k_beta = [ks[i] * betas[i] for i in n]
    cross = [_mm(jnp.concatenate([k_beta[i], qs[i]], axis=0), ks[i], NT) for i in n]
    lmat = [jnp.where(strict, cross[i][:chunk] * decay[i], 0.0) for i in n]
    attn = [jnp.where(incl, cross[i][chunk:] * decay[i], 0.0) for i in n]
    t_inv = _unit_lower_inverses([-x for x in lmat], eye, chunk)
    e_gc = [jnp.exp(gcols[i]) for i in n]
    uw = [_mm(t_inv[i], jnp.concatenate([vs[i] * betas[i], k_beta[i] * e_gc[i]], axis=1), NN) for i in n]
    ss = [s_scr[bi, h] for bi, h in chains]
    v_new = [uw[i][:, :D_G] - _mm(uw[i][:, D_G:], ss[i], NN) for i in n]
    os_ = [_mm(jnp.concatenate([qs[i] * e_gc[i], attn[i]], axis=1), jnp.concatenate([ss[i], v_new[i]], axis=0), NN)
           for i in n]
    for i, (bi, h) in enumerate(chains):
        s_scr[bi, h] = ss[i] * jnp.exp(g_last[i]) + _mm(ks[i] * jnp.exp(g_last[i] - gcols[i]), v_new[i], TN)
    for i, (bi, h) in enumerate(chains):
        o = os_[i]
        o = o * lax.rsqrt(jnp.mean(o * o, axis=-1, keepdims=True) + RMS_EPS) * nw_ref[...]
        z_h = pre[bi]['z'][:, sls[h]]
        y_ref[bi, :, sls[h]] = (o * (z_h * _sigmoid(z_h))).astype(y_ref.dtype)

    @pl.when(c == n_chunks - 1)
    def _fin():
        sout_ref[...] = s_scr[...]


def _gdn_mix(pg, conv_buf, s0, p, chunk, t_valid, bb):
    b, t, _ = pg.shape
    assert b % bb == 0 and t % chunk == 0
    n_chunks = t // chunk
    cbuf = jnp.pad(conv_buf, ((0, 0), (SUBLANES - (CONV_W - 1), 0), (0, 0)))
    const = lambda shape: pl.BlockSpec(shape, lambda i, j: (0,) * len(shape))
    return pl.pallas_call(
        functools.partial(_gdn_body, bb=bb, chunk=chunk, t_valid=t_valid, n_chunks=n_chunks),
        grid=(b // bb, n_chunks),
        in_specs=[
            pl.BlockSpec((bb, chunk, G_PROJ_PAD), lambda i, j: (i, j, 0)),
            pl.BlockSpec((bb, SUBLANES, GDN_CONV_DIM), lambda i, j: (i, 0, 0)),
            pl.BlockSpec((bb, H_G, D_G, D_G), lambda i, j: (i, 0, 0, 0)),
            const((CONV_W, GDN_CONV_DIM)), const((1, LANES)), const((1, LANES)), const((1, D_G)),
        ],
        out_specs=[
            pl.BlockSpec((bb, chunk, W_GDN), lambda i, j: (i, j, 0)),
            pl.BlockSpec((bb, H_G, D_G, D_G), lambda i, j: (i, 0, 0, 0)),
        ],
        out_shape=[jax.ShapeDtypeStruct((b, t, W_GDN), BF16), jax.ShapeDtypeStruct((b, H_G, D_G, D_G), F32)],
        scratch_shapes=[pltpu.VMEM((bb, H_G, D_G, D_G), F32),
                        pltpu.VMEM((bb, SUBLANES + chunk, GDN_CONV_DIM), F32)],
        compiler_params=pltpu.CompilerParams(dimension_semantics=("arbitrary", "arbitrary"),
                                             vmem_limit_bytes=VMEM_LIMIT),
        name="gdn_mix",
    )(pg, cbuf, s0, p['conv_w'], p['a_log'], p['dt_bias'], p['gdn_norm_w'])


def _out_q_body(x_ref, yr_ref, yg_ref, wor_ref, wog_ref, gn_ref, wq_ref, x1_ref, q_ref):
    x1 = x_ref[...] + _dot(yr_ref[...], wor_ref[...]) + _dot(yg_ref[...], wog_ref[...])
    x1_ref[...] = x1
    q_ref[...] = _dot(_rmsnorm(x1, gn_ref[...]).astype(BF16), wq_ref[...]).astype(q_ref.dtype)


def _out_q(x, yr, yg, p, tm):
    n, d = x.shape
    assert n % tm == 0
    row = lambda w: pl.BlockSpec((tm, w), lambda i: (i, 0))
    const = lambda shape: pl.BlockSpec(shape, lambda i: (0, 0))
    return pl.pallas_call(
        _out_q_body,
        grid=(n // tm,),
        in_specs=[row(d), row(W_RWKV), row(W_GDN), const((W_RWKV, d)), const((W_GDN, d)), const((1, d)),
                  const((d, d))],
        out_specs=[row(d), row(d)],
        out_shape=[jax.ShapeDtypeStruct((n, d), F32), jax.ShapeDtypeStruct((n, d), BF16)],
        compiler_params=pltpu.CompilerParams(dimension_semantics=("parallel",), vmem_limit_bytes=VMEM_LIMIT),
        name="out_q",
    )(x, yr, yg, p['w_out_r'], p['w_out_g'], p['norm_cross'], p['wq'])


def _attn_body(q_ref, x_ref, mk_ref, mv_ref, wo_ref, gn_ref, rw_ref, rb_ref, x2_ref, h_ref, ti_ref, gt_ref,
               *, bb, tq):
    for i in range(bb):
        q = q_ref[i]
        heads = []
        for hh in range(H_X):
            sl = slice(hh * D_X, (hh + 1) * D_X)
            s = _dot_nt(q[:, sl], mk_ref[i, :, sl].astype(BF16)) * (D_X ** -0.5)
            e = jnp.exp(s - jnp.max(s, axis=-1, keepdims=True))
            prob = e / jnp.sum(e, axis=-1, keepdims=True)
            heads.append(_dot(prob.astype(BF16), mv_ref[i, :, sl].astype(BF16)))
        o = jnp.concatenate(heads, axis=-1).astype(BF16)
        x2 = x_ref[i] + _dot(o, wo_ref[...])
        x2_ref[i] = x2
        h = _rmsnorm(x2, gn_ref[...])
        for sub in range(SUBLANES):
            h_ref[pl.ds(i * tq * SUBLANES + sub, tq, stride=SUBLANES), :] = h[:, sub * LANES:(sub + 1) * LANES]
        h_hi, h_lo = _split2(h)
        pieces = _dot(jnp.concatenate([h_hi, h_lo], axis=0), rw_ref[...])
        logits = ((pieces[:tq, :LANES] + pieces[:tq, LANES:]) + (pieces[tq:, :LANES] + pieces[tq:, LANES:])
                  + rb_ref[...])
        lane = lax.broadcasted_iota(jnp.int32, logits.shape, 1)
        vals, idxs = [], []
        for _ in range(TOP_K):
            m = jnp.max(logits, axis=-1, keepdims=True)
            first = jnp.min(jnp.where(logits == m, lane, LANES), axis=-1, keepdims=True)
            vals.append(m)
            idxs.append(first)
            logits = jnp.where(lane == first, -jnp.inf, logits)
        es = [jnp.exp(vv - vals[0]) for vv in vals]
        den = es[0] + es[1] + es[2] + es[3]
        ti = jnp.zeros(lane.shape, jnp.int32)
        gt = jnp.zeros(lane.shape, F32)
        for j in range(TOP_K):
            ti = jnp.where(lane == j, idxs[j], ti)
            gt = jnp.where(lane == j, es[j] / den, gt)
        ti_ref[i] = ti
        gt_ref[i] = gt


def _attn_route(q, x1, mk, mv, p, bb, tq):
    b, t, d = x1.shape
    assert d == SUBLANES * LANES
    rows = bb * tq
    assert b % bb == 0 and t % tq == 0 and (bb == 1 or tq == t)
    n_tq = t // tq
    blk = lambda w: pl.BlockSpec((bb, tq, w), lambda i, j: (i, j, 0))
    mem = pl.BlockSpec((bb, N_MEM, d), lambda i, j: (i, 0, 0))
    const = lambda shape: pl.BlockSpec(shape, lambda i, j: (0, 0))
    return pl.pallas_call(
        functools.partial(_attn_body, bb=bb, tq=tq),
        grid=(b // bb, n_tq),
        in_specs=[blk(d), blk(d), mem, mem, const((d, d)), const((1, d)), const((d, 2 * LANES)), const((1, LANES))],
        out_specs=[blk(d), pl.BlockSpec((rows * SUBLANES, LANES), lambda i, j: (i * n_tq + j, 0)), blk(LANES),
                   blk(LANES)],
        out_shape=[jax.ShapeDtypeStruct((b, t, d), F32), jax.ShapeDtypeStruct((b * t * SUBLANES, LANES), F32),
                   jax.ShapeDtypeStruct((b, t, LANES), jnp.int32), jax.ShapeDtypeStruct((b, t, LANES), F32)],
        compiler_params=pltpu.CompilerParams(dimension_semantics=("parallel", "parallel"),
                                             vmem_limit_bytes=VMEM_LIMIT),
        name="attn_route",
    )(q, x1, mk, mv, p['wo'], p['norm_ffn'], p['router_w'], p['router_b'])


def _moe_body(be_ref, gnext_ref, sprev_ref, h_hbm, w1_ref, b1_ref, w2_ref, b2_ref, out_hbm,
              xb0, xb1, yb0, yb1, w1b, w2b, gsem, ssem, *, tm, n_steps):
    k = pl.program_id(0)
    xbufs, ybufs = (xb0, xb1), (yb0, yb1)

    def row(ref, at):
        return ref.at[pl.ds(at, SUBLANES)]

    def gather_all(slot):
        return pltpu.make_async_copy(h_hbm.at[pl.ds(0, tm * SUBLANES)], xbufs[slot], gsem.at[slot])

    def scatter_all(slot):
        return pltpu.make_async_copy(ybufs[slot], out_hbm.at[pl.ds(0, tm * SUBLANES)], ssem.at[slot])

    @pl.when(k == 0)
    def _prologue():
        yb0[...] = jnp.zeros(yb0.shape, F32)
        yb1[...] = jnp.zeros(yb1.shape, F32)
        for r in range(tm):
            pltpu.make_async_copy(row(h_hbm, 0), row(xb0, r * SUBLANES), gsem.at[0]).start()

    @pl.when(jnp.logical_or(k == 0, be_ref[k] != be_ref[jnp.maximum(k - 1, 0)]))
    def _cast():
        w1b[...] = w1_ref[0].astype(BF16)
        w2b[...] = w2_ref[0].astype(BF16)

    def phase(cur):
        nxt = 1 - cur
        gather_all(cur).wait()

        @pl.when(k >= 1)
        def _():
            scatter_all(cur).wait()

        for r in range(tm):
            src = pl.multiple_of(gnext_ref[0, 0, r], SUBLANES)
            dst = pl.multiple_of(sprev_ref[0, 0, r], SUBLANES)
            pltpu.make_async_copy(row(h_hbm, src), row(xbufs[nxt], r * SUBLANES), gsem.at[nxt]).start()
            pltpu.make_async_copy(row(ybufs[nxt], r * SUBLANES), row(out_hbm, dst), ssem.at[nxt]).start(priority=1)
        x = jnp.concatenate([xbufs[cur][pl.ds(sub, tm, stride=SUBLANES), :] for sub in range(SUBLANES)], axis=-1)
        hc = _dot(x.astype(BF16), w1b[...]) + b1_ref[0]
        hg = jnp.minimum(hc[:, :D_FF], SWIGLU_LIMIT)
        hl = jnp.clip(hc[:, D_FF:], -SWIGLU_LIMIT, SWIGLU_LIMIT)
        act = hg * _sigmoid(SWIGLU_ALPHA * hg) * (hl + 1.0)
        y = _dot(act.astype(BF16), w2b[...]) + b2_ref[0]
        for sub in range(SUBLANES):
            ybufs[cur][pl.ds(sub, tm, stride=SUBLANES), :] = y[:, sub * LANES:(sub + 1) * LANES]

        @pl.when(k == n_steps - 1)
        def _epilogue():
            gather_all(nxt).wait()
            scatter_all(nxt).wait()

    for parity in range(2):
        pl.when(lax.rem(k, 2) == parity)(functools.partial(phase, parity))


def _moe(h, gtok, sdst, block_e, p, tm):
    n, d = h.shape[0] // SUBLANES, D_MODEL
    n_steps = block_e.shape[0]
    grid_spec = pltpu.PrefetchScalarGridSpec(
        num_scalar_prefetch=1,
        grid=(n_steps,),
        in_specs=[
            pl.BlockSpec((1, 1, tm), lambda k, be: (jnp.minimum(k + 1, n_steps - 1), 0, 0),
                         memory_space=pltpu.SMEM),
            pl.BlockSpec((1, 1, tm), lambda k, be: (jnp.maximum(k - 1, 0), 0, 0), memory_space=pltpu.SMEM),
            pl.BlockSpec(memory_space=pl.ANY),
            pl.BlockSpec((1, d, 2 * D_FF), lambda k, be: (be[k], 0, 0)),
            pl.BlockSpec((1, 1, 2 * D_FF), lambda k, be: (be[k], 0, 0)),
            pl.BlockSpec((1, D_FF, d), lambda k, be: (be[k], 0, 0)),
            pl.BlockSpec((1, 1, d), lambda k, be: (be[k], 0, 0)),
        ],
        out_specs=pl.BlockSpec(memory_space=pl.ANY),
        scratch_shapes=[
            pltpu.VMEM((tm * SUBLANES, LANES), F32), pltpu.VMEM((tm * SUBLANES, LANES), F32),
            pltpu.VMEM((tm * SUBLANES, LANES), F32), pltpu.VMEM((tm * SUBLANES, LANES), F32),
            pltpu.VMEM((d, 2 * D_FF), BF16), pltpu.VMEM((D_FF, d), BF16),
            pltpu.SemaphoreType.DMA((2,)), pltpu.SemaphoreType.DMA((2,)),
        ],
    )
    return pl.pallas_call(
        functools.partial(_moe_body, tm=tm, n_steps=n_steps),
        grid_spec=grid_spec,
        out_shape=jax.ShapeDtypeStruct(((n * TOP_K + tm) * SUBLANES, LANES), F32),
        compiler_params=pltpu.CompilerParams(dimension_semantics=("arbitrary",), vmem_limit_bytes=VMEM_LIMIT),
        name="moe_experts",
    )(block_e, gtok.reshape(n_steps, 1, tm), sdst.reshape(n_steps, 1, tm), h, p['w1_e'], p['b1_e'], p['w2_e'],
      p['b2_e'])


def _route_plan(top_i, tm):
    n = top_i.shape[0]
    na = n * TOP_K
    flat_e = top_i.reshape(na)
    order = jnp.argsort(flat_e).astype(jnp.int32)
    counts = jnp.sum((flat_e[:, None] == jnp.arange(N_EXPERTS, dtype=jnp.int32)[None, :]).astype(jnp.int32), axis=0)
    padded = (counts + tm - 1) // tm * tm
    starts = jnp.cumsum(counts) - counts
    pends = jnp.cumsum(padded)
    pstarts = pends - padded
    n_steps = -(-(na + N_EXPERTS * (tm - 1)) // tm) + 2
    blk_start = (jnp.arange(n_steps, dtype=jnp.int32) - 1) * tm
    block_e = jnp.sum((blk_start[:, None] >= pends[None, :]).astype(jnp.int32), axis=1)
    block_e = jnp.minimum(block_e, N_EXPERTS - 1)
    block_e = block_e.at[0].set(block_e[1])
    lane = jnp.arange(tm, dtype=jnp.int32)[None, :]
    local = blk_start[:, None] + lane - pstarts[block_e][:, None]
    valid = (local >= 0) & (local < counts[block_e][:, None]) & (blk_start[:, None] >= 0)
    asg = order[jnp.clip(starts[block_e][:, None] + local, 0, na - 1)]
    gtok = jnp.where(valid, asg // TOP_K, 0)
    sdst = jnp.where(valid, (asg % TOP_K) * n + asg // TOP_K, TOP_K * n + lane)
    return gtok * SUBLANES, sdst * SUBLANES, block_e


def _combine_body(x_ref, gt_ref, gn_ref, *refs, final):
    slot_refs, y_ref = refs[:TOP_K], refs[TOP_K]
    x = x_ref[...]
    gt = gt_ref[...]
    tm = x.shape[0]
    for j in range(TOP_K):
        slot = jnp.concatenate([slot_refs[j][pl.ds(sub, tm, stride=SUBLANES), :] for sub in range(SUBLANES)],
                               axis=-1)
        x = x + gt[:, j:j + 1] * slot
    y_ref[...] = _rmsnorm(x, gn_ref[...]) if final else x


def _combine(x2, gates, slots, gn, tm, n_all, row0, final):
    n, d = x2.shape
    assert n % tm == 0 and row0 % tm == 0 and n_all % tm == 0
    slot_spec = lambda j: pl.BlockSpec((tm * SUBLANES, LANES), lambda i: ((j * n_all + row0) // tm + i, 0))
    return pl.pallas_call(
        functools.partial(_combine_body, final=final),
        grid=(n // tm,),
        in_specs=[pl.BlockSpec((tm, d), lambda i: (i, 0)), pl.BlockSpec((tm, LANES), lambda i: (i, 0)),
                  pl.BlockSpec((1, d), lambda i: (0, 0))] + [slot_spec(j) for j in range(TOP_K)],
        out_specs=pl.BlockSpec((tm, d), lambda i: (i, 0)),
        out_shape=jax.ShapeDtypeStruct((n, d), F32),
        compiler_params=pltpu.CompilerParams(dimension_semantics=("parallel",), vmem_limit_bytes=VMEM_LIMIT),
        name="combine",
    )(x2, gates, gn, *([slots] * TOP_K))


def _layer_params(l, norm_mix, w_in, mu_shift, w0, w2_decay, a0, a2_iclr, g2_gate, k_k, k_a, r_k, lnx_w, lnx_b,
                  conv_w, a_log, dt_bias, gdn_norm_w, w_out, norm_cross, norm_mem, wq_x, wk_x, wv_x, wo_x,
                  norm_ffn, router_w, router_b, w1_e, b1_e, w2_e, b2_e):
    row = lambda z: z.reshape(1, -1).astype(F32)
    lane_pad = lambda z, at: jnp.zeros((1, LANES), F32).at[0, at:at + z.shape[0]].set(z)
    return {
        'norm_mix': row(norm_mix[l]),
        'w_in_r': w_in[l][:, :R_PROJ].astype(BF16),
        'w_in_g': jnp.pad(w_in[l][:, R_PROJ:], ((0, 0), (0, G_PROJ_PAD - G_PROJ))).astype(BF16),
        'mu': row(mu_shift[l]), 'w0': row(w0[l]), 'a0': row(a0[l]),
        'w2': jnp.pad(w2_decay[l], ((0, LORA_A), (0, 0))).astype(BF16),
        'a2': jnp.pad(a2_iclr[l], ((LORA_W, 0), (0, 0))).astype(BF16),
        'g2': g2_gate[l].astype(BF16),
        'k_k': row(k_k[l]), 'k_a': row(k_a[l]), 'r_k': row(r_k[l]), 'lnx_w': row(lnx_w[l]), 'lnx_b': row(lnx_b[l]),
        'conv_w': conv_w[l].astype(F32),
        'a_log': lane_pad(a_log[l], H_G), 'dt_bias': lane_pad(dt_bias[l], H_G),
        'gdn_norm_w': row(gdn_norm_w[l]),
        'w_out_r': w_out[l][:W_RWKV].astype(BF16), 'w_out_g': w_out[l][W_RWKV:].astype(BF16),
        'norm_cross': row(norm_cross[l]), 'norm_mem': row(norm_mem[l]),
        'wq': wq_x[l].astype(BF16), 'wk': wk_x[l].astype(BF16), 'wv': wv_x[l].astype(BF16),
        'wo': wo_x[l].astype(BF16),
        'norm_ffn': row(norm_ffn[l]),
        'router_w': jnp.concatenate(_split2(jnp.pad(router_w[l].astype(F32), ((0, 0), (0, LANES - N_EXPERTS)))),
                                    axis=1),
        'router_b': jnp.full((1, LANES), NEG_BIG, F32).at[0, :N_EXPERTS].set(router_b[l].astype(F32)),
        'w1_e': w1_e[l], 'b1_e': b1_e[l][:, None, :], 'w2_e': w2_e[l], 'b2_e': b2_e[l][:, None, :],
    }


def _mix_and_attend(x, mk, mv, shift_prev, s_r, conv_buf, s_g, p, *, chunk, mix_bb, tm, bb, tq):
    b, t, d = x.shape
    assert t >= CONV_W - 1
    pr, pg = _norm_proj(x.reshape(b * t, d), p['norm_mix'], [p['w_in_r'], p['w_in_g']], [F32, F32], tm)
    pr = pr.reshape(b, t, R_PROJ)
    pg = pg.reshape(b, t, G_PROJ_PAD)
    shift_new = pr[:, t - 1]
    conv_new = pg[:, t - (CONV_W - 1):, :GDN_CONV_DIM]
    t_pad = -(-t // chunk) * chunk
    if t_pad != t:
        pr = jnp.pad(pr, ((0, 0), (0, t_pad - t), (0, 0)))
        pg = jnp.pad(pg, ((0, 0), (0, t_pad - t), (0, 0)))
    t_valid = chunk if t_pad == t else t
    y_r, s_r_new = _rwkv_mix(pr, shift_prev, s_r, p, chunk, t_valid, mix_bb)
    y_g, s_g_new = _gdn_mix(pg, conv_buf, s_g, p, chunk, t_valid, mix_bb)
    if t_pad != t:
        y_r, y_g = y_r[:, :t], y_g[:, :t]
    x1, q = _out_q(x.reshape(b * t, d), y_r.reshape(b * t, W_RWKV), y_g.reshape(b * t, W_GDN), p, tm)
    x2, h, top_i, gates = _attn_route(q.reshape(b, t, d), x1.reshape(b, t, d), mk, mv, p, bb, tq)
    return (x2.reshape(b * t, d), h, top_i.reshape(b * t, LANES), gates.reshape(b * t, LANES),
            shift_new, s_r_new, conv_new, s_g_new)


def kernel(x_prompt, x_sample, mem_prompt, state_rwkv, state_rwkv_shift, state_gdn, state_gdn_conv, cache_mem_k, cache_mem_v, norm_mix, w_in, mu_shift, w0, w2_decay, a0, a2_iclr, g2_gate, k_k, k_a, r_k, lnx_w, lnx_b, conv_w, a_log, dt_bias, gdn_norm_w, w_out, norm_cross, norm_mem, wq_x, wk_x, wv_x, wo_x, norm_ffn, router_w, router_b, w1_e, b1_e, w2_e, b2_e, final_norm):
    bp, tp, d = x_prompt.shape
    bs, ts, _ = x_sample.shape
    depth = w_in.shape[0]
    np_, ns = bp * tp, bs * ts
    xp, xs = x_prompt, x_sample
    outs = [[] for _ in range(10)]
    for l in range(depth):
        p = _layer_params(l, norm_mix, w_in, mu_shift, w0, w2_decay, a0, a2_iclr, g2_gate, k_k, k_a, r_k, lnx_w,
                          lnx_b, conv_w, a_log, dt_bias, gdn_norm_w, w_out, norm_cross, norm_mem, wq_x, wk_x, wv_x,
                          wo_x, norm_ffn, router_w, router_b, w1_e, b1_e, w2_e, b2_e)
        n_mem = mem_prompt.shape[1]
        mk, mv = _norm_proj(mem_prompt.reshape(bp * n_mem, d), p['norm_mem'], [p['wk'], p['wv']], [F32, F32], 256)
        mk, mv = mk.reshape(bp, n_mem, d), mv.reshape(bp, n_mem, d)
        res_p = _mix_and_attend(
            xp, mk, mv, jnp.zeros((bp, R_PROJ), F32), jnp.zeros((bp, H_R, N_R, N_R), F32),
            jnp.zeros((bp, CONV_W - 1, GDN_CONV_DIM), F32), jnp.zeros((bp, H_G, D_G, D_G), F32), p,
            chunk=MIX_CHUNK, mix_bb=2, tm=256, bb=1, tq=512)
        mk_s = cache_mem_k[l].astype(BF16).reshape(bs, n_mem, d)
        mv_s = cache_mem_v[l].astype(BF16).reshape(bs, n_mem, d)
        res_s = _mix_and_attend(
            xs, mk_s, mv_s, state_rwkv_shift[l], state_rwkv[l], state_gdn_conv[l],
            state_gdn[l], p, chunk=SUBLANES, mix_bb=4, tm=256, bb=8, tq=ts)
        h = jnp.concatenate([res_p[1], res_s[1]], axis=0)
        top_i = jnp.concatenate([res_p[2], res_s[2]], axis=0)[:, :TOP_K]
        gtok, sdst, block_e = _route_plan(top_i, MOE_ROWS)
        slots = _moe(h, gtok, sdst, block_e, p, MOE_ROWS)
        gn = final_norm.reshape(1, d).astype(F32)
        last = l == depth - 1
        xp = _combine(res_p[0], res_p[3], slots, gn, 256, np_ + ns, 0, last).reshape(bp, tp, d)
        xs = _combine(res_s[0], res_s[3], slots, gn, 256, np_ + ns, np_, last).reshape(bs, ts, d)
        new = [res_p[5], res_p[4], res_p[7], res_p[6], mk.reshape(bp, n_mem, H_X, D_X),
               mv.reshape(bp, n_mem, H_X, D_X), res_s[5], res_s[4], res_s[7], res_s[6]]
        for acc, val in zip(outs, new):
            acc.append(val)
    return (xp, xs) + tuple(jnp.stack(o) for o in outs)
```

```python
import functools

import jax
import jax.numpy as jnp
from jax import lax
from jax.experimental import pallas as pl
from jax.experimental.pallas import tpu as pltpu

F32 = jnp.float32
BF16 = jnp.bfloat16
DEFAULT = lax.Precision.DEFAULT

D_MODEL = 1024
W_RWKV = 512
N_R = 64
H_R = W_RWKV // N_R
LORA_W = 64
LORA_A = 64
LORA_G = 128
R_PROJ = 3 * W_RWKV + LORA_W + LORA_A + LORA_G
GN_EPS = 64e-5
W_GDN = 512
D_G = 128
H_G = W_GDN // D_G
GDN_CONV_DIM = 3 * W_GDN
CONV_W = 4
G_PROJ = GDN_CONV_DIM + W_GDN + 2 * H_G
LANES = 128
SUBLANES = 8
G_PROJ_PAD = GDN_CONV_DIM + W_GDN + LANES
MIX_CHUNK = 64
N_MEM = 256
H_X = 4
D_X = D_MODEL // H_X
N_EXPERTS = 32
TOP_K = 4
D_FF = D_MODEL
SWIGLU_LIMIT = 7.0
SWIGLU_ALPHA = 1.702
MOE_ROWS = 512
RMS_EPS = 1e-6
L2_EPS = 1e-6
NEG_BIG = -1e30
VMEM_LIMIT = 56 * 1024 * 1024


def _dot(a, b, precision=DEFAULT):
    return jnp.dot(a, b, preferred_element_type=F32, precision=precision)


def _dot_nt(a, b, precision=DEFAULT):
    return lax.dot_general(a, b, (((1,), (1,)), ((), ())), preferred_element_type=F32, precision=precision)


NN = (((1,), (0,)), ((), ()))
NT = (((1,), (1,)), ((), ()))
TN = (((0,), (0,)), ((), ()))


def _split2(x):
    hi = x.astype(BF16)
    return hi, (x - hi.astype(F32)).astype(BF16)


def _mm(a, b, dims):
    return lax.dot_general(a.astype(BF16), b.astype(BF16), dims, preferred_element_type=F32)


def _sel_mm(sel, x, dims):
    dg = lambda y: lax.dot_general(sel, y, dims, preferred_element_type=F32)
    hi = x.astype(BF16)
    r1 = x - hi.astype(F32)
    mid = r1.astype(BF16)
    lo = (r1 - mid.astype(F32)).astype(BF16)
    return dg(hi) + (dg(mid) + dg(lo))


def _sigmoid(x):
    return 1.0 / (1.0 + jnp.exp(-x))


def _softplus(x):
    return jnp.maximum(x, 0.0) + jnp.log(1.0 + jnp.exp(-jnp.abs(x)))


def _rmsnorm(x, g):
    return x * lax.rsqrt(jnp.mean(x * x, axis=-1, keepdims=True) + RMS_EPS) * g


def _tri_masks(c):
    row = lax.broadcasted_iota(jnp.int32, (c, c), 0)
    col = lax.broadcasted_iota(jnp.int32, (c, c), 1)
    return col <= row, col < row, (col == row).astype(F32)


def _unit_lower_inverses(ms, eye, c):
    ts = [eye + m for m in ms]
    ps = list(ms)
    covered = 2
    while covered < c:
        ps = [_mm(p, p, NN) for p in ps]
        ts = [t + _mm(t, p, NN) for t, p in zip(ts, ps)]
        covered *= 2
    return ts


def _norm_proj_body(x_ref, g_ref, *refs, n_out):
    w_refs, o_refs = refs[:n_out], refs[n_out:]
    hb = _rmsnorm(x_ref[...], g_ref[...]).astype(BF16)
    for w_ref, o_ref in zip(w_refs, o_refs):
        o_ref[...] = _dot(hb, w_ref[...]).astype(o_ref.dtype)


def _norm_proj(x, g, ws, out_dtypes, tm):
    n, d = x.shape
    assert n % tm == 0
    in_specs = [pl.BlockSpec((tm, d), lambda i: (i, 0)), pl.BlockSpec((1, d), lambda i: (0, 0))]
    in_specs += [pl.BlockSpec(w.shape, lambda i: (0, 0)) for w in ws]
    return pl.pallas_call(
        functools.partial(_norm_proj_body, n_out=len(ws)),
        grid=(n // tm,),
        in_specs=in_specs,
        out_specs=[pl.BlockSpec((tm, w.shape[1]), lambda i: (i, 0)) for w in ws],
        out_shape=[jax.ShapeDtypeStruct((n, w.shape[1]), dt) for w, dt in zip(ws, out_dtypes)],
        compiler_params=pltpu.CompilerParams(dimension_semantics=("parallel",), vmem_limit_bytes=VMEM_LIMIT),
        name="norm_proj",
    )(x, g, *ws)


def _rwkv_body(pr_ref, shift_ref, s0_ref, mu_ref, w0_ref, w2_ref, a0_ref, a2_ref, g2_ref, kk_ref, ka_ref,
               rk_ref, lnw_ref, lnb_ref, y_ref, sout_ref, s_scr, prev_scr, *, bb, chunk, t_valid, n_chunks):
    c = pl.program_id(1)

    @pl.when(c == 0)
    def _init():
        s_scr[...] = s0_ref[...]
        prev_scr[...] = shift_ref[...]

    incl, strict, eye = _tri_masks(chunk)
    incl_bf = incl.astype(BF16)
    rows = lax.broadcasted_iota(jnp.int32, (chunk, 1), 0)
    sls = [slice(h * N_R, (h + 1) * N_R) for h in range(H_R)]

    def per_batch(bi):
        pr = pr_ref[bi]
        prev = jnp.where(rows == 0, prev_scr[bi], pltpu.roll(pr, 1, 0))
        prev_scr[bi] = pr[chunk - 1:chunk, :]
        xm = pr + (prev - pr) * mu_ref[...]
        r = xm[:, :W_RWKV]
        k = xm[:, W_RWKV:2 * W_RWKV]
        v = xm[:, 2 * W_RWKV:3 * W_RWKV]
        lo = xm[:, 3 * W_RWKV:3 * W_RWKV + LORA_W + LORA_A]
        g_lo = xm[:, 3 * W_RWKV + LORA_W + LORA_A:]
        logw = -_softplus(-(w0_ref[...] + _dot(jnp.tanh(lo).astype(BF16), w2_ref[...]))) - 0.5
        wl = -jnp.exp(logw)
        a = _sigmoid(a0_ref[...] + _dot(lo.astype(BF16), a2_ref[...]))
        g = _dot(_sigmoid(g_lo).astype(BF16), g2_ref[...])
        kkv = k * kk_ref[...]
        k = k * (1.0 + (a - 1.0) * ka_ref[...])
        if t_valid < chunk:
            valid = rows < t_valid
            wl = jnp.where(valid, wl, 0.0)
            kkv = jnp.where(valid, kkv, 0.0)
            k = jnp.where(valid, k, 0.0)
            v = jnp.where(valid, v, 0.0)
        cum = _sel_mm(incl_bf, wl, NN)
        cum_last = cum[chunk - 1:chunk, :]
        return dict(r=r, k=k, v=v, a=a, g=g, kkv=kkv, w_incl=jnp.exp(cum), w_prev=jnp.exp(cum - wl),
                    w_inv=jnp.exp(-cum), w_tail=jnp.exp(cum_last - cum), w_last=jnp.exp(cum_last))

    pre = [per_batch(bi) for bi in range(bb)]
    chains = [(bi, h) for bi in range(bb) for h in range(H_R)]
    col = lambda name: [pre[bi][name][:, sls[h]] for bi, h in chains]
    n = range(len(chains))
    r_, k_, v_, a_ = col('r'), col('k'), col('v'), col('a')
    w_incl, w_prev, w_inv, w_tail, w_last = col('w_incl'), col('w_prev'), col('w_inv'), col('w_tail'), col('w_last')
    kks = [x * lax.rsqrt(jnp.sum(x * x, axis=-1, keepdims=True) + L2_EPS) for x in col('kkv')]
    a_hat = [-(kks[i] * w_prev[i]) for i in n]
    kka = [kks[i] * a_[i] for i in n]
    b_hat = [kka[i] * w_inv[i] for i in n]
    k_hat = [k_[i] * w_inv[i] for i in n]
    r_hat = [r_[i] * w_incl[i] for i in n]
    cross = [_mm(jnp.concatenate([a_hat[i], r_hat[i]], axis=0), jnp.concatenate([b_hat[i], k_hat[i]], axis=0), NT)
             for i in n]
    row2 = lax.broadcasted_iota(jnp.int32, (chunk, 2 * chunk), 0)
    col2 = lax.broadcasted_iota(jnp.int32, (chunk, 2 * chunk), 1)
    col2 = jnp.where(col2 >= chunk, col2 - chunk, col2)
    strict2, incl2 = col2 < row2, col2 <= row2
    m_top = [jnp.where(strict2, x[:chunk], 0.0) for x in cross]
    a_bot = [jnp.where(incl2, x[chunk:], 0.0) for x in cross]
    t_inv = _unit_lower_inverses([x[:, :chunk] for x in m_top], eye, chunk)
    w_hat = [_mm(t_inv[i], a_hat[i], NN) for i in n]
    mv = [_mm(m_top[i], jnp.concatenate([jnp.zeros_like(v_[i]), v_[i]], axis=0), NN) for i in n]
    u = [_mm(t_inv[i], mv[i], NN) for i in n]
    ss = [s_scr[bi, h] for bi, h in chains]
    p = [_mm(w_hat[i], ss[i], NT) + u[i] for i in n]
    pv = [jnp.concatenate([p[i], v_[i]], axis=0) for i in n]
    ys = [_mm(r_hat[i], ss[i], NT) + _mm(a_bot[i], pv[i], NN) for i in n]
    for i, (bi, h) in enumerate(chains):
        tails = jnp.concatenate([kka[i] * w_tail[i], k_[i] * w_tail[i]], axis=0)
        s_scr[bi, h] = ss[i] * w_last[i] + _mm(pv[i], tails, TN)
    for i, (bi, h) in enumerate(chains):
        sl = sls[h]
        y = ys[i]
        mean = jnp.mean(y, axis=-1, keepdims=True)
        yc = y - mean
        var = jnp.mean(yc * yc, axis=-1, keepdims=True)
        yn = yc * lax.rsqrt(var + GN_EPS) * lnw_ref[:, sl] + lnb_ref[:, sl]
        bonus = jnp.sum(r_[i] * k_[i] * rk_ref[:, sl], axis=-1, keepdims=True) * v_[i]
        y_ref[bi, :, sl] = ((yn + bonus) * pre[bi]['g'][:, sl]).astype(y_ref.dtype)

    @pl.when(c == n_chunks - 1)
    def _fin():
        sout_ref[...] = s_scr[...]


def _rwkv_mix(pr, shift_prev, s0, p, chunk, t_valid, bb):
    b, t, _ = pr.shape
    assert b % bb == 0 and t % chunk == 0
    n_chunks = t // chunk
    const = lambda shape: pl.BlockSpec(shape, lambda i, j: (0,) * len(shape))
    return pl.pallas_call(
        functools.partial(_rwkv_body, bb=bb, chunk=chunk, t_valid=t_valid, n_chunks=n_chunks),
        grid=(b // bb, n_chunks),
        in_specs=[
            pl.BlockSpec((bb, chunk, R_PROJ), lambda i, j: (i, j, 0)),
            pl.BlockSpec((bb, 1, R_PROJ), lambda i, j: (i, 0, 0)),
            pl.BlockSpec((bb, H_R, N_R, N_R), lambda i, j: (i, 0, 0, 0)),
            const((1, R_PROJ)), const((1, W_RWKV)), const((LANES, W_RWKV)), const((1, W_RWKV)),
            const((LANES, W_RWKV)), const((LORA_G, W_RWKV)), const((1, W_RWKV)), const((1, W_RWKV)),
            const((1, W_RWKV)), const((1, W_RWKV)), const((1, W_RWKV)),
        ],
        out_specs=[
            pl.BlockSpec((bb, chunk, W_RWKV), lambda i, j: (i, j, 0)),
            pl.BlockSpec((bb, H_R, N_R, N_R), lambda i, j: (i, 0, 0, 0)),
        ],
        out_shape=[jax.ShapeDtypeStruct((b, t, W_RWKV), BF16), jax.ShapeDtypeStruct((b, H_R, N_R, N_R), F32)],
        scratch_shapes=[pltpu.VMEM((bb, H_R, N_R, N_R), F32), pltpu.VMEM((bb, 1, R_PROJ), F32)],
        compiler_params=pltpu.CompilerParams(dimension_semantics=("arbitrary", "arbitrary"),
                                             vmem_limit_bytes=VMEM_LIMIT),
        name="rwkv_mix",
    )(pr, shift_prev[:, None, :], s0, p['mu'], p['w0'], p['w2'], p['a0'], p['a2'], p['g2'], p['k_k'], p['k_a'],
      p['r_k'], p['lnx_w'], p['lnx_b'])


def _gdn_body(pg_ref, cbuf_ref, s0_ref, cw_ref, alog_ref, dtb_ref, nw_ref, y_ref, sout_ref, s_scr, xp_scr,
              *, bb, chunk, t_valid, n_chunks):
    c = pl.program_id(1)

    @pl.when(c == 0)
    def _init():
        s_scr[...] = s0_ref[...]
        xp_scr[:, 0:SUBLANES, :] = cbuf_ref[...]

    incl, strict, eye = _tri_masks(chunk)
    incl_bf = incl.astype(BF16)
    lane = lax.broadcasted_iota(jnp.int32, (chunk, LANES), 1)
    sls = [slice(h * D_G, (h + 1) * D_G) for h in range(H_G)]

    def per_batch(bi):
        xp_scr[bi, SUBLANES:SUBLANES + chunk, :] = pg_ref[bi, :, :GDN_CONV_DIM]
        base = SUBLANES - (CONV_W - 1)
        conv = xp_scr[bi, base:base + chunk, :] * cw_ref[0:1, :]
        for j in range(1, CONV_W):
            conv = conv + xp_scr[bi, base + j:base + j + chunk, :] * cw_ref[j:j + 1, :]
        xp_scr[bi, 0:SUBLANES, :] = xp_scr[bi, chunk:chunk + SUBLANES, :]
        qkv = conv * _sigmoid(conv)
        z = pg_ref[bi, :, GDN_CONV_DIM:GDN_CONV_DIM + W_GDN]
        ba = pg_ref[bi, :, GDN_CONV_DIM + W_GDN:]
        beta_blk = _sigmoid(ba)
        g_blk = -jnp.exp(alog_ref[...]) * _softplus(ba + dtb_ref[...])
        if t_valid < chunk:
            valid = lax.broadcasted_iota(jnp.int32, (chunk, 1), 0) < t_valid
            beta_blk = jnp.where(valid, beta_blk, 0.0)
            g_blk = jnp.where(valid, g_blk, 0.0)
        gc_blk = _sel_mm(incl_bf, g_blk, NN)
        return dict(qkv=qkv, z=z, beta_blk=beta_blk, gc_blk=gc_blk)

    pre = [per_batch(bi) for bi in range(bb)]
    chains = [(bi, h) for bi in range(bb) for h in range(H_G)]
    n = range(len(chains))
    qs = [pre[bi]['qkv'][:, sls[h]] for bi, h in chains]
    qs = [x * lax.rsqrt(jnp.sum(x * x, axis=-1, keepdims=True) + L2_EPS) * (D_G ** -0.5) for x in qs]
    ks = [pre[bi]['qkv'][:, W_GDN + h * D_G:W_GDN + (h + 1) * D_G] for bi, h in chains]
    ks = [x * lax.rsqrt(jnp.sum(x * x, axis=-1, keepdims=True) + L2_EPS) for x in ks]
    vs = [pre[bi]['qkv'][:, 2 * W_GDN + h * D_G:2 * W_GDN + (h + 1) * D_G] for bi, h in chains]
    betas = [pre[bi]['beta_blk'][:, h:h + 1] for bi, h in chains]
    gcols = [pre[bi]['gc_blk'][:, H_G + h:H_G + h + 1] for bi, h in chains]
    grows = [_sel_mm((lane == H_G + h).astype(BF16), pre[bi]['gc_blk'], NT) for bi, h in chains]
    g_last = [gcols[i][chunk - 1:chunk, :] for i in n]
    decay = [jnp.where(incl, jnp.exp(jnp.where(incl, gcols[i] - grows[i], 0.0)), 0.0) for i in n]
    k_beta = [ks[i] * betas[i] for i in n]
    cross = [_mm(jnp.concatenate([k_beta[i], qs[i]], axis=0), ks[i], NT) for i in n]
    lmat = [jnp.where(strict, cross[i][:chunk] * decay[i], 0.0) for i in n]
    attn = [jnp.where(incl, cross[i][chunk:] * decay[i], 0.0) for i in n]
    t_inv = _unit_lower_inverses([-x for x in lmat], eye, chunk)
    e_gc = [jnp.exp(gcols[i]) for i in n]
    uw = [_mm(t_inv[i], jnp.concatenate([vs[i] * betas[i], k_beta[i] * e_gc[i]], axis=1), NN) for i in n]
    ss = [s_scr[bi, h] for bi, h in chains]
    v_new = [uw[i][:, :D_G] - _mm(uw[i][:, D_G:], ss[i], NN) for i in n]
    os_ = [_mm(jnp.concatenate([qs[i] * e_gc[i], attn[i]], axis=1), jnp.concatenate([ss[i], v_new[i]], axis=0), NN)
           for i in n]
    for i, (bi, h) in enumerate(chains):
        s_scr[bi, h] = ss[i] * jnp.exp(g_last[i]) + _mm(ks[i] * jnp.exp(g_last[i] - gcols[i]), v_new[i], TN)
    for i, (bi, h) in enumerate(chains):
        o = os_[i]
        o = o * lax.rsqrt(jnp.mean(o * o, axis=-1, keepdims=True) + RMS_EPS) * nw_ref[...]
        z_h = pre[bi]['z'][:, sls[h]]
        y_ref[bi, :, sls[h]] = (o * (z_h * _sigmoid(z_h))).astype(y_ref.dtype)

    @pl.when(c == n_chunks - 1)
    def _fin():
        sout_ref[...] = s_scr[...]


def _gdn_mix(pg, conv_buf, s0, p, chunk, t_valid, bb):
    b, t, _ = pg.shape
    assert b % bb == 0 and t % chunk == 0
    n_chunks = t // chunk
    cbuf = jnp.pad(conv_buf, ((0, 0), (SUBLANES - (CONV_W - 1), 0), (0, 0)))
    const = lambda shape: pl.BlockSpec(shape, lambda i, j: (0,) * len(shape))
    return pl.pallas_call(
        functools.partial(_gdn_body, bb=bb, chunk=chunk, t_valid=t_valid, n_chunks=n_chunks),
        grid=(b // bb, n_chunks),
        in_specs=[
            pl.BlockSpec((bb, chunk, G_PROJ_PAD), lambda i, j: (i, j, 0)),
            pl.BlockSpec((bb, SUBLANES, GDN_CONV_DIM), lambda i, j: (i, 0, 0)),
            pl.BlockSpec((bb, H_G, D_G, D_G), lambda i, j: (i, 0, 0, 0)),
            const((CONV_W, GDN_CONV_DIM)), const((1, LANES)), const((1, LANES)), const((1, D_G)),
        ],
        out_specs=[
            pl.BlockSpec((bb, chunk, W_GDN), lambda i, j: (i, j, 0)),
            pl.BlockSpec((bb, H_G, D_G, D_G), lambda i, j: (i, 0, 0, 0)),
        ],
        out_shape=[jax.ShapeDtypeStruct((b, t, W_GDN), BF16), jax.ShapeDtypeStruct((b, H_G, D_G, D_G), F32)],
        scratch_shapes=[pltpu.VMEM((bb, H_G, D_G, D_G), F32),
                        pltpu.VMEM((bb, SUBLANES + chunk, GDN_CONV_DIM), F32)],
        compiler_params=pltpu.CompilerParams(dimension_semantics=("arbitrary", "arbitrary"),
                                             vmem_limit_bytes=VMEM_LIMIT),
        name="gdn_mix",
    )(pg, cbuf, s0, p['conv_w'], p['a_log'], p['dt_bias'], p['gdn_norm_w'])


def _out_q_body(x_ref, yr_ref, yg_ref, wor_ref, wog_ref, gn_ref, wq_ref, x1_ref, q_ref):
    x1 = x_ref[...] + _dot(yr_ref[...], wor_ref[...]) + _dot(yg_ref[...], wog_ref[...])
    x1_ref[...] = x1
    q_ref[...] = _dot(_rmsnorm(x1, gn_ref[...]).astype(BF16), wq_ref[...]).astype(q_ref.dtype)


def _out_q(x, yr, yg, p, tm):
    n, d = x.shape
    assert n % tm == 0
    row = lambda w: pl.BlockSpec((tm, w), lambda i: (i, 0))
    const = lambda shape: pl.BlockSpec(shape, lambda i: (0, 0))
    return pl.pallas_call(
        _out_q_body,
        grid=(n // tm,),
        in_specs=[row(d), row(W_RWKV), row(W_GDN), const((W_RWKV, d)), const((W_GDN, d)), const((1, d)),
                  const((d, d))],
        out_specs=[row(d), row(d)],
        out_shape=[jax.ShapeDtypeStruct((n, d), F32), jax.ShapeDtypeStruct((n, d), BF16)],
        compiler_params=pltpu.CompilerParams(dimension_semantics=("parallel",), vmem_limit_bytes=VMEM_LIMIT),
        name="out_q",
    )(x, yr, yg, p['w_out_r'], p['w_out_g'], p['norm_cross'], p['wq'])


def _attn_body(q_ref, x_ref, mk_ref, mv_ref, wo_ref, gn_ref, rw_ref, rb_ref, x2_ref, h_ref, ti_ref, gt_ref,
               *, bb, tq, head_rows):
    d_tiles = D_X // LANES
    mem_rows = N_MEM * d_tiles * H_X

    def head_mem(ref, i, hh):
        if not head_rows:
            return ref[i, :, hh * D_X:(hh + 1) * D_X]
        return jnp.concatenate(
            [ref[pl.ds(i * mem_rows + dt * H_X + hh, N_MEM, stride=d_tiles * H_X), :] for dt in range(d_tiles)],
            axis=-1)

    for i in range(bb):
        q = q_ref[i]
        heads = []
        for hh in range(H_X):
            sl = slice(hh * D_X, (hh + 1) * D_X)
            s = _dot_nt(q[:, sl], head_mem(mk_ref, i, hh).astype(BF16)) * (D_X ** -0.5)
            e = jnp.exp(s - jnp.max(s, axis=-1, keepdims=True))
            prob = e / jnp.sum(e, axis=-1, keepdims=True)
            heads.append(_dot(prob.astype(BF16), head_mem(mv_ref, i, hh).astype(BF16)))
        o = jnp.concatenate(heads, axis=-1).astype(BF16)
        x2 = x_ref[i] + _dot(o, wo_ref[...])
        x2_ref[i] = x2
        h = _rmsnorm(x2, gn_ref[...])
        for sub in range(SUBLANES):
            h_ref[pl.ds(i * tq * SUBLANES + sub, tq, stride=SUBLANES), :] = h[:, sub * LANES:(sub + 1) * LANES]
        h_hi, h_lo = _split2(h)
        pieces = _dot(jnp.concatenate([h_hi, h_lo], axis=0), rw_ref[...])
        logits = ((pieces[:tq, :LANES] + pieces[:tq, LANES:]) + (pieces[tq:, :LANES] + pieces[tq:, LANES:])
                  + rb_ref[...])
        lane = lax.broadcasted_iota(jnp.int32, logits.shape, 1)
        vals, idxs = [], []
        for _ in range(TOP_K):
            m = jnp.max(logits, axis=-1, keepdims=True)
            first = jnp.min(jnp.where(logits == m, lane, LANES), axis=-1, keepdims=True)
            vals.append(m)
            idxs.append(first)
            logits = jnp.where(lane == first, -jnp.inf, logits)
        es = [jnp.exp(vv - vals[0]) for vv in vals]
        den = es[0] + es[1] + es[2] + es[3]
        ti = jnp.zeros(lane.shape, jnp.int32)
        gt = jnp.zeros(lane.shape, F32)
        for j in range(TOP_K):
            ti = jnp.where(lane == j, idxs[j], ti)
            gt = jnp.where(lane == j, es[j] / den, gt)
        ti_ref[i] = ti
        gt_ref[i] = gt


def _attn_route(q, x1, mk, mv, p, bb, tq):
    b, t, d = x1.shape
    assert d == SUBLANES * LANES
    rows = bb * tq
    assert b % bb == 0 and t % tq == 0 and (bb == 1 or tq == t)
    n_tq = t // tq
    blk = lambda w: pl.BlockSpec((bb, tq, w), lambda i, j: (i, j, 0))
    head_rows = mk.ndim == 2
    mem = (pl.BlockSpec((bb * (mk.shape[0] // b), LANES), lambda i, j: (i, 0)) if head_rows
           else pl.BlockSpec((bb, N_MEM, d), lambda i, j: (i, 0, 0)))
    const = lambda shape: pl.BlockSpec(shape, lambda i, j: (0, 0))
    return pl.pallas_call(
        functools.partial(_attn_body, bb=bb, tq=tq, head_rows=head_rows),
        grid=(b // bb, n_tq),
        in_specs=[blk(d), blk(d), mem, mem, const((d, d)), const((1, d)), const((d, 2 * LANES)), const((1, LANES))],
        out_specs=[blk(d), pl.BlockSpec((rows * SUBLANES, LANES), lambda i, j: (i * n_tq + j, 0)), blk(LANES),
                   blk(LANES)],
        out_shape=[jax.ShapeDtypeStruct((b, t, d), F32), jax.ShapeDtypeStruct((b * t * SUBLANES, LANES), F32),
                   jax.ShapeDtypeStruct((b, t, LANES), jnp.int32), jax.ShapeDtypeStruct((b, t, LANES), F32)],
        compiler_params=pltpu.CompilerParams(dimension_semantics=("parallel", "parallel"),
                                             vmem_limit_bytes=VMEM_LIMIT),
        name="attn_route",
    )(q, x1, mk, mv, p['wo'], p['norm_ffn'], p['router_w'], p['router_b'])


def _moe_body(be_ref, gnext_ref, sprev_ref, h_hbm, w1_ref, b1_ref, w2_ref, b2_ref, out_hbm,
              xb0, xb1, yb0, yb1, w1b, w2b, gsem, ssem, *, tm, n_steps):
    k = pl.program_id(0)
    xbufs, ybufs = (xb0, xb1), (yb0, yb1)

    def row(ref, at):
        return ref.at[pl.ds(at, SUBLANES)]

    def gather_all(slot):
        return pltpu.make_async_copy(h_hbm.at[pl.ds(0, tm * SUBLANES)], xbufs[slot], gsem.at[slot])

    def scatter_all(slot):
        return pltpu.make_async_copy(ybufs[slot], out_hbm.at[pl.ds(0, tm * SUBLANES)], ssem.at[slot])

    @pl.when(k == 0)
    def _prologue():
        yb0[...] = jnp.zeros(yb0.shape, F32)
        yb1[...] = jnp.zeros(yb1.shape, F32)
        for r in range(tm):
            pltpu.make_async_copy(row(h_hbm, 0), row(xb0, r * SUBLANES), gsem.at[0]).start()

    @pl.when(jnp.logical_or(k == 0, be_ref[k] != be_ref[jnp.maximum(k - 1, 0)]))
    def _cast():
        w1b[...] = w1_ref[0].astype(BF16)
        w2b[...] = w2_ref[0].astype(BF16)

    def phase(cur):
        nxt = 1 - cur
        gather_all(cur).wait()

        @pl.when(k >= 1)
        def _():
            scatter_all(cur).wait()

        for r in range(tm):
            src = pl.multiple_of(gnext_ref[0, 0, r], SUBLANES)
            dst = pl.multiple_of(sprev_ref[0, 0, r], SUBLANES)
            pltpu.make_async_copy(row(h_hbm, src), row(xbufs[nxt], r * SUBLANES), gsem.at[nxt]).start()
            pltpu.make_async_copy(row(ybufs[nxt], r * SUBLANES), row(out_hbm, dst), ssem.at[nxt]).start(priority=1)
        x = jnp.concatenate([xbufs[cur][pl.ds(sub, tm, stride=SUBLANES), :] for sub in range(SUBLANES)], axis=-1)
        hc = _dot(x.astype(BF16), w1b[...]) + b1_ref[0]
        hg = jnp.minimum(hc[:, :D_FF], SWIGLU_LIMIT)
        hl = jnp.clip(hc[:, D_FF:], -SWIGLU_LIMIT, SWIGLU_LIMIT)
        act = hg * _sigmoid(SWIGLU_ALPHA * hg) * (hl + 1.0)
        y = _dot(act.astype(BF16), w2b[...]) + b2_ref[0]
        for sub in range(SUBLANES):
            ybufs[cur][pl.ds(sub, tm, stride=SUBLANES), :] = y[:, sub * LANES:(sub + 1) * LANES]

        @pl.when(k == n_steps - 1)
        def _epilogue():
            gather_all(nxt).wait()
            scatter_all(nxt).wait()

    for parity in range(2):
        pl.when(lax.rem(k, 2) == parity)(functools.partial(phase, parity))


def _moe(h, gtok, sdst, block_e, p, tm):
    n, d = h.shape[0] // SUBLANES, D_MODEL
    n_steps = block_e.shape[0]
    grid_spec = pltpu.PrefetchScalarGridSpec(
        num_scalar_prefetch=1,
        grid=(n_steps,),
        in_specs=[
            pl.BlockSpec((1, 1, tm), lambda k, be: (jnp.minimum(k + 1, n_steps - 1), 0, 0),
                         memory_space=pltpu.SMEM),
            pl.BlockSpec((1, 1, tm), lambda k, be: (jnp.maximum(k - 1, 0), 0, 0), memory_space=pltpu.SMEM),
            pl.BlockSpec(memory_space=pl.ANY),
            pl.BlockSpec((1, d, 2 * D_FF), lambda k, be: (be[k], 0, 0)),
            pl.BlockSpec((1, 1, 2 * D_FF), lambda k, be: (be[k], 0, 0)),
            pl.BlockSpec((1, D_FF, d), lambda k, be: (be[k], 0, 0)),
            pl.BlockSpec((1, 1, d), lambda k, be: (be[k], 0, 0)),
        ],
        out_specs=pl.BlockSpec(memory_space=pl.ANY),
        scratch_shapes=[
            pltpu.VMEM((tm * SUBLANES, LANES), F32), pltpu.VMEM((tm * SUBLANES, LANES), F32),
            pltpu.VMEM((tm * SUBLANES, LANES), F32), pltpu.VMEM((tm * SUBLANES, LANES), F32),
            pltpu.VMEM((d, 2 * D_FF), BF16), pltpu.VMEM((D_FF, d), BF16),
            pltpu.SemaphoreType.DMA((2,)), pltpu.SemaphoreType.DMA((2,)),
        ],
    )
    return pl.pallas_call(
        functools.partial(_moe_body, tm=tm, n_steps=n_steps),
        grid_spec=grid_spec,
        out_shape=jax.ShapeDtypeStruct(((n * TOP_K + tm) * SUBLANES, LANES), F32),
        compiler_params=pltpu.CompilerParams(dimension_semantics=("arbitrary",), vmem_limit_bytes=VMEM_LIMIT),
        name="moe_experts",
    )(block_e, gtok.reshape(n_steps, 1, tm), sdst.reshape(n_steps, 1, tm), h, p['w1_e'], p['b1_e'], p['w2_e'],
      p['b2_e'])


def _route_plan(top_i, tm):
    n = top_i.shape[0]
    na = n * TOP_K
    flat_e = top_i.reshape(na)
    order = jnp.argsort(flat_e).astype(jnp.int32)
    counts = jnp.sum((flat_e[:, None] == jnp.arange(N_EXPERTS, dtype=jnp.int32)[None, :]).astype(jnp.int32), axis=0)
    padded = (counts + tm - 1) // tm * tm
    starts = jnp.cumsum(counts) - counts
    pends = jnp.cumsum(padded)
    pstarts = pends - padded
    n_steps = -(-(na + N_EXPERTS * (tm - 1)) // tm) + 2
    blk_start = (jnp.arange(n_steps, dtype=jnp.int32) - 1) * tm
    block_e = jnp.sum((blk_start[:, None] >= pends[None, :]).astype(jnp.int32), axis=1)
    block_e = jnp.minimum(block_e, N_EXPERTS - 1)
    block_e = block_e.at[0].set(block_e[1])
    lane = jnp.arange(tm, dtype=jnp.int32)[None, :]
    local = blk_start[:, None] + lane - pstarts[block_e][:, None]
    valid = (local >= 0) & (local < counts[block_e][:, None]) & (blk_start[:, None] >= 0)
    asg = order[jnp.clip(starts[block_e][:, None] + local, 0, na - 1)]
    gtok = jnp.where(valid, asg // TOP_K, 0)
    sdst = jnp.where(valid, (asg % TOP_K) * n + asg // TOP_K, TOP_K * n + lane)
    return gtok * SUBLANES, sdst * SUBLANES, block_e


def _combine_body(x_ref, gt_ref, gn_ref, *refs, final):
    slot_refs, y_ref = refs[:TOP_K], refs[TOP_K]
    x = x_ref[...]
    gt = gt_ref[...]
    tm = x.shape[0]
    for j in range(TOP_K):
        slot = jnp.concatenate([slot_refs[j][pl.ds(sub, tm, stride=SUBLANES), :] for sub in range(SUBLANES)],
                               axis=-1)
        x = x + gt[:, j:j + 1] * slot
    y_ref[...] = _rmsnorm(x, gn_ref[...]) if final else x


def _combine(x2, gates, slots, gn, tm, n_all, row0, final):
    n, d = x2.shape
    assert n % tm == 0 and row0 % tm == 0 and n_all % tm == 0
    slot_spec = lambda j: pl.BlockSpec((tm * SUBLANES, LANES), lambda i: ((j * n_all + row0) // tm + i, 0))
    return pl.pallas_call(
        functools.partial(_combine_body, final=final),
        grid=(n // tm,),
        in_specs=[pl.BlockSpec((tm, d), lambda i: (i, 0)), pl.BlockSpec((tm, LANES), lambda i: (i, 0)),
                  pl.BlockSpec((1, d), lambda i: (0, 0))] + [slot_spec(j) for j in range(TOP_K)],
        out_specs=pl.BlockSpec((tm, d), lambda i: (i, 0)),
        out_shape=jax.ShapeDtypeStruct((n, d), F32),
        compiler_params=pltpu.CompilerParams(dimension_semantics=("parallel",), vmem_limit_bytes=VMEM_LIMIT),
        name="combine",
    )(x2, gates, gn, *([slots] * TOP_K))


def _layer_params(l, norm_mix, w_in, mu_shift, w0, w2_decay, a0, a2_iclr, g2_gate, k_k, k_a, r_k, lnx_w, lnx_b,
                  conv_w, a_log, dt_bias, gdn_norm_w, w_out, norm_cross, norm_mem, wq_x, wk_x, wv_x, wo_x,
                  norm_ffn, router_w, router_b, w1_e, b1_e, w2_e, b2_e):
    row = lambda z: z.reshape(1, -1).astype(F32)
    lane_pad = lambda z, at: jnp.zeros((1, LANES), F32).at[0, at:at + z.shape[0]].set(z)
    return {
        'norm_mix': row(norm_mix[l]),
        'w_in_r': w_in[l][:, :R_PROJ].astype(BF16),
        'w_in_g': jnp.pad(w_in[l][:, R_PROJ:], ((0, 0), (0, G_PROJ_PAD - G_PROJ))).astype(BF16),
        'mu': row(mu_shift[l]), 'w0': row(w0[l]), 'a0': row(a0[l]),
        'w2': jnp.pad(w2_decay[l], ((0, LORA_A), (0, 0))).astype(BF16),
        'a2': jnp.pad(a2_iclr[l], ((LORA_W, 0), (0, 0))).astype(BF16),
        'g2': g2_gate[l].astype(BF16),
        'k_k': row(k_k[l]), 'k_a': row(k_a[l]), 'r_k': row(r_k[l]), 'lnx_w': row(lnx_w[l]), 'lnx_b': row(lnx_b[l]),
        'conv_w': conv_w[l].astype(F32),
        'a_log': lane_pad(a_log[l], H_G), 'dt_bias': lane_pad(dt_bias[l], H_G),
        'gdn_norm_w': row(gdn_norm_w[l]),
        'w_out_r': w_out[l][:W_RWKV].astype(BF16), 'w_out_g': w_out[l][W_RWKV:].astype(BF16),
        'norm_cross': row(norm_cross[l]), 'norm_mem': row(norm_mem[l]),
        'wq': wq_x[l].astype(BF16), 'wk': wk_x[l].astype(BF16), 'wv': wv_x[l].astype(BF16),
        'wo': wo_x[l].astype(BF16),
        'norm_ffn': row(norm_ffn[l]),
        'router_w': jnp.concatenate(_split2(jnp.pad(router_w[l].astype(F32), ((0, 0), (0, LANES - N_EXPERTS)))),
                                    axis=1),
        'router_b': jnp.full((1, LANES), NEG_BIG, F32).at[0, :N_EXPERTS].set(router_b[l].astype(F32)),
        'w1_e': w1_e[l], 'b1_e': b1_e[l][:, None, :], 'w2_e': w2_e[l], 'b2_e': b2_e[l][:, None, :],
    }


def _mix_and_attend(x, mk, mv, shift_prev, s_r, conv_buf, s_g, p, *, chunk, mix_bb, tm, bb, tq):
    b, t, d = x.shape
    assert t >= CONV_W - 1
    pr, pg = _norm_proj(x.reshape(b * t, d), p['norm_mix'], [p['w_in_r'], p['w_in_g']], [F32, F32], tm)
    pr = pr.reshape(b, t, R_PROJ)
    pg = pg.reshape(b, t, G_PROJ_PAD)
    shift_new = pr[:, t - 1]
    conv_new = pg[:, t - (CONV_W - 1):, :GDN_CONV_DIM]
    t_pad = -(-t // chunk) * chunk
    if t_pad != t:
        pr = jnp.pad(pr, ((0, 0), (0, t_pad - t), (0, 0)))
        pg = jnp.pad(pg, ((0, 0), (0, t_pad - t), (0, 0)))
    t_valid = chunk if t_pad == t else t
    y_r, s_r_new = _rwkv_mix(pr, shift_prev, s_r, p, chunk, t_valid, mix_bb)
    y_g, s_g_new = _gdn_mix(pg, conv_buf, s_g, p, chunk, t_valid, mix_bb)
    if t_pad != t:
        y_r, y_g = y_r[:, :t], y_g[:, :t]
    x1, q = _out_q(x.reshape(b * t, d), y_r.reshape(b * t, W_RWKV), y_g.reshape(b * t, W_GDN), p, tm)
    x2, h, top_i, gates = _attn_route(q.reshape(b, t, d), x1.reshape(b, t, d), mk, mv, p, bb, tq)
    return (x2.reshape(b * t, d), h, top_i.reshape(b * t, LANES), gates.reshape(b * t, LANES),
            shift_new, s_r_new, conv_new, s_g_new)


def kernel(x_prompt, x_sample, mem_prompt, state_rwkv, state_rwkv_shift, state_gdn, state_gdn_conv, cache_mem_k, cache_mem_v, norm_mix, w_in, mu_shift, w0, w2_decay, a0, a2_iclr, g2_gate, k_k, k_a, r_k, lnx_w, lnx_b, conv_w, a_log, dt_bias, gdn_norm_w, w_out, norm_cross, norm_mem, wq_x, wk_x, wv_x, wo_x, norm_ffn, router_w, router_b, w1_e, b1_e, w2_e, b2_e, final_norm):
    bp, tp, d = x_prompt.shape
    bs, ts, _ = x_sample.shape
    depth = w_in.shape[0]
    np_, ns = bp * tp, bs * ts
    xp, xs = x_prompt, x_sample
    outs = [[] for _ in range(10)]
    for l in range(depth):
        p = _layer_params(l, norm_mix, w_in, mu_shift, w0, w2_decay, a0, a2_iclr, g2_gate, k_k, k_a, r_k, lnx_w,
                          lnx_b, conv_w, a_log, dt_bias, gdn_norm_w, w_out, norm_cross, norm_mem, wq_x, wk_x, wv_x,
                          wo_x, norm_ffn, router_w, router_b, w1_e, b1_e, w2_e, b2_e)
        n_mem = mem_prompt.shape[1]
        mk, mv = _norm_proj(mem_prompt.reshape(bp * n_mem, d), p['norm_mem'], [p['wk'], p['wv']], [F32, F32], 256)
        mk, mv = mk.reshape(bp, n_mem, d), mv.reshape(bp, n_mem, d)
        res_p = _mix_and_attend(
            xp, mk, mv, jnp.zeros((bp, R_PROJ), F32), jnp.zeros((bp, H_R, N_R, N_R), F32),
            jnp.zeros((bp, CONV_W - 1, GDN_CONV_DIM), F32), jnp.zeros((bp, H_G, D_G, D_G), F32), p,
            chunk=MIX_CHUNK, mix_bb=2, tm=256, bb=1, tq=512)
        head_rows = lambda c: c.reshape(bs, n_mem, H_X, D_X // LANES, LANES).transpose(0, 1, 3, 2, 4).reshape(-1, LANES)
        mk_s, mv_s = head_rows(cache_mem_k[l]), head_rows(cache_mem_v[l])
        res_s = _mix_and_attend(
            xs, mk_s, mv_s, state_rwkv_shift[l], state_rwkv[l], state_gdn_conv[l],
            state_gdn[l], p, chunk=SUBLANES, mix_bb=4, tm=256, bb=8, tq=ts)
        h = jnp.concatenate([res_p[1], res_s[1]], axis=0)
        top_i = jnp.concatenate([res_p[2], res_s[2]], axis=0)[:, :TOP_K]
        gtok, sdst, block_e = _route_plan(top_i, MOE_ROWS)
        slots = _moe(h, gtok, sdst, block_e, p, MOE_ROWS)
        gn = final_norm.reshape(1, d).astype(F32)
        last = l == depth - 1
        xp = _combine(res_p[0], res_p[3], slots, gn, 256, np_ + ns, 0, last).reshape(bp, tp, d)
        xs = _combine(res_s[0], res_s[3], slots, gn, 256, np_ + ns, np_, last).reshape(bs, ts, d)
        new = [res_p[5], res_p[4], res_p[7], res_p[6], mk.reshape(bp, n_mem, H_X, D_X),
               mv.reshape(bp, n_mem, H_X, D_X), res_s[5], res_s[4], res_s[7], res_s[6]]
        for acc, val in zip(outs, new):
            acc.append(val)
    return (xp, xs) + tuple(jnp.stack(o) for o in outs)
```

```python
import functools

import jax
import jax.numpy as jnp
from jax import lax
from jax.experimental import pallas as pl
from jax.experimental.pallas import tpu as pltpu

F32 = jnp.float32
BF16 = jnp.bfloat16
DEFAULT = lax.Precision.DEFAULT

D_MODEL = 1024
W_RWKV = 512
N_R = 64
H_R = W_RWKV // N_R
LORA_W = 64
LORA_A = 64
LORA_G = 128
R_PROJ = 3 * W_RWKV + LORA_W + LORA_A + LORA_G
GN_EPS = 64e-5
W_GDN = 512
D_G = 128
H_G = W_GDN // D_G
GDN_CONV_DIM = 3 * W_GDN
CONV_W = 4
G_PROJ = GDN_CONV_DIM + W_GDN + 2 * H_G
LANES = 128
SUBLANES = 8
G_PROJ_PAD = GDN_CONV_DIM + W_GDN + LANES
MIX_CHUNK = 64
N_MEM = 256
H_X = 4
D_X = D_MODEL // H_X
N_EXPERTS = 32
TOP_K = 4
D_FF = D_MODEL
SWIGLU_LIMIT = 7.0
SWIGLU_ALPHA = 1.702
MOE_ROWS = 256
RMS_EPS = 1e-6
L2_EPS = 1e-6
NEG_BIG = -1e30
VMEM_LIMIT = 56 * 1024 * 1024


def _dot(a, b, precision=DEFAULT):
    return jnp.dot(a, b, preferred_element_type=F32, precision=precision)


def _dot_nt(a, b, precision=DEFAULT):
    return lax.dot_general(a, b, (((1,), (1,)), ((), ())), preferred_element_type=F32, precision=precision)


NN = (((1,), (0,)), ((), ()))
NT = (((1,), (1,)), ((), ()))
TN = (((0,), (0,)), ((), ()))


def _split2(x):
    hi = x.astype(BF16)
    return hi, (x - hi.astype(F32)).astype(BF16)


def _mm(a, b, dims):
    return lax.dot_general(a.astype(BF16), b.astype(BF16), dims, preferred_element_type=F32)


def _sel_mm(sel, x, dims):
    dg = lambda y: lax.dot_general(sel, y, dims, preferred_element_type=F32)
    hi = x.astype(BF16)
    r1 = x - hi.astype(F32)
    mid = r1.astype(BF16)
    lo = (r1 - mid.astype(F32)).astype(BF16)
    return dg(hi) + (dg(mid) + dg(lo))


def _sigmoid(x):
    return 1.0 / (1.0 + jnp.exp(-x))


def _softplus(x):
    return jnp.maximum(x, 0.0) + jnp.log(1.0 + jnp.exp(-jnp.abs(x)))


def _rmsnorm(x, g):
    return x * lax.rsqrt(jnp.mean(x * x, axis=-1, keepdims=True) + RMS_EPS) * g


def _tri_masks(c):
    row = lax.broadcasted_iota(jnp.int32, (c, c), 0)
    col = lax.broadcasted_iota(jnp.int32, (c, c), 1)
    return col <= row, col < row, (col == row).astype(F32)


def _unit_lower_inverses(ms, eye, c):
    ts = [eye + m for m in ms]
    ps = list(ms)
    covered = 2
    while covered < c:
        ps = [_mm(p, p, NN) for p in ps]
        ts = [t + _mm(t, p, NN) for t, p in zip(ts, ps)]
        covered *= 2
    return ts


def _norm_proj_body(x_ref, g_ref, *refs, n_out):
    w_refs, o_refs = refs[:n_out], refs[n_out:]
    hb = _rmsnorm(x_ref[...], g_ref[...]).astype(BF16)
    for w_ref, o_ref in zip(w_refs, o_refs):
        o_ref[...] = _dot(hb, w_ref[...]).astype(o_ref.dtype)


def _norm_proj(x, g, ws, out_dtypes, tm):
    n, d = x.shape
    assert n % tm == 0
    in_specs = [pl.BlockSpec((tm, d), lambda i: (i, 0)), pl.BlockSpec((1, d), lambda i: (0, 0))]
    in_specs += [pl.BlockSpec(w.shape, lambda i: (0, 0)) for w in ws]
    return pl.pallas_call(
        functools.partial(_norm_proj_body, n_out=len(ws)),
        grid=(n // tm,),
        in_specs=in_specs,
        out_specs=[pl.BlockSpec((tm, w.shape[1]), lambda i: (i, 0)) for w in ws],
        out_shape=[jax.ShapeDtypeStruct((n, w.shape[1]), dt) for w, dt in zip(ws, out_dtypes)],
        compiler_params=pltpu.CompilerParams(dimension_semantics=("parallel",), vmem_limit_bytes=VMEM_LIMIT),
        name="norm_proj",
    )(x, g, *ws)


def _rwkv_body(pr_ref, shift_ref, s0_ref, mu_ref, w0_ref, w2_ref, a0_ref, a2_ref, g2_ref, kk_ref, ka_ref,
               rk_ref, lnw_ref, lnb_ref, y_ref, sout_ref, s_scr, prev_scr, *, bb, chunk, t_valid, n_chunks):
    c = pl.program_id(1)

    @pl.when(c == 0)
    def _init():
        s_scr[...] = s0_ref[...]
        prev_scr[...] = shift_ref[...]

    incl, strict, eye = _tri_masks(chunk)
    incl_bf = incl.astype(BF16)
    rows = lax.broadcasted_iota(jnp.int32, (chunk, 1), 0)
    sls = [slice(h * N_R, (h + 1) * N_R) for h in range(H_R)]

    def per_batch(bi):
        pr = pr_ref[bi]
        prev = jnp.where(rows == 0, prev_scr[bi], pltpu.roll(pr, 1, 0))
        prev_scr[bi] = pr[chunk - 1:chunk, :]
        xm = pr + (prev - pr) * mu_ref[...]
        r = xm[:, :W_RWKV]
        k = xm[:, W_RWKV:2 * W_RWKV]
        v = xm[:, 2 * W_RWKV:3 * W_RWKV]
        lo = xm[:, 3 * W_RWKV:3 * W_RWKV + LORA_W + LORA_A]
        g_lo = xm[:, 3 * W_RWKV + LORA_W + LORA_A:]
        logw = -_softplus(-(w0_ref[...] + _dot(jnp.tanh(lo).astype(BF16), w2_ref[...]))) - 0.5
        wl = -jnp.exp(logw)
        a = _sigmoid(a0_ref[...] + _dot(lo.astype(BF16), a2_ref[...]))
        g = _dot(_sigmoid(g_lo).astype(BF16), g2_ref[...])
        kkv = k * kk_ref[...]
        k = k * (1.0 + (a - 1.0) * ka_ref[...])
        if t_valid < chunk:
            valid = rows < t_valid
            wl = jnp.where(valid, wl, 0.0)
            kkv = jnp.where(valid, kkv, 0.0)
            k = jnp.where(valid, k, 0.0)
            v = jnp.where(valid, v, 0.0)
        cum = _sel_mm(incl_bf, wl, NN)
        cum_last = cum[chunk - 1:chunk, :]
        return dict(r=r, k=k, v=v, a=a, g=g, kkv=kkv, w_incl=jnp.exp(cum), w_prev=jnp.exp(cum - wl),
                    w_inv=jnp.exp(-cum), w_tail=jnp.exp(cum_last - cum), w_last=jnp.exp(cum_last))

    pre = [per_batch(bi) for bi in range(bb)]
    chains = [(bi, h) for bi in range(bb) for h in range(H_R)]
    col = lambda name: [pre[bi][name][:, sls[h]] for bi, h in chains]
    n = range(len(chains))
    r_, k_, v_, a_ = col('r'), col('k'), col('v'), col('a')
    w_incl, w_prev, w_inv, w_tail, w_last = col('w_incl'), col('w_prev'), col('w_inv'), col('w_tail'), col('w_last')
    kks = [x * lax.rsqrt(jnp.sum(x * x, axis=-1, keepdims=True) + L2_EPS) for x in col('kkv')]
    a_hat = [-(kks[i] * w_prev[i]) for i in n]
    kka = [kks[i] * a_[i] for i in n]
    b_hat = [kka[i] * w_inv[i] for i in n]
    k_hat = [k_[i] * w_inv[i] for i in n]
    r_hat = [r_[i] * w_incl[i] for i in n]
    cross = [_mm(jnp.concatenate([a_hat[i], r_hat[i]], axis=0), jnp.concatenate([b_hat[i], k_hat[i]], axis=0), NT)
             for i in n]
    row2 = lax.broadcasted_iota(jnp.int32, (chunk, 2 * chunk), 0)
    col2 = lax.broadcasted_iota(jnp.int32, (chunk, 2 * chunk), 1)
    col2 = jnp.where(col2 >= chunk, col2 - chunk, col2)
    strict2, incl2 = col2 < row2, col2 <= row2
    m_top = [jnp.where(strict2, x[:chunk], 0.0) for x in cross]
    a_bot = [jnp.where(incl2, x[chunk:], 0.0) for x in cross]
    t_inv = _unit_lower_inverses([x[:, :chunk] for x in m_top], eye, chunk)
    w_hat = [_mm(t_inv[i], a_hat[i], NN) for i in n]
    mv = [_mm(m_top[i], jnp.concatenate([jnp.zeros_like(v_[i]), v_[i]], axis=0), NN) for i in n]
    u = [_mm(t_inv[i], mv[i], NN) for i in n]
    ss = [s_scr[bi, h] for bi, h in chains]
    p = [_mm(w_hat[i], ss[i], NT) + u[i] for i in n]
    pv = [jnp.concatenate([p[i], v_[i]], axis=0) for i in n]
    ys = [_mm(r_hat[i], ss[i], NT) + _mm(a_bot[i], pv[i], NN) for i in n]
    for i, (bi, h) in enumerate(chains):
        tails = jnp.concatenate([kka[i] * w_tail[i], k_[i] * w_tail[i]], axis=0)
        s_scr[bi, h] = ss[i] * w_last[i] + _mm(pv[i], tails, TN)
    for i, (bi, h) in enumerate(chains):
        sl = sls[h]
        y = ys[i]
        mean = jnp.mean(y, axis=-1, keepdims=True)
        yc = y - mean
        var = jnp.mean(yc * yc, axis=-1, keepdims=True)
        yn = yc * lax.rsqrt(var + GN_EPS) * lnw_ref[:, sl] + lnb_ref[:, sl]
        bonus = jnp.sum(r_[i] * k_[i] * rk_ref[:, sl], axis=-1, keepdims=True) * v_[i]
        y_ref[bi, :, sl] = ((yn + bonus) * pre[bi]['g'][:, sl]).astype(y_ref.dtype)

    @pl.when(c == n_chunks - 1)
    def _fin():
        sout_ref[...] = s_scr[...]


def _rwkv_mix(pr, shift_prev, s0, p, chunk, t_valid, bb):
    b, t, _ = pr.shape
    assert b % bb == 0 and t % chunk == 0
    n_chunks = t // chunk
    const = lambda shape: pl.BlockSpec(shape, lambda i, j: (0,) * len(shape))
    return pl.pallas_call(
        functools.partial(_rwkv_body, bb=bb, chunk=chunk, t_valid=t_valid, n_chunks=n_chunks),
        grid=(b // bb, n_chunks),
        in_specs=[
            pl.BlockSpec((bb, chunk, R_PROJ), lambda i, j: (i, j, 0)),
            pl.BlockSpec((bb, 1, R_PROJ), lambda i, j: (i, 0, 0)),
            pl.BlockSpec((bb, H_R, N_R, N_R), lambda i, j: (i, 0, 0, 0)),
            const((1, R_PROJ)), const((1, W_RWKV)), const((LANES, W_RWKV)), const((1, W_RWKV)),
            const((LANES, W_RWKV)), const((LORA_G, W_RWKV)), const((1, W_RWKV)), const((1, W_RWKV)),
            const((1, W_RWKV)), const((1, W_RWKV)), const((1, W_RWKV)),
        ],
        out_specs=[
            pl.BlockSpec((bb, chunk, W_RWKV), lambda i, j: (i, j, 0)),
            pl.BlockSpec((bb, H_R, N_R, N_R), lambda i, j: (i, 0, 0, 0)),
        ],
        out_shape=[jax.ShapeDtypeStruct((b, t, W_RWKV), BF16), jax.ShapeDtypeStruct((b, H_R, N_R, N_R), F32)],
        scratch_shapes=[pltpu.VMEM((bb, H_R, N_R, N_R), F32), pltpu.VMEM((bb, 1, R_PROJ), F32)],
        compiler_params=pltpu.CompilerParams(dimension_semantics=("arbitrary", "arbitrary"),
                                             vmem_limit_bytes=VMEM_LIMIT),
        name="rwkv_mix",
    )(pr, shift_prev[:, None, :], s0, p['mu'], p['w0'], p['w2'], p['a0'], p['a2'], p['g2'], p['k_k'], p['k_a'],
      p['r_k'], p['lnx_w'], p['lnx_b'])


def _gdn_body(pg_ref, cbuf_ref, s0_ref, cw_ref, alog_ref, dtb_ref, nw_ref, y_ref, sout_ref, s_scr, xp_scr,
              *, bb, chunk, t_valid, n_chunks):
    c = pl.program_id(1)

    @pl.when(c == 0)
    def _init():
        s_scr[...] = s0_ref[...]
        xp_scr[:, 0:SUBLANES, :] = cbuf_ref[...]

    incl, strict, eye = _tri_masks(chunk)
    incl_bf = incl.astype(BF16)
    lane = lax.broadcasted_iota(jnp.int32, (chunk, LANES), 1)
    sls = [slice(h * D_G, (h + 1) * D_G) for h in range(H_G)]

    def per_batch(bi):
        xp_scr[bi, SUBLANES:SUBLANES + chunk, :] = pg_ref[bi, :, :GDN_CONV_DIM]
        base = SUBLANES - (CONV_W - 1)
        conv = xp_scr[bi, base:base + chunk, :] * cw_ref[0:1, :]
        for j in range(1, CONV_W):
            conv = conv + xp_scr[bi, base + j:base + j + chunk, :] * cw_ref[j:j + 1, :]
        xp_scr[bi, 0:SUBLANES, :] = xp_scr[bi, chunk:chunk + SUBLANES, :]
        qkv = conv * _sigmoid(conv)
        z = pg_ref[bi, :, GDN_CONV_DIM:GDN_CONV_DIM + W_GDN]
        ba = pg_ref[bi, :, GDN_CONV_DIM + W_GDN:]
        beta_blk = _sigmoid(ba)
        g_blk = -jnp.exp(alog_ref[...]) * _softplus(ba + dtb_ref[...])
        if t_valid < chunk:
            valid = lax.broadcasted_iota(jnp.int32, (chunk, 1), 0) < t_valid
            beta_blk = jnp.where(valid, beta_blk, 0.0)
            g_blk = jnp.where(valid, g_blk, 0.0)
        gc_blk = _sel_mm(incl_bf, g_blk, NN)
        return dict(qkv=qkv, z=z, beta_blk=beta_blk, gc_blk=gc_blk)

    pre = [per_batch(bi) for bi in range(bb)]
    chains = [(bi, h) for bi in range(bb) for h in range(H_G)]
    n = range(len(chains))
    qs = [pre[bi]['qkv'][:, sls[h]] for bi, h in chains]
    qs = [x * lax.rsqrt(jnp.sum(x * x, axis=-1, keepdims=True) + L2_EPS) * (D_G ** -0.5) for x in qs]
    ks = [pre[bi]['qkv'][:, W_GDN + h * D_G:W_GDN + (h + 1) * D_G] for bi, h in chains]
    ks = [x * lax.rsqrt(jnp.sum(x * x, axis=-1, keepdims=True) + L2_EPS) for x in ks]
    vs = [pre[bi]['qkv'][:, 2 * W_GDN + h * D_G:2 * W_GDN + (h + 1) * D_G] for bi, h in chains]
    betas = [pre[bi]['beta_blk'][:, h:h + 1] for bi, h in chains]
    gcols = [pre[bi]['gc_blk'][:, H_G + h:H_G + h + 1] for bi, h in chains]
    grows = [_sel_mm((lane == H_G + h).astype(BF16), pre[bi]['gc_blk'], NT) for bi, h in chains]
    g_last = [gcols[i][chunk - 1:chunk, :] for i in n]
    decay = [jnp.where(incl, jnp.exp(jnp.where(incl, gcols[i] - grows[i], 0.0)), 0.0) for i in n]
    k_beta = [ks[i] * betas[i] for i in n]
    cross = [_mm(jnp.concatenate([k_beta[i], qs[i]], axis=0), ks[i], NT) for i in n]
    lmat = [jnp.where(strict, cross[i][:chunk] * decay[i], 0.0) for i in n]
    attn = [jnp.where(incl, cross[i][chunk:] * decay[i], 0.0) for i in n]
    t_inv = _unit_lower_inverses([-x for x in lmat], eye, chunk)
    e_gc = [jnp.exp(gcols[i]) for i in n]
    uw = [_mm(t_inv[i], jnp.concatenate([vs[i] * betas[i], k_beta[i] * e_gc[i]], axis=1), NN) for i in n]
    ss = [s_scr[bi, h] for bi, h in chains]
    v_new = [uw[i][:, :D_G] - _mm(uw[i][:, D_G:], ss[i], NN) for i in n]
    os_ = [_mm(jnp.concatenate([qs[i] * e_gc[i], attn[i]], axis=1), jnp.concatenate([ss[i], v_new[i]], axis=0), NN)
           for i in n]
    for i, (bi, h) in enumerate(chains):
        s_scr[bi, h] = ss[i] * jnp.exp(g_last[i]) + _mm(ks[i] * jnp.exp(g_last[i] - gcols[i]), v_new[i], TN)
    for i, (bi, h) in enumerate(chains):
        o = os_[i]
        o = o * lax.rsqrt(jnp.mean(o * o, axis=-1, keepdims=True) + RMS_EPS) * nw_ref[...]
        z_h = pre[bi]['z'][:, sls[h]]
        y_ref[bi, :, sls[h]] = (o * (z_h * _sigmoid(z_h))).astype(y_ref.dtype)

    @pl.when(c == n_chunks - 1)
    def _fin():
        sout_ref[...] = s_scr[...]


def _gdn_mix(pg, conv_buf, s0, p, chunk, t_valid, bb):
    b, t, _ = pg.shape
    assert b % bb == 0 and t % chunk == 0
    n_chunks = t // chunk
    cbuf = jnp.pad(conv_buf, ((0, 0), (SUBLANES - (CONV_W - 1), 0), (0, 0)))
    const = lambda shape: pl.BlockSpec(shape, lambda i, j: (0,) * len(shape))
    return pl.pallas_call(
        functools.partial(_gdn_body, bb=bb, chunk=chunk, t_valid=t_valid, n_chunks=n_chunks),
        grid=(b // bb, n_chunks),
        in_specs=[
            pl.BlockSpec((bb, chunk, G_PROJ_PAD), lambda i, j: (i, j, 0)),
            pl.BlockSpec((bb, SUBLANES, GDN_CONV_DIM), lambda i, j: (i, 0, 0)),
            pl.BlockSpec((bb, H_G, D_G, D_G), lambda i, j: (i, 0, 0, 0)),
            const((CONV_W, GDN_CONV_DIM)), const((1, LANES)), const((1, LANES)), const((1, D_G)),
        ],
        out_specs=[
            pl.BlockSpec((bb, chunk, W_GDN), lambda i, j: (i, j, 0)),
            pl.BlockSpec((bb, H_G, D_G, D_G), lambda i, j: (i, 0, 0, 0)),
        ],
        out_shape=[jax.ShapeDtypeStruct((b, t, W_GDN), BF16), jax.ShapeDtypeStruct((b, H_G, D_G, D_G), F32)],
        scratch_shapes=[pltpu.VMEM((bb, H_G, D_G, D_G), F32),
                        pltpu.VMEM((bb, SUBLANES + chunk, GDN_CONV_DIM), F32)],
        compiler_params=pltpu.CompilerParams(dimension_semantics=("arbitrary", "arbitrary"),
                                             vmem_limit_bytes=VMEM_LIMIT),
        name="gdn_mix",
    )(pg, cbuf, s0, p['conv_w'], p['a_log'], p['dt_bias'], p['gdn_norm_w'])


def _out_q_body(x_ref, yr_ref, yg_ref, wor_ref, wog_ref, gn_ref, wq_ref, x1_ref, q_ref):
    x1 = x_ref[...] + _dot(yr_ref[...], wor_ref[...]) + _dot(yg_ref[...], wog_ref[...])
    x1_ref[...] = x1
    q_ref[...] = _dot(_rmsnorm(x1, gn_ref[...]).astype(BF16), wq_ref[...]).astype(q_ref.dtype)


def _out_q(x, yr, yg, p, tm):
    n, d = x.shape
    assert n % tm == 0
    row = lambda w: pl.BlockSpec((tm, w), lambda i: (i, 0))
    const = lambda shape: pl.BlockSpec(shape, lambda i: (0, 0))
    return pl.pallas_call(
        _out_q_body,
        grid=(n // tm,),
        in_specs=[row(d), row(W_RWKV), row(W_GDN), const((W_RWKV, d)), const((W_GDN, d)), const((1, d)),
                  const((d, d))],
        out_specs=[row(d), row(d)],
        out_shape=[jax.ShapeDtypeStruct((n, d), F32), jax.ShapeDtypeStruct((n, d), BF16)],
        compiler_params=pltpu.CompilerParams(dimension_semantics=("parallel",), vmem_limit_bytes=VMEM_LIMIT),
        name="out_q",
    )(x, yr, yg, p['w_out_r'], p['w_out_g'], p['norm_cross'], p['wq'])


def _attn_body(q_ref, x_ref, mk_ref, mv_ref, wo_ref, gn_ref, rw_ref, rb_ref, x2_ref, h_ref, ti_ref, gt_ref,
               *, bb, tq, head_rows):
    d_tiles = D_X // LANES
    mem_rows = N_MEM * d_tiles * H_X

    def head_mem(ref, i, hh):
        if not head_rows:
            return ref[i, :, hh * D_X:(hh + 1) * D_X]
        return jnp.concatenate(
            [ref[pl.ds(i * mem_rows + dt * H_X + hh, N_MEM, stride=d_tiles * H_X), :] for dt in range(d_tiles)],
            axis=-1)

    for i in range(bb):
        q = q_ref[i]
        heads = []
        for hh in range(H_X):
            sl = slice(hh * D_X, (hh + 1) * D_X)
            s = _dot_nt(q[:, sl], head_mem(mk_ref, i, hh).astype(BF16)) * (D_X ** -0.5)
            e = jnp.exp(s - jnp.max(s, axis=-1, keepdims=True))
            prob = e / jnp.sum(e, axis=-1, keepdims=True)
            heads.append(_dot(prob.astype(BF16), head_mem(mv_ref, i, hh).astype(BF16)))
        o = jnp.concatenate(heads, axis=-1).astype(BF16)
        x2 = x_ref[i] + _dot(o, wo_ref[...])
        x2_ref[i] = x2
        h = _rmsnorm(x2, gn_ref[...])
        for sub in range(SUBLANES):
            h_ref[pl.ds(i * tq * SUBLANES + sub, tq, stride=SUBLANES), :] = h[:, sub * LANES:(sub + 1) * LANES]
        h_hi, h_lo = _split2(h)
        pieces = _dot(jnp.concatenate([h_hi, h_lo], axis=0), rw_ref[...])
        logits = ((pieces[:tq, :LANES] + pieces[:tq, LANES:]) + (pieces[tq:, :LANES] + pieces[tq:, LANES:])
                  + rb_ref[...])
        lane = lax.broadcasted_iota(jnp.int32, logits.shape, 1)
        vals, idxs = [], []
        for _ in range(TOP_K):
            m = jnp.max(logits, axis=-1, keepdims=True)
            first = jnp.min(jnp.where(logits == m, lane, LANES), axis=-1, keepdims=True)
            vals.append(m)
            idxs.append(first)
            logits = jnp.where(lane == first, -jnp.inf, logits)
        es = [jnp.exp(vv - vals[0]) for vv in vals]
        den = es[0] + es[1] + es[2] + es[3]
        ti = jnp.zeros(lane.shape, jnp.int32)
        gt = jnp.zeros(lane.shape, F32)
        for j in range(TOP_K):
            ti = jnp.where(lane == j, idxs[j], ti)
            gt = jnp.where(lane == j, es[j] / den, gt)
        ti_ref[i] = ti
        gt_ref[i] = gt


def _attn_route(q, x1, mk, mv, p, bb, tq):
    b, t, d = x1.shape
    assert d == SUBLANES * LANES
    rows = bb * tq
    assert b % bb == 0 and t % tq == 0 and (bb == 1 or tq == t)
    n_tq = t // tq
    blk = lambda w: pl.BlockSpec((bb, tq, w), lambda i, j: (i, j, 0))
    head_rows = mk.ndim == 2
    mem = (pl.BlockSpec((bb * (mk.shape[0] // b), LANES), lambda i, j: (i, 0)) if head_rows
           else pl.BlockSpec((bb, N_MEM, d), lambda i, j: (i, 0, 0)))
    const = lambda shape: pl.BlockSpec(shape, lambda i, j: (0, 0))
    return pl.pallas_call(
        functools.partial(_attn_body, bb=bb, tq=tq, head_rows=head_rows),
        grid=(b // bb, n_tq),
        in_specs=[blk(d), blk(d), mem, mem, const((d, d)), const((1, d)), const((d, 2 * LANES)), const((1, LANES))],
        out_specs=[blk(d), pl.BlockSpec((rows * SUBLANES, LANES), lambda i, j: (i * n_tq + j, 0)), blk(LANES),
                   blk(LANES)],
        out_shape=[jax.ShapeDtypeStruct((b, t, d), F32), jax.ShapeDtypeStruct((b * t * SUBLANES, LANES), F32),
                   jax.ShapeDtypeStruct((b, t, LANES), jnp.int32), jax.ShapeDtypeStruct((b, t, LANES), F32)],
        compiler_params=pltpu.CompilerParams(dimension_semantics=("parallel", "parallel"),
                                             vmem_limit_bytes=VMEM_LIMIT),
        name="attn_route",
    )(q, x1, mk, mv, p['wo'], p['norm_ffn'], p['router_w'], p['router_b'])


def _moe_body(be_ref, gnext_ref, sprev_ref, h_hbm, w1_ref, b1_ref, w2_ref, b2_ref, out_hbm,
              xbuf, ybuf, w1b, w2b, gsem, ssem, *, tm, n_steps):
    k = pl.program_id(0)

    def row(ref, slot, at):
        return ref.at[slot, pl.ds(at, SUBLANES)]

    def gather_all(slot):
        return pltpu.make_async_copy(h_hbm.at[pl.ds(0, tm * SUBLANES)], xbuf.at[slot], gsem.at[slot])

    def scatter_all(slot):
        return pltpu.make_async_copy(ybuf.at[slot], out_hbm.at[pl.ds(0, tm * SUBLANES)], ssem.at[slot])

    @pl.when(k == 0)
    def _prologue():
        ybuf[...] = jnp.zeros(ybuf.shape, F32)
        for r in range(tm):
            pltpu.make_async_copy(h_hbm.at[pl.ds(0, SUBLANES)], row(xbuf, 0, r * SUBLANES), gsem.at[0]).start()

    @pl.when(jnp.logical_or(k == 0, be_ref[k] != be_ref[jnp.maximum(k - 1, 0)]))
    def _cast():
        w1b[...] = w1_ref[0].astype(BF16)
        w2b[...] = w2_ref[0].astype(BF16)

    def phase(cur):
        nxt = 1 - cur
        gather_all(cur).wait()

        @pl.when(k >= 1)
        def _():
            scatter_all(cur).wait()

        for r in range(tm):
            src = pl.multiple_of(gnext_ref[0, 0, r], SUBLANES)
            dst = pl.multiple_of(sprev_ref[0, 0, r], SUBLANES)
            pltpu.make_async_copy(h_hbm.at[pl.ds(src, SUBLANES)], row(xbuf, nxt, r * SUBLANES), gsem.at[nxt]).start()
            pltpu.make_async_copy(row(ybuf, nxt, r * SUBLANES), out_hbm.at[pl.ds(dst, SUBLANES)],
                                  ssem.at[nxt]).start()
        x = jnp.concatenate([xbuf[cur, pl.ds(sub, tm, stride=SUBLANES), :] for sub in range(SUBLANES)], axis=-1)
        hc = _dot(x.astype(BF16), w1b[...]) + b1_ref[0]
        hg = jnp.minimum(hc[:, :D_FF], SWIGLU_LIMIT)
        hl = jnp.clip(hc[:, D_FF:], -SWIGLU_LIMIT, SWIGLU_LIMIT)
        act = hg * _sigmoid(SWIGLU_ALPHA * hg) * (hl + 1.0)
        y = _dot(act.astype(BF16), w2b[...]) + b2_ref[0]
        for sub in range(SUBLANES):
            ybuf[cur, pl.ds(sub, tm, stride=SUBLANES), :] = y[:, sub * LANES:(sub + 1) * LANES]

        @pl.when(k == n_steps - 1)
        def _epilogue():
            gather_all(nxt).wait()
            scatter_all(nxt).wait()

    for parity in range(2):
        pl.when(lax.rem(k, 2) == parity)(functools.partial(phase, parity))


def _moe(h, gtok, sdst, block_e, p, tm):
    n, d = h.shape[0] // SUBLANES, D_MODEL
    n_steps = block_e.shape[0]
    grid_spec = pltpu.PrefetchScalarGridSpec(
        num_scalar_prefetch=1,
        grid=(n_steps,),
        in_specs=[
            pl.BlockSpec((1, 1, tm), lambda k, be: (jnp.minimum(k + 1, n_steps - 1), 0, 0),
                         memory_space=pltpu.SMEM),
            pl.BlockSpec((1, 1, tm), lambda k, be: (jnp.maximum(k - 1, 0), 0, 0), memory_space=pltpu.SMEM),
            pl.BlockSpec(memory_space=pl.ANY),
            pl.BlockSpec((1, d, 2 * D_FF), lambda k, be: (be[k], 0, 0)),
            pl.BlockSpec((1, 1, 2 * D_FF), lambda k, be: (be[k], 0, 0)),
            pl.BlockSpec((1, D_FF, d), lambda k, be: (be[k], 0, 0)),
            pl.BlockSpec((1, 1, d), lambda k, be: (be[k], 0, 0)),
        ],
        out_specs=pl.BlockSpec(memory_space=pl.ANY),
        scratch_shapes=[
            pltpu.VMEM((2, tm * SUBLANES, LANES), F32), pltpu.VMEM((2, tm * SUBLANES, LANES), F32),
            pltpu.VMEM((d, 2 * D_FF), BF16), pltpu.VMEM((D_FF, d), BF16),
            pltpu.SemaphoreType.DMA((2,)), pltpu.SemaphoreType.DMA((2,)),
        ],
    )
    return pl.pallas_call(
        functools.partial(_moe_body, tm=tm, n_steps=n_steps),
        grid_spec=grid_spec,
        out_shape=jax.ShapeDtypeStruct(((n * TOP_K + tm) * SUBLANES, LANES), F32),
        compiler_params=pltpu.CompilerParams(dimension_semantics=("arbitrary",), vmem_limit_bytes=VMEM_LIMIT),
        name="moe_experts",
    )(block_e, gtok.reshape(n_steps, 1, tm), sdst.reshape(n_steps, 1, tm), h, p['w1_e'], p['b1_e'], p['w2_e'],
      p['b2_e'])


def _route_plan(top_i, tm):
    n = top_i.shape[0]
    na = n * TOP_K
    flat_e = top_i.reshape(na)
    order = jnp.argsort(flat_e).astype(jnp.int32)
    counts = jnp.sum((flat_e[:, None] == jnp.arange(N_EXPERTS, dtype=jnp.int32)[None, :]).astype(jnp.int32), axis=0)
    padded = (counts + tm - 1) // tm * tm
    starts = jnp.cumsum(counts) - counts
    pends = jnp.cumsum(padded)
    pstarts = pends - padded
    n_steps = -(-(na + N_EXPERTS * (tm - 1)) // tm) + 2
    blk_start = (jnp.arange(n_steps, dtype=jnp.int32) - 1) * tm
    block_e = jnp.sum((blk_start[:, None] >= pends[None, :]).astype(jnp.int32), axis=1)
    block_e = jnp.minimum(block_e, N_EXPERTS - 1)
    block_e = block_e.at[0].set(block_e[1])
    lane = jnp.arange(tm, dtype=jnp.int32)[None, :]
    local = blk_start[:, None] + lane - pstarts[block_e][:, None]
    valid = (local >= 0) & (local < counts[block_e][:, None]) & (blk_start[:, None] >= 0)
    asg = order[jnp.clip(starts[block_e][:, None] + local, 0, na - 1)]
    gtok = jnp.where(valid, asg // TOP_K, 0)
    sdst = jnp.where(valid, (asg % TOP_K) * n + asg // TOP_K, TOP_K * n + lane)
    return gtok * SUBLANES, sdst * SUBLANES, block_e


def _combine_body(x_ref, gt_ref, gn_ref, *refs, final):
    slot_refs, y_ref = refs[:TOP_K], refs[TOP_K]
    x = x_ref[...]
    gt = gt_ref[...]
    tm = x.shape[0]
    for j in range(TOP_K):
        slot = jnp.concatenate([slot_refs[j][pl.ds(sub, tm, stride=SUBLANES), :] for sub in range(SUBLANES)],
                               axis=-1)
        x = x + gt[:, j:j + 1] * slot
    y_ref[...] = _rmsnorm(x, gn_ref[...]) if final else x


def _combine(x2, gates, slots, gn, tm, n_all, row0, final):
    n, d = x2.shape
    assert n % tm == 0 and row0 % tm == 0 and n_all % tm == 0
    slot_spec = lambda j: pl.BlockSpec((tm * SUBLANES, LANES), lambda i: ((j * n_all + row0) // tm + i, 0))
    return pl.pallas_call(
        functools.partial(_combine_body, final=final),
        grid=(n // tm,),
        in_specs=[pl.BlockSpec((tm, d), lambda i: (i, 0)), pl.BlockSpec((tm, LANES), lambda i: (i, 0)),
                  pl.BlockSpec((1, d), lambda i: (0, 0))] + [slot_spec(j) for j in range(TOP_K)],
        out_specs=pl.BlockSpec((tm, d), lambda i: (i, 0)),
        out_shape=jax.ShapeDtypeStruct((n, d), F32),
        compiler_params=pltpu.CompilerParams(dimension_semantics=("parallel",), vmem_limit_bytes=VMEM_LIMIT),
        name="combine",
    )(x2, gates, gn, *([slots] * TOP_K))


def _layer_params(l, norm_mix, w_in, mu_shift, w0, w2_decay, a0, a2_iclr, g2_gate, k_k, k_a, r_k, lnx_w, lnx_b,
                  conv_w, a_log, dt_bias, gdn_norm_w, w_out, norm_cross, norm_mem, wq_x, wk_x, wv_x, wo_x,
                  norm_ffn, router_w, router_b, w1_e, b1_e, w2_e, b2_e):
    row = lambda z: z.reshape(1, -1).astype(F32)
    lane_pad = lambda z, at: jnp.zeros((1, LANES), F32).at[0, at:at + z.shape[0]].set(z)
    return {
        'norm_mix': row(norm_mix[l]),
        'w_in_r': w_in[l][:, :R_PROJ].astype(BF16),
        'w_in_g': jnp.pad(w_in[l][:, R_PROJ:], ((0, 0), (0, G_PROJ_PAD - G_PROJ))).astype(BF16),
        'mu': row(mu_shift[l]), 'w0': row(w0[l]), 'a0': row(a0[l]),
        'w2': jnp.pad(w2_decay[l], ((0, LORA_A), (0, 0))).astype(BF16),
        'a2': jnp.pad(a2_iclr[l], ((LORA_W, 0), (0, 0))).astype(BF16),
        'g2': g2_gate[l].astype(BF16),
        'k_k': row(k_k[l]), 'k_a': row(k_a[l]), 'r_k': row(r_k[l]), 'lnx_w': row(lnx_w[l]), 'lnx_b': row(lnx_b[l]),
        'conv_w': conv_w[l].astype(F32),
        'a_log': lane_pad(a_log[l], H_G), 'dt_bias': lane_pad(dt_bias[l], H_G),
        'gdn_norm_w': row(gdn_norm_w[l]),
        'w_out_r': w_out[l][:W_RWKV].astype(BF16), 'w_out_g': w_out[l][W_RWKV:].astype(BF16),
        'norm_cross': row(norm_cross[l]), 'norm_mem': row(norm_mem[l]),
        'wq': wq_x[l].astype(BF16), 'wk': wk_x[l].astype(BF16), 'wv': wv_x[l].astype(BF16),
        'wo': wo_x[l].astype(BF16),
        'norm_ffn': row(norm_ffn[l]),
        'router_w': jnp.concatenate(_split2(jnp.pad(router_w[l].astype(F32), ((0, 0), (0, LANES - N_EXPERTS)))),
                                    axis=1),
        'router_b': jnp.full((1, LANES), NEG_BIG, F32).at[0, :N_EXPERTS].set(router_b[l].astype(F32)),
        'w1_e': w1_e[l], 'b1_e': b1_e[l][:, None, :], 'w2_e': w2_e[l], 'b2_e': b2_e[l][:, None, :],
    }


def _mix_and_attend(x, mk, mv, shift_prev, s_r, conv_buf, s_g, p, *, chunk, mix_bb, tm, bb, tq):
    b, t, d = x.shape
    assert t >= CONV_W - 1
    pr, pg = _norm_proj(x.reshape(b * t, d), p['norm_mix'], [p['w_in_r'], p['w_in_g']], [F32, F32], tm)
    pr = pr.reshape(b, t, R_PROJ)
    pg = pg.reshape(b, t, G_PROJ_PAD)
    shift_new = pr[:, t - 1]
    conv_new = pg[:, t - (CONV_W - 1):, :GDN_CONV_DIM]
    t_pad = -(-t // chunk) * chunk
    if t_pad != t:
        pr = jnp.pad(pr, ((0, 0), (0, t_pad - t), (0, 0)))
        pg = jnp.pad(pg, ((0, 0), (0, t_pad - t), (0, 0)))
    t_valid = chunk if t_pad == t else t
    y_r, s_r_new = _rwkv_mix(pr, shift_prev, s_r, p, chunk, t_valid, mix_bb)
    y_g, s_g_new = _gdn_mix(pg, conv_buf, s_g, p, chunk, t_valid, mix_bb)
    if t_pad != t:
        y_r, y_g = y_r[:, :t], y_g[:, :t]
    x1, q = _out_q(x.reshape(b * t, d), y_r.reshape(b * t, W_RWKV), y_g.reshape(b * t, W_GDN), p, tm)
    x2, h, top_i, gates = _attn_route(q.reshape(b, t, d), x1.reshape(b, t, d), mk, mv, p, bb, tq)
    return (x2.reshape(b * t, d), h, top_i.reshape(b * t, LANES), gates.reshape(b * t, LANES),
            shift_new, s_r_new, conv_new, s_g_new)


def kernel(x_prompt, x_sample, mem_prompt, state_rwkv, state_rwkv_shift, state_gdn, state_gdn_conv, cache_mem_k, cache_mem_v, norm_mix, w_in, mu_shift, w0, w2_decay, a0, a2_iclr, g2_gate, k_k, k_a, r_k, lnx_w, lnx_b, conv_w, a_log, dt_bias, gdn_norm_w, w_out, norm_cross, norm_mem, wq_x, wk_x, wv_x, wo_x, norm_ffn, router_w, router_b, w1_e, b1_e, w2_e, b2_e, final_norm):
    bp, tp, d = x_prompt.shape
    bs, ts, _ = x_sample.shape
    depth = w_in.shape[0]
    np_, ns = bp * tp, bs * ts
    xp, xs = x_prompt, x_sample
    outs = [[] for _ in range(10)]
    for l in range(depth):
        p = _layer_params(l, norm_mix, w_in, mu_shift, w0, w2_decay, a0, a2_iclr, g2_gate, k_k, k_a, r_k, lnx_w,
                          lnx_b, conv_w, a_log, dt_bias, gdn_norm_w, w_out, norm_cross, norm_mem, wq_x, wk_x, wv_x,
                          wo_x, norm_ffn, router_w, router_b, w1_e, b1_e, w2_e, b2_e)
        n_mem = mem_prompt.shape[1]
        mk, mv = _norm_proj(mem_prompt.reshape(bp * n_mem, d), p['norm_mem'], [p['wk'], p['wv']], [F32, F32], 256)
        mk, mv = mk.reshape(bp, n_mem, d), mv.reshape(bp, n_mem, d)
        res_p = _mix_and_attend(
            xp, mk, mv, jnp.zeros((bp, R_PROJ), F32), jnp.zeros((bp, H_R, N_R, N_R), F32),
            jnp.zeros((bp, CONV_W - 1, GDN_CONV_DIM), F32), jnp.zeros((bp, H_G, D_G, D_G), F32), p,
            chunk=MIX_CHUNK, mix_bb=2, tm=256, bb=1, tq=512)
        head_rows = lambda c: c.reshape(bs, n_mem, H_X, D_X // LANES, LANES).transpose(0, 1, 3, 2, 4).reshape(-1, LANES)
        mk_s, mv_s = head_rows(cache_mem_k[l]), head_rows(cache_mem_v[l])
        res_s = _mix_and_attend(
            xs, mk_s, mv_s, state_rwkv_shift[l], state_rwkv[l], state_gdn_conv[l],
            state_gdn[l], p, chunk=SUBLANES, mix_bb=4, tm=256, bb=8, tq=ts)
        h = jnp.concatenate([res_p[1], res_s[1]], axis=0)
        top_i = jnp.concatenate([res_p[2], res_s[2]], axis=0)[:, :TOP_K]
        gtok, sdst, block_e = _route_plan(top_i, MOE_ROWS)
        slots = _moe(h, gtok, sdst, block_e, p, MOE_ROWS)
        gn = final_norm.reshape(1, d).astype(F32)
        last = l == depth - 1
        xp = _combine(res_p[0], res_p[3], slots, gn, 256, np_ + ns, 0, last).reshape(bp, tp, d)
        xs = _combine(res_s[0], res_s[3], slots, gn, 256, np_ + ns, np_, last).reshape(bs, ts, d)
        new = [res_p[5], res_p[4], res_p[7], res_p[6], mk.reshape(bp, n_mem, H_X, D_X),
               mv.reshape(bp, n_mem, H_X, D_X), res_s[5], res_s[4], res_s[7], res_s[6]]
        for acc, val in zip(outs, new):
            acc.append(val)
    return (xp, xs) + tuple(jnp.stack(o) for o in outs)
```

```python
import functools

import jax
import jax.numpy as jnp
from jax import lax
from jax.experimental import pallas as pl
from jax.experimental.pallas import tpu as pltpu

F32 = jnp.float32
BF16 = jnp.bfloat16
DEFAULT = lax.Precision.DEFAULT

D_MODEL = 1024
W_RWKV = 512
N_R = 64
H_R = W_RWKV // N_R
LORA_W = 64
LORA_A = 64
LORA_G = 128
R_PROJ = 3 * W_RWKV + LORA_W + LORA_A + LORA_G
GN_EPS = 64e-5
W_GDN = 512
D_G = 128
H_G = W_GDN // D_G
GDN_CONV_DIM = 3 * W_GDN
CONV_W = 4
G_PROJ = GDN_CONV_DIM + W_GDN + 2 * H_G
LANES = 128
SUBLANES = 8
G_PROJ_PAD = GDN_CONV_DIM + W_GDN + LANES
MIX_CHUNK = 64
N_MEM = 256
H_X = 4
D_X = D_MODEL // H_X
N_EXPERTS = 32
TOP_K = 4
D_FF = D_MODEL
SWIGLU_LIMIT = 7.0
SWIGLU_ALPHA = 1.702
MOE_ROWS = 256
RMS_EPS = 1e-6
L2_EPS = 1e-6
NEG_BIG = -1e30
VMEM_LIMIT = 56 * 1024 * 1024


def _dot(a, b, precision=DEFAULT):
    return jnp.dot(a, b, preferred_element_type=F32, precision=precision)


def _dot_nt(a, b, precision=DEFAULT):
    return lax.dot_general(a, b, (((1,), (1,)), ((), ())), preferred_element_type=F32, precision=precision)


NN = (((1,), (0,)), ((), ()))
NT = (((1,), (1,)), ((), ()))
TN = (((0,), (0,)), ((), ()))


def _split2(x):
    hi = x.astype(BF16)
    return hi, (x - hi.astype(F32)).astype(BF16)


def _mm(a, b, dims):
    return lax.dot_general(a.astype(BF16), b.astype(BF16), dims, preferred_element_type=F32)


def _sel_mm(sel, x, dims):
    dg = lambda y: lax.dot_general(sel, y, dims, preferred_element_type=F32)
    hi = x.astype(BF16)
    r1 = x - hi.astype(F32)
    mid = r1.astype(BF16)
    lo = (r1 - mid.astype(F32)).astype(BF16)
    return dg(hi) + (dg(mid) + dg(lo))


def _sigmoid(x):
    return 1.0 / (1.0 + jnp.exp(-x))


def _softplus(x):
    return jnp.maximum(x, 0.0) + jnp.log(1.0 + jnp.exp(-jnp.abs(x)))


def _rmsnorm(x, g):
    return x * lax.rsqrt(jnp.mean(x * x, axis=-1, keepdims=True) + RMS_EPS) * g


def _tri_masks(c):
    row = lax.broadcasted_iota(jnp.int32, (c, c), 0)
    col = lax.broadcasted_iota(jnp.int32, (c, c), 1)
    return col <= row, col < row, (col == row).astype(F32)


def _unit_lower_inverses(ms, eye, c):
    ts = [eye + m for m in ms]
    ps = list(ms)
    covered = 2
    while covered < c:
        ps = [_mm(p, p, NN) for p in ps]
        ts = [t + _mm(t, p, NN) for t, p in zip(ts, ps)]
        covered *= 2
    return ts


def _norm_proj_body(x_ref, g_ref, *refs, n_out):
    w_refs, o_refs = refs[:n_out], refs[n_out:]
    hb = _rmsnorm(x_ref[...], g_ref[...]).astype(BF16)
    for w_ref, o_ref in zip(w_refs, o_refs):
        o_ref[...] = _dot(hb, w_ref[...]).astype(o_ref.dtype)


def _norm_proj(x, g, ws, out_dtypes, tm):
    n, d = x.shape
    assert n % tm == 0
    in_specs = [pl.BlockSpec((tm, d), lambda i: (i, 0)), pl.BlockSpec((1, d), lambda i: (0, 0))]
    in_specs += [pl.BlockSpec(w.shape, lambda i: (0, 0)) for w in ws]
    return pl.pallas_call(
        functools.partial(_norm_proj_body, n_out=len(ws)),
        grid=(n // tm,),
        in_specs=in_specs,
        out_specs=[pl.BlockSpec((tm, w.shape[1]), lambda i: (i, 0)) for w in ws],
        out_shape=[jax.ShapeDtypeStruct((n, w.shape[1]), dt) for w, dt in zip(ws, out_dtypes)],
        compiler_params=pltpu.CompilerParams(dimension_semantics=("parallel",), vmem_limit_bytes=VMEM_LIMIT),
        name="norm_proj",
    )(x, g, *ws)


def _rwkv_body(pr_ref, shift_ref, s0_ref, mu_ref, w0_ref, w2_ref, a0_ref, a2_ref, g2_ref, kk_ref, ka_ref,
               rk_ref, lnw_ref, lnb_ref, y_ref, sout_ref, s_scr, prev_scr, *, bb, chunk, t_valid, n_chunks):
    c = pl.program_id(1)

    @pl.when(c == 0)
    def _init():
        s_scr[...] = s0_ref[...]
        prev_scr[...] = shift_ref[...]

    incl, strict, eye = _tri_masks(chunk)
    incl_bf = incl.astype(BF16)
    rows = lax.broadcasted_iota(jnp.int32, (chunk, 1), 0)
    sls = [slice(h * N_R, (h + 1) * N_R) for h in range(H_R)]

    def per_batch(bi):
        pr = pr_ref[bi]
        prev = jnp.where(rows == 0, prev_scr[bi], pltpu.roll(pr, 1, 0))
        prev_scr[bi] = pr[chunk - 1:chunk, :]
        xm = pr + (prev - pr) * mu_ref[...]
        r = xm[:, :W_RWKV]
        k = xm[:, W_RWKV:2 * W_RWKV]
        v = xm[:, 2 * W_RWKV:3 * W_RWKV]
        lo = xm[:, 3 * W_RWKV:3 * W_RWKV + LORA_W + LORA_A]
        g_lo = xm[:, 3 * W_RWKV + LORA_W + LORA_A:]
        logw = -_softplus(-(w0_ref[...] + _dot(jnp.tanh(lo).astype(BF16), w2_ref[...]))) - 0.5
        wl = -jnp.exp(logw)
        a = _sigmoid(a0_ref[...] + _dot(lo.astype(BF16), a2_ref[...]))
        g = _dot(_sigmoid(g_lo).astype(BF16), g2_ref[...])
        kkv = k * kk_ref[...]
        k = k * (1.0 + (a - 1.0) * ka_ref[...])
        if t_valid < chunk:
            valid = rows < t_valid
            wl = jnp.where(valid, wl, 0.0)
            kkv = jnp.where(valid, kkv, 0.0)
            k = jnp.where(valid, k, 0.0)
            v = jnp.where(valid, v, 0.0)
        cum = _sel_mm(incl_bf, wl, NN)
        cum_last = cum[chunk - 1:chunk, :]
        return dict(r=r, k=k, v=v, a=a, g=g, kkv=kkv, w_incl=jnp.exp(cum), w_prev=jnp.exp(cum - wl),
                    w_inv=jnp.exp(-cum), w_tail=jnp.exp(cum_last - cum), w_last=jnp.exp(cum_last))

    pre = [per_batch(bi) for bi in range(bb)]
    chains = [(bi, h) for bi in range(bb) for h in range(H_R)]
    col = lambda name: [pre[bi][name][:, sls[h]] for bi, h in chains]
    n = range(len(chains))
    r_, k_, v_, a_ = col('r'), col('k'), col('v'), col('a')
    w_incl, w_prev, w_inv, w_tail, w_last = col('w_incl'), col('w_prev'), col('w_inv'), col('w_tail'), col('w_last')
    kks = [x * lax.rsqrt(jnp.sum(x * x, axis=-1, keepdims=True) + L2_EPS) for x in col('kkv')]
    a_hat = [-(kks[i] * w_prev[i]) for i in n]
    kka = [kks[i] * a_[i] for i in n]
    b_hat = [kka[i] * w_inv[i] for i in n]
    k_hat = [k_[i] * w_inv[i] for i in n]
    r_hat = [r_[i] * w_incl[i] for i in n]
    cross = [_mm(jnp.concatenate([a_hat[i], r_hat[i]], axis=0), jnp.concatenate([b_hat[i], k_hat[i]], axis=0), NT)
             for i in n]
    row2 = lax.broadcasted_iota(jnp.int32, (chunk, 2 * chunk), 0)
    col2 = lax.broadcasted_iota(jnp.int32, (chunk, 2 * chunk), 1)
    col2 = jnp.where(col2 >= chunk, col2 - chunk, col2)
    strict2, incl2 = col2 < row2, col2 <= row2
    m_top = [jnp.where(strict2, x[:chunk], 0.0) for x in cross]
    a_bot = [jnp.where(incl2, x[chunk:], 0.0) for x in cross]
    t_inv = _unit_lower_inverses([x[:, :chunk] for x in m_top], eye, chunk)
    w_hat = [_mm(t_inv[i], a_hat[i], NN) for i in n]
    mv = [_mm(m_top[i], jnp.concatenate([jnp.zeros_like(v_[i]), v_[i]], axis=0), NN) for i in n]
    u = [_mm(t_inv[i], mv[i], NN) for i in n]
    ss = [s_scr[bi, h] for bi, h in chains]
    p = [_mm(w_hat[i], ss[i], NT) + u[i] for i in n]
    pv = [jnp.concatenate([p[i], v_[i]], axis=0) for i in n]
    ys = [_mm(r_hat[i], ss[i], NT) + _mm(a_bot[i], pv[i], NN) for i in n]
    for i, (bi, h) in enumerate(chains):
        tails = jnp.concatenate([kka[i] * w_tail[i], k_[i] * w_tail[i]], axis=0)
        s_scr[bi, h] = ss[i] * w_last[i] + _mm(pv[i], tails, TN)
    for i, (bi, h) in enumerate(chains):
        sl = sls[h]
        y = ys[i]
        mean = jnp.mean(y, axis=-1, keepdims=True)
        yc = y - mean
        var = jnp.mean(yc * yc, axis=-1, keepdims=True)
        yn = yc * lax.rsqrt(var + GN_EPS) * lnw_ref[:, sl] + lnb_ref[:, sl]
        bonus = jnp.sum(r_[i] * k_[i] * rk_ref[:, sl], axis=-1, keepdims=True) * v_[i]
        y_ref[bi, :, sl] = ((yn + bonus) * pre[bi]['g'][:, sl]).astype(y_ref.dtype)

    @pl.when(c == n_chunks - 1)
    def _fin():
        sout_ref[...] = s_scr[...]


def _rwkv_mix(pr, shift_prev, s0, p, chunk, t_valid, bb):
    b, t, _ = pr.shape
    assert b % bb == 0 and t % chunk == 0
    n_chunks = t // chunk
    const = lambda shape: pl.BlockSpec(shape, lambda i, j: (0,) * len(shape))
    return pl.pallas_call(
        functools.partial(_rwkv_body, bb=bb, chunk=chunk, t_valid=t_valid, n_chunks=n_chunks),
        grid=(b // bb, n_chunks),
        in_specs=[
            pl.BlockSpec((bb, chunk, R_PROJ), lambda i, j: (i, j, 0)),
            pl.BlockSpec((bb, 1, R_PROJ), lambda i, j: (i, 0, 0)),
            pl.BlockSpec((bb, H_R, N_R, N_R), lambda i, j: (i, 0, 0, 0)),
            const((1, R_PROJ)), const((1, W_RWKV)), const((LANES, W_RWKV)), const((1, W_RWKV)),
            const((LANES, W_RWKV)), const((LORA_G, W_RWKV)), const((1, W_RWKV)), const((1, W_RWKV)),
            const((1, W_RWKV)), const((1, W_RWKV)), const((1, W_RWKV)),
        ],
        out_specs=[
            pl.BlockSpec((bb, chunk, W_RWKV), lambda i, j: (i, j, 0)),
            pl.BlockSpec((bb, H_R, N_R, N_R), lambda i, j: (i, 0, 0, 0)),
        ],
        out_shape=[jax.ShapeDtypeStruct((b, t, W_RWKV), BF16), jax.ShapeDtypeStruct((b, H_R, N_R, N_R), F32)],
        scratch_shapes=[pltpu.VMEM((bb, H_R, N_R, N_R), F32), pltpu.VMEM((bb, 1, R_PROJ), F32)],
        compiler_params=pltpu.CompilerParams(dimension_semantics=("arbitrary", "arbitrary"),
                                             vmem_limit_bytes=VMEM_LIMIT),
        name="rwkv_mix",
    )(pr, shift_prev[:, None, :], s0, p['mu'], p['w0'], p['w2'], p['a0'], p['a2'], p['g2'], p['k_k'], p['k_a'],
      p['r_k'], p['lnx_w'], p['lnx_b'])


def _gdn_body(pg_ref, cbuf_ref, s0_ref, cw_ref, alog_ref, dtb_ref, nw_ref, y_ref, sout_ref, s_scr, xp_scr,
              *, bb, chunk, t_valid, n_chunks):
    c = pl.program_id(1)

    @pl.when(c == 0)
    def _init():
        s_scr[...] = s0_ref[...]
        xp_scr[:, 0:SUBLANES, :] = cbuf_ref[...]

    incl, strict, eye = _tri_masks(chunk)
    incl_bf = incl.astype(BF16)
    lane = lax.broadcasted_iota(jnp.int32, (chunk, LANES), 1)
    sls = [slice(h * D_G, (h + 1) * D_G) for h in range(H_G)]

    def per_batch(bi):
        xp_scr[bi, SUBLANES:SUBLANES + chunk, :] = pg_ref[bi, :, :GDN_CONV_DIM]
        base = SUBLANES - (CONV_W - 1)
        conv = xp_scr[bi, base:base + chunk, :] * cw_ref[0:1, :]
        for j in range(1, CONV_W):
            conv = conv + xp_scr[bi, base + j:base + j + chunk, :] * cw_ref[j:j + 1, :]
        xp_scr[bi, 0:SUBLANES, :] = xp_scr[bi, chunk:chunk + SUBLANES, :]
        qkv = conv * _sigmoid(conv)
        z = pg_ref[bi, :, GDN_CONV_DIM:GDN_CONV_DIM + W_GDN]
        ba = pg_ref[bi, :, GDN_CONV_DIM + W_GDN:]
        beta_blk = _sigmoid(ba)
        g_blk = -jnp.exp(alog_ref[...]) * _softplus(ba + dtb_ref[...])
        if t_valid < chunk:
            valid = lax.broadcasted_iota(jnp.int32, (chunk, 1), 0) < t_valid
            beta_blk = jnp.where(valid, beta_blk, 0.0)
            g_blk = jnp.where(valid, g_blk, 0.0)
        gc_blk = _sel_mm(incl_bf, g_blk, NN)
        return dict(qkv=qkv, z=z, beta_blk=beta_blk, gc_blk=gc_blk)

    pre = [per_batch(bi) for bi in range(bb)]
    chains = [(bi, h) for bi in range(bb) for h in range(H_G)]
    n = range(len(chains))
    qs = [pre[bi]['qkv'][:, sls[h]] for bi, h in chains]
    qs = [x * lax.rsqrt(jnp.sum(x * x, axis=-1, keepdims=True) + L2_EPS) * (D_G ** -0.5) for x in qs]
    ks = [pre[bi]['qkv'][:, W_GDN + h * D_G:W_GDN + (h + 1) * D_G] for bi, h in chains]
    ks = [x * lax.rsqrt(jnp.sum(x * x, axis=-1, keepdims=True) + L2_EPS) for x in ks]
    vs = [pre[bi]['qkv'][:, 2 * W_GDN + h * D_G:2 * W_GDN + (h + 1) * D_G] for bi, h in chains]
    betas = [pre[bi]['beta_blk'][:, h:h + 1] for bi, h in chains]
    gcols = [pre[bi]['gc_blk'][:, H_G + h:H_G + h + 1] for bi, h in chains]
    grows = [_sel_mm((lane == H_G + h).astype(BF16), pre[bi]['gc_blk'], NT) for bi, h in chains]
    g_last = [gcols[i][chunk - 1:chunk, :] for i in n]
    decay = [jnp.where(incl, jnp.exp(jnp.where(incl, gcols[i] - grows[i], 0.0)), 0.0) for i in n]
    k_beta = [ks[i] * betas[i] for i in n]
    cross = [_mm(jnp.concatenate([k_beta[i], qs[i]], axis=0), ks[i], NT) for i in n]
    lmat = [jnp.where(strict, cross[i][:chunk] * decay[i], 0.0) for i in n]
    attn = [jnp.where(incl, cross[i][chunk:] * decay[i], 0.0) for i in n]
    t_inv = _unit_lower_inverses([-x for x in lmat], eye, chunk)
    e_gc = [jnp.exp(gcols[i]) for i in n]
    uw = [_mm(t_inv[i], jnp.concatenate([vs[i] * betas[i], k_beta[i] * e_gc[i]], axis=1), NN) for i in n]
    ss = [s_scr[bi, h] for bi, h in chains]
    v_new = [uw[i][:, :D_G] - _mm(uw[i][:, D_G:], ss[i], NN) for i in n]
    os_ = [_mm(jnp.concatenate([qs[i] * e_gc[i], attn[i]], axis=1), jnp.concatenate([ss[i], v_new[i]], axis=0), NN)
           for i in n]
    for i, (bi, h) in enumerate(chains):
        s_scr[bi, h] = ss[i] * jnp.exp(g_last[i]) + _mm(ks[i] * jnp.exp(g_last[i] - gcols[i]), v_new[i], TN)
    for i, (bi, h) in enumerate(chains):
        o = os_[i]
        o = o * lax.rsqrt(jnp.mean(o * o, axis=-1, keepdims=True) + RMS_EPS) * nw_ref[...]
        z_h = pre[bi]['z'][:, sls[h]]
        y_ref[bi, :, sls[h]] = (o * (z_h * _sigmoid(z_h))).astype(y_ref.dtype)

    @pl.when(c == n_chunks - 1)
    def _fin():
        sout_ref[...] = s_scr[...]


def _gdn_mix(pg, conv_buf, s0, p, chunk, t_valid, bb):
    b, t, _ = pg.shape
    assert b % bb == 0 and t % chunk == 0
    n_chunks = t // chunk
    cbuf = jnp.pad(conv_buf, ((0, 0), (SUBLANES - (CONV_W - 1), 0), (0, 0)))
    const = lambda shape: pl.BlockSpec(shape, lambda i, j: (0,) * len(shape))
    return pl.pallas_call(
        functools.partial(_gdn_body, bb=bb, chunk=chunk, t_valid=t_valid, n_chunks=n_chunks),
        grid=(b // bb, n_chunks),
        in_specs=[
            pl.BlockSpec((bb, chunk, G_PROJ_PAD), lambda i, j: (i, j, 0)),
            pl.BlockSpec((bb, SUBLANES, GDN_CONV_DIM), lambda i, j: (i, 0, 0)),
            pl.BlockSpec((bb, H_G, D_G, D_G), lambda i, j: (i, 0, 0, 0)),
            const((CONV_W, GDN_CONV_DIM)), const((1, LANES)), const((1, LANES)), const((1, D_G)),
        ],
        out_specs=[
            pl.BlockSpec((bb, chunk, W_GDN), lambda i, j: (i, j, 0)),
            pl.BlockSpec((bb, H_G, D_G, D_G), lambda i, j: (i, 0, 0, 0)),
        ],
        out_shape=[jax.ShapeDtypeStruct((b, t, W_GDN), BF16), jax.ShapeDtypeStruct((b, H_G, D_G, D_G), F32)],
        scratch_shapes=[pltpu.VMEM((bb, H_G, D_G, D_G), F32),
                        pltpu.VMEM((bb, SUBLANES + chunk, GDN_CONV_DIM), F32)],
        compiler_params=pltpu.CompilerParams(dimension_semantics=("arbitrary", "arbitrary"),
                                             vmem_limit_bytes=VMEM_LIMIT),
        name="gdn_mix",
    )(pg, cbuf, s0, p['conv_w'], p['a_log'], p['dt_bias'], p['gdn_norm_w'])


def _out_q_body(x_ref, yr_ref, yg_ref, wor_ref, wog_ref, gn_ref, wq_ref, x1_ref, q_ref):
    x1 = x_ref[...] + _dot(yr_ref[...], wor_ref[...]) + _dot(yg_ref[...], wog_ref[...])
    x1_ref[...] = x1
    q_ref[...] = _dot(_rmsnorm(x1, gn_ref[...]).astype(BF16), wq_ref[...]).astype(q_ref.dtype)


def _out_q(x, yr, yg, p, tm):
    n, d = x.shape
    assert n % tm == 0
    row = lambda w: pl.BlockSpec((tm, w), lambda i: (i, 0))
    const = lambda shape: pl.BlockSpec(shape, lambda i: (0, 0))
    return pl.pallas_call(
        _out_q_body,
        grid=(n // tm,),
        in_specs=[row(d), row(W_RWKV), row(W_GDN), const((W_RWKV, d)), const((W_GDN, d)), const((1, d)),
                  const((d, d))],
        out_specs=[row(d), row(d)],
        out_shape=[jax.ShapeDtypeStruct((n, d), F32), jax.ShapeDtypeStruct((n, d), BF16)],
        compiler_params=pltpu.CompilerParams(dimension_semantics=("parallel",), vmem_limit_bytes=VMEM_LIMIT),
        name="out_q",
    )(x, yr, yg, p['w_out_r'], p['w_out_g'], p['norm_cross'], p['wq'])


def _attn_body(q_ref, x_ref, mk_ref, mv_ref, wo_ref, gn_ref, rw_ref, rb_ref, x2_ref, h_ref, ti_ref, gt_ref,
               *, bb, tq, head_rows):
    d_tiles = D_X // LANES
    mem_rows = N_MEM * d_tiles * H_X

    def head_mem(ref, i, hh):
        if not head_rows:
            return ref[i, :, hh * D_X:(hh + 1) * D_X]
        return jnp.concatenate(
            [ref[pl.ds(i * mem_rows + dt * H_X + hh, N_MEM, stride=d_tiles * H_X), :] for dt in range(d_tiles)],
            axis=-1)

    for i in range(bb):
        q = q_ref[i]
        heads = []
        for hh in range(H_X):
            sl = slice(hh * D_X, (hh + 1) * D_X)
            s = _dot_nt(q[:, sl], head_mem(mk_ref, i, hh).astype(BF16)) * (D_X ** -0.5)
            e = jnp.exp(s - jnp.max(s, axis=-1, keepdims=True))
            prob = e / jnp.sum(e, axis=-1, keepdims=True)
            heads.append(_dot(prob.astype(BF16), head_mem(mv_ref, i, hh).astype(BF16)))
        o = jnp.concatenate(heads, axis=-1).astype(BF16)
        x2 = x_ref[i] + _dot(o, wo_ref[...])
        x2_ref[i] = x2
        h = _rmsnorm(x2, gn_ref[...])
        for sub in range(SUBLANES):
            h_ref[pl.ds(i * tq * SUBLANES + sub, tq, stride=SUBLANES), :] = h[:, sub * LANES:(sub + 1) * LANES]
        h_hi, h_lo = _split2(h)
        pieces = _dot(jnp.concatenate([h_hi, h_lo], axis=0), rw_ref[...])
        logits = ((pieces[:tq, :LANES] + pieces[:tq, LANES:]) + (pieces[tq:, :LANES] + pieces[tq:, LANES:])
                  + rb_ref[...])
        lane = lax.broadcasted_iota(jnp.int32, logits.shape, 1)
        vals, idxs = [], []
        for _ in range(TOP_K):
            m = jnp.max(logits, axis=-1, keepdims=True)
            first = jnp.min(jnp.where(logits == m, lane, LANES), axis=-1, keepdims=True)
            vals.append(m)
            idxs.append(first)
            logits = jnp.where(lane == first, -jnp.inf, logits)
        es = [jnp.exp(vv - vals[0]) for vv in vals]
        den = es[0] + es[1] + es[2] + es[3]
        ti = jnp.zeros(lane.shape, jnp.int32)
        gt = jnp.zeros(lane.shape, F32)
        for j in range(TOP_K):
            ti = jnp.where(lane == j, idxs[j], ti)
            gt = jnp.where(lane == j, es[j] / den, gt)
        ti_ref[i] = ti
        gt_ref[i] = gt


def _attn_route(q, x1, mk, mv, p, bb, tq):
    b, t, d = x1.shape
    assert d == SUBLANES * LANES
    rows = bb * tq
    assert b % bb == 0 and t % tq == 0 and (bb == 1 or tq == t)
    n_tq = t // tq
    blk = lambda w: pl.BlockSpec((bb, tq, w), lambda i, j: (i, j, 0))
    head_rows = mk.ndim == 2
    mem = (pl.BlockSpec((bb * (mk.shape[0] // b), LANES), lambda i, j: (i, 0)) if head_rows
           else pl.BlockSpec((bb, N_MEM, d), lambda i, j: (i, 0, 0)))
    const = lambda shape: pl.BlockSpec(shape, lambda i, j: (0, 0))
    return pl.pallas_call(
        functools.partial(_attn_body, bb=bb, tq=tq, head_rows=head_rows),
        grid=(b // bb, n_tq),
        in_specs=[blk(d), blk(d), mem, mem, const((d, d)), const((1, d)), const((d, 2 * LANES)), const((1, LANES))],
        out_specs=[blk(d), pl.BlockSpec((rows * SUBLANES, LANES), lambda i, j: (i * n_tq + j, 0)), blk(LANES),
                   blk(LANES)],
        out_shape=[jax.ShapeDtypeStruct((b, t, d), F32), jax.ShapeDtypeStruct((b * t * SUBLANES, LANES), F32),
                   jax.ShapeDtypeStruct((b, t, LANES), jnp.int32), jax.ShapeDtypeStruct((b, t, LANES), F32)],
        compiler_params=pltpu.CompilerParams(dimension_semantics=("parallel", "parallel"),
                                             vmem_limit_bytes=VMEM_LIMIT),
        name="attn_route",
    )(q, x1, mk, mv, p['wo'], p['norm_ffn'], p['router_w'], p['router_b'])


def _moe_body(be_ref, gnext_ref, sprev_ref, h_hbm, w1_ref, b1_ref, w2_ref, b2_ref, out_hbm,
              xbuf, ybuf, w1b, w2b, gsem, ssem, *, tm, n_steps):
    k = pl.program_id(0)

    def row(ref, slot, at):
        return ref.at[slot, pl.ds(at, SUBLANES)]

    def gather_all(slot):
        return pltpu.make_async_copy(h_hbm.at[pl.ds(0, tm * SUBLANES)], xbuf.at[slot], gsem.at[slot])

    def scatter_all(slot):
        return pltpu.make_async_copy(ybuf.at[slot], out_hbm.at[pl.ds(0, tm * SUBLANES)], ssem.at[slot])

    @pl.when(k == 0)
    def _prologue():
        ybuf[...] = jnp.zeros(ybuf.shape, F32)
        for r in range(tm):
            pltpu.make_async_copy(h_hbm.at[pl.ds(0, SUBLANES)], row(xbuf, 0, r * SUBLANES), gsem.at[0]).start()

    @pl.when(jnp.logical_or(k == 0, be_ref[k] != be_ref[jnp.maximum(k - 1, 0)]))
    def _cast():
        w1b[...] = w1_ref[0].astype(BF16)
        w2b[...] = w2_ref[0].astype(BF16)

    def phase(cur):
        nxt = 1 - cur
        gather_all(cur).wait()

        @pl.when(k >= 1)
        def _():
            scatter_all(cur).wait()

        for r in range(tm):
            src = pl.multiple_of(gnext_ref[0, 0, r], SUBLANES)
            dst = pl.multiple_of(sprev_ref[0, 0, r], SUBLANES)
            pltpu.make_async_copy(h_hbm.at[pl.ds(src, SUBLANES)], row(xbuf, nxt, r * SUBLANES),
                                  gsem.at[nxt]).start(priority=r % 2)
            pltpu.make_async_copy(row(ybuf, nxt, r * SUBLANES), out_hbm.at[pl.ds(dst, SUBLANES)],
                                  ssem.at[nxt]).start()
        x = jnp.concatenate([xbuf[cur, pl.ds(sub, tm, stride=SUBLANES), :] for sub in range(SUBLANES)], axis=-1)
        hc = _dot(x.astype(BF16), w1b[...]) + b1_ref[0]
        hg = jnp.minimum(hc[:, :D_FF], SWIGLU_LIMIT)
        hl = jnp.clip(hc[:, D_FF:], -SWIGLU_LIMIT, SWIGLU_LIMIT)
        act = hg * _sigmoid(SWIGLU_ALPHA * hg) * (hl + 1.0)
        y = _dot(act.astype(BF16), w2b[...]) + b2_ref[0]
        for sub in range(SUBLANES):
            ybuf[cur, pl.ds(sub, tm, stride=SUBLANES), :] = y[:, sub * LANES:(sub + 1) * LANES]

        @pl.when(k == n_steps - 1)
        def _epilogue():
            gather_all(nxt).wait()
            scatter_all(nxt).wait()

    for parity in range(2):
        pl.when(lax.rem(k, 2) == parity)(functools.partial(phase, parity))


def _moe(h, gtok, sdst, block_e, p, tm):
    n, d = h.shape[0] // SUBLANES, D_MODEL
    n_steps = block_e.shape[0]
    grid_spec = pltpu.PrefetchScalarGridSpec(
        num_scalar_prefetch=1,
        grid=(n_steps,),
        in_specs=[
            pl.BlockSpec((1, 1, tm), lambda k, be: (jnp.minimum(k + 1, n_steps - 1), 0, 0),
                         memory_space=pltpu.SMEM),
            pl.BlockSpec((1, 1, tm), lambda k, be: (jnp.maximum(k - 1, 0), 0, 0), memory_space=pltpu.SMEM),
            pl.BlockSpec(memory_space=pl.ANY),
            pl.BlockSpec((1, d, 2 * D_FF), lambda k, be: (be[k], 0, 0)),
            pl.BlockSpec((1, 1, 2 * D_FF), lambda k, be: (be[k], 0, 0)),
            pl.BlockSpec((1, D_FF, d), lambda k, be: (be[k], 0, 0)),
            pl.BlockSpec((1, 1, d), lambda k, be: (be[k], 0, 0)),
        ],
        out_specs=pl.BlockSpec(memory_space=pl.ANY),
        scratch_shapes=[
            pltpu.VMEM((2, tm * SUBLANES, LANES), F32), pltpu.VMEM((2, tm * SUBLANES, LANES), F32),
            pltpu.VMEM((d, 2 * D_FF), BF16), pltpu.VMEM((D_FF, d), BF16),
            pltpu.SemaphoreType.DMA((2,)), pltpu.SemaphoreType.DMA((2,)),
        ],
    )
    return pl.pallas_call(
        functools.partial(_moe_body, tm=tm, n_steps=n_steps),
        grid_spec=grid_spec,
        out_shape=jax.ShapeDtypeStruct(((n * TOP_K + tm) * SUBLANES, LANES), F32),
        compiler_params=pltpu.CompilerParams(dimension_semantics=("arbitrary",), vmem_limit_bytes=VMEM_LIMIT),
        name="moe_experts",
    )(block_e, gtok.reshape(n_steps, 1, tm), sdst.reshape(n_steps, 1, tm), h, p['w1_e'], p['b1_e'], p['w2_e'],
      p['b2_e'])


def _route_plan(top_i, tm):
    n = top_i.shape[0]
    na = n * TOP_K
    flat_e = top_i.reshape(na)
    order = jnp.argsort(flat_e).astype(jnp.int32)
    counts = jnp.sum((flat_e[:, None] == jnp.arange(N_EXPERTS, dtype=jnp.int32)[None, :]).astype(jnp.int32), axis=0)
    padded = (counts + tm - 1) // tm * tm
    starts = jnp.cumsum(counts) - counts
    pends = jnp.cumsum(padded)
    pstarts = pends - padded
    n_steps = -(-(na + N_EXPERTS * (tm - 1)) // tm) + 2
    blk_start = (jnp.arange(n_steps, dtype=jnp.int32) - 1) * tm
    block_e = jnp.sum((blk_start[:, None] >= pends[None, :]).astype(jnp.int32), axis=1)
    block_e = jnp.minimum(block_e, N_EXPERTS - 1)
    block_e = block_e.at[0].set(block_e[1])
    lane = jnp.arange(tm, dtype=jnp.int32)[None, :]
    local = blk_start[:, None] + lane - pstarts[block_e][:, None]
    valid = (local >= 0) & (local < counts[block_e][:, None]) & (blk_start[:, None] >= 0)
    asg = order[jnp.clip(starts[block_e][:, None] + local, 0, na - 1)]
    gtok = jnp.where(valid, asg // TOP_K, 0)
    sdst = jnp.where(valid, (asg % TOP_K) * n + asg // TOP_K, TOP_K * n + lane)
    return gtok * SUBLANES, sdst * SUBLANES, block_e


def _combine_body(x_ref, gt_ref, gn_ref, *refs, final):
    slot_refs, y_ref = refs[:TOP_K], refs[TOP_K]
    x = x_ref[...]
    gt = gt_ref[...]
    tm = x.shape[0]
    for j in range(TOP_K):
        slot = jnp.concatenate([slot_refs[j][pl.ds(sub, tm, stride=SUBLANES), :] for sub in range(SUBLANES)],
                               axis=-1)
        x = x + gt[:, j:j + 1] * slot
    y_ref[...] = _rmsnorm(x, gn_ref[...]) if final else x


def _combine(x2, gates, slots, gn, tm, n_all, row0, final):
    n, d = x2.shape
    assert n % tm == 0 and row0 % tm == 0 and n_all % tm == 0
    slot_spec = lambda j: pl.BlockSpec((tm * SUBLANES, LANES), lambda i: ((j * n_all + row0) // tm + i, 0))
    return pl.pallas_call(
        functools.partial(_combine_body, final=final),
        grid=(n // tm,),
        in_specs=[pl.BlockSpec((tm, d), lambda i: (i, 0)), pl.BlockSpec((tm, LANES), lambda i: (i, 0)),
                  pl.BlockSpec((1, d), lambda i: (0, 0))] + [slot_spec(j) for j in range(TOP_K)],
        out_specs=pl.BlockSpec((tm, d), lambda i: (i, 0)),
        out_shape=jax.ShapeDtypeStruct((n, d), F32),
        compiler_params=pltpu.CompilerParams(dimension_semantics=("parallel",), vmem_limit_bytes=VMEM_LIMIT),
        name="combine",
    )(x2, gates, gn, *([slots] * TOP_K))


def _layer_params(l, norm_mix, w_in, mu_shift, w0, w2_decay, a0, a2_iclr, g2_gate, k_k, k_a, r_k, lnx_w, lnx_b,
                  conv_w, a_log, dt_bias, gdn_norm_w, w_out, norm_cross, norm_mem, wq_x, wk_x, wv_x, wo_x,
                  norm_ffn, router_w, router_b, w1_e, b1_e, w2_e, b2_e):
    row = lambda z: z.reshape(1, -1).astype(F32)
    lane_pad = lambda z, at: jnp.zeros((1, LANES), F32).at[0, at:at + z.shape[0]].set(z)
    return {
        'norm_mix': row(norm_mix[l]),
        'w_in_r': w_in[l][:, :R_PROJ].astype(BF16),
        'w_in_g': jnp.pad(w_in[l][:, R_PROJ:], ((0, 0), (0, G_PROJ_PAD - G_PROJ))).astype(BF16),
        'mu': row(mu_shift[l]), 'w0': row(w0[l]), 'a0': row(a0[l]),
        'w2': jnp.pad(w2_decay[l], ((0, LORA_A), (0, 0))).astype(BF16),
        'a2': jnp.pad(a2_iclr[l], ((LORA_W, 0), (0, 0))).astype(BF16),
        'g2': g2_gate[l].astype(BF16),
        'k_k': row(k_k[l]), 'k_a': row(k_a[l]), 'r_k': row(r_k[l]), 'lnx_w': row(lnx_w[l]), 'lnx_b': row(lnx_b[l]),
        'conv_w': conv_w[l].astype(F32),
        'a_log': lane_pad(a_log[l], H_G), 'dt_bias': lane_pad(dt_bias[l], H_G),
        'gdn_norm_w': row(gdn_norm_w[l]),
        'w_out_r': w_out[l][:W_RWKV].astype(BF16), 'w_out_g': w_out[l][W_RWKV:].astype(BF16),
        'norm_cross': row(norm_cross[l]), 'norm_mem': row(norm_mem[l]),
        'wq': wq_x[l].astype(BF16), 'wk': wk_x[l].astype(BF16), 'wv': wv_x[l].astype(BF16),
        'wo': wo_x[l].astype(BF16),
        'norm_ffn': row(norm_ffn[l]),
        'router_w': jnp.concatenate(_split2(jnp.pad(router_w[l].astype(F32), ((0, 0), (0, LANES - N_EXPERTS)))),
                                    axis=1),
        'router_b': jnp.full((1, LANES), NEG_BIG, F32).at[0, :N_EXPERTS].set(router_b[l].astype(F32)),
        'w1_e': w1_e[l], 'b1_e': b1_e[l][:, None, :], 'w2_e': w2_e[l], 'b2_e': b2_e[l][:, None, :],
    }


def _mix_and_attend(x, mk, mv, shift_prev, s_r, conv_buf, s_g, p, *, chunk, mix_bb, tm, bb, tq):
    b, t, d = x.shape
    assert t >= CONV_W - 1
    pr, pg = _norm_proj(x.reshape(b * t, d), p['norm_mix'], [p['w_in_r'], p['w_in_g']], [F32, F32], tm)
    pr = pr.reshape(b, t, R_PROJ)
    pg = pg.reshape(b, t, G_PROJ_PAD)
    shift_new = pr[:, t - 1]
    conv_new = pg[:, t - (CONV_W - 1):, :GDN_CONV_DIM]
    t_pad = -(-t // chunk) * chunk
    if t_pad != t:
        pr = jnp.pad(pr, ((0, 0), (0, t_pad - t), (0, 0)))
        pg = jnp.pad(pg, ((0, 0), (0, t_pad - t), (0, 0)))
    t_valid = chunk if t_pad == t else t
    y_r, s_r_new = _rwkv_mix(pr, shift_prev, s_r, p, chunk, t_valid, mix_bb)
    y_g, s_g_new = _gdn_mix(pg, conv_buf, s_g, p, chunk, t_valid, mix_bb)
    if t_pad != t:
        y_r, y_g = y_r[:, :t], y_g[:, :t]
    x1, q = _out_q(x.reshape(b * t, d), y_r.reshape(b * t, W_RWKV), y_g.reshape(b * t, W_GDN), p, tm)
    x2, h, top_i, gates = _attn_route(q.reshape(b, t, d), x1.reshape(b, t, d), mk, mv, p, bb, tq)
    return (x2.reshape(b * t, d), h, top_i.reshape(b * t, LANES), gates.reshape(b * t, LANES),
            shift_new, s_r_new, conv_new, s_g_new)


def kernel(x_prompt, x_sample, mem_prompt, state_rwkv, state_rwkv_shift, state_gdn, state_gdn_conv, cache_mem_k, cache_mem_v, norm_mix, w_in, mu_shift, w0, w2_decay, a0, a2_iclr, g2_gate, k_k, k_a, r_k, lnx_w, lnx_b, conv_w, a_log, dt_bias, gdn_norm_w, w_out, norm_cross, norm_mem, wq_x, wk_x, wv_x, wo_x, norm_ffn, router_w, router_b, w1_e, b1_e, w2_e, b2_e, final_norm):
    bp, tp, d = x_prompt.shape
    bs, ts, _ = x_sample.shape
    depth = w_in.shape[0]
    np_, ns = bp * tp, bs * ts
    xp, xs = x_prompt, x_sample
    outs = [[] for _ in range(10)]
    for l in range(depth):
        p = _layer_params(l, norm_mix, w_in, mu_shift, w0, w2_decay, a0, a2_iclr, g2_gate, k_k, k_a, r_k, lnx_w,
                          lnx_b, conv_w, a_log, dt_bias, gdn_norm_w, w_out, norm_cross, norm_mem, wq_x, wk_x, wv_x,
                          wo_x, norm_ffn, router_w, router_b, w1_e, b1_e, w2_e, b2_e)
        n_mem = mem_prompt.shape[1]
        mk, mv = _norm_proj(mem_prompt.reshape(bp * n_mem, d), p['norm_mem'], [p['wk'], p['wv']], [F32, F32], 256)
        mk, mv = mk.reshape(bp, n_mem, d), mv.reshape(bp, n_mem, d)
        res_p = _mix_and_attend(
            xp, mk, mv, jnp.zeros((bp, R_PROJ), F32), jnp.zeros((bp, H_R, N_R, N_R), F32),
            jnp.zeros((bp, CONV_W - 1, GDN_CONV_DIM), F32), jnp.zeros((bp, H_G, D_G, D_G), F32), p,
            chunk=MIX_CHUNK, mix_bb=4, tm=256, bb=1, tq=512)
        head_rows = lambda c: c.reshape(bs, n_mem, H_X, D_X // LANES, LANES).transpose(0, 1, 3, 2, 4).reshape(-1, LANES)
        mk_s, mv_s = head_rows(cache_mem_k[l]), head_rows(cache_mem_v[l])
        res_s = _mix_and_attend(
            xs, mk_s, mv_s, state_rwkv_shift[l], state_rwkv[l], state_gdn_conv[l],
            state_gdn[l], p, chunk=SUBLANES, mix_bb=8, tm=256, bb=8, tq=ts)
        h = jnp.concatenate([res_p[1], res_s[1]], axis=0)
        top_i = jnp.concatenate([res_p[2], res_s[2]], axis=0)[:, :TOP_K]
        gtok, sdst, block_e = _route_plan(top_i, MOE_ROWS)
        slots = _moe(h, gtok, sdst, block_e, p, MOE_ROWS)
        gn = final_norm.reshape(1, d).astype(F32)
        last = l == depth - 1
        xp = _combine(res_p[0], res_p[3], slots, gn, 256, np_ + ns, 0, last).reshape(bp, tp, d)
        xs = _combine(res_s[0], res_s[3], slots, gn, 256, np_ + ns, np_, last).reshape(bs, ts, d)
        new = [res_p[5], res_p[4], res_p[7], res_p[6], mk.reshape(bp, n_mem, H_X, D_X),
               mv.reshape(bp, n_mem, H_X, D_X), res_s[5], res_s[4], res_s[7], res_s[6]]
        for acc, val in zip(outs, new):
            acc.append(val)
    return (xp, xs) + tuple(jnp.stack(o) for o in outs)
```

```python
import functools

import jax
import jax.numpy as jnp
from jax import lax
from jax.experimental import pallas as pl
from jax.experimental.pallas import tpu as pltpu

F32 = jnp.float32
BF16 = jnp.bfloat16
DEFAULT = lax.Precision.DEFAULT

D_MODEL = 1024
W_RWKV = 512
N_R = 64
H_R = W_RWKV // N_R
LORA_W = 64
LORA_A = 64
LORA_G = 128
R_PROJ = 3 * W_RWKV + LORA_W + LORA_A + LORA_G
GN_EPS = 64e-5
W_GDN = 512
D_G = 128
H_G = W_GDN // D_G
GDN_CONV_DIM = 3 * W_GDN
CONV_W = 4
G_PROJ = GDN_CONV_DIM + W_GDN + 2 * H_G
LANES = 128
SUBLANES = 8
G_PROJ_PAD = GDN_CONV_DIM + W_GDN + LANES
MIX_CHUNK = 64
N_MEM = 256
H_X = 4
D_X = D_MODEL // H_X
N_EXPERTS = 32
TOP_K = 4
D_FF = D_MODEL
SWIGLU_LIMIT = 7.0
SWIGLU_ALPHA = 1.702
MOE_ROWS = 256
RMS_EPS = 1e-6
L2_EPS = 1e-6
NEG_BIG = -1e30
VMEM_LIMIT = 56 * 1024 * 1024


def _dot(a, b, precision=DEFAULT):
    return jnp.dot(a, b, preferred_element_type=F32, precision=precision)


def _dot_nt(a, b, precision=DEFAULT):
    return lax.dot_general(a, b, (((1,), (1,)), ((), ())), preferred_element_type=F32, precision=precision)


NN = (((1,), (0,)), ((), ()))
NT = (((1,), (1,)), ((), ()))
TN = (((0,), (0,)), ((), ()))


def _split2(x):
    hi = x.astype(BF16)
    return hi, (x - hi.astype(F32)).astype(BF16)


def _mm(a, b, dims):
    return lax.dot_general(a.astype(BF16), b.astype(BF16), dims, preferred_element_type=F32)


def _sel_mm(sel, x, dims):
    dg = lambda y: lax.dot_general(sel, y, dims, preferred_element_type=F32)
    hi = x.astype(BF16)
    r1 = x - hi.astype(F32)
    mid = r1.astype(BF16)
    lo = (r1 - mid.astype(F32)).astype(BF16)
    return dg(hi) + (dg(mid) + dg(lo))


def _sigmoid(x):
    return 1.0 / (1.0 + jnp.exp(-x))


def _softplus(x):
    return jnp.maximum(x, 0.0) + jnp.log(1.0 + jnp.exp(-jnp.abs(x)))


def _rmsnorm(x, g):
    return x * lax.rsqrt(jnp.mean(x * x, axis=-1, keepdims=True) + RMS_EPS) * g


def _tri_masks(c):
    row = lax.broadcasted_iota(jnp.int32, (c, c), 0)
    col = lax.broadcasted_iota(jnp.int32, (c, c), 1)
    return col <= row, col < row, (col == row).astype(F32)


def _unit_lower_inverses(ms, eye, c):
    ts = [eye + m for m in ms]
    ps = list(ms)
    covered = 2
    while covered < c:
        ps = [_mm(p, p, NN) for p in ps]
        ts = [t + _mm(t, p, NN) for t, p in zip(ts, ps)]
        covered *= 2
    return ts


def _norm_proj_body(x_ref, g_ref, *refs, n_out):
    w_refs, o_refs = refs[:n_out], refs[n_out:]
    hb = _rmsnorm(x_ref[...], g_ref[...]).astype(BF16)
    for w_ref, o_ref in zip(w_refs, o_refs):
        o_ref[...] = _dot(hb, w_ref[...]).astype(o_ref.dtype)


def _norm_proj(x, g, ws, out_dtypes, tm):
    n, d = x.shape
    assert n % tm == 0
    in_specs = [pl.BlockSpec((tm, d), lambda i: (i, 0)), pl.BlockSpec((1, d), lambda i: (0, 0))]
    in_specs += [pl.BlockSpec(w.shape, lambda i: (0, 0)) for w in ws]
    return pl.pallas_call(
        functools.partial(_norm_proj_body, n_out=len(ws)),
        grid=(n // tm,),
        in_specs=in_specs,
        out_specs=[pl.BlockSpec((tm, w.shape[1]), lambda i: (i, 0)) for w in ws],
        out_shape=[jax.ShapeDtypeStruct((n, w.shape[1]), dt) for w, dt in zip(ws, out_dtypes)],
        compiler_params=pltpu.CompilerParams(dimension_semantics=("parallel",), vmem_limit_bytes=VMEM_LIMIT),
        name="norm_proj",
    )(x, g, *ws)


def _rwkv_body(pr_ref, shift_ref, s0_ref, mu_ref, w0_ref, w2_ref, a0_ref, a2_ref, g2_ref, kk_ref, ka_ref,
               rk_ref, lnw_ref, lnb_ref, y_ref, sout_ref, s_scr, prev_scr, *, bb, chunk, t_valid, n_chunks):
    c = pl.program_id(1)

    @pl.when(c == 0)
    def _init():
        s_scr[...] = s0_ref[...]
        prev_scr[...] = shift_ref[...]

    incl, strict, eye = _tri_masks(chunk)
    incl_bf = incl.astype(BF16)
    rows = lax.broadcasted_iota(jnp.int32, (chunk, 1), 0)
    sls = [slice(h * N_R, (h + 1) * N_R) for h in range(H_R)]

    def per_batch(bi):
        pr = pr_ref[bi]
        prev = jnp.where(rows == 0, prev_scr[bi], pltpu.roll(pr, 1, 0))
        prev_scr[bi] = pr[chunk - 1:chunk, :]
        xm = pr + (prev - pr) * mu_ref[...]
        r = xm[:, :W_RWKV]
        k = xm[:, W_RWKV:2 * W_RWKV]
        v = xm[:, 2 * W_RWKV:3 * W_RWKV]
        lo = xm[:, 3 * W_RWKV:3 * W_RWKV + LORA_W + LORA_A]
        g_lo = xm[:, 3 * W_RWKV + LORA_W + LORA_A:]
        logw = -_softplus(-(w0_ref[...] + _dot(jnp.tanh(lo).astype(BF16), w2_ref[...]))) - 0.5
        wl = -jnp.exp(logw)
        a = _sigmoid(a0_ref[...] + _dot(lo.astype(BF16), a2_ref[...]))
        g = _dot(_sigmoid(g_lo).astype(BF16), g2_ref[...])
        kkv = k * kk_ref[...]
        k = k * (1.0 + (a - 1.0) * ka_ref[...])
        if t_valid < chunk:
            valid = rows < t_valid
            wl = jnp.where(valid, wl, 0.0)
            kkv = jnp.where(valid, kkv, 0.0)
            k = jnp.where(valid, k, 0.0)
            v = jnp.where(valid, v, 0.0)
        cum = _sel_mm(incl_bf, wl, NN)
        cum_last = cum[chunk - 1:chunk, :]
        return dict(r=r, k=k, v=v, a=a, g=g, kkv=kkv, w_incl=jnp.exp(cum), w_prev=jnp.exp(cum - wl),
                    w_inv=jnp.exp(-cum), w_tail=jnp.exp(cum_last - cum), w_last=jnp.exp(cum_last))

    pre = [per_batch(bi) for bi in range(bb)]
    chains = [(bi, h) for bi in range(bb) for h in range(H_R)]
    col = lambda name: [pre[bi][name][:, sls[h]] for bi, h in chains]
    n = range(len(chains))
    r_, k_, v_, a_ = col('r'), col('k'), col('v'), col('a')
    w_incl, w_prev, w_inv, w_tail, w_last = col('w_incl'), col('w_prev'), col('w_inv'), col('w_tail'), col('w_last')
    kks = [x * lax.rsqrt(jnp.sum(x * x, axis=-1, keepdims=True) + L2_EPS) for x in col('kkv')]
    a_hat = [-(kks[i] * w_prev[i]) for i in n]
    kka = [kks[i] * a_[i] for i in n]
    b_hat = [kka[i] * w_inv[i] for i in n]
    k_hat = [k_[i] * w_inv[i] for i in n]
    r_hat = [r_[i] * w_incl[i] for i in n]
    cross = [_mm(jnp.concatenate([a_hat[i], r_hat[i]], axis=0), jnp.concatenate([b_hat[i], k_hat[i]], axis=0), NT)
             for i in n]
    row2 = lax.broadcasted_iota(jnp.int32, (chunk, 2 * chunk), 0)
    col2 = lax.broadcasted_iota(jnp.int32, (chunk, 2 * chunk), 1)
    col2 = jnp.where(col2 >= chunk, col2 - chunk, col2)
    strict2, incl2 = col2 < row2, col2 <= row2
    m_top = [jnp.where(strict2, x[:chunk], 0.0) for x in cross]
    a_bot = [jnp.where(incl2, x[chunk:], 0.0) for x in cross]
    t_inv = _unit_lower_inverses([x[:, :chunk] for x in m_top], eye, chunk)
    w_hat = [_mm(t_inv[i], a_hat[i], NN) for i in n]
    mv = [_mm(m_top[i], jnp.concatenate([jnp.zeros_like(v_[i]), v_[i]], axis=0), NN) for i in n]
    u = [_mm(t_inv[i], mv[i], NN) for i in n]
    ss = [s_scr[bi, h] for bi, h in chains]
    p = [_mm(w_hat[i], ss[i], NT) + u[i] for i in n]
    pv = [jnp.concatenate([p[i], v_[i]], axis=0) for i in n]
    ys = [_mm(r_hat[i], ss[i], NT) + _mm(a_bot[i], pv[i], NN) for i in n]
    for i, (bi, h) in enumerate(chains):
        tails = jnp.concatenate([kka[i] * w_tail[i], k_[i] * w_tail[i]], axis=0)
        s_scr[bi, h] = ss[i] * w_last[i] + _mm(pv[i], tails, TN)
    for i, (bi, h) in enumerate(chains):
        sl = sls[h]
        y = ys[i]
        mean = jnp.mean(y, axis=-1, keepdims=True)
        yc = y - mean
        var = jnp.mean(yc * yc, axis=-1, keepdims=True)
        yn = yc * lax.rsqrt(var + GN_EPS) * lnw_ref[:, sl] + lnb_ref[:, sl]
        bonus = jnp.sum(r_[i] * k_[i] * rk_ref[:, sl], axis=-1, keepdims=True) * v_[i]
        y_ref[bi, :, sl] = ((yn + bonus) * pre[bi]['g'][:, sl]).astype(y_ref.dtype)

    @pl.when(c == n_chunks - 1)
    def _fin():
        sout_ref[...] = s_scr[...]


def _rwkv_mix(pr, shift_prev, s0, p, chunk, t_valid, bb):
    b, t, _ = pr.shape
    assert b % bb == 0 and t % chunk == 0
    n_chunks = t // chunk
    const = lambda shape: pl.BlockSpec(shape, lambda i, j: (0,) * len(shape))
    return pl.pallas_call(
        functools.partial(_rwkv_body, bb=bb, chunk=chunk, t_valid=t_valid, n_chunks=n_chunks),
        grid=(b // bb, n_chunks),
        in_specs=[
            pl.BlockSpec((bb, chunk, R_PROJ), lambda i, j: (i, j, 0)),
            pl.BlockSpec((bb, 1, R_PROJ), lambda i, j: (i, 0, 0)),
            pl.BlockSpec((bb, H_R, N_R, N_R), lambda i, j: (i, 0, 0, 0)),
            const((1, R_PROJ)), const((1, W_RWKV)), const((LANES, W_RWKV)), const((1, W_RWKV)),
            const((LANES, W_RWKV)), const((LORA_G, W_RWKV)), const((1, W_RWKV)), const((1, W_RWKV)),
            const((1, W_RWKV)), const((1, W_RWKV)), const((1, W_RWKV)),
        ],
        out_specs=[
            pl.BlockSpec((bb, chunk, W_RWKV), lambda i, j: (i, j, 0)),
            pl.BlockSpec((bb, H_R, N_R, N_R), lambda i, j: (i, 0, 0, 0)),
        ],
        out_shape=[jax.ShapeDtypeStruct((b, t, W_RWKV), BF16), jax.ShapeDtypeStruct((b, H_R, N_R, N_R), F32)],
        scratch_shapes=[pltpu.VMEM((bb, H_R, N_R, N_R), F32), pltpu.VMEM((bb, 1, R_PROJ), F32)],
        compiler_params=pltpu.CompilerParams(dimension_semantics=("arbitrary", "arbitrary"),
                                             vmem_limit_bytes=VMEM_LIMIT),
        name="rwkv_mix",
    )(pr, shift_prev[:, None, :], s0, p['mu'], p['w0'], p['w2'], p['a0'], p['a2'], p['g2'], p['k_k'], p['k_a'],
      p['r_k'], p['lnx_w'], p['lnx_b'])


def _gdn_body(pg_ref, cbuf_ref, s0_ref, cw_ref, alog_ref, dtb_ref, nw_ref, y_ref, sout_ref, s_scr, xp_scr,
              *, bb, chunk, t_valid, n_chunks):
    c = pl.program_id(1)

    @pl.when(c == 0)
    def _init():
        s_scr[...] = s0_ref[...]
        xp_scr[:, 0:SUBLANES, :] = cbuf_ref[...]

    incl, strict, eye = _tri_masks(chunk)
    incl_bf = incl.astype(BF16)
    lane = lax.broadcasted_iota(jnp.int32, (chunk, LANES), 1)
    sls = [slice(h * D_G, (h + 1) * D_G) for h in range(H_G)]

    def per_batch(bi):
        xp_scr[bi, SUBLANES:SUBLANES + chunk, :] = pg_ref[bi, :, :GDN_CONV_DIM]
        base = SUBLANES - (CONV_W - 1)
        conv = xp_scr[bi, base:base + chunk, :] * cw_ref[0:1, :]
        for j in range(1, CONV_W):
            conv = conv + xp_scr[bi, base + j:base + j + chunk, :] * cw_ref[j:j + 1, :]
        xp_scr[bi, 0:SUBLANES, :] = xp_scr[bi, chunk:chunk + SUBLANES, :]
        qkv = conv * _sigmoid(conv)
        z = pg_ref[bi, :, GDN_CONV_DIM:GDN_CONV_DIM + W_GDN]
        ba = pg_ref[bi, :, GDN_CONV_DIM + W_GDN:]
        beta_blk = _sigmoid(ba)
        g_blk = -jnp.exp(alog_ref[...]) * _softplus(ba + dtb_ref[...])
        if t_valid < chunk:
            valid = lax.broadcasted_iota(jnp.int32, (chunk, 1), 0) < t_valid
            beta_blk = jnp.where(valid, beta_blk, 0.0)
            g_blk = jnp.where(valid, g_blk, 0.0)
        gc_blk = _sel_mm(incl_bf, g_blk, NN)
        return dict(qkv=qkv, z=z, beta_blk=beta_blk, gc_blk=gc_blk)

    pre = [per_batch(bi) for bi in range(bb)]
    chains = [(bi, h) for bi in range(bb) for h in range(H_G)]
    n = range(len(chains))
    qs = [pre[bi]['qkv'][:, sls[h]] for bi, h in chains]
    qs = [x * lax.rsqrt(jnp.sum(x * x, axis=-1, keepdims=True) + L2_EPS) * (D_G ** -0.5) for x in qs]
    ks = [pre[bi]['qkv'][:, W_GDN + h * D_G:W_GDN + (h + 1) * D_G] for bi, h in chains]
    ks = [x * lax.rsqrt(jnp.sum(x * x, axis=-1, keepdims=True) + L2_EPS) for x in ks]
    vs = [pre[bi]['qkv'][:, 2 * W_GDN + h * D_G:2 * W_GDN + (h + 1) * D_G] for bi, h in chains]
    betas = [pre[bi]['beta_blk'][:, h:h + 1] for bi, h in chains]
    gcols = [pre[bi]['gc_blk'][:, H_G + h:H_G + h + 1] for bi, h in chains]
    grows = [_sel_mm((lane == H_G + h).astype(BF16), pre[bi]['gc_blk'], NT) for bi, h in chains]
    g_last = [gcols[i][chunk - 1:chunk, :] for i in n]
    decay = [jnp.where(incl, jnp.exp(jnp.where(incl, gcols[i] - grows[i], 0.0)), 0.0) for i in n]
    k_beta = [ks[i] * betas[i] for i in n]
    cross = [_mm(jnp.concatenate([k_beta[i], qs[i]], axis=0), ks[i], NT) for i in n]
    lmat = [jnp.where(strict, cross[i][:chunk] * decay[i], 0.0) for i in n]
    attn = [jnp.where(incl, cross[i][chunk:] * decay[i], 0.0) for i in n]
    t_inv = _unit_lower_inverses([-x for x in lmat], eye, chunk)
    e_gc = [jnp.exp(gcols[i]) for i in n]
    uw = [_mm(t_inv[i], jnp.concatenate([vs[i] * betas[i], k_beta[i] * e_gc[i]], axis=1), NN) for i in n]
    ss = [s_scr[bi, h] for bi, h in chains]
    v_new = [uw[i][:, :D_G] - _mm(uw[i][:, D_G:], ss[i], NN) for i in n]
    os_ = [_mm(jnp.concatenate([qs[i] * e_gc[i], attn[i]], axis=1), jnp.concatenate([ss[i], v_new[i]], axis=0), NN)
           for i in n]
    for i, (bi, h) in enumerate(chains):
        s_scr[bi, h] = ss[i] * jnp.exp(g_last[i]) + _mm(ks[i] * jnp.exp(g_last[i] - gcols[i]), v_new[i], TN)
    for i, (bi, h) in enumerate(chains):
        o = os_[i]
        o = o * lax.rsqrt(jnp.mean(o * o, axis=-1, keepdims=True) + RMS_EPS) * nw_ref[...]
        z_h = pre[bi]['z'][:, sls[h]]
        y_ref[bi, :, sls[h]] = (o * (z_h * _sigmoid(z_h))).astype(y_ref.dtype)

    @pl.when(c == n_chunks - 1)
    def _fin():
        sout_ref[...] = s_scr[...]


def _gdn_mix(pg, conv_buf, s0, p, chunk, t_valid, bb):
    b, t, _ = pg.shape
    assert b % bb == 0 and t % chunk == 0
    n_chunks = t // chunk
    cbuf = jnp.pad(conv_buf, ((0, 0), (SUBLANES - (CONV_W - 1), 0), (0, 0)))
    const = lambda shape: pl.BlockSpec(shape, lambda i, j: (0,) * len(shape))
    return pl.pallas_call(
        functools.partial(_gdn_body, bb=bb, chunk=chunk, t_valid=t_valid, n_chunks=n_chunks),
        grid=(b // bb, n_chunks),
        in_specs=[
            pl.BlockSpec((bb, chunk, G_PROJ_PAD), lambda i, j: (i, j, 0)),
            pl.BlockSpec((bb, SUBLANES, GDN_CONV_DIM), lambda i, j: (i, 0, 0)),
            pl.BlockSpec((bb, H_G, D_G, D_G), lambda i, j: (i, 0, 0, 0)),
            const((CONV_W, GDN_CONV_DIM)), const((1, LANES)), const((1, LANES)), const((1, D_G)),
        ],
        out_specs=[
            pl.BlockSpec((bb, chunk, W_GDN), lambda i, j: (i, j, 0)),
            pl.BlockSpec((bb, H_G, D_G, D_G), lambda i, j: (i, 0, 0, 0)),
        ],
        out_shape=[jax.ShapeDtypeStruct((b, t, W_GDN), BF16), jax.ShapeDtypeStruct((b, H_G, D_G, D_G), F32)],
        scratch_shapes=[pltpu.VMEM((bb, H_G, D_G, D_G), F32),
                        pltpu.VMEM((bb, SUBLANES + chunk, GDN_CONV_DIM), F32)],
        compiler_params=pltpu.CompilerParams(dimension_semantics=("arbitrary", "arbitrary"),
                                             vmem_limit_bytes=VMEM_LIMIT),
        name="gdn_mix",
    )(pg, cbuf, s0, p['conv_w'], p['a_log'], p['dt_bias'], p['gdn_norm_w'])


def _out_q_body(x_ref, yr_ref, yg_ref, wor_ref, wog_ref, gn_ref, wq_ref, x1_ref, q_ref):
    x1 = x_ref[...] + _dot(yr_ref[...], wor_ref[...]) + _dot(yg_ref[...], wog_ref[...])
    x1_ref[...] = x1
    q_ref[...] = _dot(_rmsnorm(x1, gn_ref[...]).astype(BF16), wq_ref[...]).astype(q_ref.dtype)


def _out_q(x, yr, yg, p, tm):
    n, d = x.shape
    assert n % tm == 0
    row = lambda w: pl.BlockSpec((tm, w), lambda i: (i, 0))
    const = lambda shape: pl.BlockSpec(shape, lambda i: (0, 0))
    return pl.pallas_call(
        _out_q_body,
        grid=(n // tm,),
        in_specs=[row(d), row(W_RWKV), row(W_GDN), const((W_RWKV, d)), const((W_GDN, d)), const((1, d)),
                  const((d, d))],
        out_specs=[row(d), row(d)],
        out_shape=[jax.ShapeDtypeStruct((n, d), F32), jax.ShapeDtypeStruct((n, d), BF16)],
        compiler_params=pltpu.CompilerParams(dimension_semantics=("parallel",), vmem_limit_bytes=VMEM_LIMIT),
        name="out_q",
    )(x, yr, yg, p['w_out_r'], p['w_out_g'], p['norm_cross'], p['wq'])


def _attn_body(q_ref, x_ref, mk_ref, mv_ref, wo_ref, gn_ref, rw_ref, rb_ref, x2_ref, h_ref, ti_ref, gt_ref,
               *, bb, tq, head_rows):
    d_tiles = D_X // LANES
    mem_rows = N_MEM * d_tiles * H_X

    def head_mem(ref, i, hh):
        if not head_rows:
            return ref[i, :, hh * D_X:(hh + 1) * D_X]
        return jnp.concatenate(
            [ref[pl.ds(i * mem_rows + dt * H_X + hh, N_MEM, stride=d_tiles * H_X), :] for dt in range(d_tiles)],
            axis=-1)

    os_ = []
    for i in range(bb):
        q = q_ref[i]
        heads = []
        for hh in range(H_X):
            sl = slice(hh * D_X, (hh + 1) * D_X)
            s = _dot_nt(q[:, sl], head_mem(mk_ref, i, hh).astype(BF16)) * (D_X ** -0.5)
            e = jnp.exp(s - jnp.max(s, axis=-1, keepdims=True))
            prob = e / jnp.sum(e, axis=-1, keepdims=True)
            heads.append(_dot(prob.astype(BF16), head_mem(mv_ref, i, hh).astype(BF16)))
        os_.append(jnp.concatenate(heads, axis=-1))
    rows = bb * tq
    o = (os_[0] if bb == 1 else jnp.concatenate(os_, axis=0)).astype(BF16)
    x1 = x_ref[0] if bb == 1 else jnp.concatenate([x_ref[i] for i in range(bb)], axis=0)
    x2 = x1 + _dot(o, wo_ref[...])
    h = _rmsnorm(x2, gn_ref[...])
    for sub in range(SUBLANES):
        h_ref[pl.ds(sub, rows, stride=SUBLANES), :] = h[:, sub * LANES:(sub + 1) * LANES]
    h_hi, h_lo = _split2(h)
    pieces = _dot(jnp.concatenate([h_hi, h_lo], axis=0), rw_ref[...])
    logits = ((pieces[:rows, :LANES] + pieces[:rows, LANES:]) + (pieces[rows:, :LANES] + pieces[rows:, LANES:])
              + rb_ref[...])
    lane = lax.broadcasted_iota(jnp.int32, logits.shape, 1)
    vals, idxs = [], []
    for _ in range(TOP_K):
        m = jnp.max(logits, axis=-1, keepdims=True)
        first = jnp.min(jnp.where(logits == m, lane, LANES), axis=-1, keepdims=True)
        vals.append(m)
        idxs.append(first)
        logits = jnp.where(lane == first, -jnp.inf, logits)
    es = [jnp.exp(vv - vals[0]) for vv in vals]
    den = es[0] + es[1] + es[2] + es[3]
    ti = jnp.zeros(lane.shape, jnp.int32)
    gt = jnp.zeros(lane.shape, F32)
    for j in range(TOP_K):
        ti = jnp.where(lane == j, idxs[j], ti)
        gt = jnp.where(lane == j, es[j] / den, gt)
    for i in range(bb):
        x2_ref[i] = x2[i * tq:(i + 1) * tq]
        ti_ref[i] = ti[i * tq:(i + 1) * tq]
        gt_ref[i] = gt[i * tq:(i + 1) * tq]


def _attn_route(q, x1, mk, mv, p, bb, tq):
    b, t, d = x1.shape
    assert d == SUBLANES * LANES
    rows = bb * tq
    assert b % bb == 0 and t % tq == 0 and (bb == 1 or tq == t)
    n_tq = t // tq
    blk = lambda w: pl.BlockSpec((bb, tq, w), lambda i, j: (i, j, 0))
    head_rows = mk.ndim == 2
    mem = (pl.BlockSpec((bb * (mk.shape[0] // b), LANES), lambda i, j: (i, 0)) if head_rows
           else pl.BlockSpec((bb, N_MEM, d), lambda i, j: (i, 0, 0)))
    const = lambda shape: pl.BlockSpec(shape, lambda i, j: (0, 0))
    return pl.pallas_call(
        functools.partial(_attn_body, bb=bb, tq=tq, head_rows=head_rows),
        grid=(b // bb, n_tq),
        in_specs=[blk(d), blk(d), mem, mem, const((d, d)), const((1, d)), const((d, 2 * LANES)), const((1, LANES))],
        out_specs=[blk(d), pl.BlockSpec((rows * SUBLANES, LANES), lambda i, j: (i * n_tq + j, 0)), blk(LANES),
                   blk(LANES)],
        out_shape=[jax.ShapeDtypeStruct((b, t, d), F32), jax.ShapeDtypeStruct((b * t * SUBLANES, LANES), F32),
                   jax.ShapeDtypeStruct((b, t, LANES), jnp.int32), jax.ShapeDtypeStruct((b, t, LANES), F32)],
        compiler_params=pltpu.CompilerParams(dimension_semantics=("parallel", "parallel"),
                                             vmem_limit_bytes=VMEM_LIMIT),
        name="attn_route",
    )(q, x1, mk, mv, p['wo'], p['norm_ffn'], p['router_w'], p['router_b'])


def _moe_body(be_ref, gnext_ref, sprev_ref, h_hbm, w1_ref, b1_ref, w2_ref, b2_ref, out_hbm,
              xbuf, ybuf, w1b, w2b, gsem, ssem, *, tm, n_steps):
    k = pl.program_id(0)

    def row(ref, slot, at):
        return ref.at[slot, pl.ds(at, SUBLANES)]

    def gather_all(slot):
        return pltpu.make_async_copy(h_hbm.at[pl.ds(0, tm * SUBLANES)], xbuf.at[slot], gsem.at[slot])

    def scatter_all(slot):
        return pltpu.make_async_copy(ybuf.at[slot], out_hbm.at[pl.ds(0, tm * SUBLANES)], ssem.at[slot])

    @pl.when(k == 0)
    def _prologue():
        ybuf[...] = jnp.zeros(ybuf.shape, F32)
        for r in range(tm):
            pltpu.make_async_copy(h_hbm.at[pl.ds(0, SUBLANES)], row(xbuf, 0, r * SUBLANES), gsem.at[0]).start()

    @pl.when(jnp.logical_or(k == 0, be_ref[k] != be_ref[jnp.maximum(k - 1, 0)]))
    def _cast():
        w1b[...] = w1_ref[0].astype(BF16)
        w2b[...] = w2_ref[0].astype(BF16)

    def phase(cur):
        nxt = 1 - cur
        gather_all(cur).wait()

        @pl.when(k >= 1)
        def _():
            scatter_all(cur).wait()

        for r in range(tm):
            src = pl.multiple_of(gnext_ref[0, 0, r], SUBLANES)
            dst = pl.multiple_of(sprev_ref[0, 0, r], SUBLANES)
            pltpu.make_async_copy(h_hbm.at[pl.ds(src, SUBLANES)], row(xbuf, nxt, r * SUBLANES), gsem.at[nxt]).start()
            pltpu.make_async_copy(row(ybuf, nxt, r * SUBLANES), out_hbm.at[pl.ds(dst, SUBLANES)],
                                  ssem.at[nxt]).start()
        x = jnp.concatenate([xbuf[cur, pl.ds(sub, tm, stride=SUBLANES), :] for sub in range(SUBLANES)], axis=-1)
        hc = _dot(x.astype(BF16), w1b[...]) + b1_ref[0]
        hg = jnp.minimum(hc[:, :D_FF], SWIGLU_LIMIT)
        hl = jnp.clip(hc[:, D_FF:], -SWIGLU_LIMIT, SWIGLU_LIMIT)
        act = hg * _sigmoid(SWIGLU_ALPHA * hg) * (hl + 1.0)
        y = _dot(act.astype(BF16), w2b[...]) + b2_ref[0]
        for sub in range(SUBLANES):
            ybuf[cur, pl.ds(sub, tm, stride=SUBLANES), :] = y[:, sub * LANES:(sub + 1) * LANES]

        @pl.when(k == n_steps - 1)
        def _epilogue():
            gather_all(nxt).wait()
            scatter_all(nxt).wait()

    for parity in range(2):
        pl.when(lax.rem(k, 2) == parity)(functools.partial(phase, parity))


def _moe(h, gtok, sdst, block_e, p, tm):
    n, d = h.shape[0] // SUBLANES, D_MODEL
    n_steps = block_e.shape[0]
    grid_spec = pltpu.PrefetchScalarGridSpec(
        num_scalar_prefetch=1,
        grid=(n_steps,),
        in_specs=[
            pl.BlockSpec((1, 1, tm), lambda k, be: (jnp.minimum(k + 1, n_steps - 1), 0, 0),
                         memory_space=pltpu.SMEM),
            pl.BlockSpec((1, 1, tm), lambda k, be: (jnp.maximum(k - 1, 0), 0, 0), memory_space=pltpu.SMEM),
            pl.BlockSpec(memory_space=pl.ANY),
            pl.BlockSpec((1, d, 2 * D_FF), lambda k, be: (be[k], 0, 0)),
            pl.BlockSpec((1, 1, 2 * D_FF), lambda k, be: (be[k], 0, 0)),
            pl.BlockSpec((1, D_FF, d), lambda k, be: (be[k], 0, 0)),
            pl.BlockSpec((1, 1, d), lambda k, be: (be[k], 0, 0)),
        ],
        out_specs=pl.BlockSpec(memory_space=pl.ANY),
        scratch_shapes=[
            pltpu.VMEM((2, tm * SUBLANES, LANES), F32), pltpu.VMEM((2, tm * SUBLANES, LANES), F32),
            pltpu.VMEM((d, 2 * D_FF), BF16), pltpu.VMEM((D_FF, d), BF16),
            pltpu.SemaphoreType.DMA((2,)), pltpu.SemaphoreType.DMA((2,)),
        ],
    )
    return pl.pallas_call(
        functools.partial(_moe_body, tm=tm, n_steps=n_steps),
        grid_spec=grid_spec,
        out_shape=jax.ShapeDtypeStruct(((n * TOP_K + tm) * SUBLANES, LANES), F32),
        compiler_params=pltpu.CompilerParams(dimension_semantics=("arbitrary",), vmem_limit_bytes=VMEM_LIMIT),
        name="moe_experts",
    )(block_e, gtok.reshape(n_steps, 1, tm), sdst.reshape(n_steps, 1, tm), h, p['w1_e'], p['b1_e'], p['w2_e'],
      p['b2_e'])


def _route_plan(top_i, tm):
    n = top_i.shape[0]
    na = n * TOP_K
    flat_e = top_i.reshape(na)
    order = jnp.argsort(flat_e).astype(jnp.int32)
    counts = jnp.sum((flat_e[:, None] == jnp.arange(N_EXPERTS, dtype=jnp.int32)[None, :]).astype(jnp.int32), axis=0)
    padded = (counts + tm - 1) // tm * tm
    starts = jnp.cumsum(counts) - counts
    pends = jnp.cumsum(padded)
    pstarts = pends - padded
    n_steps = -(-(na + N_EXPERTS * (tm - 1)) // tm) + 2
    blk_start = (jnp.arange(n_steps, dtype=jnp.int32) - 1) * tm
    block_e = jnp.sum((blk_start[:, None] >= pends[None, :]).astype(jnp.int32), axis=1)
    block_e = jnp.minimum(block_e, N_EXPERTS - 1)
    block_e = block_e.at[0].set(block_e[1])
    lane = jnp.arange(tm, dtype=jnp.int32)[None, :]
    local = blk_start[:, None] + lane - pstarts[block_e][:, None]
    valid = (local >= 0) & (local < counts[block_e][:, None]) & (blk_start[:, None] >= 0)
    asg = order[jnp.clip(starts[block_e][:, None] + local, 0, na - 1)]
    gtok = jnp.where(valid, asg // TOP_K, 0)
    sdst = jnp.where(valid, (asg % TOP_K) * n + asg // TOP_K, TOP_K * n + lane)
    return gtok * SUBLANES, sdst * SUBLANES, block_e


def _combine_body(x_ref, gt_ref, gn_ref, *refs, final):
    slot_refs, y_ref = refs[:TOP_K], refs[TOP_K]
    x = x_ref[...]
    gt = gt_ref[...]
    tm = x.shape[0]
    for j in range(TOP_K):
        slot = jnp.concatenate([slot_refs[j][pl.ds(sub, tm, stride=SUBLANES), :] for sub in range(SUBLANES)],
                               axis=-1)
        x = x + gt[:, j:j + 1] * slot
    y_ref[...] = _rmsnorm(x, gn_ref[...]) if final else x


def _combine(x2, gates, slots, gn, tm, n_all, row0, final):
    n, d = x2.shape
    assert n % tm == 0 and row0 % tm == 0 and n_all % tm == 0
    slot_spec = lambda j: pl.BlockSpec((tm * SUBLANES, LANES), lambda i: ((j * n_all + row0) // tm + i, 0))
    return pl.pallas_call(
        functools.partial(_combine_body, final=final),
        grid=(n // tm,),
        in_specs=[pl.BlockSpec((tm, d), lambda i: (i, 0)), pl.BlockSpec((tm, LANES), lambda i: (i, 0)),
                  pl.BlockSpec((1, d), lambda i: (0, 0))] + [slot_spec(j) for j in range(TOP_K)],
        out_specs=pl.BlockSpec((tm, d), lambda i: (i, 0)),
        out_shape=jax.ShapeDtypeStruct((n, d), F32),
        compiler_params=pltpu.CompilerParams(dimension_semantics=("parallel",), vmem_limit_bytes=VMEM_LIMIT),
        name="combine",
    )(x2, gates, gn, *([slots] * TOP_K))


def _layer_params(l, norm_mix, w_in, mu_shift, w0, w2_decay, a0, a2_iclr, g2_gate, k_k, k_a, r_k, lnx_w, lnx_b,
                  conv_w, a_log, dt_bias, gdn_norm_w, w_out, norm_cross, norm_mem, wq_x, wk_x, wv_x, wo_x,
                  norm_ffn, router_w, router_b, w1_e, b1_e, w2_e, b2_e):
    row = lambda z: z.reshape(1, -1).astype(F32)
    lane_pad = lambda z, at: jnp.zeros((1, LANES), F32).at[0, at:at + z.shape[0]].set(z)
    return {
        'norm_mix': row(norm_mix[l]),
        'w_in_r': w_in[l][:, :R_PROJ].astype(BF16),
        'w_in_g': jnp.pad(w_in[l][:, R_PROJ:], ((0, 0), (0, G_PROJ_PAD - G_PROJ))).astype(BF16),
        'mu': row(mu_shift[l]), 'w0': row(w0[l]), 'a0': row(a0[l]),
        'w2': jnp.pad(w2_decay[l], ((0, LORA_A), (0, 0))).astype(BF16),
        'a2': jnp.pad(a2_iclr[l], ((LORA_W, 0), (0, 0))).astype(BF16),
        'g2': g2_gate[l].astype(BF16),
        'k_k': row(k_k[l]), 'k_a': row(k_a[l]), 'r_k': row(r_k[l]), 'lnx_w': row(lnx_w[l]), 'lnx_b': row(lnx_b[l]),
        'conv_w': conv_w[l].astype(F32),
        'a_log': lane_pad(a_log[l], H_G), 'dt_bias': lane_pad(dt_bias[l], H_G),
        'gdn_norm_w': row(gdn_norm_w[l]),
        'w_out_r': w_out[l][:W_RWKV].astype(BF16), 'w_out_g': w_out[l][W_RWKV:].astype(BF16),
        'norm_cross': row(norm_cross[l]), 'norm_mem': row(norm_mem[l]),
        'wq': wq_x[l].astype(BF16), 'wk': wk_x[l].astype(BF16), 'wv': wv_x[l].astype(BF16),
        'wo': wo_x[l].astype(BF16),
        'norm_ffn': row(norm_ffn[l]),
        'router_w': jnp.concatenate(_split2(jnp.pad(router_w[l].astype(F32), ((0, 0), (0, LANES - N_EXPERTS)))),
                                    axis=1),
        'router_b': jnp.full((1, LANES), NEG_BIG, F32).at[0, :N_EXPERTS].set(router_b[l].astype(F32)),
        'w1_e': w1_e[l], 'b1_e': b1_e[l][:, None, :], 'w2_e': w2_e[l], 'b2_e': b2_e[l][:, None, :],
    }


def _mix_and_attend(x, mk, mv, shift_prev, s_r, conv_buf, s_g, p, *, chunk, mix_bb, tm, bb, tq):
    b, t, d = x.shape
    assert t >= CONV_W - 1
    pr, pg = _norm_proj(x.reshape(b * t, d), p['norm_mix'], [p['w_in_r'], p['w_in_g']], [F32, F32], tm)
    pr = pr.reshape(b, t, R_PROJ)
    pg = pg.reshape(b, t, G_PROJ_PAD)
    shift_new = pr[:, t - 1]
    conv_new = pg[:, t - (CONV_W - 1):, :GDN_CONV_DIM]
    t_pad = -(-t // chunk) * chunk
    if t_pad != t:
        pr = jnp.pad(pr, ((0, 0), (0, t_pad - t), (0, 0)))
        pg = jnp.pad(pg, ((0, 0), (0, t_pad - t), (0, 0)))
    t_valid = chunk if t_pad == t else t
    y_r, s_r_new = _rwkv_mix(pr, shift_prev, s_r, p, chunk, t_valid, mix_bb)
    y_g, s_g_new = _gdn_mix(pg, conv_buf, s_g, p, chunk, t_valid, mix_bb)
    if t_pad != t:
        y_r, y_g = y_r[:, :t], y_g[:, :t]
    x1, q = _out_q(x.reshape(b * t, d), y_r.reshape(b * t, W_RWKV), y_g.reshape(b * t, W_GDN), p, tm)
    x2, h, top_i, gates = _attn_route(q.reshape(b, t, d), x1.reshape(b, t, d), mk, mv, p, bb, tq)
    return (x2.reshape(b * t, d), h, top_i.reshape(b * t, LANES), gates.reshape(b * t, LANES),
            shift_new, s_r_new, conv_new, s_g_new)


def kernel(x_prompt, x_sample, mem_prompt, state_rwkv, state_rwkv_shift, state_gdn, state_gdn_conv, cache_mem_k, cache_mem_v, norm_mix, w_in, mu_shift, w0, w2_decay, a0, a2_iclr, g2_gate, k_k, k_a, r_k, lnx_w, lnx_b, conv_w, a_log, dt_bias, gdn_norm_w, w_out, norm_cross, norm_mem, wq_x, wk_x, wv_x, wo_x, norm_ffn, router_w, router_b, w1_e, b1_e, w2_e, b2_e, final_norm):
    bp, tp, d = x_prompt.shape
    bs, ts, _ = x_sample.shape
    depth = w_in.shape[0]
    np_, ns = bp * tp, bs * ts
    xp, xs = x_prompt, x_sample
    outs = [[] for _ in range(10)]
    for l in range(depth):
        p = _layer_params(l, norm_mix, w_in, mu_shift, w0, w2_decay, a0, a2_iclr, g2_gate, k_k, k_a, r_k, lnx_w,
                          lnx_b, conv_w, a_log, dt_bias, gdn_norm_w, w_out, norm_cross, norm_mem, wq_x, wk_x, wv_x,
                          wo_x, norm_ffn, router_w, router_b, w1_e, b1_e, w2_e, b2_e)
        n_mem = mem_prompt.shape[1]
        mk, mv = _norm_proj(mem_prompt.reshape(bp * n_mem, d), p['norm_mem'], [p['wk'], p['wv']], [F32, F32], 256)
        mk, mv = mk.reshape(bp, n_mem, d), mv.reshape(bp, n_mem, d)
        res_p = _mix_and_attend(
            xp, mk, mv, jnp.zeros((bp, R_PROJ), F32), jnp.zeros((bp, H_R, N_R, N_R), F32),
            jnp.zeros((bp, CONV_W - 1, GDN_CONV_DIM), F32), jnp.zeros((bp, H_G, D_G, D_G), F32), p,
            chunk=MIX_CHUNK, mix_bb=4, tm=256, bb=1, tq=512)
        head_rows = lambda c: c.reshape(bs, n_mem, H_X, D_X // LANES, LANES).transpose(0, 1, 3, 2, 4).reshape(-1, LANES)
        mk_s, mv_s = head_rows(cache_mem_k[l]), head_rows(cache_mem_v[l])
        res_s = _mix_and_attend(
            xs, mk_s, mv_s, state_rwkv_shift[l], state_rwkv[l], state_gdn_conv[l],
            state_gdn[l], p, chunk=SUBLANES, mix_bb=8, tm=256, bb=8, tq=ts)
        h = jnp.concatenate([res_p[1], res_s[1]], axis=0)
        top_i = jnp.concatenate([res_p[2], res_s[2]], axis=0)[:, :TOP_K]
        gtok, sdst, block_e = _route_plan(top_i, MOE_ROWS)
        slots = _moe(h, gtok, sdst, block_e, p, MOE_ROWS)
        gn = final_norm.reshape(1, d).astype(F32)
        last = l == depth - 1
        xp = _combine(res_p[0], res_p[3], slots, gn, 256, np_ + ns, 0, last).reshape(bp, tp, d)
        xs = _combine(res_s[0], res_s[3], slots, gn, 256, np_ + ns, np_, last).reshape(bs, ts, d)
        new = [res_p[5], res_p[4], res_p[7], res_p[6], mk.reshape(bp, n_mem, H_X, D_X),
               mv.reshape(bp, n_mem, H_X, D_X), res_s[5], res_s[4], res_s[7], res_s[6]]
        for acc, val in zip(outs, new):
            acc.append(val)
    return (xp, xs) + tuple(jnp.stack(o) for o in outs)
```

```python
import functools

import jax
import jax.numpy as jnp
from jax import lax
from jax.experimental import pallas as pl
from jax.experimental.pallas import tpu as pltpu

F32 = jnp.float32
BF16 = jnp.bfloat16
DEFAULT = lax.Precision.DEFAULT

D_MODEL = 1024
W_RWKV = 512
N_R = 64
H_R = W_RWKV // N_R
LORA_W = 64
LORA_A = 64
LORA_G = 128
R_PROJ = 3 * W_RWKV + LORA_W + LORA_A + LORA_G
GN_EPS = 64e-5
W_GDN = 512
D_G = 128
H_G = W_GDN // D_G
GDN_CONV_DIM = 3 * W_GDN
CONV_W = 4
G_PROJ = GDN_CONV_DIM + W_GDN + 2 * H_G
LANES = 128
SUBLANES = 8
G_PROJ_PAD = GDN_CONV_DIM + W_GDN + LANES
MIX_CHUNK = 64
N_MEM = 256
H_X = 4
D_X = D_MODEL // H_X
N_EXPERTS = 32
TOP_K = 4
D_FF = D_MODEL
SWIGLU_LIMIT = 7.0
SWIGLU_ALPHA = 1.702
MOE_ROWS = 256
RMS_EPS = 1e-6
L2_EPS = 1e-6
NEG_BIG = -1e30
VMEM_LIMIT = 56 * 1024 * 1024


def _dot(a, b, precision=DEFAULT):
    return jnp.dot(a, b, preferred_element_type=F32, precision=precision)


def _dot_nt(a, b, precision=DEFAULT):
    return lax.dot_general(a, b, (((1,), (1,)), ((), ())), preferred_element_type=F32, precision=precision)


NN = (((1,), (0,)), ((), ()))
NT = (((1,), (1,)), ((), ()))
TN = (((0,), (0,)), ((), ()))


def _split2(x):
    hi = x.astype(BF16)
    return hi, (x - hi.astype(F32)).astype(BF16)


def _mm(a, b, dims):
    return lax.dot_general(a.astype(BF16), b.astype(BF16), dims, preferred_element_type=F32)


def _sel_mm(sel, x, dims):
    dg = lambda y: lax.dot_general(sel, y, dims, preferred_element_type=F32)
    hi = x.astype(BF16)
    r1 = x - hi.astype(F32)
    mid = r1.astype(BF16)
    lo = (r1 - mid.astype(F32)).astype(BF16)
    return dg(hi) + (dg(mid) + dg(lo))


def _sigmoid(x):
    return 1.0 / (1.0 + jnp.exp(-x))


def _softplus(x):
    return jnp.maximum(x, 0.0) + jnp.log(1.0 + jnp.exp(-jnp.abs(x)))


def _rmsnorm(x, g):
    return x * lax.rsqrt(jnp.mean(x * x, axis=-1, keepdims=True) + RMS_EPS) * g


def _tri_masks(c):
    row = lax.broadcasted_iota(jnp.int32, (c, c), 0)
    col = lax.broadcasted_iota(jnp.int32, (c, c), 1)
    return col <= row, col < row, (col == row).astype(F32)


def _unit_lower_inverses(ms, eye, c):
    ts = [eye + m for m in ms]
    ps = list(ms)
    covered = 2
    while covered < c:
        ps = [_mm(p, p, NN) for p in ps]
        ts = [t + _mm(t, p, NN) for t, p in zip(ts, ps)]
        covered *= 2
    return ts


def _norm_proj_body(x_ref, g_ref, *refs, n_out):
    w_refs, o_refs = refs[:n_out], refs[n_out:]
    hb = _rmsnorm(x_ref[...], g_ref[...]).astype(BF16)
    for w_ref, o_ref in zip(w_refs, o_refs):
        o_ref[...] = _dot(hb, w_ref[...]).astype(o_ref.dtype)


def _norm_proj(x, g, ws, out_dtypes, tm):
    n, d = x.shape
    assert n % tm == 0
    in_specs = [pl.BlockSpec((tm, d), lambda i: (i, 0)), pl.BlockSpec((1, d), lambda i: (0, 0))]
    in_specs += [pl.BlockSpec(w.shape, lambda i: (0, 0)) for w in ws]
    return pl.pallas_call(
        functools.partial(_norm_proj_body, n_out=len(ws)),
        grid=(n // tm,),
        in_specs=in_specs,
        out_specs=[pl.BlockSpec((tm, w.shape[1]), lambda i: (i, 0)) for w in ws],
        out_shape=[jax.ShapeDtypeStruct((n, w.shape[1]), dt) for w, dt in zip(ws, out_dtypes)],
        compiler_params=pltpu.CompilerParams(dimension_semantics=("parallel",), vmem_limit_bytes=VMEM_LIMIT),
        name="norm_proj",
    )(x, g, *ws)


def _rwkv_body(pr_ref, shift_ref, s0_ref, mu_ref, w0_ref, w2_ref, a0_ref, a2_ref, g2_ref, kk_ref, ka_ref,
               rk_ref, lnw_ref, lnb_ref, y_ref, sout_ref, s_scr, prev_scr, *, bb, chunk, t_valid, n_chunks):
    c = pl.program_id(1)

    @pl.when(c == 0)
    def _init():
        s_scr[...] = s0_ref[...]
        prev_scr[...] = shift_ref[...]

    incl, strict, eye = _tri_masks(chunk)
    incl_bf = incl.astype(BF16)
    rows = lax.broadcasted_iota(jnp.int32, (chunk, 1), 0)
    sls = [slice(h * N_R, (h + 1) * N_R) for h in range(H_R)]

    def per_batch(bi):
        pr = pr_ref[bi]
        prev = jnp.where(rows == 0, prev_scr[bi], pltpu.roll(pr, 1, 0))
        prev_scr[bi] = pr[chunk - 1:chunk, :]
        xm = pr + (prev - pr) * mu_ref[...]
        r = xm[:, :W_RWKV]
        k = xm[:, W_RWKV:2 * W_RWKV]
        v = xm[:, 2 * W_RWKV:3 * W_RWKV]
        lo = xm[:, 3 * W_RWKV:3 * W_RWKV + LORA_W + LORA_A]
        g_lo = xm[:, 3 * W_RWKV + LORA_W + LORA_A:]
        logw = -_softplus(-(w0_ref[...] + _dot(jnp.tanh(lo).astype(BF16), w2_ref[...]))) - 0.5
        wl = -jnp.exp(logw)
        a = _sigmoid(a0_ref[...] + _dot(lo.astype(BF16), a2_ref[...]))
        g = _dot(_sigmoid(g_lo).astype(BF16), g2_ref[...])
        kkv = k * kk_ref[...]
        k = k * (1.0 + (a - 1.0) * ka_ref[...])
        if t_valid < chunk:
            valid = rows < t_valid
            wl = jnp.where(valid, wl, 0.0)
            kkv = jnp.where(valid, kkv, 0.0)
            k = jnp.where(valid, k, 0.0)
            v = jnp.where(valid, v, 0.0)
        cum = _sel_mm(incl_bf, wl, NN)
        cum_last = cum[chunk - 1:chunk, :]
        return dict(r=r, k=k, v=v, a=a, g=g, kkv=kkv, w_incl=jnp.exp(cum), w_prev=jnp.exp(cum - wl),
                    w_inv=jnp.exp(-cum), w_tail=jnp.exp(cum_last - cum), w_last=jnp.exp(cum_last))

    pre = [per_batch(bi) for bi in range(bb)]
    chains = [(bi, h) for bi in range(bb) for h in range(H_R)]
    col = lambda name: [pre[bi][name][:, sls[h]] for bi, h in chains]
    n = range(len(chains))
    r_, k_, v_, a_ = col('r'), col('k'), col('v'), col('a')
    w_incl, w_prev, w_inv, w_tail, w_last = col('w_incl'), col('w_prev'), col('w_inv'), col('w_tail'), col('w_last')
    kks = [x * lax.rsqrt(jnp.sum(x * x, axis=-1, keepdims=True) + L2_EPS) for x in col('kkv')]
    a_hat = [-(kks[i] * w_prev[i]) for i in n]
    kka = [kks[i] * a_[i] for i in n]
    b_hat = [kka[i] * w_inv[i] for i in n]
    k_hat = [k_[i] * w_inv[i] for i in n]
    r_hat = [r_[i] * w_incl[i] for i in n]
    cross = [_mm(jnp.concatenate([a_hat[i], r_hat[i]], axis=0), jnp.concatenate([b_hat[i], k_hat[i]], axis=0), NT)
             for i in n]
    row2 = lax.broadcasted_iota(jnp.int32, (chunk, 2 * chunk), 0)
    col2 = lax.broadcasted_iota(jnp.int32, (chunk, 2 * chunk), 1)
    col2 = jnp.where(col2 >= chunk, col2 - chunk, col2)
    strict2, incl2 = col2 < row2, col2 <= row2
    m_top = [jnp.where(strict2, x[:chunk], 0.0) for x in cross]
    a_bot = [jnp.where(incl2, x[chunk:], 0.0) for x in cross]
    t_inv = _unit_lower_inverses([x[:, :chunk] for x in m_top], eye, chunk)
    w_hat = [_mm(t_inv[i], a_hat[i], NN) for i in n]
    mv = [_mm(m_top[i], jnp.concatenate([jnp.zeros_like(v_[i]), v_[i]], axis=0), NN) for i in n]
    u = [_mm(t_inv[i], mv[i], NN) for i in n]
    ss = [s_scr[bi, h] for bi, h in chains]
    p = [_mm(w_hat[i], ss[i], NT) + u[i] for i in n]
    pv = [jnp.concatenate([p[i], v_[i]], axis=0) for i in n]
    ys = [_mm(r_hat[i], ss[i], NT) + _mm(a_bot[i], pv[i], NN) for i in n]
    for i, (bi, h) in enumerate(chains):
        tails = jnp.concatenate([kka[i] * w_tail[i], k_[i] * w_tail[i]], axis=0)
        s_scr[bi, h] = ss[i] * w_last[i] + _mm(pv[i], tails, TN)
    for i, (bi, h) in enumerate(chains):
        sl = sls[h]
        y = ys[i]
        mean = jnp.mean(y, axis=-1, keepdims=True)
        yc = y - mean
        var = jnp.mean(yc * yc, axis=-1, keepdims=True)
        yn = yc * lax.rsqrt(var + GN_EPS) * lnw_ref[:, sl] + lnb_ref[:, sl]
        bonus = jnp.sum(r_[i] * k_[i] * rk_ref[:, sl], axis=-1, keepdims=True) * v_[i]
        y_ref[bi, :, sl] = ((yn + bonus) * pre[bi]['g'][:, sl]).astype(y_ref.dtype)

    @pl.when(c == n_chunks - 1)
    def _fin():
        sout_ref[...] = s_scr[...]


def _rwkv_mix(pr, shift_prev, s0, p, chunk, t_valid, bb):
    b, t, _ = pr.shape
    assert b % bb == 0 and t % chunk == 0
    n_chunks = t // chunk
    const = lambda shape: pl.BlockSpec(shape, lambda i, j: (0,) * len(shape))
    return pl.pallas_call(
        functools.partial(_rwkv_body, bb=bb, chunk=chunk, t_valid=t_valid, n_chunks=n_chunks),
        grid=(b // bb, n_chunks),
        in_specs=[
            pl.BlockSpec((bb, chunk, R_PROJ), lambda i, j: (i, j, 0)),
            pl.BlockSpec((bb, 1, R_PROJ), lambda i, j: (i, 0, 0)),
            pl.BlockSpec((bb, H_R, N_R, N_R), lambda i, j: (i, 0, 0, 0)),
            const((1, R_PROJ)), const((1, W_RWKV)), const((LANES, W_RWKV)), const((1, W_RWKV)),
            const((LANES, W_RWKV)), const((LORA_G, W_RWKV)), const((1, W_RWKV)), const((1, W_RWKV)),
            const((1, W_RWKV)), const((1, W_RWKV)), const((1, W_RWKV)),
        ],
        out_specs=[
            pl.BlockSpec((bb, chunk, W_RWKV), lambda i, j: (i, j, 0)),
            pl.BlockSpec((bb, H_R, N_R, N_R), lambda i, j: (i, 0, 0, 0)),
        ],
        out_shape=[jax.ShapeDtypeStruct((b, t, W_RWKV), BF16), jax.ShapeDtypeStruct((b, H_R, N_R, N_R), F32)],
        scratch_shapes=[pltpu.VMEM((bb, H_R, N_R, N_R), F32), pltpu.VMEM((bb, 1, R_PROJ), F32)],
        compiler_params=pltpu.CompilerParams(dimension_semantics=("arbitrary", "arbitrary"),
                                             vmem_limit_bytes=VMEM_LIMIT),
        name="rwkv_mix",
    )(pr, shift_prev[:, None, :], s0, p['mu'], p['w0'], p['w2'], p['a0'], p['a2'], p['g2'], p['k_k'], p['k_a'],
      p['r_k'], p['lnx_w'], p['lnx_b'])


def _gdn_body(pg_ref, cbuf_ref, s0_ref, cw_ref, alog_ref, dtb_ref, nw_ref, y_ref, sout_ref, s_scr, xp_scr,
              *, bb, chunk, t_valid, n_chunks):
    c = pl.program_id(1)

    @pl.when(c == 0)
    def _init():
        s_scr[...] = s0_ref[...]
        xp_scr[:, 0:SUBLANES, :] = cbuf_ref[...]

    incl, strict, eye = _tri_masks(chunk)
    incl_bf = incl.astype(BF16)
    lane = lax.broadcasted_iota(jnp.int32, (chunk, LANES), 1)
    sls = [slice(h * D_G, (h + 1) * D_G) for h in range(H_G)]

    def per_batch(bi):
        xp_scr[bi, SUBLANES:SUBLANES + chunk, :] = pg_ref[bi, :, :GDN_CONV_DIM]
        base = SUBLANES - (CONV_W - 1)
        conv = xp_scr[bi, base:base + chunk, :] * cw_ref[0:1, :]
        for j in range(1, CONV_W):
            conv = conv + xp_scr[bi, base + j:base + j + chunk, :] * cw_ref[j:j + 1, :]
        xp_scr[bi, 0:SUBLANES, :] = xp_scr[bi, chunk:chunk + SUBLANES, :]
        qkv = conv * _sigmoid(conv)
        z = pg_ref[bi, :, GDN_CONV_DIM:GDN_CONV_DIM + W_GDN]
        ba = pg_ref[bi, :, GDN_CONV_DIM + W_GDN:]
        beta_blk = _sigmoid(ba)
        g_blk = -jnp.exp(alog_ref[...]) * _softplus(ba + dtb_ref[...])
        if t_valid < chunk:
            valid = lax.broadcasted_iota(jnp.int32, (chunk, 1), 0) < t_valid
            beta_blk = jnp.where(valid, beta_blk, 0.0)
            g_blk = jnp.where(valid, g_blk, 0.0)
        gc_blk = _sel_mm(incl_bf, g_blk, NN)
        return dict(qkv=qkv, z=z, beta_blk=beta_blk, gc_blk=gc_blk)

    pre = [per_batch(bi) for bi in range(bb)]
    chains = [(bi, h) for bi in range(bb) for h in range(H_G)]
    n = range(len(chains))
    qs = [pre[bi]['qkv'][:, sls[h]] for bi, h in chains]
    qs = [x * lax.rsqrt(jnp.sum(x * x, axis=-1, keepdims=True) + L2_EPS) * (D_G ** -0.5) for x in qs]
    ks = [pre[bi]['qkv'][:, W_GDN + h * D_G:W_GDN + (h + 1) * D_G] for bi, h in chains]
    ks = [x * lax.rsqrt(jnp.sum(x * x, axis=-1, keepdims=True) + L2_EPS) for x in ks]
    vs = [pre[bi]['qkv'][:, 2 * W_GDN + h * D_G:2 * W_GDN + (h + 1) * D_G] for bi, h in chains]
    betas = [pre[bi]['beta_blk'][:, h:h + 1] for bi, h in chains]
    gcols = [pre[bi]['gc_blk'][:, H_G + h:H_G + h + 1] for bi, h in chains]
    grows = [_sel_mm((lane == H_G + h).astype(BF16), pre[bi]['gc_blk'], NT) for bi, h in chains]
    g_last = [gcols[i][chunk - 1:chunk, :] for i in n]
    decay = [jnp.where(incl, jnp.exp(jnp.where(incl, gcols[i] - grows[i], 0.0)), 0.0) for i in n]
    k_beta = [ks[i] * betas[i] for i in n]
    cross = [_mm(jnp.concatenate([k_beta[i], qs[i]], axis=0), ks[i], NT) for i in n]
    lmat = [jnp.where(strict, cross[i][:chunk] * decay[i], 0.0) for i in n]
    attn = [jnp.where(incl, cross[i][chunk:] * decay[i], 0.0) for i in n]
    t_inv = _unit_lower_inverses([-x for x in lmat], eye, chunk)
    e_gc = [jnp.exp(gcols[i]) for i in n]
    uw = [_mm(t_inv[i], jnp.concatenate([vs[i] * betas[i], k_beta[i] * e_gc[i]], axis=1), NN) for i in n]
    ss = [s_scr[bi, h] for bi, h in chains]
    v_new = [uw[i][:, :D_G] - _mm(uw[i][:, D_G:], ss[i], NN) for i in n]
    os_ = [_mm(jnp.concatenate([qs[i] * e_gc[i], attn[i]], axis=1), jnp.concatenate([ss[i], v_new[i]], axis=0), NN)
           for i in n]
    for i, (bi, h) in enumerate(chains):
        s_scr[bi, h] = ss[i] * jnp.exp(g_last[i]) + _mm(ks[i] * jnp.exp(g_last[i] - gcols[i]), v_new[i], TN)
    for i, (bi, h) in enumerate(chains):
        o = os_[i]
        o = o * lax.rsqrt(jnp.mean(o * o, axis=-1, keepdims=True) + RMS_EPS) * nw_ref[...]
        z_h = pre[bi]['z'][:, sls[h]]
        y_ref[bi, :, sls[h]] = (o * (z_h * _sigmoid(z_h))).astype(y_ref.dtype)

    @pl.when(c == n_chunks - 1)
    def _fin():
        sout_ref[...] = s_scr[...]


def _gdn_mix(pg, conv_buf, s0, p, chunk, t_valid, bb):
    b, t, _ = pg.shape
    assert b % bb == 0 and t % chunk == 0
    n_chunks = t // chunk
    cbuf = jnp.pad(conv_buf, ((0, 0), (SUBLANES - (CONV_W - 1), 0), (0, 0)))
    const = lambda shape: pl.BlockSpec(shape, lambda i, j: (0,) * len(shape))
    return pl.pallas_call(
        functools.partial(_gdn_body, bb=bb, chunk=chunk, t_valid=t_valid, n_chunks=n_chunks),
        grid=(b // bb, n_chunks),
        in_specs=[
            pl.BlockSpec((bb, chunk, G_PROJ_PAD), lambda i, j: (i, j, 0)),
            pl.BlockSpec((bb, SUBLANES, GDN_CONV_DIM), lambda i, j: (i, 0, 0)),
            pl.BlockSpec((bb, H_G, D_G, D_G), lambda i, j: (i, 0, 0, 0)),
            const((CONV_W, GDN_CONV_DIM)), const((1, LANES)), const((1, LANES)), const((1, D_G)),
        ],
        out_specs=[
            pl.BlockSpec((bb, chunk, W_GDN), lambda i, j: (i, j, 0)),
            pl.BlockSpec((bb, H_G, D_G, D_G), lambda i, j: (i, 0, 0, 0)),
        ],
        out_shape=[jax.ShapeDtypeStruct((b, t, W_GDN), BF16), jax.ShapeDtypeStruct((b, H_G, D_G, D_G), F32)],
        scratch_shapes=[pltpu.VMEM((bb, H_G, D_G, D_G), F32),
                        pltpu.VMEM((bb, SUBLANES + chunk, GDN_CONV_DIM), F32)],
        compiler_params=pltpu.CompilerParams(dimension_semantics=("arbitrary", "arbitrary"),
                                             vmem_limit_bytes=VMEM_LIMIT),
        name="gdn_mix",
    )(pg, cbuf, s0, p['conv_w'], p['a_log'], p['dt_bias'], p['gdn_norm_w'])


def _out_q_body(x_ref, yr_ref, yg_ref, wor_ref, wog_ref, gn_ref, wq_ref, x1_ref, q_ref):
    x1 = x_ref[...] + _dot(yr_ref[...], wor_ref[...]) + _dot(yg_ref[...], wog_ref[...])
    x1_ref[...] = x1
    q_ref[...] = _dot(_rmsnorm(x1, gn_ref[...]).astype(BF16), wq_ref[...]).astype(q_ref.dtype)


def _out_q(x, yr, yg, p, tm):
    n, d = x.shape
    assert n % tm == 0
    row = lambda w: pl.BlockSpec((tm, w), lambda i: (i, 0))
    const = lambda shape: pl.BlockSpec(shape, lambda i: (0, 0))
    return pl.pallas_call(
        _out_q_body,
        grid=(n // tm,),
        in_specs=[row(d), row(W_RWKV), row(W_GDN), const((W_RWKV, d)), const((W_GDN, d)), const((1, d)),
                  const((d, d))],
        out_specs=[row(d), row(d)],
        out_shape=[jax.ShapeDtypeStruct((n, d), F32), jax.ShapeDtypeStruct((n, d), BF16)],
        compiler_params=pltpu.CompilerParams(dimension_semantics=("parallel",), vmem_limit_bytes=VMEM_LIMIT),
        name="out_q",
    )(x, yr, yg, p['w_out_r'], p['w_out_g'], p['norm_cross'], p['wq'])


def _attn_body(q_ref, x_ref, mk_ref, mv_ref, wo_ref, gn_ref, rw_ref, rb_ref, x2_ref, h_ref, ti_ref, gt_ref,
               *, bb, tq, head_rows):
    d_tiles = D_X // LANES
    mem_rows = N_MEM * d_tiles * H_X

    def head_mem(ref, i, hh):
        if not head_rows:
            return ref[i, :, hh * D_X:(hh + 1) * D_X]
        return jnp.concatenate(
            [ref[pl.ds(i * mem_rows + dt * H_X + hh, N_MEM, stride=d_tiles * H_X), :] for dt in range(d_tiles)],
            axis=-1)

    os_ = []
    for i in range(bb):
        q = q_ref[i]
        heads = []
        for hh in range(H_X):
            sl = slice(hh * D_X, (hh + 1) * D_X)
            s = _dot_nt(q[:, sl], head_mem(mk_ref, i, hh).astype(BF16)) * (D_X ** -0.5)
            e = jnp.exp(s - jnp.max(s, axis=-1, keepdims=True))
            prob = e / jnp.sum(e, axis=-1, keepdims=True)
            heads.append(_dot(prob.astype(BF16), head_mem(mv_ref, i, hh).astype(BF16)))
        os_.append(jnp.concatenate(heads, axis=-1))
    rows = bb * tq
    o = (os_[0] if bb == 1 else jnp.concatenate(os_, axis=0)).astype(BF16)
    x1 = x_ref[0] if bb == 1 else jnp.concatenate([x_ref[i] for i in range(bb)], axis=0)
    x2 = x1 + _dot(o, wo_ref[...])
    h = _rmsnorm(x2, gn_ref[...])
    for sub in range(SUBLANES):
        h_ref[pl.ds(sub, rows, stride=SUBLANES), :] = h[:, sub * LANES:(sub + 1) * LANES]
    h_hi, h_lo = _split2(h)
    pieces = _dot(jnp.concatenate([h_hi, h_lo], axis=0), rw_ref[...])
    logits = ((pieces[:rows, :LANES] + pieces[:rows, LANES:]) + (pieces[rows:, :LANES] + pieces[rows:, LANES:])
              + rb_ref[...])
    lane = lax.broadcasted_iota(jnp.int32, logits.shape, 1)
    vals, idxs = [], []
    for _ in range(TOP_K):
        m = jnp.max(logits, axis=-1, keepdims=True)
        first = jnp.min(jnp.where(logits == m, lane, LANES), axis=-1, keepdims=True)
        vals.append(m)
        idxs.append(first)
        logits = jnp.where(lane == first, -jnp.inf, logits)
    es = [jnp.exp(vv - vals[0]) for vv in vals]
    den = es[0] + es[1] + es[2] + es[3]
    ti = jnp.zeros(lane.shape, jnp.int32)
    gt = jnp.zeros(lane.shape, F32)
    for j in range(TOP_K):
        ti = jnp.where(lane == j, idxs[j], ti)
        gt = jnp.where(lane == j, es[j] / den, gt)
    for i in range(bb):
        x2_ref[i] = x2[i * tq:(i + 1) * tq]
        ti_ref[i] = ti[i * tq:(i + 1) * tq]
        gt_ref[i] = gt[i * tq:(i + 1) * tq]


def _attn_route(q, x1, mk, mv, p, bb, tq):
    b, t, d = x1.shape
    assert d == SUBLANES * LANES
    rows = bb * tq
    assert b % bb == 0 and t % tq == 0 and (bb == 1 or tq == t)
    n_tq = t // tq
    blk = lambda w: pl.BlockSpec((bb, tq, w), lambda i, j: (i, j, 0))
    head_rows = mk.ndim == 2
    mem = (pl.BlockSpec((bb * (mk.shape[0] // b), LANES), lambda i, j: (i, 0)) if head_rows
           else pl.BlockSpec((bb, N_MEM, d), lambda i, j: (i, 0, 0)))
    const = lambda shape: pl.BlockSpec(shape, lambda i, j: (0, 0))
    return pl.pallas_call(
        functools.partial(_attn_body, bb=bb, tq=tq, head_rows=head_rows),
        grid=(b // bb, n_tq),
        in_specs=[blk(d), blk(d), mem, mem, const((d, d)), const((1, d)), const((d, 2 * LANES)), const((1, LANES))],
        out_specs=[blk(d), pl.BlockSpec((rows * SUBLANES, LANES), lambda i, j: (i * n_tq + j, 0)), blk(LANES),
                   blk(LANES)],
        out_shape=[jax.ShapeDtypeStruct((b, t, d), F32), jax.ShapeDtypeStruct((b * t * SUBLANES, LANES), F32),
                   jax.ShapeDtypeStruct((b, t, LANES), jnp.int32), jax.ShapeDtypeStruct((b, t, LANES), F32)],
        compiler_params=pltpu.CompilerParams(dimension_semantics=("parallel", "parallel"),
                                             vmem_limit_bytes=VMEM_LIMIT),
        name="attn_route",
    )(q, x1, mk, mv, p['wo'], p['norm_ffn'], p['router_w'], p['router_b'])


def _moe_body(be_ref, live_ref, gnext_ref, sprev_ref, h_hbm, w1_ref, b1_ref, w2_ref, b2_ref, out_hbm,
              xbuf, ybuf, w1b, w2b, gsem, ssem, *, tm, n_steps):
    k = pl.program_id(0)

    def row(ref, slot, at):
        return ref.at[slot, pl.ds(at, SUBLANES)]

    def gather_all(slot):
        return pltpu.make_async_copy(h_hbm.at[pl.ds(0, tm * SUBLANES)], xbuf.at[slot], gsem.at[slot])

    def scatter_all(slot):
        return pltpu.make_async_copy(ybuf.at[slot], out_hbm.at[pl.ds(0, tm * SUBLANES)], ssem.at[slot])

    @pl.when(k == 0)
    def _prologue():
        ybuf[...] = jnp.zeros(ybuf.shape, F32)
        for r in range(tm):
            pltpu.make_async_copy(h_hbm.at[pl.ds(0, SUBLANES)], row(xbuf, 0, r * SUBLANES), gsem.at[0]).start()

    @pl.when(jnp.logical_or(k == 0, be_ref[k] != be_ref[jnp.maximum(k - 1, 0)]))
    def _cast():
        w1b[...] = w1_ref[0].astype(BF16)
        w2b[...] = w2_ref[0].astype(BF16)

    def phase(cur):
        nxt = 1 - cur
        gather_all(cur).wait()

        @pl.when(k >= 1)
        def _():
            scatter_all(cur).wait()

        for r in range(tm):
            src = pl.multiple_of(gnext_ref[0, 0, r], SUBLANES)
            dst = pl.multiple_of(sprev_ref[0, 0, r], SUBLANES)
            pltpu.make_async_copy(h_hbm.at[pl.ds(src, SUBLANES)], row(xbuf, nxt, r * SUBLANES), gsem.at[nxt]).start()
            pltpu.make_async_copy(row(ybuf, nxt, r * SUBLANES), out_hbm.at[pl.ds(dst, SUBLANES)],
                                  ssem.at[nxt]).start()

        @pl.when(live_ref[k] > 0)
        def _compute():
            x = jnp.concatenate([xbuf[cur, pl.ds(sub, tm, stride=SUBLANES), :] for sub in range(SUBLANES)],
                                axis=-1)
            hc = _dot(x.astype(BF16), w1b[...]) + b1_ref[0]
            hg = jnp.minimum(hc[:, :D_FF], SWIGLU_LIMIT)
            hl = jnp.clip(hc[:, D_FF:], -SWIGLU_LIMIT, SWIGLU_LIMIT)
            act = hg * _sigmoid(SWIGLU_ALPHA * hg) * (hl + 1.0)
            y = _dot(act.astype(BF16), w2b[...]) + b2_ref[0]
            for sub in range(SUBLANES):
                ybuf[cur, pl.ds(sub, tm, stride=SUBLANES), :] = y[:, sub * LANES:(sub + 1) * LANES]

        @pl.when(k == n_steps - 1)
        def _epilogue():
            gather_all(nxt).wait()
            scatter_all(nxt).wait()

    for parity in range(2):
        pl.when(lax.rem(k, 2) == parity)(functools.partial(phase, parity))


def _moe(h, gtok, sdst, block_e, block_live, p, tm):
    n, d = h.shape[0] // SUBLANES, D_MODEL
    n_steps = block_e.shape[0]
    grid_spec = pltpu.PrefetchScalarGridSpec(
        num_scalar_prefetch=2,
        grid=(n_steps,),
        in_specs=[
            pl.BlockSpec((1, 1, tm), lambda k, be, live: (jnp.minimum(k + 1, n_steps - 1), 0, 0),
                         memory_space=pltpu.SMEM),
            pl.BlockSpec((1, 1, tm), lambda k, be, live: (jnp.maximum(k - 1, 0), 0, 0), memory_space=pltpu.SMEM),
            pl.BlockSpec(memory_space=pl.ANY),
            pl.BlockSpec((1, d, 2 * D_FF), lambda k, be, live: (be[k], 0, 0)),
            pl.BlockSpec((1, 1, 2 * D_FF), lambda k, be, live: (be[k], 0, 0)),
            pl.BlockSpec((1, D_FF, d), lambda k, be, live: (be[k], 0, 0)),
            pl.BlockSpec((1, 1, d), lambda k, be, live: (be[k], 0, 0)),
        ],
        out_specs=pl.BlockSpec(memory_space=pl.ANY),
        scratch_shapes=[
            pltpu.VMEM((2, tm * SUBLANES, LANES), F32), pltpu.VMEM((2, tm * SUBLANES, LANES), F32),
            pltpu.VMEM((d, 2 * D_FF), BF16), pltpu.VMEM((D_FF, d), BF16),
            pltpu.SemaphoreType.DMA((2,)), pltpu.SemaphoreType.DMA((2,)),
        ],
    )
    return pl.pallas_call(
        functools.partial(_moe_body, tm=tm, n_steps=n_steps),
        grid_spec=grid_spec,
        out_shape=jax.ShapeDtypeStruct(((n * TOP_K + tm) * SUBLANES, LANES), F32),
        compiler_params=pltpu.CompilerParams(dimension_semantics=("arbitrary",), vmem_limit_bytes=VMEM_LIMIT),
        name="moe_experts",
    )(block_e, block_live, gtok.reshape(n_steps, 1, tm), sdst.reshape(n_steps, 1, tm), h, p['w1_e'], p['b1_e'], p['w2_e'],
      p['b2_e'])


def _route_plan(top_i, tm):
    n = top_i.shape[0]
    na = n * TOP_K
    flat_e = top_i.reshape(na)
    order = jnp.argsort(flat_e).astype(jnp.int32)
    counts = jnp.sum((flat_e[:, None] == jnp.arange(N_EXPERTS, dtype=jnp.int32)[None, :]).astype(jnp.int32), axis=0)
    padded = (counts + tm - 1) // tm * tm
    starts = jnp.cumsum(counts) - counts
    pends = jnp.cumsum(padded)
    pstarts = pends - padded
    n_steps = -(-(na + N_EXPERTS * (tm - 1)) // tm) + 2
    blk_start = (jnp.arange(n_steps, dtype=jnp.int32) - 1) * tm
    block_e = jnp.sum((blk_start[:, None] >= pends[None, :]).astype(jnp.int32), axis=1)
    block_e = jnp.minimum(block_e, N_EXPERTS - 1)
    block_e = block_e.at[0].set(block_e[1])
    lane = jnp.arange(tm, dtype=jnp.int32)[None, :]
    local = blk_start[:, None] + lane - pstarts[block_e][:, None]
    valid = (local >= 0) & (local < counts[block_e][:, None]) & (blk_start[:, None] >= 0)
    asg = order[jnp.clip(starts[block_e][:, None] + local, 0, na - 1)]
    gtok = jnp.where(valid, asg // TOP_K, 0)
    sdst = jnp.where(valid, (asg % TOP_K) * n + asg // TOP_K, TOP_K * n + lane)
    block_live = valid[:, 0].astype(jnp.int32)
    return gtok * SUBLANES, sdst * SUBLANES, block_e, block_live


def _combine_body(x_ref, gt_ref, gn_ref, *refs, final):
    slot_refs, y_ref = refs[:TOP_K], refs[TOP_K]
    x = x_ref[...]
    gt = gt_ref[...]
    tm = x.shape[0]
    for j in range(TOP_K):
        slot = jnp.concatenate([slot_refs[j][pl.ds(sub, tm, stride=SUBLANES), :] for sub in range(SUBLANES)],
                               axis=-1)
        x = x + gt[:, j:j + 1] * slot
    y_ref[...] = _rmsnorm(x, gn_ref[...]) if final else x


def _combine(x2, gates, slots, gn, tm, n_all, row0, final):
    n, d = x2.shape
    assert n % tm == 0 and row0 % tm == 0 and n_all % tm == 0
    slot_spec = lambda j: pl.BlockSpec((tm * SUBLANES, LANES), lambda i: ((j * n_all + row0) // tm + i, 0))
    return pl.pallas_call(
        functools.partial(_combine_body, final=final),
        grid=(n // tm,),
        in_specs=[pl.BlockSpec((tm, d), lambda i: (i, 0)), pl.BlockSpec((tm, LANES), lambda i: (i, 0)),
                  pl.BlockSpec((1, d), lambda i: (0, 0))] + [slot_spec(j) for j in range(TOP_K)],
        out_specs=pl.BlockSpec((tm, d), lambda i: (i, 0)),
        out_shape=jax.ShapeDtypeStruct((n, d), F32),
        compiler_params=pltpu.CompilerParams(dimension_semantics=("parallel",), vmem_limit_bytes=VMEM_LIMIT),
        name="combine",
    )(x2, gates, gn, *([slots] * TOP_K))


def _layer_params(l, norm_mix, w_in, mu_shift, w0, w2_decay, a0, a2_iclr, g2_gate, k_k, k_a, r_k, lnx_w, lnx_b,
                  conv_w, a_log, dt_bias, gdn_norm_w, w_out, norm_cross, norm_mem, wq_x, wk_x, wv_x, wo_x,
                  norm_ffn, router_w, router_b, w1_e, b1_e, w2_e, b2_e):
    row = lambda z: z.reshape(1, -1).astype(F32)
    lane_pad = lambda z, at: jnp.zeros((1, LANES), F32).at[0, at:at + z.shape[0]].set(z)
    return {
        'norm_mix': row(norm_mix[l]),
        'w_in_r': w_in[l][:, :R_PROJ].astype(BF16),
        'w_in_g': jnp.pad(w_in[l][:, R_PROJ:], ((0, 0), (0, G_PROJ_PAD - G_PROJ))).astype(BF16),
        'mu': row(mu_shift[l]), 'w0': row(w0[l]), 'a0': row(a0[l]),
        'w2': jnp.pad(w2_decay[l], ((0, LORA_A), (0, 0))).astype(BF16),
        'a2': jnp.pad(a2_iclr[l], ((LORA_W, 0), (0, 0))).astype(BF16),
        'g2': g2_gate[l].astype(BF16),
        'k_k': row(k_k[l]), 'k_a': row(k_a[l]), 'r_k': row(r_k[l]), 'lnx_w': row(lnx_w[l]), 'lnx_b': row(lnx_b[l]),
        'conv_w': conv_w[l].astype(F32),
        'a_log': lane_pad(a_log[l], H_G), 'dt_bias': lane_pad(dt_bias[l], H_G),
        'gdn_norm_w': row(gdn_norm_w[l]),
        'w_out_r': w_out[l][:W_RWKV].astype(BF16), 'w_out_g': w_out[l][W_RWKV:].astype(BF16),
        'norm_cross': row(norm_cross[l]), 'norm_mem': row(norm_mem[l]),
        'wq': wq_x[l].astype(BF16), 'wk': wk_x[l].astype(BF16), 'wv': wv_x[l].astype(BF16),
        'wo': wo_x[l].astype(BF16),
        'norm_ffn': row(norm_ffn[l]),
        'router_w': jnp.concatenate(_split2(jnp.pad(router_w[l].astype(F32), ((0, 0), (0, LANES - N_EXPERTS)))),
                                    axis=1),
        'router_b': jnp.full((1, LANES), NEG_BIG, F32).at[0, :N_EXPERTS].set(router_b[l].astype(F32)),
        'w1_e': w1_e[l], 'b1_e': b1_e[l][:, None, :], 'w2_e': w2_e[l], 'b2_e': b2_e[l][:, None, :],
    }


def _mix_and_attend(x, mk, mv, shift_prev, s_r, conv_buf, s_g, p, *, chunk, mix_bb, tm, bb, tq):
    b, t, d = x.shape
    assert t >= CONV_W - 1
    pr, pg = _norm_proj(x.reshape(b * t, d), p['norm_mix'], [p['w_in_r'], p['w_in_g']], [F32, F32], tm)
    pr = pr.reshape(b, t, R_PROJ)
    pg = pg.reshape(b, t, G_PROJ_PAD)
    shift_new = pr[:, t - 1]
    conv_new = pg[:, t - (CONV_W - 1):, :GDN_CONV_DIM]
    t_pad = -(-t // chunk) * chunk
    if t_pad != t:
        pr = jnp.pad(pr, ((0, 0), (0, t_pad - t), (0, 0)))
        pg = jnp.pad(pg, ((0, 0), (0, t_pad - t), (0, 0)))
    t_valid = chunk if t_pad == t else t
    y_r, s_r_new = _rwkv_mix(pr, shift_prev, s_r, p, chunk, t_valid, mix_bb)
    y_g, s_g_new = _gdn_mix(pg, conv_buf, s_g, p, chunk, t_valid, mix_bb)
    if t_pad != t:
        y_r, y_g = y_r[:, :t], y_g[:, :t]
    x1, q = _out_q(x.reshape(b * t, d), y_r.reshape(b * t, W_RWKV), y_g.reshape(b * t, W_GDN), p, tm)
    x2, h, top_i, gates = _attn_route(q.reshape(b, t, d), x1.reshape(b, t, d), mk, mv, p, bb, tq)
    return (x2.reshape(b * t, d), h, top_i.reshape(b * t, LANES), gates.reshape(b * t, LANES),
            shift_new, s_r_new, conv_new, s_g_new)


def kernel(x_prompt, x_sample, mem_prompt, state_rwkv, state_rwkv_shift, state_gdn, state_gdn_conv, cache_mem_k, cache_mem_v, norm_mix, w_in, mu_shift, w0, w2_decay, a0, a2_iclr, g2_gate, k_k, k_a, r_k, lnx_w, lnx_b, conv_w, a_log, dt_bias, gdn_norm_w, w_out, norm_cross, norm_mem, wq_x, wk_x, wv_x, wo_x, norm_ffn, router_w, router_b, w1_e, b1_e, w2_e, b2_e, final_norm):
    bp, tp, d = x_prompt.shape
    bs, ts, _ = x_sample.shape
    depth = w_in.shape[0]
    np_, ns = bp * tp, bs * ts
    xp, xs = x_prompt, x_sample
    outs = [[] for _ in range(10)]
    for l in range(depth):
        p = _layer_params(l, norm_mix, w_in, mu_shift, w0, w2_decay, a0, a2_iclr, g2_gate, k_k, k_a, r_k, lnx_w,
                          lnx_b, conv_w, a_log, dt_bias, gdn_norm_w, w_out, norm_cross, norm_mem, wq_x, wk_x, wv_x,
                          wo_x, norm_ffn, router_w, router_b, w1_e, b1_e, w2_e, b2_e)
        n_mem = mem_prompt.shape[1]
        mk, mv = _norm_proj(mem_prompt.reshape(bp * n_mem, d), p['norm_mem'], [p['wk'], p['wv']], [F32, F32], 256)
        mk, mv = mk.reshape(bp, n_mem, d), mv.reshape(bp, n_mem, d)
        res_p = _mix_and_attend(
            xp, mk, mv, jnp.zeros((bp, R_PROJ), F32), jnp.zeros((bp, H_R, N_R, N_R), F32),
            jnp.zeros((bp, CONV_W - 1, GDN_CONV_DIM), F32), jnp.zeros((bp, H_G, D_G, D_G), F32), p,
            chunk=MIX_CHUNK, mix_bb=8, tm=256, bb=1, tq=512)
        head_rows = lambda c: c.reshape(bs, n_mem, H_X, D_X // LANES, LANES).transpose(0, 1, 3, 2, 4).reshape(-1, LANES)
        mk_s, mv_s = head_rows(cache_mem_k[l]), head_rows(cache_mem_v[l])
        res_s = _mix_and_attend(
            xs, mk_s, mv_s, state_rwkv_shift[l], state_rwkv[l], state_gdn_conv[l],
            state_gdn[l], p, chunk=SUBLANES, mix_bb=8, tm=256, bb=8, tq=ts)
        h = jnp.concatenate([res_p[1], res_s[1]], axis=0)
        top_i = jnp.concatenate([res_p[2], res_s[2]], axis=0)[:, :TOP_K]
        gtok, sdst, block_e, block_live = _route_plan(top_i, MOE_ROWS)
        slots = _moe(h, gtok, sdst, block_e, block_live, p, MOE_ROWS)
        gn = final_norm.reshape(1, d).astype(F32)
        last = l == depth - 1
        xp = _combine(res_p[0], res_p[3], slots, gn, 256, np_ + ns, 0, last).reshape(bp, tp, d)
        xs = _combine(res_s[0], res_s[3], slots, gn, 256, np_ + ns, np_, last).reshape(bs, ts, d)
        new = [res_p[5], res_p[4], res_p[7], res_p[6], mk.reshape(bp, n_mem, H_X, D_X),
               mv.reshape(bp, n_mem, H_X, D_X), res_s[5], res_s[4], res_s[7], res_s[6]]
        for acc, val in zip(outs, new):
            acc.append(val)
    return (xp, xs) + tuple(jnp.stack(o) for o in outs)
```

```python
import functools

import jax
import jax.numpy as jnp
from jax import lax
from jax.experimental import pallas as pl
from jax.experimental.pallas import tpu as pltpu

F32 = jnp.float32
BF16 = jnp.bfloat16
DEFAULT = lax.Precision.DEFAULT

D_MODEL = 1024
W_RWKV = 512
N_R = 64
H_R = W_RWKV // N_R
LORA_W = 64
LORA_A = 64
LORA_G = 128
R_PROJ = 3 * W_RWKV + LORA_W + LORA_A + LORA_G
GN_EPS = 64e-5
W_GDN = 512
D_G = 128
H_G = W_GDN // D_G
GDN_CONV_DIM = 3 * W_GDN
CONV_W = 4
G_PROJ = GDN_CONV_DIM + W_GDN + 2 * H_G
LANES = 128
SUBLANES = 8
G_PROJ_PAD = GDN_CONV_DIM + W_GDN + LANES
MIX_CHUNK = 64
N_MEM = 256
H_X = 4
D_X = D_MODEL // H_X
N_EXPERTS = 32
TOP_K = 4
D_FF = D_MODEL
SWIGLU_LIMIT = 7.0
SWIGLU_ALPHA = 1.702
MOE_ROWS = 256
RMS_EPS = 1e-6
L2_EPS = 1e-6
NEG_BIG = -1e30
VMEM_LIMIT = 56 * 1024 * 1024


def _dot(a, b, precision=DEFAULT):
    return jnp.dot(a, b, preferred_element_type=F32, precision=precision)


def _dot_nt(a, b, precision=DEFAULT):
    return lax.dot_general(a, b, (((1,), (1,)), ((), ())), preferred_element_type=F32, precision=precision)


NN = (((1,), (0,)), ((), ()))
NT = (((1,), (1,)), ((), ()))
TN = (((0,), (0,)), ((), ()))


def _split2(x):
    hi = x.astype(BF16)
    return hi, (x - hi.astype(F32)).astype(BF16)


def _mm(a, b, dims):
    return lax.dot_general(a.astype(BF16), b.astype(BF16), dims, preferred_element_type=F32)


def _sel_mm(sel, x, dims):
    dg = lambda y: lax.dot_general(sel, y, dims, preferred_element_type=F32)
    hi = x.astype(BF16)
    r1 = x - hi.astype(F32)
    mid = r1.astype(BF16)
    lo = (r1 - mid.astype(F32)).astype(BF16)
    return dg(hi) + (dg(mid) + dg(lo))


def _sigmoid(x):
    return 1.0 / (1.0 + jnp.exp(-x))


def _softplus(x):
    return jnp.maximum(x, 0.0) + jnp.log(1.0 + jnp.exp(-jnp.abs(x)))


def _rmsnorm(x, g):
    return x * lax.rsqrt(jnp.mean(x * x, axis=-1, keepdims=True) + RMS_EPS) * g


def _tri_masks(c):
    row = lax.broadcasted_iota(jnp.int32, (c, c), 0)
    col = lax.broadcasted_iota(jnp.int32, (c, c), 1)
    return col <= row, col < row, (col == row).astype(F32)


def _unit_lower_inverses(ms, eye, c):
    ts = [eye + m for m in ms]
    ps = list(ms)
    covered = 2
    while covered < c:
        ps = [_mm(p, p, NN) for p in ps]
        ts = [t + _mm(t, p, NN) for t, p in zip(ts, ps)]
        covered *= 2
    return ts


def _norm_proj_body(x_ref, g_ref, *refs, n_out):
    w_refs, o_refs = refs[:n_out], refs[n_out:]
    hb = _rmsnorm(x_ref[...], g_ref[...]).astype(BF16)
    for w_ref, o_ref in zip(w_refs, o_refs):
        o_ref[...] = _dot(hb, w_ref[...]).astype(o_ref.dtype)


def _norm_proj(x, g, ws, out_dtypes, tm):
    n, d = x.shape
    assert n % tm == 0
    in_specs = [pl.BlockSpec((tm, d), lambda i: (i, 0)), pl.BlockSpec((1, d), lambda i: (0, 0))]
    in_specs += [pl.BlockSpec(w.shape, lambda i: (0, 0)) for w in ws]
    return pl.pallas_call(
        functools.partial(_norm_proj_body, n_out=len(ws)),
        grid=(n // tm,),
        in_specs=in_specs,
        out_specs=[pl.BlockSpec((tm, w.shape[1]), lambda i: (i, 0)) for w in ws],
        out_shape=[jax.ShapeDtypeStruct((n, w.shape[1]), dt) for w, dt in zip(ws, out_dtypes)],
        compiler_params=pltpu.CompilerParams(dimension_semantics=("parallel",), vmem_limit_bytes=VMEM_LIMIT),
        name="norm_proj",
    )(x, g, *ws)


def _rwkv_body(pr_ref, shift_ref, s0_ref, mu_ref, w0_ref, w2_ref, a0_ref, a2_ref, g2_ref, kk_ref, ka_ref,
               rk_ref, lnw_ref, lnb_ref, y_ref, sout_ref, s_scr, prev_scr, *, bb, chunk, t_valid, n_chunks):
    c = pl.program_id(1)

    @pl.when(c == 0)
    def _init():
        s_scr[...] = s0_ref[...]
        prev_scr[...] = shift_ref[...]

    incl, strict, eye = _tri_masks(chunk)
    incl_bf = incl.astype(BF16)
    rows = lax.broadcasted_iota(jnp.int32, (chunk, 1), 0)
    sls = [slice(h * N_R, (h + 1) * N_R) for h in range(H_R)]

    def per_batch(bi):
        pr = pr_ref[bi]
        prev = jnp.where(rows == 0, prev_scr[bi], pltpu.roll(pr, 1, 0))
        prev_scr[bi] = pr[chunk - 1:chunk, :]
        xm = pr + (prev - pr) * mu_ref[...]
        r = xm[:, :W_RWKV]
        k = xm[:, W_RWKV:2 * W_RWKV]
        v = xm[:, 2 * W_RWKV:3 * W_RWKV]
        lo = xm[:, 3 * W_RWKV:3 * W_RWKV + LORA_W + LORA_A]
        g_lo = xm[:, 3 * W_RWKV + LORA_W + LORA_A:]
        logw = -_softplus(-(w0_ref[...] + _dot(jnp.tanh(lo).astype(BF16), w2_ref[...]))) - 0.5
        wl = -jnp.exp(logw)
        a = _sigmoid(a0_ref[...] + _dot(lo.astype(BF16), a2_ref[...]))
        g = _dot(_sigmoid(g_lo).astype(BF16), g2_ref[...])
        kkv = k * kk_ref[...]
        k = k * (1.0 + (a - 1.0) * ka_ref[...])
        if t_valid < chunk:
            valid = rows < t_valid
            wl = jnp.where(valid, wl, 0.0)
            kkv = jnp.where(valid, kkv, 0.0)
            k = jnp.where(valid, k, 0.0)
            v = jnp.where(valid, v, 0.0)
        cum = _sel_mm(incl_bf, wl, NN)
        cum_last = cum[chunk - 1:chunk, :]
        return dict(r=r, k=k, v=v, a=a, g=g, kkv=kkv, w_incl=jnp.exp(cum), w_prev=jnp.exp(cum - wl),
                    w_inv=jnp.exp(-cum), w_tail=jnp.exp(cum_last - cum), w_last=jnp.exp(cum_last))

    pre = [per_batch(bi) for bi in range(bb)]
    chains = [(bi, h) for bi in range(bb) for h in range(H_R)]
    col = lambda name: [pre[bi][name][:, sls[h]] for bi, h in chains]
    n = range(len(chains))
    r_, k_, v_, a_ = col('r'), col('k'), col('v'), col('a')
    w_incl, w_prev, w_inv, w_tail, w_last = col('w_incl'), col('w_prev'), col('w_inv'), col('w_tail'), col('w_last')
    kks = [x * lax.rsqrt(jnp.sum(x * x, axis=-1, keepdims=True) + L2_EPS) for x in col('kkv')]
    a_hat = [-(kks[i] * w_prev[i]) for i in n]
    kka = [kks[i] * a_[i] for i in n]
    b_hat = [kka[i] * w_inv[i] for i in n]
    k_hat = [k_[i] * w_inv[i] for i in n]
    r_hat = [r_[i] * w_incl[i] for i in n]
    cross = [_mm(jnp.concatenate([a_hat[i], r_hat[i]], axis=0), jnp.concatenate([b_hat[i], k_hat[i]], axis=0), NT)
             for i in n]
    row2 = lax.broadcasted_iota(jnp.int32, (chunk, 2 * chunk), 0)
    col2 = lax.broadcasted_iota(jnp.int32, (chunk, 2 * chunk), 1)
    col2 = jnp.where(col2 >= chunk, col2 - chunk, col2)
    strict2, incl2 = col2 < row2, col2 <= row2
    m_top = [jnp.where(strict2, x[:chunk], 0.0) for x in cross]
    a_bot = [jnp.where(incl2, x[chunk:], 0.0) for x in cross]
    t_inv = _unit_lower_inverses([x[:, :chunk] for x in m_top], eye, chunk)
    w_hat = [_mm(t_inv[i], a_hat[i], NN) for i in n]
    mv = [_mm(m_top[i], jnp.concatenate([jnp.zeros_like(v_[i]), v_[i]], axis=0), NN) for i in n]
    u = [_mm(t_inv[i], mv[i], NN) for i in n]
    ss = [s_scr[bi, h] for bi, h in chains]
    p = [_mm(w_hat[i], ss[i], NT) + u[i] for i in n]
    pv = [jnp.concatenate([p[i], v_[i]], axis=0) for i in n]
    ys = [_mm(r_hat[i], ss[i], NT) + _mm(a_bot[i], pv[i], NN) for i in n]
    for i, (bi, h) in enumerate(chains):
        tails = jnp.concatenate([kka[i] * w_tail[i], k_[i] * w_tail[i]], axis=0)
        s_scr[bi, h] = ss[i] * w_last[i] + _mm(pv[i], tails, TN)
    for i, (bi, h) in enumerate(chains):
        sl = sls[h]
        y = ys[i]
        mean = jnp.mean(y, axis=-1, keepdims=True)
        yc = y - mean
        var = jnp.mean(yc * yc, axis=-1, keepdims=True)
        yn = yc * lax.rsqrt(var + GN_EPS) * lnw_ref[:, sl] + lnb_ref[:, sl]
        bonus = jnp.sum(r_[i] * k_[i] * rk_ref[:, sl], axis=-1, keepdims=True) * v_[i]
        y_ref[bi, :, sl] = ((yn + bonus) * pre[bi]['g'][:, sl]).astype(y_ref.dtype)

    @pl.when(c == n_chunks - 1)
    def _fin():
        sout_ref[...] = s_scr[...]


def _rwkv_mix(pr, shift_prev, s0, p, chunk, t_valid, bb):
    b, t, _ = pr.shape
    assert b % bb == 0 and t % chunk == 0
    n_chunks = t // chunk
    const = lambda shape: pl.BlockSpec(shape, lambda i, j: (0,) * len(shape))
    return pl.pallas_call(
        functools.partial(_rwkv_body, bb=bb, chunk=chunk, t_valid=t_valid, n_chunks=n_chunks),
        grid=(b // bb, n_chunks),
        in_specs=[
            pl.BlockSpec((bb, chunk, R_PROJ), lambda i, j: (i, j, 0)),
            pl.BlockSpec((bb, 1, R_PROJ), lambda i, j: (i, 0, 0)),
            pl.BlockSpec((bb, H_R, N_R, N_R), lambda i, j: (i, 0, 0, 0)),
            const((1, R_PROJ)), const((1, W_RWKV)), const((LANES, W_RWKV)), const((1, W_RWKV)),
            const((LANES, W_RWKV)), const((LORA_G, W_RWKV)), const((1, W_RWKV)), const((1, W_RWKV)),
            const((1, W_RWKV)), const((1, W_RWKV)), const((1, W_RWKV)),
        ],
        out_specs=[
            pl.BlockSpec((bb, chunk, W_RWKV), lambda i, j: (i, j, 0)),
            pl.BlockSpec((bb, H_R, N_R, N_R), lambda i, j: (i, 0, 0, 0)),
        ],
        out_shape=[jax.ShapeDtypeStruct((b, t, W_RWKV), BF16), jax.ShapeDtypeStruct((b, H_R, N_R, N_R), F32)],
        scratch_shapes=[pltpu.VMEM((bb, H_R, N_R, N_R), F32), pltpu.VMEM((bb, 1, R_PROJ), F32)],
        compiler_params=pltpu.CompilerParams(dimension_semantics=("arbitrary", "arbitrary"),
                                             vmem_limit_bytes=VMEM_LIMIT),
        name="rwkv_mix",
    )(pr, shift_prev[:, None, :], s0, p['mu'], p['w0'], p['w2'], p['a0'], p['a2'], p['g2'], p['k_k'], p['k_a'],
      p['r_k'], p['lnx_w'], p['lnx_b'])


def _gdn_body(pg_ref, cbuf_ref, s0_ref, cw_ref, alog_ref, dtb_ref, nw_ref, y_ref, sout_ref, s_scr, xp_scr,
              *, bb, chunk, t_valid, n_chunks):
    c = pl.program_id(1)

    @pl.when(c == 0)
    def _init():
        s_scr[...] = s0_ref[...]
        xp_scr[:, 0:SUBLANES, :] = cbuf_ref[...]

    incl, strict, eye = _tri_masks(chunk)
    incl_bf = incl.astype(BF16)
    lane = lax.broadcasted_iota(jnp.int32, (chunk, LANES), 1)
    sls = [slice(h * D_G, (h + 1) * D_G) for h in range(H_G)]

    def per_batch(bi):
        xp_scr[bi, SUBLANES:SUBLANES + chunk, :] = pg_ref[bi, :, :GDN_CONV_DIM]
        base = SUBLANES - (CONV_W - 1)
        conv = xp_scr[bi, base:base + chunk, :] * cw_ref[0:1, :]
        for j in range(1, CONV_W):
            conv = conv + xp_scr[bi, base + j:base + j + chunk, :] * cw_ref[j:j + 1, :]
        xp_scr[bi, 0:SUBLANES, :] = xp_scr[bi, chunk:chunk + SUBLANES, :]
        qkv = conv * _sigmoid(conv)
        z = pg_ref[bi, :, GDN_CONV_DIM:GDN_CONV_DIM + W_GDN]
        ba = pg_ref[bi, :, GDN_CONV_DIM + W_GDN:]
        beta_blk = _sigmoid(ba)
        g_blk = -jnp.exp(alog_ref[...]) * _softplus(ba + dtb_ref[...])
        if t_valid < chunk:
            valid = lax.broadcasted_iota(jnp.int32, (chunk, 1), 0) < t_valid
            beta_blk = jnp.where(valid, beta_blk, 0.0)
            g_blk = jnp.where(valid, g_blk, 0.0)
        gc_blk = _sel_mm(incl_bf, g_blk, NN)
        return dict(qkv=qkv, z=z, beta_blk=beta_blk, gc_blk=gc_blk)

    pre = [per_batch(bi) for bi in range(bb)]
    chains = [(bi, h) for bi in range(bb) for h in range(H_G)]
    n = range(len(chains))
    qs = [pre[bi]['qkv'][:, sls[h]] for bi, h in chains]
    qs = [x * lax.rsqrt(jnp.sum(x * x, axis=-1, keepdims=True) + L2_EPS) * (D_G ** -0.5) for x in qs]
    ks = [pre[bi]['qkv'][:, W_GDN + h * D_G:W_GDN + (h + 1) * D_G] for bi, h in chains]
    ks = [x * lax.rsqrt(jnp.sum(x * x, axis=-1, keepdims=True) + L2_EPS) for x in ks]
    vs = [pre[bi]['qkv'][:, 2 * W_GDN + h * D_G:2 * W_GDN + (h + 1) * D_G] for bi, h in chains]
    betas = [pre[bi]['beta_blk'][:, h:h + 1] for bi, h in chains]
    gcols = [pre[bi]['gc_blk'][:, H_G + h:H_G + h + 1] for bi, h in chains]
    grows = [_sel_mm((lane == H_G + h).astype(BF16), pre[bi]['gc_blk'], NT) for bi, h in chains]
    g_last = [gcols[i][chunk - 1:chunk, :] for i in n]
    decay = [jnp.where(incl, jnp.exp(jnp.where(incl, gcols[i] - grows[i], 0.0)), 0.0) for i in n]
    k_beta = [ks[i] * betas[i] for i in n]
    cross = [_mm(jnp.concatenate([k_beta[i], qs[i]], axis=0), ks[i], NT) for i in n]
    lmat = [jnp.where(strict, cross[i][:chunk] * decay[i], 0.0) for i in n]
    attn = [jnp.where(incl, cross[i][chunk:] * decay[i], 0.0) for i in n]
    t_inv = _unit_lower_inverses([-x for x in lmat], eye, chunk)
    e_gc = [jnp.exp(gcols[i]) for i in n]
    uw = [_mm(t_inv[i], jnp.concatenate([vs[i] * betas[i], k_beta[i] * e_gc[i]], axis=1), NN) for i in n]
    ss = [s_scr[bi, h] for bi, h in chains]
    v_new = [uw[i][:, :D_G] - _mm(uw[i][:, D_G:], ss[i], NN) for i in n]
    os_ = [_mm(jnp.concatenate([qs[i] * e_gc[i], attn[i]], axis=1), jnp.concatenate([ss[i], v_new[i]], axis=0), NN)
           for i in n]
    for i, (bi, h) in enumerate(chains):
        s_scr[bi, h] = ss[i] * jnp.exp(g_last[i]) + _mm(ks[i] * jnp.exp(g_last[i] - gcols[i]), v_new[i], TN)
    for i, (bi, h) in enumerate(chains):
        o = os_[i]
        o = o * lax.rsqrt(jnp.mean(o * o, axis=-1, keepdims=True) + RMS_EPS) * nw_ref[...]
        z_h = pre[bi]['z'][:, sls[h]]
        y_ref[bi, :, sls[h]] = (o * (z_h * _sigmoid(z_h))).astype(y_ref.dtype)

    @pl.when(c == n_chunks - 1)
    def _fin():
        sout_ref[...] = s_scr[...]


def _gdn_mix(pg, conv_buf, s0, p, chunk, t_valid, bb):
    b, t, _ = pg.shape
    assert b % bb == 0 and t % chunk == 0
    n_chunks = t // chunk
    cbuf = jnp.pad(conv_buf, ((0, 0), (SUBLANES - (CONV_W - 1), 0), (0, 0)))
    const = lambda shape: pl.BlockSpec(shape, lambda i, j: (0,) * len(shape))
    return pl.pallas_call(
        functools.partial(_gdn_body, bb=bb, chunk=chunk, t_valid=t_valid, n_chunks=n_chunks),
        grid=(b // bb, n_chunks),
        in_specs=[
            pl.BlockSpec((bb, chunk, G_PROJ_PAD), lambda i, j: (i, j, 0)),
            pl.BlockSpec((bb, SUBLANES, GDN_CONV_DIM), lambda i, j: (i, 0, 0)),
            pl.BlockSpec((bb, H_G, D_G, D_G), lambda i, j: (i, 0, 0, 0)),
            const((CONV_W, GDN_CONV_DIM)), const((1, LANES)), const((1, LANES)), const((1, D_G)),
        ],
        out_specs=[
            pl.BlockSpec((bb, chunk, W_GDN), lambda i, j: (i, j, 0)),
            pl.BlockSpec((bb, H_G, D_G, D_G), lambda i, j: (i, 0, 0, 0)),
        ],
        out_shape=[jax.ShapeDtypeStruct((b, t, W_GDN), BF16), jax.ShapeDtypeStruct((b, H_G, D_G, D_G), F32)],
        scratch_shapes=[pltpu.VMEM((bb, H_G, D_G, D_G), F32),
                        pltpu.VMEM((bb, SUBLANES + chunk, GDN_CONV_DIM), F32)],
        compiler_params=pltpu.CompilerParams(dimension_semantics=("arbitrary", "arbitrary"),
                                             vmem_limit_bytes=VMEM_LIMIT),
        name="gdn_mix",
    )(pg, cbuf, s0, p['conv_w'], p['a_log'], p['dt_bias'], p['gdn_norm_w'])


def _out_q_body(x_ref, yr_ref, yg_ref, wor_ref, wog_ref, gn_ref, wq_ref, x1_ref, q_ref):
    x1 = x_ref[...] + _dot(yr_ref[...], wor_ref[...]) + _dot(yg_ref[...], wog_ref[...])
    x1_ref[...] = x1
    q_ref[...] = _dot(_rmsnorm(x1, gn_ref[...]).astype(BF16), wq_ref[...]).astype(q_ref.dtype)


def _out_q(x, yr, yg, p, tm):
    n, d = x.shape
    assert n % tm == 0
    row = lambda w: pl.BlockSpec((tm, w), lambda i: (i, 0))
    const = lambda shape: pl.BlockSpec(shape, lambda i: (0, 0))
    return pl.pallas_call(
        _out_q_body,
        grid=(n // tm,),
        in_specs=[row(d), row(W_RWKV), row(W_GDN), const((W_RWKV, d)), const((W_GDN, d)), const((1, d)),
                  const((d, d))],
        out_specs=[row(d), row(d)],
        out_shape=[jax.ShapeDtypeStruct((n, d), F32), jax.ShapeDtypeStruct((n, d), BF16)],
        compiler_params=pltpu.CompilerParams(dimension_semantics=("parallel",), vmem_limit_bytes=VMEM_LIMIT),
        name="out_q",
    )(x, yr, yg, p['w_out_r'], p['w_out_g'], p['norm_cross'], p['wq'])


def _attn_body(q_ref, x_ref, mk_ref, mv_ref, wo_ref, gn_ref, rw_ref, rb_ref, x2_ref, h_ref, ti_ref, gt_ref,
               *, bb, tq, head_rows):
    d_tiles = D_X // LANES
    mem_rows = N_MEM * d_tiles * H_X

    def head_mem(ref, i, hh):
        if not head_rows:
            return ref[i, :, hh * D_X:(hh + 1) * D_X]
        return jnp.concatenate(
            [ref[pl.ds(i * mem_rows + dt * H_X + hh, N_MEM, stride=d_tiles * H_X), :] for dt in range(d_tiles)],
            axis=-1)

    os_ = []
    for i in range(bb):
        q = q_ref[i]
        heads = []
        for hh in range(H_X):
            sl = slice(hh * D_X, (hh + 1) * D_X)
            s = _dot_nt(q[:, sl], head_mem(mk_ref, i, hh).astype(BF16)) * (D_X ** -0.5)
            e = jnp.exp(s - jnp.max(s, axis=-1, keepdims=True))
            prob = e / jnp.sum(e, axis=-1, keepdims=True)
            heads.append(_dot(prob.astype(BF16), head_mem(mv_ref, i, hh).astype(BF16)))
        os_.append(jnp.concatenate(heads, axis=-1))
    rows = bb * tq
    o = (os_[0] if bb == 1 else jnp.concatenate(os_, axis=0)).astype(BF16)
    x1 = x_ref[0] if bb == 1 else jnp.concatenate([x_ref[i] for i in range(bb)], axis=0)
    x2 = x1 + _dot(o, wo_ref[...])
    h = _rmsnorm(x2, gn_ref[...])
    for sub in range(SUBLANES):
        h_ref[pl.ds(sub, rows, stride=SUBLANES), :] = h[:, sub * LANES:(sub + 1) * LANES]
    h_hi, h_lo = _split2(h)
    pieces = _dot(jnp.concatenate([h_hi, h_lo], axis=0), rw_ref[...])
    logits = ((pieces[:rows, :LANES] + pieces[:rows, LANES:]) + (pieces[rows:, :LANES] + pieces[rows:, LANES:])
              + rb_ref[...])
    lane = lax.broadcasted_iota(jnp.int32, logits.shape, 1)
    vals, idxs = [], []
    for _ in range(TOP_K):
        m = jnp.max(logits, axis=-1, keepdims=True)
        first = jnp.min(jnp.where(logits == m, lane, LANES), axis=-1, keepdims=True)
        vals.append(m)
        idxs.append(first)
        logits = jnp.where(lane == first, -jnp.inf, logits)
    es = [jnp.exp(vv - vals[0]) for vv in vals]
    den = es[0] + es[1] + es[2] + es[3]
    ti = jnp.zeros(lane.shape, jnp.int32)
    gt = jnp.zeros(lane.shape, F32)
    for j in range(TOP_K):
        ti = jnp.where(lane == j, idxs[j], ti)
        gt = jnp.where(lane == j, es[j] / den, gt)
    for i in range(bb):
        x2_ref[i] = x2[i * tq:(i + 1) * tq]
        ti_ref[i] = ti[i * tq:(i + 1) * tq]
        gt_ref[i] = gt[i * tq:(i + 1) * tq]


def _attn_route(q, x1, mk, mv, p, bb, tq):
    b, t, d = x1.shape
    assert d == SUBLANES * LANES
    rows = bb * tq
    assert b % bb == 0 and t % tq == 0 and (bb == 1 or tq == t)
    n_tq = t // tq
    blk = lambda w: pl.BlockSpec((bb, tq, w), lambda i, j: (i, j, 0))
    head_rows = mk.ndim == 2
    mem = (pl.BlockSpec((bb * (mk.shape[0] // b), LANES), lambda i, j: (i, 0)) if head_rows
           else pl.BlockSpec((bb, N_MEM, d), lambda i, j: (i, 0, 0)))
    const = lambda shape: pl.BlockSpec(shape, lambda i, j: (0, 0))
    return pl.pallas_call(
        functools.partial(_attn_body, bb=bb, tq=tq, head_rows=head_rows),
        grid=(b // bb, n_tq),
        in_specs=[blk(d), blk(d), mem, mem, const((d, d)), const((1, d)), const((d, 2 * LANES)), const((1, LANES))],
        out_specs=[blk(d), pl.BlockSpec((rows * SUBLANES, LANES), lambda i, j: (i * n_tq + j, 0)), blk(LANES),
                   blk(LANES)],
        out_shape=[jax.ShapeDtypeStruct((b, t, d), F32), jax.ShapeDtypeStruct((b * t * SUBLANES, LANES), F32),
                   jax.ShapeDtypeStruct((b, t, LANES), jnp.int32), jax.ShapeDtypeStruct((b, t, LANES), F32)],
        compiler_params=pltpu.CompilerParams(dimension_semantics=("parallel", "parallel"),
                                             vmem_limit_bytes=VMEM_LIMIT),
        name="attn_route",
    )(q, x1, mk, mv, p['wo'], p['norm_ffn'], p['router_w'], p['router_b'])


def _moe_body(be_ref, gnext_ref, sprev_ref, h_hbm, w1_ref, b1_ref, w2_ref, b2_ref, out_hbm,
              xbuf, ybuf, w1b, w2b, gsem, ssem, *, tm, n_steps):
    k = pl.program_id(0)

    def row(ref, slot, at):
        return ref.at[slot, pl.ds(at, SUBLANES)]

    def gather_all(slot):
        return pltpu.make_async_copy(h_hbm.at[pl.ds(0, tm * SUBLANES)], xbuf.at[slot], gsem.at[slot])

    def scatter_all(slot):
        return pltpu.make_async_copy(ybuf.at[slot], out_hbm.at[pl.ds(0, tm * SUBLANES)], ssem.at[slot])

    @pl.when(k == 0)
    def _prologue():
        ybuf[...] = jnp.zeros(ybuf.shape, F32)
        for r in range(tm):
            pltpu.make_async_copy(h_hbm.at[pl.ds(0, SUBLANES)], row(xbuf, 0, r * SUBLANES), gsem.at[0]).start()

    @pl.when(jnp.logical_or(k == 0, be_ref[k] != be_ref[jnp.maximum(k - 1, 0)]))
    def _cast():
        w1b[...] = w1_ref[0].astype(BF16)
        w2b[...] = w2_ref[0].astype(BF16)

    def phase(cur):
        nxt = 1 - cur
        gather_all(cur).wait()

        @pl.when(k >= 1)
        def _():
            scatter_all(cur).wait()

        for r in range(tm):
            src = pl.multiple_of(gnext_ref[0, 0, r], SUBLANES)
            dst = pl.multiple_of(sprev_ref[0, 0, r], SUBLANES)
            pltpu.make_async_copy(h_hbm.at[pl.ds(src, SUBLANES)], row(xbuf, nxt, r * SUBLANES), gsem.at[nxt]).start()
            pltpu.make_async_copy(row(ybuf, nxt, r * SUBLANES), out_hbm.at[pl.ds(dst, SUBLANES)],
                                  ssem.at[nxt]).start()
        x = jnp.concatenate([xbuf[cur, pl.ds(sub, tm, stride=SUBLANES), :] for sub in range(SUBLANES)], axis=-1)
        hc = _dot(x.astype(BF16), w1b[...]) + b1_ref[0]
        hg = jnp.minimum(hc[:, :D_FF], SWIGLU_LIMIT)
        hl = jnp.clip(hc[:, D_FF:], -SWIGLU_LIMIT, SWIGLU_LIMIT)
        act = hg * _sigmoid(SWIGLU_ALPHA * hg) * (hl + 1.0)
        y = _dot(act.astype(BF16), w2b[...]) + b2_ref[0]
        for sub in range(SUBLANES):
            ybuf[cur, pl.ds(sub, tm, stride=SUBLANES), :] = y[:, sub * LANES:(sub + 1) * LANES]

        @pl.when(k == n_steps - 1)
        def _epilogue():
            gather_all(nxt).wait()
            scatter_all(nxt).wait()

    for parity in range(2):
        pl.when(lax.rem(k, 2) == parity)(functools.partial(phase, parity))


def _moe(h, gtok, sdst, block_e, p, tm):
    n, d = h.shape[0] // SUBLANES, D_MODEL
    n_steps = block_e.shape[0]
    grid_spec = pltpu.PrefetchScalarGridSpec(
        num_scalar_prefetch=1,
        grid=(n_steps,),
        in_specs=[
            pl.BlockSpec((1, 1, tm), lambda k, be: (jnp.minimum(k + 1, n_steps - 1), 0, 0),
                         memory_space=pltpu.SMEM),
            pl.BlockSpec((1, 1, tm), lambda k, be: (jnp.maximum(k - 1, 0), 0, 0), memory_space=pltpu.SMEM),
            pl.BlockSpec(memory_space=pl.ANY),
            pl.BlockSpec((1, d, 2 * D_FF), lambda k, be: (be[k], 0, 0)),
            pl.BlockSpec((1, 1, 2 * D_FF), lambda k, be: (be[k], 0, 0)),
            pl.BlockSpec((1, D_FF, d), lambda k, be: (be[k], 0, 0)),
            pl.BlockSpec((1, 1, d), lambda k, be: (be[k], 0, 0)),
        ],
        out_specs=pl.BlockSpec(memory_space=pl.ANY),
        scratch_shapes=[
            pltpu.VMEM((2, tm * SUBLANES, LANES), F32), pltpu.VMEM((2, tm * SUBLANES, LANES), F32),
            pltpu.VMEM((d, 2 * D_FF), BF16), pltpu.VMEM((D_FF, d), BF16),
            pltpu.SemaphoreType.DMA((2,)), pltpu.SemaphoreType.DMA((2,)),
        ],
    )
    return pl.pallas_call(
        functools.partial(_moe_body, tm=tm, n_steps=n_steps),
        grid_spec=grid_spec,
        out_shape=jax.ShapeDtypeStruct(((n * TOP_K + tm) * SUBLANES, LANES), F32),
        compiler_params=pltpu.CompilerParams(dimension_semantics=("arbitrary",), vmem_limit_bytes=VMEM_LIMIT),
        name="moe_experts",
    )(block_e, gtok.reshape(n_steps, 1, tm), sdst.reshape(n_steps, 1, tm), h, p['w1_e'], p['b1_e'], p['w2_e'],
      p['b2_e'])


def _route_plan(top_i, tm):
    n = top_i.shape[0]
    na = n * TOP_K
    flat_e = top_i.reshape(na)
    order = jnp.argsort(flat_e).astype(jnp.int32)
    counts = jnp.sum((flat_e[:, None] == jnp.arange(N_EXPERTS, dtype=jnp.int32)[None, :]).astype(jnp.int32), axis=0)
    padded = (counts + tm - 1) // tm * tm
    starts = jnp.cumsum(counts) - counts
    pends = jnp.cumsum(padded)
    pstarts = pends - padded
    n_steps = -(-(na + N_EXPERTS * (tm - 1)) // tm) + 2
    blk_start = (jnp.arange(n_steps, dtype=jnp.int32) - 1) * tm
    block_e = jnp.sum((blk_start[:, None] >= pends[None, :]).astype(jnp.int32), axis=1)
    block_e = jnp.minimum(block_e, N_EXPERTS - 1)
    block_e = block_e.at[0].set(block_e[1])
    lane = jnp.arange(tm, dtype=jnp.int32)[None, :]
    local = blk_start[:, None] + lane - pstarts[block_e][:, None]
    valid = (local >= 0) & (local < counts[block_e][:, None]) & (blk_start[:, None] >= 0)
    asg = order[jnp.clip(starts[block_e][:, None] + local, 0, na - 1)]
    gtok = jnp.where(valid, asg // TOP_K, 0)
    sdst = jnp.where(valid, (asg % TOP_K) * n + asg // TOP_K, TOP_K * n + lane)
    return gtok * SUBLANES, sdst * SUBLANES, block_e


def _combine_body(x_ref, gt_ref, gn_ref, *refs, final):
    slot_refs, y_ref = refs[:TOP_K], refs[TOP_K]
    x = x_ref[...]
    gt = gt_ref[...]
    tm = x.shape[0]
    for j in range(TOP_K):
        slot = jnp.concatenate([slot_refs[j][pl.ds(sub, tm, stride=SUBLANES), :] for sub in range(SUBLANES)],
                               axis=-1)
        x = x + gt[:, j:j + 1] * slot
    y_ref[...] = _rmsnorm(x, gn_ref[...]) if final else x


def _combine(x2, gates, slots, gn, tm, n_all, row0, final):
    n, d = x2.shape
    assert n % tm == 0 and row0 % tm == 0 and n_all % tm == 0
    slot_spec = lambda j: pl.BlockSpec((tm * SUBLANES, LANES), lambda i: ((j * n_all + row0) // tm + i, 0))
    return pl.pallas_call(
        functools.partial(_combine_body, final=final),
        grid=(n // tm,),
        in_specs=[pl.BlockSpec((tm, d), lambda i: (i, 0)), pl.BlockSpec((tm, LANES), lambda i: (i, 0)),
                  pl.BlockSpec((1, d), lambda i: (0, 0))] + [slot_spec(j) for j in range(TOP_K)],
        out_specs=pl.BlockSpec((tm, d), lambda i: (i, 0)),
        out_shape=jax.ShapeDtypeStruct((n, d), F32),
        compiler_params=pltpu.CompilerParams(dimension_semantics=("parallel",), vmem_limit_bytes=VMEM_LIMIT),
        name="combine",
    )(x2, gates, gn, *([slots] * TOP_K))


def _layer_params(l, norm_mix, w_in, mu_shift, w0, w2_decay, a0, a2_iclr, g2_gate, k_k, k_a, r_k, lnx_w, lnx_b,
                  conv_w, a_log, dt_bias, gdn_norm_w, w_out, norm_cross, norm_mem, wq_x, wk_x, wv_x, wo_x,
                  norm_ffn, router_w, router_b, w1_e, b1_e, w2_e, b2_e):
    row = lambda z: z.reshape(1, -1).astype(F32)
    lane_pad = lambda z, at: jnp.zeros((1, LANES), F32).at[0, at:at + z.shape[0]].set(z)
    return {
        'norm_mix': row(norm_mix[l]),
        'w_in_r': w_in[l][:, :R_PROJ].astype(BF16),
        'w_in_g': jnp.pad(w_in[l][:, R_PROJ:], ((0, 0), (0, G_PROJ_PAD - G_PROJ))).astype(BF16),
        'mu': row(mu_shift[l]), 'w0': row(w0[l]), 'a0': row(a0[l]),
        'w2': jnp.pad(w2_decay[l], ((0, LORA_A), (0, 0))).astype(BF16),
        'a2': jnp.pad(a2_iclr[l], ((LORA_W, 0), (0, 0))).astype(BF16),
        'g2': g2_gate[l].astype(BF16),
        'k_k': row(k_k[l]), 'k_a': row(k_a[l]), 'r_k': row(r_k[l]), 'lnx_w': row(lnx_w[l]), 'lnx_b': row(lnx_b[l]),
        'conv_w': conv_w[l].astype(F32),
        'a_log': lane_pad(a_log[l], H_G), 'dt_bias': lane_pad(dt_bias[l], H_G),
        'gdn_norm_w': row(gdn_norm_w[l]),
        'w_out_r': w_out[l][:W_RWKV].astype(BF16), 'w_out_g': w_out[l][W_RWKV:].astype(BF16),
        'norm_cross': row(norm_cross[l]), 'norm_mem': row(norm_mem[l]),
        'wq': wq_x[l].astype(BF16), 'wk': wk_x[l].astype(BF16), 'wv': wv_x[l].astype(BF16),
        'wo': wo_x[l].astype(BF16),
        'norm_ffn': row(norm_ffn[l]),
        'router_w': jnp.concatenate(_split2(jnp.pad(router_w[l].astype(F32), ((0, 0), (0, LANES - N_EXPERTS)))),
                                    axis=1),
        'router_b': jnp.full((1, LANES), NEG_BIG, F32).at[0, :N_EXPERTS].set(router_b[l].astype(F32)),
        'w1_e': w1_e[l], 'b1_e': b1_e[l][:, None, :], 'w2_e': w2_e[l], 'b2_e': b2_e[l][:, None, :],
    }


def _mix_and_attend(x, mk, mv, shift_prev, s_r, conv_buf, s_g, p, *, chunk, rwkv_bb, gdn_bb, tm, bb, tq):
    b, t, d = x.shape
    assert t >= CONV_W - 1
    pr, pg = _norm_proj(x.reshape(b * t, d), p['norm_mix'], [p['w_in_r'], p['w_in_g']], [F32, F32], tm)
    pr = pr.reshape(b, t, R_PROJ)
    pg = pg.reshape(b, t, G_PROJ_PAD)
    shift_new = pr[:, t - 1]
    conv_new = pg[:, t - (CONV_W - 1):, :GDN_CONV_DIM]
    t_pad = -(-t // chunk) * chunk
    if t_pad != t:
        pr = jnp.pad(pr, ((0, 0), (0, t_pad - t), (0, 0)))
        pg = jnp.pad(pg, ((0, 0), (0, t_pad - t), (0, 0)))
    t_valid = chunk if t_pad == t else t
    y_r, s_r_new = _rwkv_mix(pr, shift_prev, s_r, p, chunk, t_valid, rwkv_bb)
    y_g, s_g_new = _gdn_mix(pg, conv_buf, s_g, p, chunk, t_valid, gdn_bb)
    if t_pad != t:
        y_r, y_g = y_r[:, :t], y_g[:, :t]
    x1, q = _out_q(x.reshape(b * t, d), y_r.reshape(b * t, W_RWKV), y_g.reshape(b * t, W_GDN), p, tm)
    x2, h, top_i, gates = _attn_route(q.reshape(b, t, d), x1.reshape(b, t, d), mk, mv, p, bb, tq)
    return (x2.reshape(b * t, d), h, top_i.reshape(b * t, LANES), gates.reshape(b * t, LANES),
            shift_new, s_r_new, conv_new, s_g_new)


def kernel(x_prompt, x_sample, mem_prompt, state_rwkv, state_rwkv_shift, state_gdn, state_gdn_conv, cache_mem_k, cache_mem_v, norm_mix, w_in, mu_shift, w0, w2_decay, a0, a2_iclr, g2_gate, k_k, k_a, r_k, lnx_w, lnx_b, conv_w, a_log, dt_bias, gdn_norm_w, w_out, norm_cross, norm_mem, wq_x, wk_x, wv_x, wo_x, norm_ffn, router_w, router_b, w1_e, b1_e, w2_e, b2_e, final_norm):
    bp, tp, d = x_prompt.shape
    bs, ts, _ = x_sample.shape
    depth = w_in.shape[0]
    np_, ns = bp * tp, bs * ts
    xp, xs = x_prompt, x_sample
    outs = [[] for _ in range(10)]
    for l in range(depth):
        p = _layer_params(l, norm_mix, w_in, mu_shift, w0, w2_decay, a0, a2_iclr, g2_gate, k_k, k_a, r_k, lnx_w,
                          lnx_b, conv_w, a_log, dt_bias, gdn_norm_w, w_out, norm_cross, norm_mem, wq_x, wk_x, wv_x,
                          wo_x, norm_ffn, router_w, router_b, w1_e, b1_e, w2_e, b2_e)
        n_mem = mem_prompt.shape[1]
        mk, mv = _norm_proj(mem_prompt.reshape(bp * n_mem, d), p['norm_mem'], [p['wk'], p['wv']], [F32, F32], 256)
        mk, mv = mk.reshape(bp, n_mem, d), mv.reshape(bp, n_mem, d)
        res_p = _mix_and_attend(
            xp, mk, mv, jnp.zeros((bp, R_PROJ), F32), jnp.zeros((bp, H_R, N_R, N_R), F32),
            jnp.zeros((bp, CONV_W - 1, GDN_CONV_DIM), F32), jnp.zeros((bp, H_G, D_G, D_G), F32), p,
            chunk=MIX_CHUNK, rwkv_bb=4, gdn_bb=8, tm=512, bb=1, tq=512)
        head_rows = lambda c: c.reshape(bs, n_mem, H_X, D_X // LANES, LANES).transpose(0, 1, 3, 2, 4).reshape(-1, LANES)
        mk_s, mv_s = head_rows(cache_mem_k[l]), head_rows(cache_mem_v[l])
        res_s = _mix_and_attend(
            xs, mk_s, mv_s, state_rwkv_shift[l], state_rwkv[l], state_gdn_conv[l],
            state_gdn[l], p, chunk=SUBLANES, rwkv_bb=8, gdn_bb=8, tm=256, bb=8, tq=ts)
        h = jnp.concatenate([res_p[1], res_s[1]], axis=0)
        top_i = jnp.concatenate([res_p[2], res_s[2]], axis=0)[:, :TOP_K]
        gtok, sdst, block_e = _route_plan(top_i, MOE_ROWS)
        slots = _moe(h, gtok, sdst, block_e, p, MOE_ROWS)
        gn = final_norm.reshape(1, d).astype(F32)
        last = l == depth - 1
        xp = _combine(res_p[0], res_p[3], slots, gn, 512, np_ + ns, 0, last).reshape(bp, tp, d)
        xs = _combine(res_s[0], res_s[3], slots, gn, 512, np_ + ns, np_, last).reshape(bs, ts, d)
        new = [res_p[5], res_p[4], res_p[7], res_p[6], mk.reshape(bp, n_mem, H_X, D_X),
               mv.reshape(bp, n_mem, H_X, D_X), res_s[5], res_s[4], res_s[7], res_s[6]]
        for acc, val in zip(outs, new):
            acc.append(val)
    return (xp, xs) + tuple(jnp.stack(o) for o in outs)
```

```python
import functools

import jax
import jax.numpy as jnp
from jax import lax
from jax.experimental import pallas as pl
from jax.experimental.pallas import tpu as pltpu

F32 = jnp.float32
BF16 = jnp.bfloat16
DEFAULT = lax.Precision.DEFAULT

D_MODEL = 1024
W_RWKV = 512
N_R = 64
H_R = W_RWKV // N_R
LORA_W = 64
LORA_A = 64
LORA_G = 128
R_PROJ = 3 * W_RWKV + LORA_W + LORA_A + LORA_G
GN_EPS = 64e-5
W_GDN = 512
D_G = 128
H_G = W_GDN // D_G
GDN_CONV_DIM = 3 * W_GDN
CONV_W = 4
G_PROJ = GDN_CONV_DIM + W_GDN + 2 * H_G
LANES = 128
SUBLANES = 8
G_PROJ_PAD = GDN_CONV_DIM + W_GDN + LANES
MIX_CHUNK = 64
N_MEM = 256
H_X = 4
D_X = D_MODEL // H_X
N_EXPERTS = 32
TOP_K = 4
D_FF = D_MODEL
SWIGLU_LIMIT = 7.0
SWIGLU_ALPHA = 1.702
MOE_ROWS = 256
RMS_EPS = 1e-6
L2_EPS = 1e-6
NEG_BIG = -1e30
VMEM_LIMIT = 56 * 1024 * 1024


def _dot(a, b, precision=DEFAULT):
    return jnp.dot(a, b, preferred_element_type=F32, precision=precision)


def _dot_nt(a, b, precision=DEFAULT):
    return lax.dot_general(a, b, (((1,), (1,)), ((), ())), preferred_element_type=F32, precision=precision)


NN = (((1,), (0,)), ((), ()))
NT = (((1,), (1,)), ((), ()))
TN = (((0,), (0,)), ((), ()))


def _split2(x):
    hi = x.astype(BF16)
    return hi, (x - hi.astype(F32)).astype(BF16)


def _mm(a, b, dims):
    return lax.dot_general(a.astype(BF16), b.astype(BF16), dims, preferred_element_type=F32)


def _sel_mm(sel, x, dims):
    dg = lambda y: lax.dot_general(sel, y, dims, preferred_element_type=F32)
    hi = x.astype(BF16)
    r1 = x - hi.astype(F32)
    mid = r1.astype(BF16)
    lo = (r1 - mid.astype(F32)).astype(BF16)
    return dg(hi) + (dg(mid) + dg(lo))


def _sigmoid(x):
    return 1.0 / (1.0 + jnp.exp(-x))


def _softplus(x):
    return jnp.maximum(x, 0.0) + jnp.log(1.0 + jnp.exp(-jnp.abs(x)))


def _rmsnorm(x, g):
    return x * lax.rsqrt(jnp.mean(x * x, axis=-1, keepdims=True) + RMS_EPS) * g


def _tri_masks(c):
    row = lax.broadcasted_iota(jnp.int32, (c, c), 0)
    col = lax.broadcasted_iota(jnp.int32, (c, c), 1)
    return col <= row, col < row, (col == row).astype(F32)


def _unit_lower_inverses(ms, eye, c):
    ts = [eye + m for m in ms]
    ps = list(ms)
    covered = 2
    while covered < c:
        ps = [_mm(p, p, NN) for p in ps]
        ts = [t + _mm(t, p, NN) for t, p in zip(ts, ps)]
        covered *= 2
    return ts


def _norm_proj_body(x_ref, g_ref, *refs, n_out):
    w_refs, o_refs = refs[:n_out], refs[n_out:]
    hb = _rmsnorm(x_ref[...], g_ref[...]).astype(BF16)
    for w_ref, o_ref in zip(w_refs, o_refs):
        o_ref[...] = _dot(hb, w_ref[...]).astype(o_ref.dtype)


def _norm_proj(x, g, ws, out_dtypes, tm):
    n, d = x.shape
    assert n % tm == 0
    in_specs = [pl.BlockSpec((tm, d), lambda i: (i, 0)), pl.BlockSpec((1, d), lambda i: (0, 0))]
    in_specs += [pl.BlockSpec(w.shape, lambda i: (0, 0)) for w in ws]
    return pl.pallas_call(
        functools.partial(_norm_proj_body, n_out=len(ws)),
        grid=(n // tm,),
        in_specs=in_specs,
        out_specs=[pl.BlockSpec((tm, w.shape[1]), lambda i: (i, 0)) for w in ws],
        out_shape=[jax.ShapeDtypeStruct((n, w.shape[1]), dt) for w, dt in zip(ws, out_dtypes)],
        compiler_params=pltpu.CompilerParams(dimension_semantics=("parallel",), vmem_limit_bytes=VMEM_LIMIT),
        name="norm_proj",
    )(x, g, *ws)


def _rwkv_body(pr_ref, shift_ref, s0_ref, mu_ref, w0_ref, w2_ref, a0_ref, a2_ref, g2_ref, kk_ref, ka_ref,
               rk_ref, lnw_ref, lnb_ref, y_ref, sout_ref, s_scr, prev_scr, *, bb, chunk, t_valid, n_chunks):
    c = pl.program_id(1)

    @pl.when(c == 0)
    def _init():
        s_scr[...] = s0_ref[...]
        prev_scr[...] = shift_ref[...]

    incl, strict, eye = _tri_masks(chunk)
    incl_bf = incl.astype(BF16)
    rows = lax.broadcasted_iota(jnp.int32, (chunk, 1), 0)
    sls = [slice(h * N_R, (h + 1) * N_R) for h in range(H_R)]

    def per_batch(bi):
        pr = pr_ref[bi]
        prev = jnp.where(rows == 0, prev_scr[bi], pltpu.roll(pr, 1, 0))
        prev_scr[bi] = pr[chunk - 1:chunk, :]
        xm = pr + (prev - pr) * mu_ref[...]
        r = xm[:, :W_RWKV]
        k = xm[:, W_RWKV:2 * W_RWKV]
        v = xm[:, 2 * W_RWKV:3 * W_RWKV]
        lo = xm[:, 3 * W_RWKV:3 * W_RWKV + LORA_W + LORA_A]
        g_lo = xm[:, 3 * W_RWKV + LORA_W + LORA_A:]
        logw = -_softplus(-(w0_ref[...] + _dot(jnp.tanh(lo).astype(BF16), w2_ref[...]))) - 0.5
        wl = -jnp.exp(logw)
        a = _sigmoid(a0_ref[...] + _dot(lo.astype(BF16), a2_ref[...]))
        g = _dot(_sigmoid(g_lo).astype(BF16), g2_ref[...])
        kkv = k * kk_ref[...]
        k = k * (1.0 + (a - 1.0) * ka_ref[...])
        if t_valid < chunk:
            valid = rows < t_valid
            wl = jnp.where(valid, wl, 0.0)
            kkv = jnp.where(valid, kkv, 0.0)
            k = jnp.where(valid, k, 0.0)
            v = jnp.where(valid, v, 0.0)
        cum = _sel_mm(incl_bf, wl, NN)
        cum_last = cum[chunk - 1:chunk, :]
        return dict(r=r, k=k, v=v, a=a, g=g, kkv=kkv, w_incl=jnp.exp(cum), w_prev=jnp.exp(cum - wl),
                    w_inv=jnp.exp(-cum), w_tail=jnp.exp(cum_last - cum), w_last=jnp.exp(cum_last))

    pre = [per_batch(bi) for bi in range(bb)]
    chains = [(bi, h) for bi in range(bb) for h in range(H_R)]
    col = lambda name: [pre[bi][name][:, sls[h]] for bi, h in chains]
    n = range(len(chains))
    r_, k_, v_, a_ = col('r'), col('k'), col('v'), col('a')
    w_incl, w_prev, w_inv, w_tail, w_last = col('w_incl'), col('w_prev'), col('w_inv'), col('w_tail'), col('w_last')
    kks = [x * lax.rsqrt(jnp.sum(x * x, axis=-1, keepdims=True) + L2_EPS) for x in col('kkv')]
    a_hat = [-(kks[i] * w_prev[i]) for i in n]
    kka = [kks[i] * a_[i] for i in n]
    b_hat = [kka[i] * w_inv[i] for i in n]
    k_hat = [k_[i] * w_inv[i] for i in n]
    r_hat = [r_[i] * w_incl[i] for i in n]
    cross = [_mm(jnp.concatenate([a_hat[i], r_hat[i]], axis=0), jnp.concatenate([b_hat[i], k_hat[i]], axis=0), NT)
             for i in n]
    row2 = lax.broadcasted_iota(jnp.int32, (chunk, 2 * chunk), 0)
    col2 = lax.broadcasted_iota(jnp.int32, (chunk, 2 * chunk), 1)
    col2 = jnp.where(col2 >= chunk, col2 - chunk, col2)
    strict2, incl2 = col2 < row2, col2 <= row2
    m_top = [jnp.where(strict2, x[:chunk], 0.0) for x in cross]
    a_bot = [jnp.where(incl2, x[chunk:], 0.0) for x in cross]
    t_inv = _unit_lower_inverses([x[:, :chunk] for x in m_top], eye, chunk)
    w_hat = [_mm(t_inv[i], a_hat[i], NN) for i in n]
    mv = [_mm(m_top[i], jnp.concatenate([jnp.zeros_like(v_[i]), v_[i]], axis=0), NN) for i in n]
    u = [_mm(t_inv[i], mv[i], NN) for i in n]
    ss = [s_scr[bi, h] for bi, h in chains]
    p = [_mm(w_hat[i], ss[i], NT) + u[i] for i in n]
    pv = [jnp.concatenate([p[i], v_[i]], axis=0) for i in n]
    ys = [_mm(r_hat[i], ss[i], NT) + _mm(a_bot[i], pv[i], NN) for i in n]
    for i, (bi, h) in enumerate(chains):
        tails = jnp.concatenate([kka[i] * w_tail[i], k_[i] * w_tail[i]], axis=0)
        s_scr[bi, h] = ss[i] * w_last[i] + _mm(pv[i], tails, TN)
    for i, (bi, h) in enumerate(chains):
        sl = sls[h]
        y = ys[i]
        mean = jnp.mean(y, axis=-1, keepdims=True)
        yc = y - mean
        var = jnp.mean(yc * yc, axis=-1, keepdims=True)
        yn = yc * lax.rsqrt(var + GN_EPS) * lnw_ref[:, sl] + lnb_ref[:, sl]
        bonus = jnp.sum(r_[i] * k_[i] * rk_ref[:, sl], axis=-1, keepdims=True) * v_[i]
        y_ref[bi, :, sl] = ((yn + bonus) * pre[bi]['g'][:, sl]).astype(y_ref.dtype)

    @pl.when(c == n_chunks - 1)
    def _fin():
        sout_ref[...] = s_scr[...]


def _rwkv_mix(pr, shift_prev, s0, p, chunk, t_valid, bb):
    b, t, _ = pr.shape
    assert b % bb == 0 and t % chunk == 0
    n_chunks = t // chunk
    const = lambda shape: pl.BlockSpec(shape, lambda i, j: (0,) * len(shape))
    return pl.pallas_call(
        functools.partial(_rwkv_body, bb=bb, chunk=chunk, t_valid=t_valid, n_chunks=n_chunks),
        grid=(b // bb, n_chunks),
        in_specs=[
            pl.BlockSpec((bb, chunk, R_PROJ), lambda i, j: (i, j, 0)),
            pl.BlockSpec((bb, 1, R_PROJ), lambda i, j: (i, 0, 0)),
            pl.BlockSpec((bb, H_R, N_R, N_R), lambda i, j: (i, 0, 0, 0)),
            const((1, R_PROJ)), const((1, W_RWKV)), const((LANES, W_RWKV)), const((1, W_RWKV)),
            const((LANES, W_RWKV)), const((LORA_G, W_RWKV)), const((1, W_RWKV)), const((1, W_RWKV)),
            const((1, W_RWKV)), const((1, W_RWKV)), const((1, W_RWKV)),
        ],
        out_specs=[
            pl.BlockSpec((bb, chunk, W_RWKV), lambda i, j: (i, j, 0)),
            pl.BlockSpec((bb, H_R, N_R, N_R), lambda i, j: (i, 0, 0, 0)),
        ],
        out_shape=[jax.ShapeDtypeStruct((b, t, W_RWKV), BF16), jax.ShapeDtypeStruct((b, H_R, N_R, N_R), F32)],
        scratch_shapes=[pltpu.VMEM((bb, H_R, N_R, N_R), F32), pltpu.VMEM((bb, 1, R_PROJ), F32)],
        compiler_params=pltpu.CompilerParams(dimension_semantics=("arbitrary", "arbitrary"),
                                             vmem_limit_bytes=VMEM_LIMIT),
        name="rwkv_mix",
    )(pr, shift_prev[:, None, :], s0, p['mu'], p['w0'], p['w2'], p['a0'], p['a2'], p['g2'], p['k_k'], p['k_a'],
      p['r_k'], p['lnx_w'], p['lnx_b'])


def _gdn_body(pg_ref, cbuf_ref, s0_ref, cw_ref, alog_ref, dtb_ref, nw_ref, y_ref, sout_ref, s_scr, xp_scr,
              *, bb, chunk, t_valid, n_chunks):
    c = pl.program_id(1)

    @pl.when(c == 0)
    def _init():
        s_scr[...] = s0_ref[...]
        xp_scr[:, 0:SUBLANES, :] = cbuf_ref[...]

    incl, strict, eye = _tri_masks(chunk)
    incl_bf = incl.astype(BF16)
    lane = lax.broadcasted_iota(jnp.int32, (chunk, LANES), 1)
    sls = [slice(h * D_G, (h + 1) * D_G) for h in range(H_G)]

    def per_batch(bi):
        xp_scr[bi, SUBLANES:SUBLANES + chunk, :] = pg_ref[bi, :, :GDN_CONV_DIM]
        base = SUBLANES - (CONV_W - 1)
        conv = xp_scr[bi, base:base + chunk, :] * cw_ref[0:1, :]
        for j in range(1, CONV_W):
            conv = conv + xp_scr[bi, base + j:base + j + chunk, :] * cw_ref[j:j + 1, :]
        xp_scr[bi, 0:SUBLANES, :] = xp_scr[bi, chunk:chunk + SUBLANES, :]
        qkv = conv * _sigmoid(conv)
        z = pg_ref[bi, :, GDN_CONV_DIM:GDN_CONV_DIM + W_GDN]
        ba = pg_ref[bi, :, GDN_CONV_DIM + W_GDN:]
        beta_blk = _sigmoid(ba)
        g_blk = -jnp.exp(alog_ref[...]) * _softplus(ba + dtb_ref[...])
        if t_valid < chunk:
            valid = lax.broadcasted_iota(jnp.int32, (chunk, 1), 0) < t_valid
            beta_blk = jnp.where(valid, beta_blk, 0.0)
            g_blk = jnp.where(valid, g_blk, 0.0)
        gc_blk = _sel_mm(incl_bf, g_blk, NN)
        return dict(qkv=qkv, z=z, beta_blk=beta_blk, gc_blk=gc_blk)

    pre = [per_batch(bi) for bi in range(bb)]
    chains = [(bi, h) for bi in range(bb) for h in range(H_G)]
    n = range(len(chains))
    qs = [pre[bi]['qkv'][:, sls[h]] for bi, h in chains]
    qs = [x * lax.rsqrt(jnp.sum(x * x, axis=-1, keepdims=True) + L2_EPS) * (D_G ** -0.5) for x in qs]
    ks = [pre[bi]['qkv'][:, W_GDN + h * D_G:W_GDN + (h + 1) * D_G] for bi, h in chains]
    ks = [x * lax.rsqrt(jnp.sum(x * x, axis=-1, keepdims=True) + L2_EPS) for x in ks]
    vs = [pre[bi]['qkv'][:, 2 * W_GDN + h * D_G:2 * W_GDN + (h + 1) * D_G] for bi, h in chains]
    betas = [pre[bi]['beta_blk'][:, h:h + 1] for bi, h in chains]
    gcols = [pre[bi]['gc_blk'][:, H_G + h:H_G + h + 1] for bi, h in chains]
    grows = [_sel_mm((lane == H_G + h).astype(BF16), pre[bi]['gc_blk'], NT) for bi, h in chains]
    g_last = [gcols[i][chunk - 1:chunk, :] for i in n]
    decay = [jnp.where(incl, jnp.exp(jnp.where(incl, gcols[i] - grows[i], 0.0)), 0.0) for i in n]
    k_beta = [ks[i] * betas[i] for i in n]
    cross = [_mm(jnp.concatenate([k_beta[i], qs[i]], axis=0), ks[i], NT) for i in n]
    lmat = [jnp.where(strict, cross[i][:chunk] * decay[i], 0.0) for i in n]
    attn = [jnp.where(incl, cross[i][chunk:] * decay[i], 0.0) for i in n]
    t_inv = _unit_lower_inverses([-x for x in lmat], eye, chunk)
    e_gc = [jnp.exp(gcols[i]) for i in n]
    uw = [_mm(t_inv[i], jnp.concatenate([vs[i] * betas[i], k_beta[i] * e_gc[i]], axis=1), NN) for i in n]
    ss = [s_scr[bi, h] for bi, h in chains]
    v_new = [uw[i][:, :D_G] - _mm(uw[i][:, D_G:], ss[i], NN) for i in n]
    os_ = [_mm(jnp.concatenate([qs[i] * e_gc[i], attn[i]], axis=1), jnp.concatenate([ss[i], v_new[i]], axis=0), NN)
           for i in n]
    for i, (bi, h) in enumerate(chains):
        s_scr[bi, h] = ss[i] * jnp.exp(g_last[i]) + _mm(ks[i] * jnp.exp(g_last[i] - gcols[i]), v_new[i], TN)
    for i, (bi, h) in enumerate(chains):
        o = os_[i]
        o = o * lax.rsqrt(jnp.mean(o * o, axis=-1, keepdims=True) + RMS_EPS) * nw_ref[...]
        z_h = pre[bi]['z'][:, sls[h]]
        y_ref[bi, :, sls[h]] = (o * (z_h * _sigmoid(z_h))).astype(y_ref.dtype)

    @pl.when(c == n_chunks - 1)
    def _fin():
        sout_ref[...] = s_scr[...]


def _gdn_mix(pg, conv_buf, s0, p, chunk, t_valid, bb):
    b, t, _ = pg.shape
    assert b % bb == 0 and t % chunk == 0
    n_chunks = t // chunk
    cbuf = jnp.pad(conv_buf, ((0, 0), (SUBLANES - (CONV_W - 1), 0), (0, 0)))
    const = lambda shape: pl.BlockSpec(shape, lambda i, j: (0,) * len(shape))
    return pl.pallas_call(
        functools.partial(_gdn_body, bb=bb, chunk=chunk, t_valid=t_valid, n_chunks=n_chunks),
        grid=(b // bb, n_chunks),
        in_specs=[
            pl.BlockSpec((bb, chunk, G_PROJ_PAD), lambda i, j: (i, j, 0)),
            pl.BlockSpec((bb, SUBLANES, GDN_CONV_DIM), lambda i, j: (i, 0, 0)),
            pl.BlockSpec((bb, H_G, D_G, D_G), lambda i, j: (i, 0, 0, 0)),
            const((CONV_W, GDN_CONV_DIM)), const((1, LANES)), const((1, LANES)), const((1, D_G)),
        ],
        out_specs=[
            pl.BlockSpec((bb, chunk, W_GDN), lambda i, j: (i, j, 0)),
            pl.BlockSpec((bb, H_G, D_G, D_G), lambda i, j: (i, 0, 0, 0)),
        ],
        out_shape=[jax.ShapeDtypeStruct((b, t, W_GDN), BF16), jax.ShapeDtypeStruct((b, H_G, D_G, D_G), F32)],
        scratch_shapes=[pltpu.VMEM((bb, H_G, D_G, D_G), F32),
                        pltpu.VMEM((bb, SUBLANES + chunk, GDN_CONV_DIM), F32)],
        compiler_params=pltpu.CompilerParams(dimension_semantics=("arbitrary", "arbitrary"),
                                             vmem_limit_bytes=VMEM_LIMIT),
        name="gdn_mix",
    )(pg, cbuf, s0, p['conv_w'], p['a_log'], p['dt_bias'], p['gdn_norm_w'])


def _out_q_body(x_ref, yr_ref, yg_ref, wor_ref, wog_ref, gn_ref, wq_ref, x1_ref, q_ref):
    x1 = x_ref[...] + _dot(yr_ref[...], wor_ref[...]) + _dot(yg_ref[...], wog_ref[...])
    x1_ref[...] = x1
    q_ref[...] = _dot(_rmsnorm(x1, gn_ref[...]).astype(BF16), wq_ref[...]).astype(q_ref.dtype)


def _out_q(x, yr, yg, p, tm):
    n, d = x.shape
    assert n % tm == 0
    row = lambda w: pl.BlockSpec((tm, w), lambda i: (i, 0))
    const = lambda shape: pl.BlockSpec(shape, lambda i: (0, 0))
    return pl.pallas_call(
        _out_q_body,
        grid=(n // tm,),
        in_specs=[row(d), row(W_RWKV), row(W_GDN), const((W_RWKV, d)), const((W_GDN, d)), const((1, d)),
                  const((d, d))],
        out_specs=[row(d), row(d)],
        out_shape=[jax.ShapeDtypeStruct((n, d), F32), jax.ShapeDtypeStruct((n, d), BF16)],
        compiler_params=pltpu.CompilerParams(dimension_semantics=("parallel",), vmem_limit_bytes=VMEM_LIMIT),
        name="out_q",
    )(x, yr, yg, p['w_out_r'], p['w_out_g'], p['norm_cross'], p['wq'])


def _attn_body(q_ref, x_ref, mk_ref, mv_ref, wo_ref, gn_ref, rw_ref, rb_ref, x2_ref, h_ref, ti_ref, gt_ref,
               *, bb, tq, head_rows):
    d_tiles = D_X // LANES
    mem_rows = N_MEM * d_tiles * H_X

    def head_mem(ref, i, hh):
        if not head_rows:
            return ref[i, :, hh * D_X:(hh + 1) * D_X]
        return jnp.concatenate(
            [ref[pl.ds(i * mem_rows + dt * H_X + hh, N_MEM, stride=d_tiles * H_X), :] for dt in range(d_tiles)],
            axis=-1)

    os_ = []
    for i in range(bb):
        q = q_ref[i]
        heads = []
        for hh in range(H_X):
            sl = slice(hh * D_X, (hh + 1) * D_X)
            s = _dot_nt(q[:, sl], head_mem(mk_ref, i, hh).astype(BF16)) * (D_X ** -0.5)
            e = jnp.exp(s - jnp.max(s, axis=-1, keepdims=True))
            prob = e / jnp.sum(e, axis=-1, keepdims=True)
            heads.append(_dot(prob.astype(BF16), head_mem(mv_ref, i, hh).astype(BF16)))
        os_.append(jnp.concatenate(heads, axis=-1))
    rows = bb * tq
    o = (os_[0] if bb == 1 else jnp.concatenate(os_, axis=0)).astype(BF16)
    x1 = x_ref[0] if bb == 1 else jnp.concatenate([x_ref[i] for i in range(bb)], axis=0)
    x2 = x1 + _dot(o, wo_ref[...])
    h = _rmsnorm(x2, gn_ref[...])
    for sub in range(SUBLANES):
        h_ref[pl.ds(sub, rows, stride=SUBLANES), :] = h[:, sub * LANES:(sub + 1) * LANES]
    h_hi, h_lo = _split2(h)
    pieces = _dot(jnp.concatenate([h_hi, h_lo], axis=0), rw_ref[...])
    logits = ((pieces[:rows, :LANES] + pieces[:rows, LANES:]) + (pieces[rows:, :LANES] + pieces[rows:, LANES:])
              + rb_ref[...])
    lane = lax.broadcasted_iota(jnp.int32, logits.shape, 1)
    vals, idxs = [], []
    for _ in range(TOP_K):
        m = jnp.max(logits, axis=-1, keepdims=True)
        first = jnp.min(jnp.where(logits == m, lane, LANES), axis=-1, keepdims=True)
        vals.append(m)
        idxs.append(first)
        logits = jnp.where(lane == first, -jnp.inf, logits)
    es = [jnp.exp(vv - vals[0]) for vv in vals]
    den = es[0] + es[1] + es[2] + es[3]
    ti = jnp.zeros(lane.shape, jnp.int32)
    gt = jnp.zeros(lane.shape, F32)
    for j in range(TOP_K):
        ti = jnp.where(lane == j, idxs[j], ti)
        gt = jnp.where(lane == j, es[j] / den, gt)
    for i in range(bb):
        x2_ref[i] = x2[i * tq:(i + 1) * tq]
        ti_ref[i] = ti[i * tq:(i + 1) * tq]
        gt_ref[i] = gt[i * tq:(i + 1) * tq]


def _attn_route(q, x1, mk, mv, p, bb, tq):
    b, t, d = x1.shape
    assert d == SUBLANES * LANES
    rows = bb * tq
    assert b % bb == 0 and t % tq == 0 and (bb == 1 or tq == t)
    n_tq = t // tq
    blk = lambda w: pl.BlockSpec((bb, tq, w), lambda i, j: (i, j, 0))
    head_rows = mk.ndim == 2
    mem = (pl.BlockSpec((bb * (mk.shape[0] // b), LANES), lambda i, j: (i, 0)) if head_rows
           else pl.BlockSpec((bb, N_MEM, d), lambda i, j: (i, 0, 0)))
    const = lambda shape: pl.BlockSpec(shape, lambda i, j: (0, 0))
    return pl.pallas_call(
        functools.partial(_attn_body, bb=bb, tq=tq, head_rows=head_rows),
        grid=(b // bb, n_tq),
        in_specs=[blk(d), blk(d), mem, mem, const((d, d)), const((1, d)), const((d, 2 * LANES)), const((1, LANES))],
        out_specs=[blk(d), pl.BlockSpec((rows * SUBLANES, LANES), lambda i, j: (i * n_tq + j, 0)), blk(LANES),
                   blk(LANES)],
        out_shape=[jax.ShapeDtypeStruct((b, t, d), F32), jax.ShapeDtypeStruct((b * t * SUBLANES, LANES), F32),
                   jax.ShapeDtypeStruct((b, t, LANES), jnp.int32), jax.ShapeDtypeStruct((b, t, LANES), F32)],
        compiler_params=pltpu.CompilerParams(dimension_semantics=("parallel", "parallel"),
                                             vmem_limit_bytes=VMEM_LIMIT),
        name="attn_route",
    )(q, x1, mk, mv, p['wo'], p['norm_ffn'], p['router_w'], p['router_b'])


def _moe_body(be_ref, gnext_ref, sprev_ref, h_hbm, w1_ref, b1_ref, w2_ref, b2_ref, out_hbm,
              xbuf, ybuf, w1b, w2b, gsem, ssem, *, tm, n_steps):
    k = pl.program_id(0)

    def row(ref, slot, at):
        return ref.at[slot, pl.ds(at, SUBLANES)]

    def gather_all(slot):
        return pltpu.make_async_copy(h_hbm.at[pl.ds(0, tm * SUBLANES)], xbuf.at[slot], gsem.at[slot])

    def scatter_all(slot):
        return pltpu.make_async_copy(ybuf.at[slot], out_hbm.at[pl.ds(0, tm * SUBLANES)], ssem.at[slot])

    @pl.when(k == 0)
    def _prologue():
        ybuf[...] = jnp.zeros(ybuf.shape, F32)
        for r in range(tm):
            pltpu.make_async_copy(h_hbm.at[pl.ds(0, SUBLANES)], row(xbuf, 0, r * SUBLANES), gsem.at[0]).start()

    @pl.when(jnp.logical_or(k == 0, be_ref[k] != be_ref[jnp.maximum(k - 1, 0)]))
    def _cast():
        w1b[...] = w1_ref[0].astype(BF16)
        w2b[...] = w2_ref[0].astype(BF16)

    def phase(cur):
        nxt = 1 - cur
        gather_all(cur).wait()

        @pl.when(k >= 1)
        def _():
            scatter_all(cur).wait()

        for r in range(tm):
            src = pl.multiple_of(gnext_ref[0, 0, r], SUBLANES)
            dst = pl.multiple_of(sprev_ref[0, 0, r], SUBLANES)
            pltpu.make_async_copy(h_hbm.at[pl.ds(src, SUBLANES)], row(xbuf, nxt, r * SUBLANES), gsem.at[nxt]).start()
            pltpu.make_async_copy(row(ybuf, nxt, r * SUBLANES), out_hbm.at[pl.ds(dst, SUBLANES)],
                                  ssem.at[nxt]).start()
        x = jnp.concatenate([xbuf[cur, pl.ds(sub, tm, stride=SUBLANES), :] for sub in range(SUBLANES)], axis=-1)
        hc = _dot(x.astype(BF16), w1b[...]) + b1_ref[0]
        hg = jnp.minimum(hc[:, :D_FF], SWIGLU_LIMIT)
        hl = jnp.clip(hc[:, D_FF:], -SWIGLU_LIMIT, SWIGLU_LIMIT)
        act = hg * _sigmoid(SWIGLU_ALPHA * hg) * (hl + 1.0)
        y = _dot(act.astype(BF16), w2b[...]) + b2_ref[0]
        for sub in range(SUBLANES):
            ybuf[cur, pl.ds(sub, tm, stride=SUBLANES), :] = y[:, sub * LANES:(sub + 1) * LANES]

        @pl.when(k == n_steps - 1)
        def _epilogue():
            gather_all(nxt).wait()
            scatter_all(nxt).wait()

    for parity in range(2):
        pl.when(lax.rem(k, 2) == parity)(functools.partial(phase, parity))


def _moe(h, gtok, sdst, block_e, p, tm):
    n, d = h.shape[0] // SUBLANES, D_MODEL
    n_steps = block_e.shape[0]
    grid_spec = pltpu.PrefetchScalarGridSpec(
        num_scalar_prefetch=1,
        grid=(n_steps,),
        in_specs=[
            pl.BlockSpec((1, 1, tm), lambda k, be: (jnp.minimum(k + 1, n_steps - 1), 0, 0),
                         memory_space=pltpu.SMEM),
            pl.BlockSpec((1, 1, tm), lambda k, be: (jnp.maximum(k - 1, 0), 0, 0), memory_space=pltpu.SMEM),
            pl.BlockSpec(memory_space=pl.ANY),
            pl.BlockSpec((1, d, 2 * D_FF), lambda k, be: (be[k], 0, 0)),
            pl.BlockSpec((1, 1, 2 * D_FF), lambda k, be: (be[k], 0, 0)),
            pl.BlockSpec((1, D_FF, d), lambda k, be: (be[k], 0, 0)),
            pl.BlockSpec((1, 1, d), lambda k, be: (be[k], 0, 0)),
        ],
        out_specs=pl.BlockSpec(memory_space=pl.ANY),
        scratch_shapes=[
            pltpu.VMEM((2, tm * SUBLANES, LANES), F32), pltpu.VMEM((2, tm * SUBLANES, LANES), F32),
            pltpu.VMEM((d, 2 * D_FF), BF16), pltpu.VMEM((D_FF, d), BF16),
            pltpu.SemaphoreType.DMA((2,)), pltpu.SemaphoreType.DMA((2,)),
        ],
    )
    return pl.pallas_call(
        functools.partial(_moe_body, tm=tm, n_steps=n_steps),
        grid_spec=grid_spec,
        out_shape=jax.ShapeDtypeStruct(((n * TOP_K + tm) * SUBLANES, LANES), F32),
        compiler_params=pltpu.CompilerParams(dimension_semantics=("arbitrary",), vmem_limit_bytes=VMEM_LIMIT),
        name="moe_experts",
    )(block_e, gtok.reshape(n_steps, 1, tm), sdst.reshape(n_steps, 1, tm), h, p['w1_e'], p['b1_e'], p['w2_e'],
      p['b2_e'])


def _route_plan(top_i, tm):
    n = top_i.shape[0]
    na = n * TOP_K
    flat_e = top_i.reshape(na)
    id_bits = (na - 1).bit_length()
    assert N_EXPERTS << id_bits < 2 ** 31
    keys = jnp.sort((flat_e << id_bits) | jnp.arange(na, dtype=jnp.int32))
    order = keys & ((1 << id_bits) - 1)
    counts = jnp.sum((flat_e[:, None] == jnp.arange(N_EXPERTS, dtype=jnp.int32)[None, :]).astype(jnp.int32), axis=0)
    padded = (counts + tm - 1) // tm * tm
    starts = jnp.cumsum(counts) - counts
    pends = jnp.cumsum(padded)
    pstarts = pends - padded
    n_steps = -(-(na + N_EXPERTS * (tm - 1)) // tm) + 2
    blk_start = (jnp.arange(n_steps, dtype=jnp.int32) - 1) * tm
    block_e = jnp.sum((blk_start[:, None] >= pends[None, :]).astype(jnp.int32), axis=1)
    block_e = jnp.minimum(block_e, N_EXPERTS - 1)
    block_e = block_e.at[0].set(block_e[1])
    lane = jnp.arange(tm, dtype=jnp.int32)[None, :]
    local = blk_start[:, None] + lane - pstarts[block_e][:, None]
    valid = (local >= 0) & (local < counts[block_e][:, None]) & (blk_start[:, None] >= 0)
    asg = order[jnp.clip(starts[block_e][:, None] + local, 0, na - 1)]
    gtok = jnp.where(valid, asg // TOP_K, 0)
    sdst = jnp.where(valid, (asg % TOP_K) * n + asg // TOP_K, TOP_K * n + lane)
    return gtok * SUBLANES, sdst * SUBLANES, block_e


def _combine_body(x_ref, gt_ref, gn_ref, *refs, final):
    slot_refs, y_ref = refs[:TOP_K], refs[TOP_K]
    x = x_ref[...]
    gt = gt_ref[...]
    tm = x.shape[0]
    for j in range(TOP_K):
        slot = jnp.concatenate([slot_refs[j][pl.ds(sub, tm, stride=SUBLANES), :] for sub in range(SUBLANES)],
                               axis=-1)
        x = x + gt[:, j:j + 1] * slot
    y_ref[...] = _rmsnorm(x, gn_ref[...]) if final else x


def _combine(x2, gates, slots, gn, tm, n_all, row0, final):
    n, d = x2.shape
    assert n % tm == 0 and row0 % tm == 0 and n_all % tm == 0
    slot_spec = lambda j: pl.BlockSpec((tm * SUBLANES, LANES), lambda i: ((j * n_all + row0) // tm + i, 0))
    return pl.pallas_call(
        functools.partial(_combine_body, final=final),
        grid=(n // tm,),
        in_specs=[pl.BlockSpec((tm, d), lambda i: (i, 0)), pl.BlockSpec((tm, LANES), lambda i: (i, 0)),
                  pl.BlockSpec((1, d), lambda i: (0, 0))] + [slot_spec(j) for j in range(TOP_K)],
        out_specs=pl.BlockSpec((tm, d), lambda i: (i, 0)),
        out_shape=jax.ShapeDtypeStruct((n, d), F32),
        compiler_params=pltpu.CompilerParams(dimension_semantics=("parallel",), vmem_limit_bytes=VMEM_LIMIT),
        name="combine",
    )(x2, gates, gn, *([slots] * TOP_K))


def _layer_params(l, norm_mix, w_in, mu_shift, w0, w2_decay, a0, a2_iclr, g2_gate, k_k, k_a, r_k, lnx_w, lnx_b,
                  conv_w, a_log, dt_bias, gdn_norm_w, w_out, norm_cross, norm_mem, wq_x, wk_x, wv_x, wo_x,
                  norm_ffn, router_w, router_b, w1_e, b1_e, w2_e, b2_e):
    row = lambda z: z.reshape(1, -1).astype(F32)
    lane_pad = lambda z, at: jnp.zeros((1, LANES), F32).at[0, at:at + z.shape[0]].set(z)
    return {
        'norm_mix': row(norm_mix[l]),
        'w_in_r': w_in[l][:, :R_PROJ].astype(BF16),
        'w_in_g': jnp.pad(w_in[l][:, R_PROJ:], ((0, 0), (0, G_PROJ_PAD - G_PROJ))).astype(BF16),
        'mu': row(mu_shift[l]), 'w0': row(w0[l]), 'a0': row(a0[l]),
        'w2': jnp.pad(w2_decay[l], ((0, LORA_A), (0, 0))).astype(BF16),
        'a2': jnp.pad(a2_iclr[l], ((LORA_W, 0), (0, 0))).astype(BF16),
        'g2': g2_gate[l].astype(BF16),
        'k_k': row(k_k[l]), 'k_a': row(k_a[l]), 'r_k': row(r_k[l]), 'lnx_w': row(lnx_w[l]), 'lnx_b': row(lnx_b[l]),
        'conv_w': conv_w[l].astype(F32),
        'a_log': lane_pad(a_log[l], H_G), 'dt_bias': lane_pad(dt_bias[l], H_G),
        'gdn_norm_w': row(gdn_norm_w[l]),
        'w_out_r': w_out[l][:W_RWKV].astype(BF16), 'w_out_g': w_out[l][W_RWKV:].astype(BF16),
        'norm_cross': row(norm_cross[l]), 'norm_mem': row(norm_mem[l]),
        'wq': wq_x[l].astype(BF16), 'wk': wk_x[l].astype(BF16), 'wv': wv_x[l].astype(BF16),
        'wo': wo_x[l].astype(BF16),
        'norm_ffn': row(norm_ffn[l]),
        'router_w': jnp.concatenate(_split2(jnp.pad(router_w[l].astype(F32), ((0, 0), (0, LANES - N_EXPERTS)))),
                                    axis=1),
        'router_b': jnp.full((1, LANES), NEG_BIG, F32).at[0, :N_EXPERTS].set(router_b[l].astype(F32)),
        'w1_e': w1_e[l], 'b1_e': b1_e[l][:, None, :], 'w2_e': w2_e[l], 'b2_e': b2_e[l][:, None, :],
    }


def _mix_and_attend(x, mk, mv, shift_prev, s_r, conv_buf, s_g, p, *, chunk, rwkv_bb, gdn_bb, tm, bb, tq):
    b, t, d = x.shape
    assert t >= CONV_W - 1
    pr, pg = _norm_proj(x.reshape(b * t, d), p['norm_mix'], [p['w_in_r'], p['w_in_g']], [F32, F32], tm)
    pr = pr.reshape(b, t, R_PROJ)
    pg = pg.reshape(b, t, G_PROJ_PAD)
    shift_new = pr[:, t - 1]
    conv_new = pg[:, t - (CONV_W - 1):, :GDN_CONV_DIM]
    t_pad = -(-t // chunk) * chunk
    if t_pad != t:
        pr = jnp.pad(pr, ((0, 0), (0, t_pad - t), (0, 0)))
        pg = jnp.pad(pg, ((0, 0), (0, t_pad - t), (0, 0)))
    t_valid = chunk if t_pad == t else t
    y_r, s_r_new = _rwkv_mix(pr, shift_prev, s_r, p, chunk, t_valid, rwkv_bb)
    y_g, s_g_new = _gdn_mix(pg, conv_buf, s_g, p, chunk, t_valid, gdn_bb)
    if t_pad != t:
        y_r, y_g = y_r[:, :t], y_g[:, :t]
    x1, q = _out_q(x.reshape(b * t, d), y_r.reshape(b * t, W_RWKV), y_g.reshape(b * t, W_GDN), p, tm)
    x2, h, top_i, gates = _attn_route(q.reshape(b, t, d), x1.reshape(b, t, d), mk, mv, p, bb, tq)
    return (x2.reshape(b * t, d), h, top_i.reshape(b * t, LANES), gates.reshape(b * t, LANES),
            shift_new, s_r_new, conv_new, s_g_new)


def kernel(x_prompt, x_sample, mem_prompt, state_rwkv, state_rwkv_shift, state_gdn, state_gdn_conv, cache_mem_k, cache_mem_v, norm_mix, w_in, mu_shift, w0, w2_decay, a0, a2_iclr, g2_gate, k_k, k_a, r_k, lnx_w, lnx_b, conv_w, a_log, dt_bias, gdn_norm_w, w_out, norm_cross, norm_mem, wq_x, wk_x, wv_x, wo_x, norm_ffn, router_w, router_b, w1_e, b1_e, w2_e, b2_e, final_norm):
    bp, tp, d = x_prompt.shape
    bs, ts, _ = x_sample.shape
    depth = w_in.shape[0]
    np_, ns = bp * tp, bs * ts
    xp, xs = x_prompt, x_sample
    outs = [[] for _ in range(10)]
    for l in range(depth):
        p = _layer_params(l, norm_mix, w_in, mu_shift, w0, w2_decay, a0, a2_iclr, g2_gate, k_k, k_a, r_k, lnx_w,
                          lnx_b, conv_w, a_log, dt_bias, gdn_norm_w, w_out, norm_cross, norm_mem, wq_x, wk_x, wv_x,
                          wo_x, norm_ffn, router_w, router_b, w1_e, b1_e, w2_e, b2_e)
        n_mem = mem_prompt.shape[1]
        mk, mv = _norm_proj(mem_prompt.reshape(bp * n_mem, d), p['norm_mem'], [p['wk'], p['wv']], [F32, F32], 256)
        mk, mv = mk.reshape(bp, n_mem, d), mv.reshape(bp, n_mem, d)
        res_p = _mix_and_attend(
            xp, mk, mv, jnp.zeros((bp, R_PROJ), F32), jnp.zeros((bp, H_R, N_R, N_R), F32),
            jnp.zeros((bp, CONV_W - 1, GDN_CONV_DIM), F32), jnp.zeros((bp, H_G, D_G, D_G), F32), p,
            chunk=MIX_CHUNK, rwkv_bb=4, gdn_bb=8, tm=512, bb=1, tq=512)
        head_rows = lambda c: c.reshape(bs, n_mem, H_X, D_X // LANES, LANES).transpose(0, 1, 3, 2, 4).reshape(-1, LANES)
        mk_s, mv_s = head_rows(cache_mem_k[l]), head_rows(cache_mem_v[l])
        res_s = _mix_and_attend(
            xs, mk_s, mv_s, state_rwkv_shift[l], state_rwkv[l], state_gdn_conv[l],
            state_gdn[l], p, chunk=SUBLANES, rwkv_bb=8, gdn_bb=8, tm=256, bb=8, tq=ts)
        h = jnp.concatenate([res_p[1], res_s[1]], axis=0)
        top_i = jnp.concatenate([res_p[2], res_s[2]], axis=0)[:, :TOP_K]
        gtok, sdst, block_e = _route_plan(top_i, MOE_ROWS)
        slots = _moe(h, gtok, sdst, block_e, p, MOE_ROWS)
        gn = final_norm.reshape(1, d).astype(F32)
        last = l == depth - 1
        xp = _combine(res_p[0], res_p[3], slots, gn, 512, np_ + ns, 0, last).reshape(bp, tp, d)
        xs = _combine(res_s[0], res_s[3], slots, gn, 512, np_ + ns, np_, last).reshape(bs, ts, d)
        new = [res_p[5], res_p[4], res_p[7], res_p[6], mk.reshape(bp, n_mem, H_X, D_X),
               mv.reshape(bp, n_mem, H_X, D_X), res_s[5], res_s[4], res_s[7], res_s[6]]
        for acc, val in zip(outs, new):
            acc.append(val)
    return (xp, xs) + tuple(jnp.stack(o) for o in outs)
```

```python
import functools

import jax
import jax.numpy as jnp
from jax import lax
from jax.experimental import pallas as pl
from jax.experimental.pallas import tpu as pltpu

F32 = jnp.float32
BF16 = jnp.bfloat16
DEFAULT = lax.Precision.DEFAULT

D_MODEL = 1024
W_RWKV = 512
N_R = 64
H_R = W_RWKV // N_R
LORA_W = 64
LORA_A = 64
LORA_G = 128
R_PROJ = 3 * W_RWKV + LORA_W + LORA_A + LORA_G
GN_EPS = 64e-5
W_GDN = 512
D_G = 128
H_G = W_GDN // D_G
GDN_CONV_DIM = 3 * W_GDN
CONV_W = 4
G_PROJ = GDN_CONV_DIM + W_GDN + 2 * H_G
LANES = 128
SUBLANES = 8
G_PROJ_PAD = GDN_CONV_DIM + W_GDN + LANES
MIX_CHUNK = 64
N_MEM = 256
H_X = 4
D_X = D_MODEL // H_X
N_EXPERTS = 32
TOP_K = 4
D_FF = D_MODEL
SWIGLU_LIMIT = 7.0
SWIGLU_ALPHA = 1.702
MOE_ROWS = 256
RMS_EPS = 1e-6
L2_EPS = 1e-6
NEG_BIG = -1e30
VMEM_LIMIT = 56 * 1024 * 1024


def _dot(a, b, precision=DEFAULT):
    return jnp.dot(a, b, preferred_element_type=F32, precision=precision)


def _dot_nt(a, b, precision=DEFAULT):
    return lax.dot_general(a, b, (((1,), (1,)), ((), ())), preferred_element_type=F32, precision=precision)


NN = (((1,), (0,)), ((), ()))
NT = (((1,), (1,)), ((), ()))
TN = (((0,), (0,)), ((), ()))


def _split2(x):
    hi = x.astype(BF16)
    return hi, (x - hi.astype(F32)).astype(BF16)


def _mm(a, b, dims):
    return lax.dot_general(a.astype(BF16), b.astype(BF16), dims, preferred_element_type=F32)


def _sel_mm(sel, x, dims):
    dg = lambda y: lax.dot_general(sel, y, dims, preferred_element_type=F32)
    hi = x.astype(BF16)
    r1 = x - hi.astype(F32)
    mid = r1.astype(BF16)
    lo = (r1 - mid.astype(F32)).astype(BF16)
    return dg(hi) + (dg(mid) + dg(lo))


def _sigmoid(x):
    return 1.0 / (1.0 + jnp.exp(-x))


def _softplus(x):
    return jnp.maximum(x, 0.0) + jnp.log(1.0 + jnp.exp(-jnp.abs(x)))


def _rmsnorm(x, g):
    return x * lax.rsqrt(jnp.mean(x * x, axis=-1, keepdims=True) + RMS_EPS) * g


def _tri_masks(c):
    row = lax.broadcasted_iota(jnp.int32, (c, c), 0)
    col = lax.broadcasted_iota(jnp.int32, (c, c), 1)
    return col <= row, col < row, (col == row).astype(F32)


def _unit_lower_inverses(ms, eye, c):
    ts = [eye + m for m in ms]
    ps = list(ms)
    covered = 2
    while covered < c:
        ps = [_mm(p, p, NN) for p in ps]
        ts = [t + _mm(t, p, NN) for t, p in zip(ts, ps)]
        covered *= 2
    return ts


def _norm_proj_body(x_ref, g_ref, *refs, n_out):
    w_refs, o_refs = refs[:n_out], refs[n_out:]
    hb = _rmsnorm(x_ref[...], g_ref[...]).astype(BF16)
    for w_ref, o_ref in zip(w_refs, o_refs):
        o_ref[...] = _dot(hb, w_ref[...]).astype(o_ref.dtype)


def _norm_proj(x, g, ws, out_dtypes, tm):
    n, d = x.shape
    assert n % tm == 0
    in_specs = [pl.BlockSpec((tm, d), lambda i: (i, 0)), pl.BlockSpec((1, d), lambda i: (0, 0))]
    in_specs += [pl.BlockSpec(w.shape, lambda i: (0, 0)) for w in ws]
    return pl.pallas_call(
        functools.partial(_norm_proj_body, n_out=len(ws)),
        grid=(n // tm,),
        in_specs=in_specs,
        out_specs=[pl.BlockSpec((tm, w.shape[1]), lambda i: (i, 0)) for w in ws],
        out_shape=[jax.ShapeDtypeStruct((n, w.shape[1]), dt) for w, dt in zip(ws, out_dtypes)],
        compiler_params=pltpu.CompilerParams(dimension_semantics=("parallel",), vmem_limit_bytes=VMEM_LIMIT),
        name="norm_proj",
    )(x, g, *ws)


def _rwkv_body(pr_ref, shift_ref, s0_ref, mu_ref, w0_ref, w2_ref, a0_ref, a2_ref, g2_ref, kk_ref, ka_ref,
               rk_ref, lnw_ref, lnb_ref, y_ref, sout_ref, s_scr, prev_scr, *, bb, chunk, t_valid, n_chunks):
    c = pl.program_id(1)

    @pl.when(c == 0)
    def _init():
        s_scr[...] = s0_ref[...]
        prev_scr[...] = shift_ref[...]

    incl, strict, eye = _tri_masks(chunk)
    incl_bf = incl.astype(BF16)
    rows = lax.broadcasted_iota(jnp.int32, (chunk, 1), 0)
    sls = [slice(h * N_R, (h + 1) * N_R) for h in range(H_R)]

    def per_batch(bi):
        pr = pr_ref[bi]
        prev = jnp.where(rows == 0, prev_scr[bi], pltpu.roll(pr, 1, 0))
        prev_scr[bi] = pr[chunk - 1:chunk, :]
        xm = pr + (prev - pr) * mu_ref[...]
        r = xm[:, :W_RWKV]
        k = xm[:, W_RWKV:2 * W_RWKV]
        v = xm[:, 2 * W_RWKV:3 * W_RWKV]
        lo = xm[:, 3 * W_RWKV:3 * W_RWKV + LORA_W + LORA_A]
        g_lo = xm[:, 3 * W_RWKV + LORA_W + LORA_A:]
        logw = -_softplus(-(w0_ref[...] + _dot(jnp.tanh(lo).astype(BF16), w2_ref[...]))) - 0.5
        wl = -jnp.exp(logw)
        a = _sigmoid(a0_ref[...] + _dot(lo.astype(BF16), a2_ref[...]))
        g = _dot(_sigmoid(g_lo).astype(BF16), g2_ref[...])
        kkv = k * kk_ref[...]
        k = k * (1.0 + (a - 1.0) * ka_ref[...])
        if t_valid < chunk:
            valid = rows < t_valid
            wl = jnp.where(valid, wl, 0.0)
            kkv = jnp.where(valid, kkv, 0.0)
            k = jnp.where(valid, k, 0.0)
            v = jnp.where(valid, v, 0.0)
        cum = _sel_mm(incl_bf, wl, NN)
        cum_last = cum[chunk - 1:chunk, :]
        return dict(r=r, k=k, v=v, a=a, g=g, kkv=kkv, w_incl=jnp.exp(cum), w_prev=jnp.exp(cum - wl),
                    w_inv=jnp.exp(-cum), w_tail=jnp.exp(cum_last - cum), w_last=jnp.exp(cum_last))

    pre = [per_batch(bi) for bi in range(bb)]
    chains = [(bi, h) for bi in range(bb) for h in range(H_R)]
    col = lambda name: [pre[bi][name][:, sls[h]] for bi, h in chains]
    n = range(len(chains))
    r_, k_, v_, a_ = col('r'), col('k'), col('v'), col('a')
    w_incl, w_prev, w_inv, w_tail, w_last = col('w_incl'), col('w_prev'), col('w_inv'), col('w_tail'), col('w_last')
    kks = [x * lax.rsqrt(jnp.sum(x * x, axis=-1, keepdims=True) + L2_EPS) for x in col('kkv')]
    a_hat = [-(kks[i] * w_prev[i]) for i in n]
    kka = [kks[i] * a_[i] for i in n]
    b_hat = [kka[i] * w_inv[i] for i in n]
    k_hat = [k_[i] * w_inv[i] for i in n]
    r_hat = [r_[i] * w_incl[i] for i in n]
    cross = [_mm(jnp.concatenate([a_hat[i], r_hat[i]], axis=0), jnp.concatenate([b_hat[i], k_hat[i]], axis=0), NT)
             for i in n]
    row2 = lax.broadcasted_iota(jnp.int32, (chunk, 2 * chunk), 0)
    col2 = lax.broadcasted_iota(jnp.int32, (chunk, 2 * chunk), 1)
    col2 = jnp.where(col2 >= chunk, col2 - chunk, col2)
    strict2, incl2 = col2 < row2, col2 <= row2
    m_top = [jnp.where(strict2, x[:chunk], 0.0) for x in cross]
    a_bot = [jnp.where(incl2, x[chunk:], 0.0) for x in cross]
    t_inv = _unit_lower_inverses([x[:, :chunk] for x in m_top], eye, chunk)
    w_hat = [_mm(t_inv[i], a_hat[i], NN) for i in n]
    mv = [_mm(m_top[i], jnp.concatenate([jnp.zeros_like(v_[i]), v_[i]], axis=0), NN) for i in n]
    u = [_mm(t_inv[i], mv[i], NN) for i in n]
    ss = [s_scr[bi, h] for bi, h in chains]
    p = [_mm(w_hat[i], ss[i], NT) + u[i] for i in n]
    pv = [jnp.concatenate([p[i], v_[i]], axis=0) for i in n]
    ys = [_mm(r_hat[i], ss[i], NT) + _mm(a_bot[i], pv[i], NN) for i in n]
    for i, (bi, h) in enumerate(chains):
        tails = jnp.concatenate([kka[i] * w_tail[i], k_[i] * w_tail[i]], axis=0)
        s_scr[bi, h] = ss[i] * w_last[i] + _mm(pv[i], tails, TN)
    for i, (bi, h) in enumerate(chains):
        sl = sls[h]
        y = ys[i]
        mean = jnp.mean(y, axis=-1, keepdims=True)
        yc = y - mean
        var = jnp.mean(yc * yc, axis=-1, keepdims=True)
        yn = yc * lax.rsqrt(var + GN_EPS) * lnw_ref[:, sl] + lnb_ref[:, sl]
        bonus = jnp.sum(r_[i] * k_[i] * rk_ref[:, sl], axis=-1, keepdims=True) * v_[i]
        y_ref[bi, :, sl] = ((yn + bonus) * pre[bi]['g'][:, sl]).astype(y_ref.dtype)

    @pl.when(c == n_chunks - 1)
    def _fin():
        sout_ref[...] = s_scr[...]


def _rwkv_mix(pr, shift_prev, s0, p, chunk, t_valid, bb):
    b, t, _ = pr.shape
    assert b % bb == 0 and t % chunk == 0
    n_chunks = t // chunk
    const = lambda shape: pl.BlockSpec(shape, lambda i, j: (0,) * len(shape))
    return pl.pallas_call(
        functools.partial(_rwkv_body, bb=bb, chunk=chunk, t_valid=t_valid, n_chunks=n_chunks),
        grid=(b // bb, n_chunks),
        in_specs=[
            pl.BlockSpec((bb, chunk, R_PROJ), lambda i, j: (i, j, 0)),
            pl.BlockSpec((bb, 1, R_PROJ), lambda i, j: (i, 0, 0)),
            pl.BlockSpec((bb, H_R, N_R, N_R), lambda i, j: (i, 0, 0, 0)),
            const((1, R_PROJ)), const((1, W_RWKV)), const((LANES, W_RWKV)), const((1, W_RWKV)),
            const((LANES, W_RWKV)), const((LORA_G, W_RWKV)), const((1, W_RWKV)), const((1, W_RWKV)),
            const((1, W_RWKV)), const((1, W_RWKV)), const((1, W_RWKV)),
        ],
        out_specs=[
            pl.BlockSpec((bb, chunk, W_RWKV), lambda i, j: (i, j, 0)),
            pl.BlockSpec((bb, H_R, N_R, N_R), lambda i, j: (i, 0, 0, 0)),
        ],
        out_shape=[jax.ShapeDtypeStruct((b, t, W_RWKV), BF16), jax.ShapeDtypeStruct((b, H_R, N_R, N_R), F32)],
        scratch_shapes=[pltpu.VMEM((bb, H_R, N_R, N_R), F32), pltpu.VMEM((bb, 1, R_PROJ), F32)],
        compiler_params=pltpu.CompilerParams(dimension_semantics=("arbitrary", "arbitrary"),
                                             vmem_limit_bytes=VMEM_LIMIT),
        name="rwkv_mix",
    )(pr, shift_prev[:, None, :], s0, p['mu'], p['w0'], p['w2'], p['a0'], p['a2'], p['g2'], p['k_k'], p['k_a'],
      p['r_k'], p['lnx_w'], p['lnx_b'])


def _gdn_body(pg_ref, cbuf_ref, s0_ref, cw_ref, alog_ref, dtb_ref, nw_ref, y_ref, sout_ref, s_scr, xp_scr,
              *, bb, chunk, t_valid, n_chunks):
    c = pl.program_id(1)

    @pl.when(c == 0)
    def _init():
        s_scr[...] = s0_ref[...]
        xp_scr[:, 0:SUBLANES, :] = cbuf_ref[...]

    incl, strict, eye = _tri_masks(chunk)
    incl_bf = incl.astype(BF16)
    lane = lax.broadcasted_iota(jnp.int32, (chunk, LANES), 1)
    sls = [slice(h * D_G, (h + 1) * D_G) for h in range(H_G)]

    def per_batch(bi):
        xp_scr[bi, SUBLANES:SUBLANES + chunk, :] = pg_ref[bi, :, :GDN_CONV_DIM]
        base = SUBLANES - (CONV_W - 1)
        conv = xp_scr[bi, base:base + chunk, :] * cw_ref[0:1, :]
        for j in range(1, CONV_W):
            conv = conv + xp_scr[bi, base + j:base + j + chunk, :] * cw_ref[j:j + 1, :]
        xp_scr[bi, 0:SUBLANES, :] = xp_scr[bi, chunk:chunk + SUBLANES, :]
        qkv = conv * _sigmoid(conv)
        z = pg_ref[bi, :, GDN_CONV_DIM:GDN_CONV_DIM + W_GDN]
        ba = pg_ref[bi, :, GDN_CONV_DIM + W_GDN:]
        beta_blk = _sigmoid(ba)
        g_blk = -jnp.exp(alog_ref[...]) * _softplus(ba + dtb_ref[...])
        if t_valid < chunk:
            valid = lax.broadcasted_iota(jnp.int32, (chunk, 1), 0) < t_valid
            beta_blk = jnp.where(valid, beta_blk, 0.0)
            g_blk = jnp.where(valid, g_blk, 0.0)
        gc_blk = _sel_mm(incl_bf, g_blk, NN)
        return dict(qkv=qkv, z=z, beta_blk=beta_blk, gc_blk=gc_blk)

    pre = [per_batch(bi) for bi in range(bb)]
    chains = [(bi, h) for bi in range(bb) for h in range(H_G)]
    n = range(len(chains))
    qs = [pre[bi]['qkv'][:, sls[h]] for bi, h in chains]
    qs = [x * lax.rsqrt(jnp.sum(x * x, axis=-1, keepdims=True) + L2_EPS) * (D_G ** -0.5) for x in qs]
    ks = [pre[bi]['qkv'][:, W_GDN + h * D_G:W_GDN + (h + 1) * D_G] for bi, h in chains]
    ks = [x * lax.rsqrt(jnp.sum(x * x, axis=-1, keepdims=True) + L2_EPS) for x in ks]
    vs = [pre[bi]['qkv'][:, 2 * W_GDN + h * D_G:2 * W_GDN + (h + 1) * D_G] for bi, h in chains]
    betas = [pre[bi]['beta_blk'][:, h:h + 1] for bi, h in chains]
    gcols = [pre[bi]['gc_blk'][:, H_G + h:H_G + h + 1] for bi, h in chains]
    grows = [_sel_mm((lane == H_G + h).astype(BF16), pre[bi]['gc_blk'], NT) for bi, h in chains]
    g_last = [gcols[i][chunk - 1:chunk, :] for i in n]
    decay = [jnp.where(incl, jnp.exp(jnp.where(incl, gcols[i] - grows[i], 0.0)), 0.0) for i in n]
    k_beta = [ks[i] * betas[i] for i in n]
    cross = [_mm(jnp.concatenate([k_beta[i], qs[i]], axis=0), ks[i], NT) for i in n]
    lmat = [jnp.where(strict, cross[i][:chunk] * decay[i], 0.0) for i in n]
    attn = [jnp.where(incl, cross[i][chunk:] * decay[i], 0.0) for i in n]
    t_inv = _unit_lower_inverses([-x for x in lmat], eye, chunk)
    e_gc = [jnp.exp(gcols[i]) for i in n]
    uw = [_mm(t_inv[i], jnp.concatenate([vs[i] * betas[i], k_beta[i] * e_gc[i]], axis=1), NN) for i in n]
    ss = [s_scr[bi, h] for bi, h in chains]
    v_new = [uw[i][:, :D_G] - _mm(uw[i][:, D_G:], ss[i], NN) for i in n]
    os_ = [_mm(jnp.concatenate([qs[i] * e_gc[i], attn[i]], axis=1), jnp.concatenate([ss[i], v_new[i]], axis=0), NN)
           for i in n]
    for i, (bi, h) in enumerate(chains):
        s_scr[bi, h] = ss[i] * jnp.exp(g_last[i]) + _mm(ks[i] * jnp.exp(g_last[i] - gcols[i]), v_new[i], TN)
    for i, (bi, h) in enumerate(chains):
        o = os_[i]
        o = o * lax.rsqrt(jnp.mean(o * o, axis=-1, keepdims=True) + RMS_EPS) * nw_ref[...]
        z_h = pre[bi]['z'][:, sls[h]]
        y_ref[bi, :, sls[h]] = (o * (z_h * _sigmoid(z_h))).astype(y_ref.dtype)

    @pl.when(c == n_chunks - 1)
    def _fin():
        sout_ref[...] = s_scr[...]


def _gdn_mix(pg, conv_buf, s0, p, chunk, t_valid, bb):
    b, t, _ = pg.shape
    assert b % bb == 0 and t % chunk == 0
    n_chunks = t // chunk
    cbuf = jnp.pad(conv_buf, ((0, 0), (SUBLANES - (CONV_W - 1), 0), (0, 0)))
    const = lambda shape: pl.BlockSpec(shape, lambda i, j: (0,) * len(shape))
    return pl.pallas_call(
        functools.partial(_gdn_body, bb=bb, chunk=chunk, t_valid=t_valid, n_chunks=n_chunks),
        grid=(b // bb, n_chunks),
        in_specs=[
            pl.BlockSpec((bb, chunk, G_PROJ_PAD), lambda i, j: (i, j, 0)),
            pl.BlockSpec((bb, SUBLANES, GDN_CONV_DIM), lambda i, j: (i, 0, 0)),
            pl.BlockSpec((bb, H_G, D_G, D_G), lambda i, j: (i, 0, 0, 0)),
            const((CONV_W, GDN_CONV_DIM)), const((1, LANES)), const((1, LANES)), const((1, D_G)),
        ],
        out_specs=[
            pl.BlockSpec((bb, chunk, W_GDN), lambda i, j: (i, j, 0)),
            pl.BlockSpec((bb, H_G, D_G, D_G), lambda i, j: (i, 0, 0, 0)),
        ],
        out_shape=[jax.ShapeDtypeStruct((b, t, W_GDN), BF16), jax.ShapeDtypeStruct((b, H_G, D_G, D_G), F32)],
        scratch_shapes=[pltpu.VMEM((bb, H_G, D_G, D_G), F32),
                        pltpu.VMEM((bb, SUBLANES + chunk, GDN_CONV_DIM), F32)],
        compiler_params=pltpu.CompilerParams(dimension_semantics=("arbitrary", "arbitrary"),
                                             vmem_limit_bytes=VMEM_LIMIT),
        name="gdn_mix",
    )(pg, cbuf, s0, p['conv_w'], p['a_log'], p['dt_bias'], p['gdn_norm_w'])


def _out_q_body(x_ref, yr_ref, yg_ref, wor_ref, wog_ref, gn_ref, wq_ref, x1_ref, q_ref):
    x1 = x_ref[...] + _dot(yr_ref[...], wor_ref[...]) + _dot(yg_ref[...], wog_ref[...])
    x1_ref[...] = x1
    q_ref[...] = _dot(_rmsnorm(x1, gn_ref[...]).astype(BF16), wq_ref[...]).astype(q_ref.dtype)


def _out_q(x, yr, yg, p, tm):
    n, d = x.shape
    assert n % tm == 0
    row = lambda w: pl.BlockSpec((tm, w), lambda i: (i, 0))
    const = lambda shape: pl.BlockSpec(shape, lambda i: (0, 0))
    return pl.pallas_call(
        _out_q_body,
        grid=(n // tm,),
        in_specs=[row(d), row(W_RWKV), row(W_GDN), const((W_RWKV, d)), const((W_GDN, d)), const((1, d)),
                  const((d, d))],
        out_specs=[row(d), row(d)],
        out_shape=[jax.ShapeDtypeStruct((n, d), F32), jax.ShapeDtypeStruct((n, d), BF16)],
        compiler_params=pltpu.CompilerParams(dimension_semantics=("parallel",), vmem_limit_bytes=VMEM_LIMIT),
        name="out_q",
    )(x, yr, yg, p['w_out_r'], p['w_out_g'], p['norm_cross'], p['wq'])


def _attn_body(q_ref, x_ref, mk_ref, mv_ref, wo_ref, gn_ref, rw_ref, rb_ref, x2_ref, h_ref, ti_ref, gt_ref,
               *, bb, tq, head_rows):
    d_tiles = D_X // LANES
    mem_rows = N_MEM * d_tiles * H_X

    def head_mem(ref, i, hh):
        if not head_rows:
            return ref[i, :, hh * D_X:(hh + 1) * D_X]
        return jnp.concatenate(
            [ref[pl.ds(i * mem_rows + dt * H_X + hh, N_MEM, stride=d_tiles * H_X), :] for dt in range(d_tiles)],
            axis=-1)

    os_ = []
    for i in range(bb):
        q = q_ref[i]
        heads = []
        for hh in range(H_X):
            sl = slice(hh * D_X, (hh + 1) * D_X)
            s = _dot_nt(q[:, sl], head_mem(mk_ref, i, hh).astype(BF16)) * (D_X ** -0.5)
            e = jnp.exp(s - jnp.max(s, axis=-1, keepdims=True))
            prob = e / jnp.sum(e, axis=-1, keepdims=True)
            heads.append(_dot(prob.astype(BF16), head_mem(mv_ref, i, hh).astype(BF16)))
        os_.append(jnp.concatenate(heads, axis=-1))
    rows = bb * tq
    o = (os_[0] if bb == 1 else jnp.concatenate(os_, axis=0)).astype(BF16)
    x1 = x_ref[0] if bb == 1 else jnp.concatenate([x_ref[i] for i in range(bb)], axis=0)
    x2 = x1 + _dot(o, wo_ref[...])
    h = _rmsnorm(x2, gn_ref[...])
    for sub in range(SUBLANES):
        h_ref[pl.ds(sub, rows, stride=SUBLANES), :] = h[:, sub * LANES:(sub + 1) * LANES]
    h_hi, h_lo = _split2(h)
    pieces = _dot(jnp.concatenate([h_hi, h_lo], axis=0), rw_ref[...])
    logits = ((pieces[:rows, :LANES] + pieces[:rows, LANES:]) + (pieces[rows:, :LANES] + pieces[rows:, LANES:])
              + rb_ref[...])
    lane = lax.broadcasted_iota(jnp.int32, logits.shape, 1)
    vals, idxs = [], []
    for _ in range(TOP_K):
        m = jnp.max(logits, axis=-1, keepdims=True)
        first = jnp.min(jnp.where(logits == m, lane, LANES), axis=-1, keepdims=True)
        vals.append(m)
        idxs.append(first)
        logits = jnp.where(lane == first, -jnp.inf, logits)
    es = [jnp.exp(vv - vals[0]) for vv in vals]
    den = es[0] + es[1] + es[2] + es[3]
    ti = jnp.zeros(lane.shape, jnp.int32)
    gt = jnp.zeros(lane.shape, F32)
    for j in range(TOP_K):
        ti = jnp.where(lane == j, idxs[j], ti)
        gt = jnp.where(lane == j, es[j] / den, gt)
    for i in range(bb):
        x2_ref[i] = x2[i * tq:(i + 1) * tq]
        ti_ref[i] = ti[i * tq:(i + 1) * tq]
        gt_ref[i] = gt[i * tq:(i + 1) * tq]


def _attn_route(q, x1, mk, mv, p, bb, tq):
    b, t, d = x1.shape
    assert d == SUBLANES * LANES
    rows = bb * tq
    assert b % bb == 0 and t % tq == 0 and (bb == 1 or tq == t)
    n_tq = t // tq
    blk = lambda w: pl.BlockSpec((bb, tq, w), lambda i, j: (i, j, 0))
    head_rows = mk.ndim == 2
    mem = (pl.BlockSpec((bb * (mk.shape[0] // b), LANES), lambda i, j: (i, 0)) if head_rows
           else pl.BlockSpec((bb, N_MEM, d), lambda i, j: (i, 0, 0)))
    const = lambda shape: pl.BlockSpec(shape, lambda i, j: (0, 0))
    return pl.pallas_call(
        functools.partial(_attn_body, bb=bb, tq=tq, head_rows=head_rows),
        grid=(b // bb, n_tq),
        in_specs=[blk(d), blk(d), mem, mem, const((d, d)), const((1, d)), const((d, 2 * LANES)), const((1, LANES))],
        out_specs=[blk(d), pl.BlockSpec((rows * SUBLANES, LANES), lambda i, j: (i * n_tq + j, 0)), blk(LANES),
                   blk(LANES)],
        out_shape=[jax.ShapeDtypeStruct((b, t, d), F32), jax.ShapeDtypeStruct((b * t * SUBLANES, LANES), F32),
                   jax.ShapeDtypeStruct((b, t, LANES), jnp.int32), jax.ShapeDtypeStruct((b, t, LANES), F32)],
        compiler_params=pltpu.CompilerParams(dimension_semantics=("parallel", "parallel"),
                                             vmem_limit_bytes=VMEM_LIMIT),
        name="attn_route",
    )(q, x1, mk, mv, p['wo'], p['norm_ffn'], p['router_w'], p['router_b'])


def _moe_body(be_ref, gnext_ref, sprev_ref, h_hbm, w1_ref, b1_ref, w2_ref, b2_ref, out_hbm,
              xbuf, ybuf, w1b, w2b, gsem, ssem, *, tm, n_steps):
    k = pl.program_id(0)

    def row(ref, slot, at):
        return ref.at[slot, pl.ds(at, SUBLANES)]

    def gather_all(slot):
        return pltpu.make_async_copy(h_hbm.at[pl.ds(0, tm * SUBLANES)], xbuf.at[slot], gsem.at[slot])

    def scatter_all(slot):
        return pltpu.make_async_copy(ybuf.at[slot], out_hbm.at[pl.ds(0, tm * SUBLANES)], ssem.at[slot])

    @pl.when(k == 0)
    def _prologue():
        ybuf[...] = jnp.zeros(ybuf.shape, F32)
        for r in range(tm):
            pltpu.make_async_copy(h_hbm.at[pl.ds(0, SUBLANES)], row(xbuf, 0, r * SUBLANES), gsem.at[0]).start()

    @pl.when(jnp.logical_or(k == 0, be_ref[k] != be_ref[jnp.maximum(k - 1, 0)]))
    def _cast():
        w1b[...] = w1_ref[0].astype(BF16)
        w2b[...] = w2_ref[0].astype(BF16)

    def phase(cur):
        nxt = 1 - cur
        gather_all(cur).wait()

        @pl.when(k >= 1)
        def _():
            scatter_all(cur).wait()

        for r in range(tm):
            src = pl.multiple_of(gnext_ref[0, 0, r], SUBLANES)
            dst = pl.multiple_of(sprev_ref[0, 0, r], SUBLANES)
            pltpu.make_async_copy(h_hbm.at[pl.ds(src, SUBLANES)], row(xbuf, nxt, r * SUBLANES), gsem.at[nxt]).start()
            pltpu.make_async_copy(row(ybuf, nxt, r * SUBLANES), out_hbm.at[pl.ds(dst, SUBLANES)],
                                  ssem.at[nxt]).start()
        x = jnp.concatenate([xbuf[cur, pl.ds(sub, tm, stride=SUBLANES), :] for sub in range(SUBLANES)], axis=-1)
        hc = _dot(x.astype(BF16), w1b[...]) + b1_ref[0]
        hg = jnp.minimum(hc[:, :D_FF], SWIGLU_LIMIT)
        hl = jnp.clip(hc[:, D_FF:], -SWIGLU_LIMIT, SWIGLU_LIMIT)
        act = hg * _sigmoid(SWIGLU_ALPHA * hg) * (hl + 1.0)
        y = _dot(act.astype(BF16), w2b[...]) + b2_ref[0]
        for sub in range(SUBLANES):
            ybuf[cur, pl.ds(sub, tm, stride=SUBLANES), :] = y[:, sub * LANES:(sub + 1) * LANES]

        @pl.when(k == n_steps - 1)
        def _epilogue():
            gather_all(nxt).wait()
            scatter_all(nxt).wait()

    for parity in range(2):
        pl.when(lax.rem(k, 2) == parity)(functools.partial(phase, parity))


def _moe(h, gtok, sdst, block_e, p, tm):
    n, d = h.shape[0] // SUBLANES, D_MODEL
    n_steps = block_e.shape[0]
    grid_spec = pltpu.PrefetchScalarGridSpec(
        num_scalar_prefetch=1,
        grid=(n_steps,),
        in_specs=[
            pl.BlockSpec((1, 1, tm), lambda k, be: (jnp.minimum(k + 1, n_steps - 1), 0, 0),
                         memory_space=pltpu.SMEM),
            pl.BlockSpec((1, 1, tm), lambda k, be: (jnp.maximum(k - 1, 0), 0, 0), memory_space=pltpu.SMEM),
            pl.BlockSpec(memory_space=pl.ANY),
            pl.BlockSpec((1, d, 2 * D_FF), lambda k, be: (be[k], 0, 0)),
            pl.BlockSpec((1, 1, 2 * D_FF), lambda k, be: (be[k], 0, 0)),
            pl.BlockSpec((1, D_FF, d), lambda k, be: (be[k], 0, 0)),
            pl.BlockSpec((1, 1, d), lambda k, be: (be[k], 0, 0)),
        ],
        out_specs=pl.BlockSpec(memory_space=pl.ANY),
        scratch_shapes=[
            pltpu.VMEM((2, tm * SUBLANES, LANES), F32), pltpu.VMEM((2, tm * SUBLANES, LANES), F32),
            pltpu.VMEM((d, 2 * D_FF), BF16), pltpu.VMEM((D_FF, d), BF16),
            pltpu.SemaphoreType.DMA((2,)), pltpu.SemaphoreType.DMA((2,)),
        ],
    )
    return pl.pallas_call(
        functools.partial(_moe_body, tm=tm, n_steps=n_steps),
        grid_spec=grid_spec,
        out_shape=jax.ShapeDtypeStruct(((n * TOP_K + tm) * SUBLANES, LANES), F32),
        compiler_params=pltpu.CompilerParams(dimension_semantics=("arbitrary",), vmem_limit_bytes=VMEM_LIMIT),
        name="moe_experts",
    )(block_e, gtok.reshape(n_steps, 1, tm), sdst.reshape(n_steps, 1, tm), h, p['w1_e'], p['b1_e'], p['w2_e'],
      p['b2_e'])


def _route_plan(top_i, tm):
    n = top_i.shape[0]
    na = n * TOP_K
    flat_e = top_i.reshape(na)
    order = jnp.argsort(flat_e).astype(jnp.int32)
    counts = jnp.sum((flat_e[:, None] == jnp.arange(N_EXPERTS, dtype=jnp.int32)[None, :]).astype(jnp.int32), axis=0)
    padded = (counts + tm - 1) // tm * tm
    starts = jnp.cumsum(counts) - counts
    pends = jnp.cumsum(padded)
    pstarts = pends - padded
    n_steps = -(-(na + N_EXPERTS * (tm - 1)) // tm) + 2
    blk_start = (jnp.arange(n_steps, dtype=jnp.int32) - 1) * tm
    block_e = jnp.sum((blk_start[:, None] >= pends[None, :]).astype(jnp.int32), axis=1)
    block_e = jnp.minimum(block_e, N_EXPERTS - 1)
    block_e = jnp.where(blk_start < 0, block_e[1], block_e)
    is_e = block_e[:, None] == jnp.arange(N_EXPERTS, dtype=jnp.int32)[None, :]
    pick = lambda table: jnp.sum(jnp.where(is_e, table[None, :], 0), axis=1)
    lane = jnp.arange(tm, dtype=jnp.int32)[None, :]
    local = blk_start[:, None] + lane - pick(pstarts)[:, None]
    valid = (local >= 0) & (local < pick(counts)[:, None]) & (blk_start[:, None] >= 0)
    asg = order[jnp.clip(pick(starts)[:, None] + local, 0, na - 1)]
    gtok = jnp.where(valid, asg // TOP_K, 0)
    sdst = jnp.where(valid, (asg % TOP_K) * n + asg // TOP_K, TOP_K * n + lane)
    return gtok * SUBLANES, sdst * SUBLANES, block_e


def _combine_body(x_ref, gt_ref, gn_ref, *refs, final):
    slot_refs, y_ref = refs[:TOP_K], refs[TOP_K]
    x = x_ref[...]
    gt = gt_ref[...]
    tm = x.shape[0]
    for j in range(TOP_K):
        slot = jnp.concatenate([slot_refs[j][pl.ds(sub, tm, stride=SUBLANES), :] for sub in range(SUBLANES)],
                               axis=-1)
        x = x + gt[:, j:j + 1] * slot
    y_ref[...] = _rmsnorm(x, gn_ref[...]) if final else x


def _combine(x2, gates, slots, gn, tm, n_all, row0, final):
    n, d = x2.shape
    assert n % tm == 0 and row0 % tm == 0 and n_all % tm == 0
    slot_spec = lambda j: pl.BlockSpec((tm * SUBLANES, LANES), lambda i: ((j * n_all + row0) // tm + i, 0))
    return pl.pallas_call(
        functools.partial(_combine_body, final=final),
        grid=(n // tm,),
        in_specs=[pl.BlockSpec((tm, d), lambda i: (i, 0)), pl.BlockSpec((tm, LANES), lambda i: (i, 0)),
                  pl.BlockSpec((1, d), lambda i: (0, 0))] + [slot_spec(j) for j in range(TOP_K)],
        out_specs=pl.BlockSpec((tm, d), lambda i: (i, 0)),
        out_shape=jax.ShapeDtypeStruct((n, d), F32),
        compiler_params=pltpu.CompilerParams(dimension_semantics=("parallel",), vmem_limit_bytes=VMEM_LIMIT),
        name="combine",
    )(x2, gates, gn, *([slots] * TOP_K))


def _layer_params(l, norm_mix, w_in, mu_shift, w0, w2_decay, a0, a2_iclr, g2_gate, k_k, k_a, r_k, lnx_w, lnx_b,
                  conv_w, a_log, dt_bias, gdn_norm_w, w_out, norm_cross, norm_mem, wq_x, wk_x, wv_x, wo_x,
                  norm_ffn, router_w, router_b, w1_e, b1_e, w2_e, b2_e):
    row = lambda z: z.reshape(1, -1).astype(F32)
    lane_pad = lambda z, at: jnp.zeros((1, LANES), F32).at[0, at:at + z.shape[0]].set(z)
    return {
        'norm_mix': row(norm_mix[l]),
        'w_in_r': w_in[l][:, :R_PROJ].astype(BF16),
        'w_in_g': jnp.pad(w_in[l][:, R_PROJ:], ((0, 0), (0, G_PROJ_PAD - G_PROJ))).astype(BF16),
        'mu': row(mu_shift[l]), 'w0': row(w0[l]), 'a0': row(a0[l]),
        'w2': jnp.pad(w2_decay[l], ((0, LORA_A), (0, 0))).astype(BF16),
        'a2': jnp.pad(a2_iclr[l], ((LORA_W, 0), (0, 0))).astype(BF16),
        'g2': g2_gate[l].astype(BF16),
        'k_k': row(k_k[l]), 'k_a': row(k_a[l]), 'r_k': row(r_k[l]), 'lnx_w': row(lnx_w[l]), 'lnx_b': row(lnx_b[l]),
        'conv_w': conv_w[l].astype(F32),
        'a_log': lane_pad(a_log[l], H_G), 'dt_bias': lane_pad(dt_bias[l], H_G),
        'gdn_norm_w': row(gdn_norm_w[l]),
        'w_out_r': w_out[l][:W_RWKV].astype(BF16), 'w_out_g': w_out[l][W_RWKV:].astype(BF16),
        'norm_cross': row(norm_cross[l]), 'norm_mem': row(norm_mem[l]),
        'wq': wq_x[l].astype(BF16), 'wk': wk_x[l].astype(BF16), 'wv': wv_x[l].astype(BF16),
        'wo': wo_x[l].astype(BF16),
        'norm_ffn': row(norm_ffn[l]),
        'router_w': jnp.concatenate(_split2(jnp.pad(router_w[l].astype(F32), ((0, 0), (0, LANES - N_EXPERTS)))),
                                    axis=1),
        'router_b': jnp.full((1, LANES), NEG_BIG, F32).at[0, :N_EXPERTS].set(router_b[l].astype(F32)),
        'w1_e': w1_e[l], 'b1_e': b1_e[l][:, None, :], 'w2_e': w2_e[l], 'b2_e': b2_e[l][:, None, :],
    }


def _mix_and_attend(x, mk, mv, shift_prev, s_r, conv_buf, s_g, p, *, chunk, rwkv_bb, gdn_bb, tm, bb, tq):
    b, t, d = x.shape
    assert t >= CONV_W - 1
    pr, pg = _norm_proj(x.reshape(b * t, d), p['norm_mix'], [p['w_in_r'], p['w_in_g']], [F32, F32], tm)
    pr = pr.reshape(b, t, R_PROJ)
    pg = pg.reshape(b, t, G_PROJ_PAD)
    shift_new = pr[:, t - 1]
    conv_new = pg[:, t - (CONV_W - 1):, :GDN_CONV_DIM]
    t_pad = -(-t // chunk) * chunk
    if t_pad != t:
        pr = jnp.pad(pr, ((0, 0), (0, t_pad - t), (0, 0)))
        pg = jnp.pad(pg, ((0, 0), (0, t_pad - t), (0, 0)))
    t_valid = chunk if t_pad == t else t
    y_r, s_r_new = _rwkv_mix(pr, shift_prev, s_r, p, chunk, t_valid, rwkv_bb)
    y_g, s_g_new = _gdn_mix(pg, conv_buf, s_g, p, chunk, t_valid, gdn_bb)
    if t_pad != t:
        y_r, y_g = y_r[:, :t], y_g[:, :t]
    x1, q = _out_q(x.reshape(b * t, d), y_r.reshape(b * t, W_RWKV), y_g.reshape(b * t, W_GDN), p, tm)
    x2, h, top_i, gates = _attn_route(q.reshape(b, t, d), x1.reshape(b, t, d), mk, mv, p, bb, tq)
    return (x2.reshape(b * t, d), h, top_i.reshape(b * t, LANES), gates.reshape(b * t, LANES),
            shift_new, s_r_new, conv_new, s_g_new)


def kernel(x_prompt, x_sample, mem_prompt, state_rwkv, state_rwkv_shift, state_gdn, state_gdn_conv, cache_mem_k, cache_mem_v, norm_mix, w_in, mu_shift, w0, w2_decay, a0, a2_iclr, g2_gate, k_k, k_a, r_k, lnx_w, lnx_b, conv_w, a_log, dt_bias, gdn_norm_w, w_out, norm_cross, norm_mem, wq_x, wk_x, wv_x, wo_x, norm_ffn, router_w, router_b, w1_e, b1_e, w2_e, b2_e, final_norm):
    bp, tp, d = x_prompt.shape
    bs, ts, _ = x_sample.shape
    depth = w_in.shape[0]
    np_, ns = bp * tp, bs * ts
    xp, xs = x_prompt, x_sample
    outs = [[] for _ in range(10)]
    for l in range(depth):
        p = _layer_params(l, norm_mix, w_in, mu_shift, w0, w2_decay, a0, a2_iclr, g2_gate, k_k, k_a, r_k, lnx_w,
                          lnx_b, conv_w, a_log, dt_bias, gdn_norm_w, w_out, norm_cross, norm_mem, wq_x, wk_x, wv_x,
                          wo_x, norm_ffn, router_w, router_b, w1_e, b1_e, w2_e, b2_e)
        n_mem = mem_prompt.shape[1]
        mk, mv = _norm_proj(mem_prompt.reshape(bp * n_mem, d), p['norm_mem'], [p['wk'], p['wv']], [F32, F32], 256)
        mk, mv = mk.reshape(bp, n_mem, d), mv.reshape(bp, n_mem, d)
        res_p = _mix_and_attend(
            xp, mk, mv, jnp.zeros((bp, R_PROJ), F32), jnp.zeros((bp, H_R, N_R, N_R), F32),
            jnp.zeros((bp, CONV_W - 1, GDN_CONV_DIM), F32), jnp.zeros((bp, H_G, D_G, D_G), F32), p,
            chunk=MIX_CHUNK, rwkv_bb=4, gdn_bb=8, tm=512, bb=1, tq=512)
        head_rows = lambda c: c.reshape(bs, n_mem, H_X, D_X // LANES, LANES).transpose(0, 1, 3, 2, 4).reshape(-1, LANES)
        mk_s, mv_s = head_rows(cache_mem_k[l]), head_rows(cache_mem_v[l])
        res_s = _mix_and_attend(
            xs, mk_s, mv_s, state_rwkv_shift[l], state_rwkv[l], state_gdn_conv[l],
            state_gdn[l], p, chunk=SUBLANES, rwkv_bb=8, gdn_bb=8, tm=256, bb=8, tq=ts)
        h = jnp.concatenate([res_p[1], res_s[1]], axis=0)
        top_i = jnp.concatenate([res_p[2], res_s[2]], axis=0)[:, :TOP_K]
        gtok, sdst, block_e = _route_plan(top_i, MOE_ROWS)
        slots = _moe(h, gtok, sdst, block_e, p, MOE_ROWS)
        gn = final_norm.reshape(1, d).astype(F32)
        last = l == depth - 1
        xp = _combine(res_p[0], res_p[3], slots, gn, 512, np_ + ns, 0, last).reshape(bp, tp, d)
        xs = _combine(res_s[0], res_s[3], slots, gn, 512, np_ + ns, np_, last).reshape(bs, ts, d)
        new = [res_p[5], res_p[4], res_p[7], res_p[6], mk.reshape(bp, n_mem, H_X, D_X),
               mv.reshape(bp, n_mem, H_X, D_X), res_s[5], res_s[4], res_s[7], res_s[6]]
        for acc, val in zip(outs, new):
            acc.append(val)
    return (xp, xs) + tuple(jnp.stack(o) for o in outs)
```

```python
import functools

import jax
import jax.numpy as jnp
from jax import lax
from jax.experimental import pallas as pl
from jax.experimental.pallas import tpu as pltpu

F32 = jnp.float32
BF16 = jnp.bfloat16
DEFAULT = lax.Precision.DEFAULT

D_MODEL = 1024
W_RWKV = 512
N_R = 64
H_R = W_RWKV // N_R
LORA_W = 64
LORA_A = 64
LORA_G = 128
R_PROJ = 3 * W_RWKV + LORA_W + LORA_A + LORA_G
GN_EPS = 64e-5
W_GDN = 512
D_G = 128
H_G = W_GDN // D_G
GDN_CONV_DIM = 3 * W_GDN
CONV_W = 4
G_PROJ = GDN_CONV_DIM + W_GDN + 2 * H_G
LANES = 128
SUBLANES = 8
G_PROJ_PAD = GDN_CONV_DIM + W_GDN + LANES
MIX_CHUNK = 64
N_MEM = 256
H_X = 4
D_X = D_MODEL // H_X
N_EXPERTS = 32
TOP_K = 4
D_FF = D_MODEL
SWIGLU_LIMIT = 7.0
SWIGLU_ALPHA = 1.702
MOE_ROWS = 256
RMS_EPS = 1e-6
L2_EPS = 1e-6
NEG_BIG = -1e30
VMEM_LIMIT = 56 * 1024 * 1024


def _dot(a, b, precision=DEFAULT):
    return jnp.dot(a, b, preferred_element_type=F32, precision=precision)


def _dot_nt(a, b, precision=DEFAULT):
    return lax.dot_general(a, b, (((1,), (1,)), ((), ())), preferred_element_type=F32, precision=precision)


NN = (((1,), (0,)), ((), ()))
NT = (((1,), (1,)), ((), ()))
TN = (((0,), (0,)), ((), ()))


def _split2(x):
    hi = x.astype(BF16)
    return hi, (x - hi.astype(F32)).astype(BF16)


def _mm(a, b, dims):
    return lax.dot_general(a.astype(BF16), b.astype(BF16), dims, preferred_element_type=F32)


def _sel_mm(sel, x, dims):
    dg = lambda y: lax.dot_general(sel, y, dims, preferred_element_type=F32)
    hi = x.astype(BF16)
    r1 = x - hi.astype(F32)
    mid = r1.astype(BF16)
    lo = (r1 - mid.astype(F32)).astype(BF16)
    return dg(hi) + (dg(mid) + dg(lo))


def _sigmoid(x):
    return 1.0 / (1.0 + jnp.exp(-x))


def _softplus(x):
    return jnp.maximum(x, 0.0) + jnp.log(1.0 + jnp.exp(-jnp.abs(x)))


def _rmsnorm(x, g):
    return x * lax.rsqrt(jnp.mean(x * x, axis=-1, keepdims=True) + RMS_EPS) * g


def _tri_masks(c):
    row = lax.broadcasted_iota(jnp.int32, (c, c), 0)
    col = lax.broadcasted_iota(jnp.int32, (c, c), 1)
    return col <= row, col < row, (col == row).astype(F32)


def _unit_lower_inverses(ms, eye, c):
    ts = [eye + m for m in ms]
    ps = list(ms)
    covered = 2
    while covered < c:
        ps = [_mm(p, p, NN) for p in ps]
        ts = [t + _mm(t, p, NN) for t, p in zip(ts, ps)]
        covered *= 2
    return ts


def _norm_proj_body(x_ref, g_ref, *refs, n_out):
    w_refs, o_refs = refs[:n_out], refs[n_out:]
    hb = _rmsnorm(x_ref[...], g_ref[...]).astype(BF16)
    for w_ref, o_ref in zip(w_refs, o_refs):
        o_ref[...] = _dot(hb, w_ref[...]).astype(o_ref.dtype)


def _norm_proj(x, g, ws, out_dtypes, tm):
    n, d = x.shape
    assert n % tm == 0
    in_specs = [pl.BlockSpec((tm, d), lambda i: (i, 0)), pl.BlockSpec((1, d), lambda i: (0, 0))]
    in_specs += [pl.BlockSpec(w.shape, lambda i: (0, 0)) for w in ws]
    return pl.pallas_call(
        functools.partial(_norm_proj_body, n_out=len(ws)),
        grid=(n // tm,),
        in_specs=in_specs,
        out_specs=[pl.BlockSpec((tm, w.shape[1]), lambda i: (i, 0)) for w in ws],
        out_shape=[jax.ShapeDtypeStruct((n, w.shape[1]), dt) for w, dt in zip(ws, out_dtypes)],
        compiler_params=pltpu.CompilerParams(dimension_semantics=("parallel",), vmem_limit_bytes=VMEM_LIMIT),
        name="norm_proj",
    )(x, g, *ws)


def _rwkv_body(pr_ref, shift_ref, s0_ref, mu_ref, w0_ref, w2_ref, a0_ref, a2_ref, g2_ref, kk_ref, ka_ref,
               rk_ref, lnw_ref, lnb_ref, y_ref, sout_ref, s_scr, prev_scr, *, bb, chunk, t_valid, n_chunks):
    c = pl.program_id(1)

    @pl.when(c == 0)
    def _init():
        s_scr[...] = s0_ref[...]
        prev_scr[...] = shift_ref[...]

    incl, strict, eye = _tri_masks(chunk)
    incl_bf = incl.astype(BF16)
    rows = lax.broadcasted_iota(jnp.int32, (chunk, 1), 0)
    sls = [slice(h * N_R, (h + 1) * N_R) for h in range(H_R)]

    def per_batch(bi):
        pr = pr_ref[bi]
        prev = jnp.where(rows == 0, prev_scr[bi], pltpu.roll(pr, 1, 0))
        prev_scr[bi] = pr[chunk - 1:chunk, :]
        xm = pr + (prev - pr) * mu_ref[...]
        r = xm[:, :W_RWKV]
        k = xm[:, W_RWKV:2 * W_RWKV]
        v = xm[:, 2 * W_RWKV:3 * W_RWKV]
        lo = xm[:, 3 * W_RWKV:3 * W_RWKV + LORA_W + LORA_A]
        g_lo = xm[:, 3 * W_RWKV + LORA_W + LORA_A:]
        logw = -_softplus(-(w0_ref[...] + _dot(jnp.tanh(lo).astype(BF16), w2_ref[...]))) - 0.5
        wl = -jnp.exp(logw)
        a = _sigmoid(a0_ref[...] + _dot(lo.astype(BF16), a2_ref[...]))
        g = _dot(_sigmoid(g_lo).astype(BF16), g2_ref[...])
        kkv = k * kk_ref[...]
        k = k * (1.0 + (a - 1.0) * ka_ref[...])
        if t_valid < chunk:
            valid = rows < t_valid
            wl = jnp.where(valid, wl, 0.0)
            kkv = jnp.where(valid, kkv, 0.0)
            k = jnp.where(valid, k, 0.0)
            v = jnp.where(valid, v, 0.0)
        cum = _sel_mm(incl_bf, wl, NN)
        cum_last = cum[chunk - 1:chunk, :]
        return dict(r=r, k=k, v=v, a=a, g=g, kkv=kkv, w_incl=jnp.exp(cum), w_prev=jnp.exp(cum - wl),
                    w_inv=jnp.exp(-cum), w_tail=jnp.exp(cum_last - cum), w_last=jnp.exp(cum_last))

    pre = [per_batch(bi) for bi in range(bb)]
    chains = [(bi, h) for bi in range(bb) for h in range(H_R)]
    col = lambda name: [pre[bi][name][:, sls[h]] for bi, h in chains]
    n = range(len(chains))
    r_, k_, v_, a_ = col('r'), col('k'), col('v'), col('a')
    w_incl, w_prev, w_inv, w_tail, w_last = col('w_incl'), col('w_prev'), col('w_inv'), col('w_tail'), col('w_last')
    kks = [x * lax.rsqrt(jnp.sum(x * x, axis=-1, keepdims=True) + L2_EPS) for x in col('kkv')]
    a_hat = [-(kks[i] * w_prev[i]) for i in n]
    kka = [kks[i] * a_[i] for i in n]
    b_hat = [kka[i] * w_inv[i] for i in n]
    k_hat = [k_[i] * w_inv[i] for i in n]
    r_hat = [r_[i] * w_incl[i] for i in n]
    cross = [_mm(jnp.concatenate([a_hat[i], r_hat[i]], axis=0), jnp.concatenate([b_hat[i], k_hat[i]], axis=0), NT)
             for i in n]
    row2 = lax.broadcasted_iota(jnp.int32, (chunk, 2 * chunk), 0)
    col2 = lax.broadcasted_iota(jnp.int32, (chunk, 2 * chunk), 1)
    col2 = jnp.where(col2 >= chunk, col2 - chunk, col2)
    strict2, incl2 = col2 < row2, col2 <= row2
    m_top = [jnp.where(strict2, x[:chunk], 0.0) for x in cross]
    a_bot = [jnp.where(incl2, x[chunk:], 0.0) for x in cross]
    t_inv = _unit_lower_inverses([x[:, :chunk] for x in m_top], eye, chunk)
    w_hat = [_mm(t_inv[i], a_hat[i], NN) for i in n]
    mv = [_mm(m_top[i], jnp.concatenate([jnp.zeros_like(v_[i]), v_[i]], axis=0), NN) for i in n]
    u = [_mm(t_inv[i], mv[i], NN) for i in n]
    ss = [s_scr[bi, h] for bi, h in chains]
    p = [_mm(w_hat[i], ss[i], NT) + u[i] for i in n]
    pv = [jnp.concatenate([p[i], v_[i]], axis=0) for i in n]
    ys = [_mm(r_hat[i], ss[i], NT) + _mm(a_bot[i], pv[i], NN) for i in n]
    for i, (bi, h) in enumerate(chains):
        tails = jnp.concatenate([kka[i] * w_tail[i], k_[i] * w_tail[i]], axis=0)
        s_scr[bi, h] = ss[i] * w_last[i] + _mm(pv[i], tails, TN)
    for i, (bi, h) in enumerate(chains):
        sl = sls[h]
        y = ys[i]
        mean = jnp.mean(y, axis=-1, keepdims=True)
        yc = y - mean
        var = jnp.mean(yc * yc, axis=-1, keepdims=True)
        yn = yc * lax.rsqrt(var + GN_EPS) * lnw_ref[:, sl] + lnb_ref[:, sl]
        bonus = jnp.sum(r_[i] * k_[i] * rk_ref[:, sl], axis=-1, keepdims=True) * v_[i]
        y_ref[bi, :, sl] = ((yn + bonus) * pre[bi]['g'][:, sl]).astype(y_ref.dtype)

    @pl.when(c == n_chunks - 1)
    def _fin():
        sout_ref[...] = s_scr[...]


def _rwkv_mix(pr, shift_prev, s0, p, chunk, t_valid, bb):
    b, t, _ = pr.shape
    assert b % bb == 0 and t % chunk == 0
    n_chunks = t // chunk
    const = lambda shape: pl.BlockSpec(shape, lambda i, j: (0,) * len(shape))
    return pl.pallas_call(
        functools.partial(_rwkv_body, bb=bb, chunk=chunk, t_valid=t_valid, n_chunks=n_chunks),
        grid=(b // bb, n_chunks),
        in_specs=[
            pl.BlockSpec((bb, chunk, R_PROJ), lambda i, j: (i, j, 0)),
            pl.BlockSpec((bb, 1, R_PROJ), lambda i, j: (i, 0, 0)),
            pl.BlockSpec((bb, H_R, N_R, N_R), lambda i, j: (i, 0, 0, 0)),
            const((1, R_PROJ)), const((1, W_RWKV)), const((LANES, W_RWKV)), const((1, W_RWKV)),
            const((LANES, W_RWKV)), const((LORA_G, W_RWKV)), const((1, W_RWKV)), const((1, W_RWKV)),
            const((1, W_RWKV)), const((1, W_RWKV)), const((1, W_RWKV)),
        ],
        out_specs=[
            pl.BlockSpec((bb, chunk, W_RWKV), lambda i, j: (i, j, 0)),
            pl.BlockSpec((bb, H_R, N_R, N_R), lambda i, j: (i, 0, 0, 0)),
        ],
        out_shape=[jax.ShapeDtypeStruct((b, t, W_RWKV), BF16), jax.ShapeDtypeStruct((b, H_R, N_R, N_R), F32)],
        scratch_shapes=[pltpu.VMEM((bb, H_R, N_R, N_R), F32), pltpu.VMEM((bb, 1, R_PROJ), F32)],
        compiler_params=pltpu.CompilerParams(dimension_semantics=("arbitrary", "arbitrary"),
                                             vmem_limit_bytes=VMEM_LIMIT),
        name="rwkv_mix",
    )(pr, shift_prev[:, None, :], s0, p['mu'], p['w0'], p['w2'], p['a0'], p['a2'], p['g2'], p['k_k'], p['k_a'],
      p['r_k'], p['lnx_w'], p['lnx_b'])


def _gdn_body(pg_ref, cbuf_ref, s0_ref, cw_ref, alog_ref, dtb_ref, nw_ref, y_ref, sout_ref, s_scr, xp_scr,
              *, bb, chunk, t_valid, n_chunks):
    c = pl.program_id(1)

    @pl.when(c == 0)
    def _init():
        s_scr[...] = s0_ref[...]
        xp_scr[:, 0:SUBLANES, :] = cbuf_ref[...]

    incl, strict, eye = _tri_masks(chunk)
    incl_bf = incl.astype(BF16)
    lane = lax.broadcasted_iota(jnp.int32, (chunk, LANES), 1)
    sls = [slice(h * D_G, (h + 1) * D_G) for h in range(H_G)]

    def per_batch(bi):
        xp_scr[bi, SUBLANES:SUBLANES + chunk, :] = pg_ref[bi, :, :GDN_CONV_DIM]
        base = SUBLANES - (CONV_W - 1)
        conv = xp_scr[bi, base:base + chunk, :] * cw_ref[0:1, :]
        for j in range(1, CONV_W):
            conv = conv + xp_scr[bi, base + j:base + j + chunk, :] * cw_ref[j:j + 1, :]
        xp_scr[bi, 0:SUBLANES, :] = xp_scr[bi, chunk:chunk + SUBLANES, :]
        qkv = conv * _sigmoid(conv)
        z = pg_ref[bi, :, GDN_CONV_DIM:GDN_CONV_DIM + W_GDN]
        ba = pg_ref[bi, :, GDN_CONV_DIM + W_GDN:]
        beta_blk = _sigmoid(ba)
        g_blk = -jnp.exp(alog_ref[...]) * _softplus(ba + dtb_ref[...])
        if t_valid < chunk:
            valid = lax.broadcasted_iota(jnp.int32, (chunk, 1), 0) < t_valid
            beta_blk = jnp.where(valid, beta_blk, 0.0)
            g_blk = jnp.where(valid, g_blk, 0.0)
        gc_blk = _sel_mm(incl_bf, g_blk, NN)
        return dict(qkv=qkv, z=z, beta_blk=beta_blk, gc_blk=gc_blk)

    pre = [per_batch(bi) for bi in range(bb)]
    chains = [(bi, h) for bi in range(bb) for h in range(H_G)]
    n = range(len(chains))
    qs = [pre[bi]['qkv'][:, sls[h]] for bi, h in chains]
    qs = [x * lax.rsqrt(jnp.sum(x * x, axis=-1, keepdims=True) + L2_EPS) * (D_G ** -0.5) for x in qs]
    ks = [pre[bi]['qkv'][:, W_GDN + h * D_G:W_GDN + (h + 1) * D_G] for bi, h in chains]
    ks = [x * lax.rsqrt(jnp.sum(x * x, axis=-1, keepdims=True) + L2_EPS) for x in ks]
    vs = [pre[bi]['qkv'][:, 2 * W_GDN + h * D_G:2 * W_GDN + (h + 1) * D_G] for bi, h in chains]
    betas = [pre[bi]['beta_blk'][:, h:h + 1] for bi, h in chains]
    gcols = [pre[bi]['gc_blk'][:, H_G + h:H_G + h + 1] for bi, h in chains]
    grows = [_sel_mm((lane == H_G + h).astype(BF16), pre[bi]['gc_blk'], NT) for bi, h in chains]
    g_last = [gcols[i][chunk - 1:chunk, :] for i in n]
    decay = [jnp.where(incl, jnp.exp(jnp.where(incl, gcols[i] - grows[i], 0.0)), 0.0) for i in n]
    k_beta = [ks[i] * betas[i] for i in n]
    cross = [_mm(jnp.concatenate([k_beta[i], qs[i]], axis=0), ks[i], NT) for i in n]
    lmat = [jnp.where(strict, cross[i][:chunk] * decay[i], 0.0) for i in n]
    attn = [jnp.where(incl, cross[i][chunk:] * decay[i], 0.0) for i in n]
    t_inv = _unit_lower_inverses([-x for x in lmat], eye, chunk)
    e_gc = [jnp.exp(gcols[i]) for i in n]
    uw = [_mm(t_inv[i], jnp.concatenate([vs[i] * betas[i], k_beta[i] * e_gc[i]], axis=1), NN) for i in n]
    ss = [s_scr[bi, h] for bi, h in chains]
    v_new = [uw[i][:, :D_G] - _mm(uw[i][:, D_G:], ss[i], NN) for i in n]
    os_ = [_mm(jnp.concatenate([qs[i] * e_gc[i], attn[i]], axis=1), jnp.concatenate([ss[i], v_new[i]], axis=0), NN)
           for i in n]
    for i, (bi, h) in enumerate(chains):
        s_scr[bi, h] = ss[i] * jnp.exp(g_last[i]) + _mm(ks[i] * jnp.exp(g_last[i] - gcols[i]), v_new[i], TN)
    for i, (bi, h) in enumerate(chains):
        o = os_[i]
        o = o * lax.rsqrt(jnp.mean(o * o, axis=-1, keepdims=True) + RMS_EPS) * nw_ref[...]
        z_h = pre[bi]['z'][:, sls[h]]
        y_ref[bi, :, sls[h]] = (o * (z_h * _sigmoid(z_h))).astype(y_ref.dtype)

    @pl.when(c == n_chunks - 1)
    def _fin():
        sout_ref[...] = s_scr[...]


def _gdn_mix(pg, conv_buf, s0, p, chunk, t_valid, bb):
    b, t, _ = pg.shape
    assert b % bb == 0 and t % chunk == 0
    n_chunks = t // chunk
    cbuf = jnp.pad(conv_buf, ((0, 0), (SUBLANES - (CONV_W - 1), 0), (0, 0)))
    const = lambda shape: pl.BlockSpec(shape, lambda i, j: (0,) * len(shape))
    return pl.pallas_call(
        functools.partial(_gdn_body, bb=bb, chunk=chunk, t_valid=t_valid, n_chunks=n_chunks),
        grid=(b // bb, n_chunks),
        in_specs=[
            pl.BlockSpec((bb, chunk, G_PROJ_PAD), lambda i, j: (i, j, 0)),
            pl.BlockSpec((bb, SUBLANES, GDN_CONV_DIM), lambda i, j: (i, 0, 0)),
            pl.BlockSpec((bb, H_G, D_G, D_G), lambda i, j: (i, 0, 0, 0)),
            const((CONV_W, GDN_CONV_DIM)), const((1, LANES)), const((1, LANES)), const((1, D_G)),
        ],
        out_specs=[
            pl.BlockSpec((bb, chunk, W_GDN), lambda i, j: (i, j, 0)),
            pl.BlockSpec((bb, H_G, D_G, D_G), lambda i, j: (i, 0, 0, 0)),
        ],
        out_shape=[jax.ShapeDtypeStruct((b, t, W_GDN), BF16), jax.ShapeDtypeStruct((b, H_G, D_G, D_G), F32)],
        scratch_shapes=[pltpu.VMEM((bb, H_G, D_G, D_G), F32),
                        pltpu.VMEM((bb, SUBLANES + chunk, GDN_CONV_DIM), F32)],
        compiler_params=pltpu.CompilerParams(dimension_semantics=("arbitrary", "arbitrary"),
                                             vmem_limit_bytes=VMEM_LIMIT),
        name="gdn_mix",
    )(pg, cbuf, s0, p['conv_w'], p['a_log'], p['dt_bias'], p['gdn_norm_w'])


def _out_q_body(x_ref, yr_ref, yg_ref, wor_ref, wog_ref, gn_ref, wq_ref, x1_ref, q_ref):
    x1 = x_ref[...] + _dot(yr_ref[...], wor_ref[...]) + _dot(yg_ref[...], wog_ref[...])
    x1_ref[...] = x1
    q_ref[...] = _dot(_rmsnorm(x1, gn_ref[...]).astype(BF16), wq_ref[...]).astype(q_ref.dtype)


def _out_q(x, yr, yg, p, tm):
    n, d = x.shape
    assert n % tm == 0
    row = lambda w: pl.BlockSpec((tm, w), lambda i: (i, 0))
    const = lambda shape: pl.BlockSpec(shape, lambda i: (0, 0))
    return pl.pallas_call(
        _out_q_body,
        grid=(n // tm,),
        in_specs=[row(d), row(W_RWKV), row(W_GDN), const((W_RWKV, d)), const((W_GDN, d)), const((1, d)),
                  const((d, d))],
        out_specs=[row(d), row(d)],
        out_shape=[jax.ShapeDtypeStruct((n, d), F32), jax.ShapeDtypeStruct((n, d), BF16)],
        compiler_params=pltpu.CompilerParams(dimension_semantics=("parallel",), vmem_limit_bytes=VMEM_LIMIT),
        name="out_q",
    )(x, yr, yg, p['w_out_r'], p['w_out_g'], p['norm_cross'], p['wq'])


def _attn_body(q_ref, x_ref, mk_ref, mv_ref, wo_ref, gn_ref, rw_ref, rb_ref, x2_ref, h_ref, ti_ref, gt_ref,
               *, bb, tq, head_rows):
    d_tiles = D_X // LANES
    mem_rows = N_MEM * d_tiles * H_X

    def head_mem(ref, i, hh):
        if not head_rows:
            return ref[i, :, hh * D_X:(hh + 1) * D_X]
        return jnp.concatenate(
            [ref[pl.ds(i * mem_rows + dt * H_X + hh, N_MEM, stride=d_tiles * H_X), :] for dt in range(d_tiles)],
            axis=-1)

    os_ = []
    for i in range(bb):
        q = q_ref[i]
        heads = []
        for hh in range(H_X):
            sl = slice(hh * D_X, (hh + 1) * D_X)
            s = _dot_nt(q[:, sl], head_mem(mk_ref, i, hh).astype(BF16)) * (D_X ** -0.5)
            e = jnp.exp(s - jnp.max(s, axis=-1, keepdims=True))
            prob = e / jnp.sum(e, axis=-1, keepdims=True)
            heads.append(_dot(prob.astype(BF16), head_mem(mv_ref, i, hh).astype(BF16)))
        os_.append(jnp.concatenate(heads, axis=-1))
    rows = bb * tq
    o = (os_[0] if bb == 1 else jnp.concatenate(os_, axis=0)).astype(BF16)
    x1 = x_ref[0] if bb == 1 else jnp.concatenate([x_ref[i] for i in range(bb)], axis=0)
    x2 = x1 + _dot(o, wo_ref[...])
    h = _rmsnorm(x2, gn_ref[...])
    for sub in range(SUBLANES):
        h_ref[pl.ds(sub, rows, stride=SUBLANES), :] = h[:, sub * LANES:(sub + 1) * LANES]
    h_hi, h_lo = _split2(h)
    pieces = _dot(jnp.concatenate([h_hi, h_lo], axis=0), rw_ref[...])
    logits = ((pieces[:rows, :LANES] + pieces[:rows, LANES:]) + (pieces[rows:, :LANES] + pieces[rows:, LANES:])
              + rb_ref[...])
    lane = lax.broadcasted_iota(jnp.int32, logits.shape, 1)
    vals, idxs = [], []
    for _ in range(TOP_K):
        m = jnp.max(logits, axis=-1, keepdims=True)
        first = jnp.min(jnp.where(logits == m, lane, LANES), axis=-1, keepdims=True)
        vals.append(m)
        idxs.append(first)
        logits = jnp.where(lane == first, -jnp.inf, logits)
    es = [jnp.exp(vv - vals[0]) for vv in vals]
    den = es[0] + es[1] + es[2] + es[3]
    ti = jnp.zeros(lane.shape, jnp.int32)
    gt = jnp.zeros(lane.shape, F32)
    for j in range(TOP_K):
        ti = jnp.where(lane == j, idxs[j], ti)
        gt = jnp.where(lane == j, es[j] / den, gt)
    for i in range(bb):
        x2_ref[i] = x2[i * tq:(i + 1) * tq]
        ti_ref[i] = ti[i * tq:(i + 1) * tq]
        gt_ref[i] = gt[i * tq:(i + 1) * tq]


def _attn_route(q, x1, mk, mv, p, bb, tq):
    b, t, d = x1.shape
    assert d == SUBLANES * LANES
    rows = bb * tq
    assert b % bb == 0 and t % tq == 0 and (bb == 1 or tq == t)
    n_tq = t // tq
    blk = lambda w: pl.BlockSpec((bb, tq, w), lambda i, j: (i, j, 0))
    head_rows = mk.ndim == 2
    mem = (pl.BlockSpec((bb * (mk.shape[0] // b), LANES), lambda i, j: (i, 0)) if head_rows
           else pl.BlockSpec((bb, N_MEM, d), lambda i, j: (i, 0, 0)))
    const = lambda shape: pl.BlockSpec(shape, lambda i, j: (0, 0))
    return pl.pallas_call(
        functools.partial(_attn_body, bb=bb, tq=tq, head_rows=head_rows),
        grid=(b // bb, n_tq),
        in_specs=[blk(d), blk(d), mem, mem, const((d, d)), const((1, d)), const((d, 2 * LANES)), const((1, LANES))],
        out_specs=[blk(d), pl.BlockSpec((rows * SUBLANES, LANES), lambda i, j: (i * n_tq + j, 0)), blk(LANES),
                   blk(LANES)],
        out_shape=[jax.ShapeDtypeStruct((b, t, d), F32), jax.ShapeDtypeStruct((b * t * SUBLANES, LANES), F32),
                   jax.ShapeDtypeStruct((b, t, LANES), jnp.int32), jax.ShapeDtypeStruct((b, t, LANES), F32)],
        compiler_params=pltpu.CompilerParams(dimension_semantics=("parallel", "parallel"),
                                             vmem_limit_bytes=VMEM_LIMIT),
        name="attn_route",
    )(q, x1, mk, mv, p['wo'], p['norm_ffn'], p['router_w'], p['router_b'])


def _moe_body(be_ref, gnext_ref, sprev_ref, h_hbm, w1_ref, b1_ref, w2_ref, b2_ref, out_hbm,
              xbuf, ybuf, w1b, w2b, gsem, ssem, *, tm, n_steps):
    k = pl.program_id(0)

    def row(ref, slot, at):
        return ref.at[slot, pl.ds(at, SUBLANES)]

    def gather_all(slot):
        return pltpu.make_async_copy(h_hbm.at[pl.ds(0, tm * SUBLANES)], xbuf.at[slot], gsem.at[slot])

    def scatter_all(slot):
        return pltpu.make_async_copy(ybuf.at[slot], out_hbm.at[pl.ds(0, tm * SUBLANES)], ssem.at[slot])

    @pl.when(k == 0)
    def _prologue():
        ybuf[...] = jnp.zeros(ybuf.shape, F32)
        for r in range(tm):
            pltpu.make_async_copy(h_hbm.at[pl.ds(0, SUBLANES)], row(xbuf, 0, r * SUBLANES), gsem.at[0]).start()

    @pl.when(jnp.logical_or(k == 0, be_ref[k] != be_ref[jnp.maximum(k - 1, 0)]))
    def _cast():
        w1b[...] = w1_ref[0].astype(BF16)
        w2b[...] = w2_ref[0].astype(BF16)

    def phase(cur):
        nxt = 1 - cur
        gather_all(cur).wait()

        @pl.when(k >= 1)
        def _():
            scatter_all(cur).wait()

        for r in range(tm):
            src = pl.multiple_of(gnext_ref[0, 0, r], SUBLANES)
            dst = pl.multiple_of(sprev_ref[0, 0, r], SUBLANES)
            pltpu.make_async_copy(h_hbm.at[pl.ds(src, SUBLANES)], row(xbuf, nxt, r * SUBLANES), gsem.at[nxt]).start()
            pltpu.make_async_copy(row(ybuf, nxt, r * SUBLANES), out_hbm.at[pl.ds(dst, SUBLANES)],
                                  ssem.at[nxt]).start()
        x = jnp.concatenate([xbuf[cur, pl.ds(sub, tm, stride=SUBLANES), :] for sub in range(SUBLANES)], axis=-1)
        hc = _dot(x.astype(BF16), w1b[...]) + b1_ref[0]
        hg = jnp.minimum(hc[:, :D_FF], SWIGLU_LIMIT)
        hl = jnp.clip(hc[:, D_FF:], -SWIGLU_LIMIT, SWIGLU_LIMIT)
        act = hg * _sigmoid(SWIGLU_ALPHA * hg) * (hl + 1.0)
        y = _dot(act.astype(BF16), w2b[...]) + b2_ref[0]
        for sub in range(SUBLANES):
            ybuf[cur, pl.ds(sub, tm, stride=SUBLANES), :] = y[:, sub * LANES:(sub + 1) * LANES]

        @pl.when(k == n_steps - 1)
        def _epilogue():
            gather_all(nxt).wait()
            scatter_all(nxt).wait()

    for parity in range(2):
        pl.when(lax.rem(k, 2) == parity)(functools.partial(phase, parity))


def _moe(h, gtok, sdst, block_e, p, tm):
    n, d = h.shape[0] // SUBLANES, D_MODEL
    n_steps = block_e.shape[0]
    grid_spec = pltpu.PrefetchScalarGridSpec(
        num_scalar_prefetch=1,
        grid=(n_steps,),
        in_specs=[
            pl.BlockSpec((1, 1, tm), lambda k, be: (jnp.minimum(k + 1, n_steps - 1), 0, 0),
                         memory_space=pltpu.SMEM),
            pl.BlockSpec((1, 1, tm), lambda k, be: (jnp.maximum(k - 1, 0), 0, 0), memory_space=pltpu.SMEM),
            pl.BlockSpec(memory_space=pl.ANY),
            pl.BlockSpec((1, d, 2 * D_FF), lambda k, be: (be[k], 0, 0)),
            pl.BlockSpec((1, 1, 2 * D_FF), lambda k, be: (be[k], 0, 0)),
            pl.BlockSpec((1, D_FF, d), lambda k, be: (be[k], 0, 0)),
            pl.BlockSpec((1, 1, d), lambda k, be: (be[k], 0, 0)),
        ],
        out_specs=pl.BlockSpec(memory_space=pl.ANY),
        scratch_shapes=[
            pltpu.VMEM((2, tm * SUBLANES, LANES), F32), pltpu.VMEM((2, tm * SUBLANES, LANES), F32),
            pltpu.VMEM((d, 2 * D_FF), BF16), pltpu.VMEM((D_FF, d), BF16),
            pltpu.SemaphoreType.DMA((2,)), pltpu.SemaphoreType.DMA((2,)),
        ],
    )
    return pl.pallas_call(
        functools.partial(_moe_body, tm=tm, n_steps=n_steps),
        grid_spec=grid_spec,
        out_shape=jax.ShapeDtypeStruct(((n * TOP_K + tm) * SUBLANES, LANES), F32),
        compiler_params=pltpu.CompilerParams(dimension_semantics=("arbitrary",), vmem_limit_bytes=VMEM_LIMIT),
        name="moe_experts",
    )(block_e, gtok.reshape(n_steps, 1, tm), sdst.reshape(n_steps, 1, tm), h, p['w1_e'], p['b1_e'], p['w2_e'],
      p['b2_e'])


def _route_plan(top_i, tm):
    n = top_i.shape[0]
    na = n * TOP_K
    flat_e = top_i.reshape(na)
    order = jnp.argsort(flat_e).astype(jnp.int32)
    counts = jnp.sum((flat_e[:, None] == jnp.arange(N_EXPERTS, dtype=jnp.int32)[None, :]).astype(jnp.int32), axis=0)
    padded = (counts + tm - 1) // tm * tm
    starts = jnp.cumsum(counts) - counts
    pends = jnp.cumsum(padded)
    pstarts = pends - padded
    n_steps = -(-(na + N_EXPERTS * (tm - 1)) // tm) + 2
    blk_start = (jnp.arange(n_steps, dtype=jnp.int32) - 1) * tm
    block_e = jnp.sum((blk_start[:, None] >= pends[None, :]).astype(jnp.int32), axis=1)
    block_e = jnp.minimum(block_e, N_EXPERTS - 1)
    block_e = jnp.where(blk_start < 0, block_e[1], block_e)
    is_e = block_e[:, None] == jnp.arange(N_EXPERTS, dtype=jnp.int32)[None, :]
    pick = lambda table: jnp.sum(jnp.where(is_e, table[None, :], 0), axis=1)
    lane = jnp.arange(tm, dtype=jnp.int32)[None, :]
    local = blk_start[:, None] + lane - pick(pstarts)[:, None]
    valid = (local >= 0) & (local < pick(counts)[:, None]) & (blk_start[:, None] >= 0)
    asg = order[jnp.clip(pick(starts)[:, None] + local, 0, na - 1)]
    gtok = jnp.where(valid, asg // TOP_K, 0)
    sdst = jnp.where(valid, (asg % TOP_K) * n + asg // TOP_K, TOP_K * n + lane)
    return gtok * SUBLANES, sdst * SUBLANES, block_e


def _combine_body(x_ref, gt_ref, gn_ref, *refs, final):
    slot_refs, y_ref = refs[:TOP_K], refs[TOP_K]
    x = x_ref[...]
    gt = gt_ref[...]
    tm = x.shape[0]
    for j in range(TOP_K):
        slot = jnp.concatenate([slot_refs[j][pl.ds(sub, tm, stride=SUBLANES), :] for sub in range(SUBLANES)],
                               axis=-1)
        x = x + gt[:, j:j + 1] * slot
    y_ref[...] = _rmsnorm(x, gn_ref[...]) if final else x


def _combine(x2, gates, slots, gn, tm, n_all, row0, final):
    n, d = x2.shape
    assert n % tm == 0 and row0 % tm == 0 and n_all % tm == 0
    slot_spec = lambda j: pl.BlockSpec((tm * SUBLANES, LANES), lambda i: ((j * n_all + row0) // tm + i, 0))
    return pl.pallas_call(
        functools.partial(_combine_body, final=final),
        grid=(n // tm,),
        in_specs=[pl.BlockSpec((tm, d), lambda i: (i, 0)), pl.BlockSpec((tm, LANES), lambda i: (i, 0)),
                  pl.BlockSpec((1, d), lambda i: (0, 0))] + [slot_spec(j) for j in range(TOP_K)],
        out_specs=pl.BlockSpec((tm, d), lambda i: (i, 0)),
        out_shape=jax.ShapeDtypeStruct((n, d), F32),
        compiler_params=pltpu.CompilerParams(dimension_semantics=("parallel",), vmem_limit_bytes=VMEM_LIMIT),
        name="combine",
    )(x2, gates, gn, *([slots] * TOP_K))


def _layer_params(l, norm_mix, w_in, mu_shift, w0, w2_decay, a0, a2_iclr, g2_gate, k_k, k_a, r_k, lnx_w, lnx_b,
                  conv_w, a_log, dt_bias, gdn_norm_w, w_out, norm_cross, norm_mem, wq_x, wk_x, wv_x, wo_x,
                  norm_ffn, router_w, router_b, w1_e, b1_e, w2_e, b2_e):
    row = lambda z: z.reshape(1, -1).astype(F32)
    lane_pad = lambda z, at: jnp.zeros((1, LANES), F32).at[0, at:at + z.shape[0]].set(z)
    return {
        'norm_mix': row(norm_mix[l]),
        'w_in_r': w_in[l][:, :R_PROJ].astype(BF16),
        'w_in_g': jnp.pad(w_in[l][:, R_PROJ:], ((0, 0), (0, G_PROJ_PAD - G_PROJ))).astype(BF16),
        'mu': row(mu_shift[l]), 'w0': row(w0[l]), 'a0': row(a0[l]),
        'w2': jnp.pad(w2_decay[l], ((0, LORA_A), (0, 0))).astype(BF16),
        'a2': jnp.pad(a2_iclr[l], ((LORA_W, 0), (0, 0))).astype(BF16),
        'g2': g2_gate[l].astype(BF16),
        'k_k': row(k_k[l]), 'k_a': row(k_a[l]), 'r_k': row(r_k[l]), 'lnx_w': row(lnx_w[l]), 'lnx_b': row(lnx_b[l]),
        'conv_w': conv_w[l].astype(F32),
        'a_log': lane_pad(a_log[l], H_G), 'dt_bias': lane_pad(dt_bias[l], H_G),
        'gdn_norm_w': row(gdn_norm_w[l]),
        'w_out_r': w_out[l][:W_RWKV].astype(BF16), 'w_out_g': w_out[l][W_RWKV:].astype(BF16),
        'norm_cross': row(norm_cross[l]), 'norm_mem': row(norm_mem[l]),
        'wq': wq_x[l].astype(BF16), 'wk': wk_x[l].astype(BF16), 'wv': wv_x[l].astype(BF16),
        'wo': wo_x[l].astype(BF16),
        'norm_ffn': row(norm_ffn[l]),
        'router_w': jnp.concatenate(_split2(jnp.pad(router_w[l].astype(F32), ((0, 0), (0, LANES - N_EXPERTS)))),
                                    axis=1),
        'router_b': jnp.full((1, LANES), NEG_BIG, F32).at[0, :N_EXPERTS].set(router_b[l].astype(F32)),
        'w1_e': w1_e[l], 'b1_e': b1_e[l][:, None, :], 'w2_e': w2_e[l], 'b2_e': b2_e[l][:, None, :],
    }


def _mix_and_attend(x, mk, mv, shift_prev, s_r, conv_buf, s_g, p, *, chunk, rwkv_bb, gdn_bb, tm, bb, tq):
    b, t, d = x.shape
    assert t >= CONV_W - 1
    pr, pg = _norm_proj(x.reshape(b * t, d), p['norm_mix'], [p['w_in_r'], p['w_in_g']], [F32, F32], tm)
    pr = pr.reshape(b, t, R_PROJ)
    pg = pg.reshape(b, t, G_PROJ_PAD)
    shift_new = pr[:, t - 1]
    conv_new = pg[:, t - (CONV_W - 1):, :GDN_CONV_DIM]
    t_pad = -(-t // chunk) * chunk
    if t_pad != t:
        pr = jnp.pad(pr, ((0, 0), (0, t_pad - t), (0, 0)))
        pg = jnp.pad(pg, ((0, 0), (0, t_pad - t), (0, 0)))
    t_valid = chunk if t_pad == t else t
    y_r, s_r_new = _rwkv_mix(pr, shift_prev, s_r, p, chunk, t_valid, rwkv_bb)
    y_g, s_g_new = _gdn_mix(pg, conv_buf, s_g, p, chunk, t_valid, gdn_bb)
    if t_pad != t:
        y_r, y_g = y_r[:, :t], y_g[:, :t]
    x1, q = _out_q(x.reshape(b * t, d), y_r.reshape(b * t, W_RWKV), y_g.reshape(b * t, W_GDN), p, tm)
    x2, h, top_i, gates = _attn_route(q.reshape(b, t, d), x1.reshape(b, t, d), mk, mv, p, bb, tq)
    return (x2.reshape(b * t, d), h, top_i.reshape(b * t, LANES), gates.reshape(b * t, LANES),
            shift_new, s_r_new, conv_new, s_g_new)


def kernel(x_prompt, x_sample, mem_prompt, state_rwkv, state_rwkv_shift, state_gdn, state_gdn_conv, cache_mem_k, cache_mem_v, norm_mix, w_in, mu_shift, w0, w2_decay, a0, a2_iclr, g2_gate, k_k, k_a, r_k, lnx_w, lnx_b, conv_w, a_log, dt_bias, gdn_norm_w, w_out, norm_cross, norm_mem, wq_x, wk_x, wv_x, wo_x, norm_ffn, router_w, router_b, w1_e, b1_e, w2_e, b2_e, final_norm):
    bp, tp, d = x_prompt.shape
    bs, ts, _ = x_sample.shape
    depth = w_in.shape[0]
    np_, ns = bp * tp, bs * ts
    xp, xs = x_prompt, x_sample
    outs = [[] for _ in range(10)]
    for l in range(depth):
        p = _layer_params(l, norm_mix, w_in, mu_shift, w0, w2_decay, a0, a2_iclr, g2_gate, k_k, k_a, r_k, lnx_w,
                          lnx_b, conv_w, a_log, dt_bias, gdn_norm_w, w_out, norm_cross, norm_mem, wq_x, wk_x, wv_x,
                          wo_x, norm_ffn, router_w, router_b, w1_e, b1_e, w2_e, b2_e)
        n_mem = mem_prompt.shape[1]
        mk, mv = _norm_proj(mem_prompt.reshape(bp * n_mem, d), p['norm_mem'], [p['wk'], p['wv']], [F32, F32], 256)
        mk, mv = mk.reshape(bp, n_mem, d), mv.reshape(bp, n_mem, d)
        res_p = _mix_and_attend(
            xp, mk, mv, jnp.zeros((bp, R_PROJ), F32), jnp.zeros((bp, H_R, N_R, N_R), F32),
            jnp.zeros((bp, CONV_W - 1, GDN_CONV_DIM), F32), jnp.zeros((bp, H_G, D_G, D_G), F32), p,
            chunk=MIX_CHUNK, rwkv_bb=4, gdn_bb=8, tm=512, bb=1, tq=1024)
        head_rows = lambda c: c.reshape(bs, n_mem, H_X, D_X // LANES, LANES).transpose(0, 1, 3, 2, 4).reshape(-1, LANES)
        mk_s, mv_s = head_rows(cache_mem_k[l]), head_rows(cache_mem_v[l])
        res_s = _mix_and_attend(
            xs, mk_s, mv_s, state_rwkv_shift[l], state_rwkv[l], state_gdn_conv[l],
            state_gdn[l], p, chunk=SUBLANES, rwkv_bb=8, gdn_bb=8, tm=256, bb=8, tq=ts)
        h = jnp.concatenate([res_p[1], res_s[1]], axis=0)
        top_i = jnp.concatenate([res_p[2], res_s[2]], axis=0)[:, :TOP_K]
        gtok, sdst, block_e = _route_plan(top_i, MOE_ROWS)
        slots = _moe(h, gtok, sdst, block_e, p, MOE_ROWS)
        gn = final_norm.reshape(1, d).astype(F32)
        last = l == depth - 1
        xp = _combine(res_p[0], res_p[3], slots, gn, 512, np_ + ns, 0, last).reshape(bp, tp, d)
        xs = _combine(res_s[0], res_s[3], slots, gn, 512, np_ + ns, np_, last).reshape(bs, ts, d)
        new = [res_p[5], res_p[4], res_p[7], res_p[6], mk.reshape(bp, n_mem, H_X, D_X),
               mv.reshape(bp, n_mem, H_X, D_X), res_s[5], res_s[4], res_s[7], res_s[6]]
        for acc, val in zip(outs, new):
            acc.append(val)
    return (xp, xs) + tuple(jnp.stack(o) for o in outs)
```

```python
import functools

import jax
import jax.numpy as jnp
from jax import lax
from jax.experimental import pallas as pl
from jax.experimental.pallas import tpu as pltpu

F32 = jnp.float32
BF16 = jnp.bfloat16
DEFAULT = lax.Precision.DEFAULT

D_MODEL = 1024
W_RWKV = 512
N_R = 64
H_R = W_RWKV // N_R
LORA_W = 64
LORA_A = 64
LORA_G = 128
R_PROJ = 3 * W_RWKV + LORA_W + LORA_A + LORA_G
GN_EPS = 64e-5
W_GDN = 512
D_G = 128
H_G = W_GDN // D_G
GDN_CONV_DIM = 3 * W_GDN
CONV_W = 4
G_PROJ = GDN_CONV_DIM + W_GDN + 2 * H_G
LANES = 128
SUBLANES = 8
G_PROJ_PAD = GDN_CONV_DIM + W_GDN + LANES
MIX_CHUNK = 64
N_MEM = 256
H_X = 4
D_X = D_MODEL // H_X
N_EXPERTS = 32
TOP_K = 4
D_FF = D_MODEL
SWIGLU_LIMIT = 7.0
SWIGLU_ALPHA = 1.702
MOE_ROWS = 256
RMS_EPS = 1e-6
L2_EPS = 1e-6
NEG_BIG = -1e30
VMEM_LIMIT = 56 * 1024 * 1024


def _dot(a, b, precision=DEFAULT):
    return jnp.dot(a, b, preferred_element_type=F32, precision=precision)


def _dot_nt(a, b, precision=DEFAULT):
    return lax.dot_general(a, b, (((1,), (1,)), ((), ())), preferred_element_type=F32, precision=precision)


NN = (((1,), (0,)), ((), ()))
NT = (((1,), (1,)), ((), ()))
TN = (((0,), (0,)), ((), ()))


def _split2(x):
    hi = x.astype(BF16)
    return hi, (x - hi.astype(F32)).astype(BF16)


def _mm(a, b, dims):
    return lax.dot_general(a.astype(BF16), b.astype(BF16), dims, preferred_element_type=F32)


def _sel_mm(sel, x, dims):
    dg = lambda y: lax.dot_general(sel, y, dims, preferred_element_type=F32)
    hi = x.astype(BF16)
    r1 = x - hi.astype(F32)
    mid = r1.astype(BF16)
    lo = (r1 - mid.astype(F32)).astype(BF16)
    return dg(hi) + (dg(mid) + dg(lo))


def _sigmoid(x):
    return 1.0 / (1.0 + jnp.exp(-x))


def _softplus(x):
    return jnp.maximum(x, 0.0) + jnp.log(1.0 + jnp.exp(-jnp.abs(x)))


def _rmsnorm(x, g):
    return x * lax.rsqrt(jnp.mean(x * x, axis=-1, keepdims=True) + RMS_EPS) * g


def _tri_masks(c):
    row = lax.broadcasted_iota(jnp.int32, (c, c), 0)
    col = lax.broadcasted_iota(jnp.int32, (c, c), 1)
    return col <= row, col < row, (col == row).astype(F32)


def _unit_lower_inverses(ms, eye, c):
    ts = [eye + m for m in ms]
    ps = list(ms)
    covered = 2
    while covered < c:
        ps = [_mm(p, p, NN) for p in ps]
        ts = [t + _mm(t, p, NN) for t, p in zip(ts, ps)]
        covered *= 2
    return ts


def _norm_proj_body(x_ref, g_ref, *refs, n_out):
    w_refs, o_refs = refs[:n_out], refs[n_out:]
    hb = _rmsnorm(x_ref[...], g_ref[...]).astype(BF16)
    for w_ref, o_ref in zip(w_refs, o_refs):
        o_ref[...] = _dot(hb, w_ref[...]).astype(o_ref.dtype)


def _norm_proj(x, g, ws, out_dtypes, tm):
    n, d = x.shape
    assert n % tm == 0
    in_specs = [pl.BlockSpec((tm, d), lambda i: (i, 0)), pl.BlockSpec((1, d), lambda i: (0, 0))]
    in_specs += [pl.BlockSpec(w.shape, lambda i: (0, 0)) for w in ws]
    return pl.pallas_call(
        functools.partial(_norm_proj_body, n_out=len(ws)),
        grid=(n // tm,),
        in_specs=in_specs,
        out_specs=[pl.BlockSpec((tm, w.shape[1]), lambda i: (i, 0)) for w in ws],
        out_shape=[jax.ShapeDtypeStruct((n, w.shape[1]), dt) for w, dt in zip(ws, out_dtypes)],
        compiler_params=pltpu.CompilerParams(dimension_semantics=("parallel",), vmem_limit_bytes=VMEM_LIMIT),
        name="norm_proj",
    )(x, g, *ws)


def _rwkv_body(pr_ref, shift_ref, s0_ref, mu_ref, w0_ref, w2_ref, a0_ref, a2_ref, g2_ref, kk_ref, ka_ref,
               rk_ref, lnw_ref, lnb_ref, y_ref, sout_ref, s_scr, prev_scr, *, bb, chunk, t_valid, n_chunks):
    c = pl.program_id(1)

    @pl.when(c == 0)
    def _init():
        s_scr[...] = s0_ref[...]
        prev_scr[...] = shift_ref[...]

    incl, strict, eye = _tri_masks(chunk)
    incl_bf = incl.astype(BF16)
    rows = lax.broadcasted_iota(jnp.int32, (chunk, 1), 0)
    sls = [slice(h * N_R, (h + 1) * N_R) for h in range(H_R)]

    def per_batch(bi):
        pr = pr_ref[bi]
        prev = jnp.where(rows == 0, prev_scr[bi], pltpu.roll(pr, 1, 0))
        prev_scr[bi] = pr[chunk - 1:chunk, :]
        xm = pr + (prev - pr) * mu_ref[...]
        r = xm[:, :W_RWKV]
        k = xm[:, W_RWKV:2 * W_RWKV]
        v = xm[:, 2 * W_RWKV:3 * W_RWKV]
        lo = xm[:, 3 * W_RWKV:3 * W_RWKV + LORA_W + LORA_A]
        g_lo = xm[:, 3 * W_RWKV + LORA_W + LORA_A:]
        logw = -_softplus(-(w0_ref[...] + _dot(jnp.tanh(lo).astype(BF16), w2_ref[...]))) - 0.5
        wl = -jnp.exp(logw)
        a = _sigmoid(a0_ref[...] + _dot(lo.astype(BF16), a2_ref[...]))
        g = _dot(_sigmoid(g_lo).astype(BF16), g2_ref[...])
        kkv = k * kk_ref[...]
        k = k * (1.0 + (a - 1.0) * ka_ref[...])
        if t_valid < chunk:
            valid = rows < t_valid
            wl = jnp.where(valid, wl, 0.0)
            kkv = jnp.where(valid, kkv, 0.0)
            k = jnp.where(valid, k, 0.0)
            v = jnp.where(valid, v, 0.0)
        cum = _sel_mm(incl_bf, wl, NN)
        cum_last = cum[chunk - 1:chunk, :]
        return dict(r=r, k=k, v=v, a=a, g=g, kkv=kkv, w_incl=jnp.exp(cum), w_prev=jnp.exp(cum - wl),
                    w_inv=jnp.exp(-cum), w_tail=jnp.exp(cum_last - cum), w_last=jnp.exp(cum_last))

    pre = [per_batch(bi) for bi in range(bb)]
    chains = [(bi, h) for bi in range(bb) for h in range(H_R)]
    col = lambda name: [pre[bi][name][:, sls[h]] for bi, h in chains]
    n = range(len(chains))
    r_, k_, v_, a_ = col('r'), col('k'), col('v'), col('a')
    w_incl, w_prev, w_inv, w_tail, w_last = col('w_incl'), col('w_prev'), col('w_inv'), col('w_tail'), col('w_last')
    kks = [x * lax.rsqrt(jnp.sum(x * x, axis=-1, keepdims=True) + L2_EPS) for x in col('kkv')]
    a_hat = [-(kks[i] * w_prev[i]) for i in n]
    kka = [kks[i] * a_[i] for i in n]
    b_hat = [kka[i] * w_inv[i] for i in n]
    k_hat = [k_[i] * w_inv[i] for i in n]
    r_hat = [r_[i] * w_incl[i] for i in n]
    cross = [_mm(jnp.concatenate([a_hat[i], r_hat[i]], axis=0), jnp.concatenate([b_hat[i], k_hat[i]], axis=0), NT)
             for i in n]
    row2 = lax.broadcasted_iota(jnp.int32, (chunk, 2 * chunk), 0)
    col2 = lax.broadcasted_iota(jnp.int32, (chunk, 2 * chunk), 1)
    col2 = jnp.where(col2 >= chunk, col2 - chunk, col2)
    strict2, incl2 = col2 < row2, col2 <= row2
    m_top = [jnp.where(strict2, x[:chunk], 0.0) for x in cross]
    a_bot = [jnp.where(incl2, x[chunk:], 0.0) for x in cross]
    t_inv = _unit_lower_inverses([x[:, :chunk] for x in m_top], eye, chunk)
    w_hat = [_mm(t_inv[i], a_hat[i], NN) for i in n]
    mv = [_mm(m_top[i], jnp.concatenate([jnp.zeros_like(v_[i]), v_[i]], axis=0), NN) for i in n]
    u = [_mm(t_inv[i], mv[i], NN) for i in n]
    ss = [s_scr[bi, h] for bi, h in chains]
    p = [_mm(w_hat[i], ss[i], NT) + u[i] for i in n]
    pv = [jnp.concatenate([p[i], v_[i]], axis=0) for i in n]
    ys = [_mm(r_hat[i], ss[i], NT) + _mm(a_bot[i], pv[i], NN) for i in n]
    for i, (bi, h) in enumerate(chains):
        tails = jnp.concatenate([kka[i] * w_tail[i], k_[i] * w_tail[i]], axis=0)
        s_scr[bi, h] = ss[i] * w_last[i] + _mm(pv[i], tails, TN)
    for i, (bi, h) in enumerate(chains):
        sl = sls[h]
        y = ys[i]
        mean = jnp.mean(y, axis=-1, keepdims=True)
        yc = y - mean
        var = jnp.mean(yc * yc, axis=-1, keepdims=True)
        yn = yc * lax.rsqrt(var + GN_EPS) * lnw_ref[:, sl] + lnb_ref[:, sl]
        bonus = jnp.sum(r_[i] * k_[i] * rk_ref[:, sl], axis=-1, keepdims=True) * v_[i]
        y_ref[bi, :, sl] = ((yn + bonus) * pre[bi]['g'][:, sl]).astype(y_ref.dtype)

    @pl.when(c == n_chunks - 1)
    def _fin():
        sout_ref[...] = s_scr[...]


def _rwkv_mix(pr, shift_prev, s0, p, chunk, t_valid, bb):
    b, t, _ = pr.shape
    assert b % bb == 0 and t % chunk == 0
    n_chunks = t // chunk
    const = lambda shape: pl.BlockSpec(shape, lambda i, j: (0,) * len(shape))
    return pl.pallas_call(
        functools.partial(_rwkv_body, bb=bb, chunk=chunk, t_valid=t_valid, n_chunks=n_chunks),
        grid=(b // bb, n_chunks),
        in_specs=[
            pl.BlockSpec((bb, chunk, R_PROJ), lambda i, j: (i, j, 0)),
            pl.BlockSpec((bb, 1, R_PROJ), lambda i, j: (i, 0, 0)),
            pl.BlockSpec((bb, H_R, N_R, N_R), lambda i, j: (i, 0, 0, 0)),
            const((1, R_PROJ)), const((1, W_RWKV)), const((LANES, W_RWKV)), const((1, W_RWKV)),
            const((LANES, W_RWKV)), const((LORA_G, W_RWKV)), const((1, W_RWKV)), const((1, W_RWKV)),
            const((1, W_RWKV)), const((1, W_RWKV)), const((1, W_RWKV)),
        ],
        out_specs=[
            pl.BlockSpec((bb, chunk, W_RWKV), lambda i, j: (i, j, 0)),
            pl.BlockSpec((bb, H_R, N_R, N_R), lambda i, j: (i, 0, 0, 0)),
        ],
        out_shape=[jax.ShapeDtypeStruct((b, t, W_RWKV), BF16), jax.ShapeDtypeStruct((b, H_R, N_R, N_R), F32)],
        scratch_shapes=[pltpu.VMEM((bb, H_R, N_R, N_R), F32), pltpu.VMEM((bb, 1, R_PROJ), F32)],
        compiler_params=pltpu.CompilerParams(dimension_semantics=("arbitrary", "arbitrary"),
                                             vmem_limit_bytes=VMEM_LIMIT),
        name="rwkv_mix",
    )(pr, shift_prev[:, None, :], s0, p['mu'], p['w0'], p['w2'], p['a0'], p['a2'], p['g2'], p['k_k'], p['k_a'],
      p['r_k'], p['lnx_w'], p['lnx_b'])


def _gdn_body(pg_ref, cbuf_ref, s0_ref, cw_ref, alog_ref, dtb_ref, nw_ref, y_ref, sout_ref, s_scr, xp_scr,
              *, bb, chunk, t_valid, n_chunks):
    c = pl.program_id(1)

    @pl.when(c == 0)
    def _init():
        s_scr[...] = s0_ref[...]
        xp_scr[:, 0:SUBLANES, :] = cbuf_ref[...]

    incl, strict, eye = _tri_masks(chunk)
    incl_bf = incl.astype(BF16)
    lane = lax.broadcasted_iota(jnp.int32, (chunk, LANES), 1)
    sls = [slice(h * D_G, (h + 1) * D_G) for h in range(H_G)]

    def per_batch(bi):
        xp_scr[bi, SUBLANES:SUBLANES + chunk, :] = pg_ref[bi, :, :GDN_CONV_DIM]
        base = SUBLANES - (CONV_W - 1)
        conv = xp_scr[bi, base:base + chunk, :] * cw_ref[0:1, :]
        for j in range(1, CONV_W):
            conv = conv + xp_scr[bi, base + j:base + j + chunk, :] * cw_ref[j:j + 1, :]
        xp_scr[bi, 0:SUBLANES, :] = xp_scr[bi, chunk:chunk + SUBLANES, :]
        qkv = conv * _sigmoid(conv)
        z = pg_ref[bi, :, GDN_CONV_DIM:GDN_CONV_DIM + W_GDN]
        ba = pg_ref[bi, :, GDN_CONV_DIM + W_GDN:]
        beta_blk = _sigmoid(ba)
        g_blk = -jnp.exp(alog_ref[...]) * _softplus(ba + dtb_ref[...])
        if t_valid < chunk:
            valid = lax.broadcasted_iota(jnp.int32, (chunk, 1), 0) < t_valid
            beta_blk = jnp.where(valid, beta_blk, 0.0)
            g_blk = jnp.where(valid, g_blk, 0.0)
        gc_blk = _sel_mm(incl_bf, g_blk, NN)
        return dict(qkv=qkv, z=z, beta_blk=beta_blk, gc_blk=gc_blk)

    pre = [per_batch(bi) for bi in range(bb)]
    chains = [(bi, h) for bi in range(bb) for h in range(H_G)]
    n = range(len(chains))
    qs = [pre[bi]['qkv'][:, sls[h]] for bi, h in chains]
    qs = [x * lax.rsqrt(jnp.sum(x * x, axis=-1, keepdims=True) + L2_EPS) * (D_G ** -0.5) for x in qs]
    ks = [pre[bi]['qkv'][:, W_GDN + h * D_G:W_GDN + (h + 1) * D_G] for bi, h in chains]
    ks = [x * lax.rsqrt(jnp.sum(x * x, axis=-1, keepdims=True) + L2_EPS) for x in ks]
    vs = [pre[bi]['qkv'][:, 2 * W_GDN + h * D_G:2 * W_GDN + (h + 1) * D_G] for bi, h in chains]
    betas = [pre[bi]['beta_blk'][:, h:h + 1] for bi, h in chains]
    gcols = [pre[bi]['gc_blk'][:, H_G + h:H_G + h + 1] for bi, h in chains]
    grows = [_sel_mm((lane == H_G + h).astype(BF16), pre[bi]['gc_blk'], NT) for bi, h in chains]
    g_last = [gcols[i][chunk - 1:chunk, :] for i in n]
    decay = [jnp.where(incl, jnp.exp(jnp.where(incl, gcols[i] - grows[i], 0.0)), 0.0) for i in n]
    k_beta = [ks[i] * betas[i] for i in n]
    cross = [_mm(jnp.concatenate([k_beta[i], qs[i]], axis=0), ks[i], NT) for i in n]
    lmat = [jnp.where(strict, cross[i][:chunk] * decay[i], 0.0) for i in n]
    attn = [jnp.where(incl, cross[i][chunk:] * decay[i], 0.0) for i in n]
    t_inv = _unit_lower_inverses([-x for x in lmat], eye, chunk)
    e_gc = [jnp.exp(gcols[i]) for i in n]
    uw = [_mm(t_inv[i], jnp.concatenate([vs[i] * betas[i], k_beta[i] * e_gc[i]], axis=1), NN) for i in n]
    ss = [s_scr[bi, h] for bi, h in chains]
    v_new = [uw[i][:, :D_G] - _mm(uw[i][:, D_G:], ss[i], NN) for i in n]
    os_ = [_mm(jnp.concatenate([qs[i] * e_gc[i], attn[i]], axis=1), jnp.concatenate([ss[i], v_new[i]], axis=0), NN)
           for i in n]
    for i, (bi, h) in enumerate(chains):
        s_scr[bi, h] = ss[i] * jnp.exp(g_last[i]) + _mm(ks[i] * jnp.exp(g_last[i] - gcols[i]), v_new[i], TN)
    for i, (bi, h) in enumerate(chains):
        o = os_[i]
        o = o * lax.rsqrt(jnp.mean(o * o, axis=-1, keepdims=True) + RMS_EPS) * nw_ref[...]
        z_h = pre[bi]['z'][:, sls[h]]
        y_ref[bi, :, sls[h]] = (o * (z_h * _sigmoid(z_h))).astype(y_ref.dtype)

    @pl.when(c == n_chunks - 1)
    def _fin():
        sout_ref[...] = s_scr[...]


def _gdn_mix(pg, conv_buf, s0, p, chunk, t_valid, bb):
    b, t, _ = pg.shape
    assert b % bb == 0 and t % chunk == 0
    n_chunks = t // chunk
    cbuf = jnp.pad(conv_buf, ((0, 0), (SUBLANES - (CONV_W - 1), 0), (0, 0)))
    const = lambda shape: pl.BlockSpec(shape, lambda i, j: (0,) * len(shape))
    return pl.pallas_call(
        functools.partial(_gdn_body, bb=bb, chunk=chunk, t_valid=t_valid, n_chunks=n_chunks),
        grid=(b // bb, n_chunks),
        in_specs=[
            pl.BlockSpec((bb, chunk, G_PROJ_PAD), lambda i, j: (i, j, 0)),
            pl.BlockSpec((bb, SUBLANES, GDN_CONV_DIM), lambda i, j: (i, 0, 0)),
            pl.BlockSpec((bb, H_G, D_G, D_G), lambda i, j: (i, 0, 0, 0)),
            const((CONV_W, GDN_CONV_DIM)), const((1, LANES)), const((1, LANES)), const((1, D_G)),
        ],
        out_specs=[
            pl.BlockSpec((bb, chunk, W_GDN), lambda i, j: (i, j, 0)),
            pl.BlockSpec((bb, H_G, D_G, D_G), lambda i, j: (i, 0, 0, 0)),
        ],
        out_shape=[jax.ShapeDtypeStruct((b, t, W_GDN), BF16), jax.ShapeDtypeStruct((b, H_G, D_G, D_G), F32)],
        scratch_shapes=[pltpu.VMEM((bb, H_G, D_G, D_G), F32),
                        pltpu.VMEM((bb, SUBLANES + chunk, GDN_CONV_DIM), F32)],
        compiler_params=pltpu.CompilerParams(dimension_semantics=("arbitrary", "arbitrary"),
                                             vmem_limit_bytes=VMEM_LIMIT),
        name="gdn_mix",
    )(pg, cbuf, s0, p['conv_w'], p['a_log'], p['dt_bias'], p['gdn_norm_w'])


def _out_q_body(x_ref, yr_ref, yg_ref, wor_ref, wog_ref, gn_ref, wq_ref, x1_ref, q_ref):
    x1 = x_ref[...] + _dot(yr_ref[...], wor_ref[...]) + _dot(yg_ref[...], wog_ref[...])
    x1_ref[...] = x1
    q_ref[...] = _dot(_rmsnorm(x1, gn_ref[...]).astype(BF16), wq_ref[...]).astype(q_ref.dtype)


def _out_q(x, yr, yg, p, tm):
    n, d = x.shape
    assert n % tm == 0
    row = lambda w: pl.BlockSpec((tm, w), lambda i: (i, 0))
    const = lambda shape: pl.BlockSpec(shape, lambda i: (0, 0))
    return pl.pallas_call(
        _out_q_body,
        grid=(n // tm,),
        in_specs=[row(d), row(W_RWKV), row(W_GDN), const((W_RWKV, d)), const((W_GDN, d)), const((1, d)),
                  const((d, d))],
        out_specs=[row(d), row(d)],
        out_shape=[jax.ShapeDtypeStruct((n, d), F32), jax.ShapeDtypeStruct((n, d), BF16)],
        compiler_params=pltpu.CompilerParams(dimension_semantics=("parallel",), vmem_limit_bytes=VMEM_LIMIT),
        name="out_q",
    )(x, yr, yg, p['w_out_r'], p['w_out_g'], p['norm_cross'], p['wq'])


def _attn_body(q_ref, x_ref, mk_ref, mv_ref, wo_ref, gn_ref, rw_ref, rb_ref, x2_ref, h_ref, ti_ref, gt_ref,
               *, bb, tq, head_rows):
    d_tiles = D_X // LANES
    mem_rows = N_MEM * d_tiles * H_X

    def head_mem(ref, i, hh):
        if not head_rows:
            return ref[i, :, hh * D_X:(hh + 1) * D_X]
        return jnp.concatenate(
            [ref[pl.ds(i * mem_rows + dt * H_X + hh, N_MEM, stride=d_tiles * H_X), :] for dt in range(d_tiles)],
            axis=-1)

    os_ = []
    for i in range(bb):
        q = q_ref[i]
        heads = []
        for hh in range(H_X):
            sl = slice(hh * D_X, (hh + 1) * D_X)
            s = _dot_nt(q[:, sl], head_mem(mk_ref, i, hh).astype(BF16)) * (D_X ** -0.5)
            e = jnp.exp(s - jnp.max(s, axis=-1, keepdims=True))
            prob = e / jnp.sum(e, axis=-1, keepdims=True)
            heads.append(_dot(prob.astype(BF16), head_mem(mv_ref, i, hh).astype(BF16)))
        os_.append(jnp.concatenate(heads, axis=-1))
    rows = bb * tq
    o = (os_[0] if bb == 1 else jnp.concatenate(os_, axis=0)).astype(BF16)
    x1 = x_ref[0] if bb == 1 else jnp.concatenate([x_ref[i] for i in range(bb)], axis=0)
    x2 = x1 + _dot(o, wo_ref[...])
    h = _rmsnorm(x2, gn_ref[...])
    for sub in range(SUBLANES):
        h_ref[pl.ds(sub, rows, stride=SUBLANES), :] = h[:, sub * LANES:(sub + 1) * LANES]
    h_hi, h_lo = _split2(h)
    pieces = _dot(jnp.concatenate([h_hi, h_lo], axis=0), rw_ref[...])
    logits = ((pieces[:rows, :LANES] + pieces[:rows, LANES:]) + (pieces[rows:, :LANES] + pieces[rows:, LANES:])
              + rb_ref[...])
    lane = lax.broadcasted_iota(jnp.int32, logits.shape, 1)
    vals, idxs = [], []
    for _ in range(TOP_K):
        m = jnp.max(logits, axis=-1, keepdims=True)
        first = jnp.min(jnp.where(logits == m, lane, LANES), axis=-1, keepdims=True)
        vals.append(m)
        idxs.append(first)
        logits = jnp.where(lane == first, -jnp.inf, logits)
    es = [jnp.exp(vv - vals[0]) for vv in vals]
    den = es[0] + es[1] + es[2] + es[3]
    ti = jnp.zeros(lane.shape, jnp.int32)
    gt = jnp.zeros(lane.shape, F32)
    for j in range(TOP_K):
        ti = jnp.where(lane == j, idxs[j], ti)
        gt = jnp.where(lane == j, es[j] / den, gt)
    for i in range(bb):
        x2_ref[i] = x2[i * tq:(i + 1) * tq]
        ti_ref[i] = ti[i * tq:(i + 1) * tq]
        gt_ref[i] = gt[i * tq:(i + 1) * tq]


def _attn_route(q, x1, mk, mv, p, bb, tq):
    b, t, d = x1.shape
    assert d == SUBLANES * LANES
    rows = bb * tq
    assert b % bb == 0 and t % tq == 0 and (bb == 1 or tq == t)
    n_tq = t // tq
    blk = lambda w: pl.BlockSpec((bb, tq, w), lambda i, j: (i, j, 0))
    head_rows = mk.ndim == 2
    mem = (pl.BlockSpec((bb * (mk.shape[0] // b), LANES), lambda i, j: (i, 0)) if head_rows
           else pl.BlockSpec((bb, N_MEM, d), lambda i, j: (i, 0, 0)))
    const = lambda shape: pl.BlockSpec(shape, lambda i, j: (0, 0))
    return pl.pallas_call(
        functools.partial(_attn_body, bb=bb, tq=tq, head_rows=head_rows),
        grid=(b // bb, n_tq),
        in_specs=[blk(d), blk(d), mem, mem, const((d, d)), const((1, d)), const((d, 2 * LANES)), const((1, LANES))],
        out_specs=[blk(d), pl.BlockSpec((rows * SUBLANES, LANES), lambda i, j: (i * n_tq + j, 0)), blk(LANES),
                   blk(LANES)],
        out_shape=[jax.ShapeDtypeStruct((b, t, d), F32), jax.ShapeDtypeStruct((b * t * SUBLANES, LANES), F32),
                   jax.ShapeDtypeStruct((b, t, LANES), jnp.int32), jax.ShapeDtypeStruct((b, t, LANES), F32)],
        compiler_params=pltpu.CompilerParams(dimension_semantics=("parallel", "parallel"),
                                             vmem_limit_bytes=VMEM_LIMIT),
        name="attn_route",
    )(q, x1, mk, mv, p['wo'], p['norm_ffn'], p['router_w'], p['router_b'])


def _moe_body(be_ref, gnext_ref, sprev_ref, h_hbm, w1_ref, b1_ref, w2_ref, b2_ref, out_hbm,
              xbuf, ybuf, w1b, w2b, gsem, ssem, *, tm, n_steps):
    k = pl.program_id(0)

    def row(ref, slot, at):
        return ref.at[slot, pl.ds(at, SUBLANES)]

    def gather_all(slot):
        return pltpu.make_async_copy(h_hbm.at[pl.ds(0, tm * SUBLANES)], xbuf.at[slot], gsem.at[slot])

    def scatter_all(slot):
        return pltpu.make_async_copy(ybuf.at[slot], out_hbm.at[pl.ds(0, tm * SUBLANES)], ssem.at[slot])

    @pl.when(k == 0)
    def _prologue():
        ybuf[...] = jnp.zeros(ybuf.shape, F32)
        for r in range(tm):
            pltpu.make_async_copy(h_hbm.at[pl.ds(0, SUBLANES)], row(xbuf, 0, r * SUBLANES), gsem.at[0]).start()

    @pl.when(jnp.logical_or(k == 0, be_ref[k] != be_ref[jnp.maximum(k - 1, 0)]))
    def _cast():
        w1b[...] = w1_ref[0].astype(BF16)
        w2b[...] = w2_ref[0].astype(BF16)

    def phase(cur):
        nxt = 1 - cur
        gather_all(cur).wait()

        @pl.when(k >= 1)
        def _():
            scatter_all(cur).wait()

        for r in range(tm):
            src = pl.multiple_of(gnext_ref[0, 0, r], SUBLANES)
            dst = pl.multiple_of(sprev_ref[0, 0, r], SUBLANES)
            pltpu.make_async_copy(h_hbm.at[pl.ds(src, SUBLANES)], row(xbuf, nxt, r * SUBLANES), gsem.at[nxt]).start()
            pltpu.make_async_copy(row(ybuf, nxt, r * SUBLANES), out_hbm.at[pl.ds(dst, SUBLANES)],
                                  ssem.at[nxt]).start()
        x = jnp.concatenate([xbuf[cur, pl.ds(sub, tm, stride=SUBLANES), :] for sub in range(SUBLANES)], axis=-1)
        hc = _dot(x.astype(BF16), w1b[...]) + b1_ref[0]
        hg = jnp.minimum(hc[:, :D_FF], SWIGLU_LIMIT)
        hl = jnp.clip(hc[:, D_FF:], -SWIGLU_LIMIT, SWIGLU_LIMIT)
        act = hg * _sigmoid(SWIGLU_ALPHA * hg) * (hl + 1.0)
        y = _dot(act.astype(BF16), w2b[...]) + b2_ref[0]
        for sub in range(SUBLANES):
            ybuf[cur, pl.ds(sub, tm, stride=SUBLANES), :] = y[:, sub * LANES:(sub + 1) * LANES]

        @pl.when(k == n_steps - 1)
        def _epilogue():
            gather_all(nxt).wait()
            scatter_all(nxt).wait()

    for parity in range(2):
        pl.when(lax.rem(k, 2) == parity)(functools.partial(phase, parity))


def _moe(h, gtok, sdst, block_e, p, tm):
    n, d = h.shape[0] // SUBLANES, D_MODEL
    n_steps = block_e.shape[0]
    grid_spec = pltpu.PrefetchScalarGridSpec(
        num_scalar_prefetch=1,
        grid=(n_steps,),
        in_specs=[
            pl.BlockSpec((1, 1, tm), lambda k, be: (jnp.minimum(k + 1, n_steps - 1), 0, 0),
                         memory_space=pltpu.SMEM),
            pl.BlockSpec((1, 1, tm), lambda k, be: (jnp.maximum(k - 1, 0), 0, 0), memory_space=pltpu.SMEM),
            pl.BlockSpec(memory_space=pl.ANY),
            pl.BlockSpec((1, d, 2 * D_FF), lambda k, be: (be[k], 0, 0)),
            pl.BlockSpec((1, 1, 2 * D_FF), lambda k, be: (be[k], 0, 0)),
            pl.BlockSpec((1, D_FF, d), lambda k, be: (be[k], 0, 0)),
            pl.BlockSpec((1, 1, d), lambda k, be: (be[k], 0, 0)),
        ],
        out_specs=pl.BlockSpec(memory_space=pl.ANY),
        scratch_shapes=[
            pltpu.VMEM((2, tm * SUBLANES, LANES), F32), pltpu.VMEM((2, tm * SUBLANES, LANES), F32),
            pltpu.VMEM((d, 2 * D_FF), BF16), pltpu.VMEM((D_FF, d), BF16),
            pltpu.SemaphoreType.DMA((2,)), pltpu.SemaphoreType.DMA((2,)),
        ],
    )
    return pl.pallas_call(
        functools.partial(_moe_body, tm=tm, n_steps=n_steps),
        grid_spec=grid_spec,
        out_shape=jax.ShapeDtypeStruct(((n * TOP_K + tm) * SUBLANES, LANES), F32),
        compiler_params=pltpu.CompilerParams(dimension_semantics=("arbitrary",), vmem_limit_bytes=VMEM_LIMIT),
        name="moe_experts",
    )(block_e, gtok.reshape(n_steps, 1, tm), sdst.reshape(n_steps, 1, tm), h, p['w1_e'], p['b1_e'], p['w2_e'],
      p['b2_e'])


def _route_plan(top_i, tm):
    n = top_i.shape[0]
    na = n * TOP_K
    flat_e = top_i.reshape(na)
    order = jnp.argsort(flat_e).astype(jnp.int32)
    counts = jnp.sum((flat_e[:, None] == jnp.arange(N_EXPERTS, dtype=jnp.int32)[None, :]).astype(jnp.int32), axis=0)
    padded = (counts + tm - 1) // tm * tm
    starts = jnp.cumsum(counts) - counts
    pends = jnp.cumsum(padded)
    pstarts = pends - padded
    n_steps = -(-(na + N_EXPERTS * (tm - 1)) // tm) + 2
    blk_start = (jnp.arange(n_steps, dtype=jnp.int32) - 1) * tm
    block_e = jnp.sum((blk_start[:, None] >= pends[None, :]).astype(jnp.int32), axis=1)
    block_e = jnp.minimum(block_e, N_EXPERTS - 1)
    block_e = jnp.where(blk_start < 0, block_e[1], block_e)
    is_e = block_e[:, None] == jnp.arange(N_EXPERTS, dtype=jnp.int32)[None, :]
    pick = lambda table: jnp.sum(jnp.where(is_e, table[None, :], 0), axis=1)
    lane = jnp.arange(tm, dtype=jnp.int32)[None, :]
    local = blk_start[:, None] + lane - pick(pstarts)[:, None]
    valid = (local >= 0) & (local < pick(counts)[:, None]) & (blk_start[:, None] >= 0)
    asg = order[jnp.clip(pick(starts)[:, None] + local, 0, na - 1)]
    gtok = jnp.where(valid, asg // TOP_K, 0)
    sdst = jnp.where(valid, (asg % TOP_K) * n + asg // TOP_K, TOP_K * n + lane)
    return gtok * SUBLANES, sdst * SUBLANES, block_e


def _combine_body(x_ref, gt_ref, gn_ref, *refs, final):
    slot_refs, y_ref = refs[:TOP_K], refs[TOP_K]
    x = x_ref[...]
    gt = gt_ref[...]
    tm = x.shape[0]
    for j in range(TOP_K):
        slot = jnp.concatenate([slot_refs[j][pl.ds(sub, tm, stride=SUBLANES), :] for sub in range(SUBLANES)],
                               axis=-1)
        x = x + gt[:, j:j + 1] * slot
    y_ref[...] = _rmsnorm(x, gn_ref[...]) if final else x


def _combine(x2, gates, slots, gn, tm, n_all, row0, final):
    n, d = x2.shape
    assert n % tm == 0 and row0 % tm == 0 and n_all % tm == 0
    slot_spec = lambda j: pl.BlockSpec((tm * SUBLANES, LANES), lambda i: ((j * n_all + row0) // tm + i, 0))
    return pl.pallas_call(
        functools.partial(_combine_body, final=final),
        grid=(n // tm,),
        in_specs=[pl.BlockSpec((tm, d), lambda i: (i, 0)), pl.BlockSpec((tm, LANES), lambda i: (i, 0)),
                  pl.BlockSpec((1, d), lambda i: (0, 0))] + [slot_spec(j) for j in range(TOP_K)],
        out_specs=pl.BlockSpec((tm, d), lambda i: (i, 0)),
        out_shape=jax.ShapeDtypeStruct((n, d), F32),
        compiler_params=pltpu.CompilerParams(dimension_semantics=("parallel",), vmem_limit_bytes=VMEM_LIMIT),
        name="combine",
    )(x2, gates, gn, *([slots] * TOP_K))


def _layer_params(l, norm_mix, w_in, mu_shift, w0, w2_decay, a0, a2_iclr, g2_gate, k_k, k_a, r_k, lnx_w, lnx_b,
                  conv_w, a_log, dt_bias, gdn_norm_w, w_out, norm_cross, norm_mem, wq_x, wk_x, wv_x, wo_x,
                  norm_ffn, router_w, router_b, w1_e, b1_e, w2_e, b2_e):
    row = lambda z: z.reshape(1, -1).astype(F32)
    lane_pad = lambda z, at: jnp.zeros((1, LANES), F32).at[0, at:at + z.shape[0]].set(z)
    return {
        'norm_mix': row(norm_mix[l]),
        'w_in_r': w_in[l][:, :R_PROJ].astype(BF16),
        'w_in_g': jnp.pad(w_in[l][:, R_PROJ:], ((0, 0), (0, G_PROJ_PAD - G_PROJ))).astype(BF16),
        'mu': row(mu_shift[l]), 'w0': row(w0[l]), 'a0': row(a0[l]),
        'w2': jnp.pad(w2_decay[l], ((0, LORA_A), (0, 0))).astype(BF16),
        'a2': jnp.pad(a2_iclr[l], ((LORA_W, 0), (0, 0))).astype(BF16),
        'g2': g2_gate[l].astype(BF16),
        'k_k': row(k_k[l]), 'k_a': row(k_a[l]), 'r_k': row(r_k[l]), 'lnx_w': row(lnx_w[l]), 'lnx_b': row(lnx_b[l]),
        'conv_w': conv_w[l].astype(F32),
        'a_log': lane_pad(a_log[l], H_G), 'dt_bias': lane_pad(dt_bias[l], H_G),
        'gdn_norm_w': row(gdn_norm_w[l]),
        'w_out_r': w_out[l][:W_RWKV].astype(BF16), 'w_out_g': w_out[l][W_RWKV:].astype(BF16),
        'norm_cross': row(norm_cross[l]), 'norm_mem': row(norm_mem[l]),
        'wq': wq_x[l].astype(BF16), 'wk': wk_x[l].astype(BF16), 'wv': wv_x[l].astype(BF16),
        'wo': wo_x[l].astype(BF16),
        'norm_ffn': row(norm_ffn[l]),
        'router_w': jnp.concatenate(_split2(jnp.pad(router_w[l].astype(F32), ((0, 0), (0, LANES - N_EXPERTS)))),
                                    axis=1),
        'router_b': jnp.full((1, LANES), NEG_BIG, F32).at[0, :N_EXPERTS].set(router_b[l].astype(F32)),
        'w1_e': w1_e[l], 'b1_e': b1_e[l][:, None, :], 'w2_e': w2_e[l], 'b2_e': b2_e[l][:, None, :],
    }


def _mix_and_attend(x, mk, mv, shift_prev, s_r, conv_buf, s_g, p, *, chunk, rwkv_bb, gdn_bb, tm, tm_out, bb, tq):
    b, t, d = x.shape
    assert t >= CONV_W - 1
    pr, pg = _norm_proj(x.reshape(b * t, d), p['norm_mix'], [p['w_in_r'], p['w_in_g']], [F32, F32], tm)
    pr = pr.reshape(b, t, R_PROJ)
    pg = pg.reshape(b, t, G_PROJ_PAD)
    shift_new = pr[:, t - 1]
    conv_new = pg[:, t - (CONV_W - 1):, :GDN_CONV_DIM]
    t_pad = -(-t // chunk) * chunk
    if t_pad != t:
        pr = jnp.pad(pr, ((0, 0), (0, t_pad - t), (0, 0)))
        pg = jnp.pad(pg, ((0, 0), (0, t_pad - t), (0, 0)))
    t_valid = chunk if t_pad == t else t
    y_r, s_r_new = _rwkv_mix(pr, shift_prev, s_r, p, chunk, t_valid, rwkv_bb)
    y_g, s_g_new = _gdn_mix(pg, conv_buf, s_g, p, chunk, t_valid, gdn_bb)
    if t_pad != t:
        y_r, y_g = y_r[:, :t], y_g[:, :t]
    x1, q = _out_q(x.reshape(b * t, d), y_r.reshape(b * t, W_RWKV), y_g.reshape(b * t, W_GDN), p, tm_out)
    x2, h, top_i, gates = _attn_route(q.reshape(b, t, d), x1.reshape(b, t, d), mk, mv, p, bb, tq)
    return (x2.reshape(b * t, d), h, top_i.reshape(b * t, LANES), gates.reshape(b * t, LANES),
            shift_new, s_r_new, conv_new, s_g_new)


def kernel(x_prompt, x_sample, mem_prompt, state_rwkv, state_rwkv_shift, state_gdn, state_gdn_conv, cache_mem_k, cache_mem_v, norm_mix, w_in, mu_shift, w0, w2_decay, a0, a2_iclr, g2_gate, k_k, k_a, r_k, lnx_w, lnx_b, conv_w, a_log, dt_bias, gdn_norm_w, w_out, norm_cross, norm_mem, wq_x, wk_x, wv_x, wo_x, norm_ffn, router_w, router_b, w1_e, b1_e, w2_e, b2_e, final_norm):
    bp, tp, d = x_prompt.shape
    bs, ts, _ = x_sample.shape
    depth = w_in.shape[0]
    np_, ns = bp * tp, bs * ts
    xp, xs = x_prompt, x_sample
    outs = [[] for _ in range(10)]
    for l in range(depth):
        p = _layer_params(l, norm_mix, w_in, mu_shift, w0, w2_decay, a0, a2_iclr, g2_gate, k_k, k_a, r_k, lnx_w,
                          lnx_b, conv_w, a_log, dt_bias, gdn_norm_w, w_out, norm_cross, norm_mem, wq_x, wk_x, wv_x,
                          wo_x, norm_ffn, router_w, router_b, w1_e, b1_e, w2_e, b2_e)
        n_mem = mem_prompt.shape[1]
        mk, mv = _norm_proj(mem_prompt.reshape(bp * n_mem, d), p['norm_mem'], [p['wk'], p['wv']], [F32, F32], 256)
        mk, mv = mk.reshape(bp, n_mem, d), mv.reshape(bp, n_mem, d)
        res_p = _mix_and_attend(
            xp, mk, mv, jnp.zeros((bp, R_PROJ), F32), jnp.zeros((bp, H_R, N_R, N_R), F32),
            jnp.zeros((bp, CONV_W - 1, GDN_CONV_DIM), F32), jnp.zeros((bp, H_G, D_G, D_G), F32), p,
            chunk=MIX_CHUNK, rwkv_bb=4, gdn_bb=8, tm=512, tm_out=1024, bb=1, tq=1024)
        head_rows = lambda c: c.reshape(bs, n_mem, H_X, D_X // LANES, LANES).transpose(0, 1, 3, 2, 4).reshape(-1, LANES)
        mk_s, mv_s = head_rows(cache_mem_k[l]), head_rows(cache_mem_v[l])
        res_s = _mix_and_attend(
            xs, mk_s, mv_s, state_rwkv_shift[l], state_rwkv[l], state_gdn_conv[l],
            state_gdn[l], p, chunk=SUBLANES, rwkv_bb=8, gdn_bb=8, tm=256, tm_out=256, bb=8, tq=ts)
        h = jnp.concatenate([res_p[1], res_s[1]], axis=0)
        top_i = jnp.concatenate([res_p[2], res_s[2]], axis=0)[:, :TOP_K]
        gtok, sdst, block_e = _route_plan(top_i, MOE_ROWS)
        slots = _moe(h, gtok, sdst, block_e, p, MOE_ROWS)
        gn = final_norm.reshape(1, d).astype(F32)
        last = l == depth - 1
        xp = _combine(res_p[0], res_p[3], slots, gn, 512, np_ + ns, 0, last).reshape(bp, tp, d)
        xs = _combine(res_s[0], res_s[3], slots, gn, 512, np_ + ns, np_, last).reshape(bs, ts, d)
        new = [res_p[5], res_p[4], res_p[7], res_p[6], mk.reshape(bp, n_mem, H_X, D_X),
               mv.reshape(bp, n_mem, H_X, D_X), res_s[5], res_s[4], res_s[7], res_s[6]]
        for acc, val in zip(outs, new):
            acc.append(val)
    return (xp, xs) + tuple(jnp.stack(o) for o in outs)
```

```python
import functools

import jax
import jax.numpy as jnp
from jax import lax
from jax.experimental import pallas as pl
from jax.experimental.pallas import tpu as pltpu

F32 = jnp.float32
BF16 = jnp.bfloat16
DEFAULT = lax.Precision.DEFAULT

D_MODEL = 1024
W_RWKV = 512
N_R = 64
H_R = W_RWKV // N_R
LORA_W = 64
LORA_A = 64
LORA_G = 128
R_PROJ = 3 * W_RWKV + LORA_W + LORA_A + LORA_G
GN_EPS = 64e-5
W_GDN = 512
D_G = 128
H_G = W_GDN // D_G
GDN_CONV_DIM = 3 * W_GDN
CONV_W = 4
G_PROJ = GDN_CONV_DIM + W_GDN + 2 * H_G
LANES = 128
SUBLANES = 8
G_PROJ_PAD = GDN_CONV_DIM + W_GDN + LANES
MIX_CHUNK = 64
N_MEM = 256
H_X = 4
D_X = D_MODEL // H_X
N_EXPERTS = 32
TOP_K = 4
D_FF = D_MODEL
SWIGLU_LIMIT = 7.0
SWIGLU_ALPHA = 1.702
MOE_ROWS = 256
RMS_EPS = 1e-6
L2_EPS = 1e-6
NEG_BIG = -1e30
VMEM_LIMIT = 56 * 1024 * 1024


def _dot(a, b, precision=DEFAULT):
    return jnp.dot(a, b, preferred_element_type=F32, precision=precision)


def _dot_nt(a, b, precision=DEFAULT):
    return lax.dot_general(a, b, (((1,), (1,)), ((), ())), preferred_element_type=F32, precision=precision)


NN = (((1,), (0,)), ((), ()))
NT = (((1,), (1,)), ((), ()))
TN = (((0,), (0,)), ((), ()))


def _split2(x):
    hi = x.astype(BF16)
    return hi, (x - hi.astype(F32)).astype(BF16)


def _mm(a, b, dims):
    return lax.dot_general(a.astype(BF16), b.astype(BF16), dims, preferred_element_type=F32)


def _sel_mm(sel, x, dims):
    dg = lambda y: lax.dot_general(sel, y, dims, preferred_element_type=F32)
    hi = x.astype(BF16)
    r1 = x - hi.astype(F32)
    mid = r1.astype(BF16)
    lo = (r1 - mid.astype(F32)).astype(BF16)
    return dg(hi) + (dg(mid) + dg(lo))


def _sigmoid(x):
    return 1.0 / (1.0 + jnp.exp(-x))


def _softplus(x):
    return jnp.maximum(x, 0.0) + jnp.log(1.0 + jnp.exp(-jnp.abs(x)))


def _rmsnorm(x, g):
    return x * lax.rsqrt(jnp.mean(x * x, axis=-1, keepdims=True) + RMS_EPS) * g


def _tri_masks(c):
    row = lax.broadcasted_iota(jnp.int32, (c, c), 0)
    col = lax.broadcasted_iota(jnp.int32, (c, c), 1)
    return col <= row, col < row, (col == row).astype(F32)


def _unit_lower_inverses(ms, eye, c):
    ts = [eye + m for m in ms]
    ps = list(ms)
    covered = 2
    while covered < c:
        ps = [_mm(p, p, NN) for p in ps]
        ts = [t + _mm(t, p, NN) for t, p in zip(ts, ps)]
        covered *= 2
    return ts


def _norm_proj_body(x_ref, g_ref, *refs, n_out):
    w_refs, o_refs = refs[:n_out], refs[n_out:]
    hb = _rmsnorm(x_ref[...], g_ref[...]).astype(BF16)
    for w_ref, o_ref in zip(w_refs, o_refs):
        o_ref[...] = _dot(hb, w_ref[...]).astype(o_ref.dtype)


def _norm_proj(x, g, ws, out_dtypes, tm):
    n, d = x.shape
    assert n % tm == 0
    in_specs = [pl.BlockSpec((tm, d), lambda i: (i, 0)), pl.BlockSpec((1, d), lambda i: (0, 0))]
    in_specs += [pl.BlockSpec(w.shape, lambda i: (0, 0)) for w in ws]
    return pl.pallas_call(
        functools.partial(_norm_proj_body, n_out=len(ws)),
        grid=(n // tm,),
        in_specs=in_specs,
        out_specs=[pl.BlockSpec((tm, w.shape[1]), lambda i: (i, 0)) for w in ws],
        out_shape=[jax.ShapeDtypeStruct((n, w.shape[1]), dt) for w, dt in zip(ws, out_dtypes)],
        compiler_params=pltpu.CompilerParams(dimension_semantics=("parallel",), vmem_limit_bytes=VMEM_LIMIT),
        name="norm_proj",
    )(x, g, *ws)


def _rwkv_body(pr_ref, shift_ref, s0_ref, mu_ref, w0_ref, w2_ref, a0_ref, a2_ref, g2_ref, kk_ref, ka_ref,
               rk_ref, lnw_ref, lnb_ref, y_ref, sout_ref, s_scr, prev_scr, *, bb, chunk, t_valid, n_chunks):
    c = pl.program_id(1)

    @pl.when(c == 0)
    def _init():
        s_scr[...] = s0_ref[...]
        prev_scr[...] = shift_ref[...]

    incl, strict, eye = _tri_masks(chunk)
    incl_bf = incl.astype(BF16)
    rows = lax.broadcasted_iota(jnp.int32, (chunk, 1), 0)
    sls = [slice(h * N_R, (h + 1) * N_R) for h in range(H_R)]

    def per_batch(bi):
        pr = pr_ref[bi]
        prev = jnp.where(rows == 0, prev_scr[bi], pltpu.roll(pr, 1, 0))
        prev_scr[bi] = pr[chunk - 1:chunk, :]
        xm = pr + (prev - pr) * mu_ref[...]
        r = xm[:, :W_RWKV]
        k = xm[:, W_RWKV:2 * W_RWKV]
        v = xm[:, 2 * W_RWKV:3 * W_RWKV]
        lo = xm[:, 3 * W_RWKV:3 * W_RWKV + LORA_W + LORA_A]
        g_lo = xm[:, 3 * W_RWKV + LORA_W + LORA_A:]
        logw = -_softplus(-(w0_ref[...] + _dot(jnp.tanh(lo).astype(BF16), w2_ref[...]))) - 0.5
        wl = -jnp.exp(logw)
        a = _sigmoid(a0_ref[...] + _dot(lo.astype(BF16), a2_ref[...]))
        g = _dot(_sigmoid(g_lo).astype(BF16), g2_ref[...])
        kkv = k * kk_ref[...]
        k = k * (1.0 + (a - 1.0) * ka_ref[...])
        if t_valid < chunk:
            valid = rows < t_valid
            wl = jnp.where(valid, wl, 0.0)
            kkv = jnp.where(valid, kkv, 0.0)
            k = jnp.where(valid, k, 0.0)
            v = jnp.where(valid, v, 0.0)
        cum = _sel_mm(incl_bf, wl, NN)
        cum_last = cum[chunk - 1:chunk, :]
        return dict(r=r, k=k, v=v, a=a, g=g, kkv=kkv, w_incl=jnp.exp(cum), w_prev=jnp.exp(cum - wl),
                    w_inv=jnp.exp(-cum), w_tail=jnp.exp(cum_last - cum), w_last=jnp.exp(cum_last))

    pre = [per_batch(bi) for bi in range(bb)]
    chains = [(bi, h) for bi in range(bb) for h in range(H_R)]
    col = lambda name: [pre[bi][name][:, sls[h]] for bi, h in chains]
    n = range(len(chains))
    r_, k_, v_, a_ = col('r'), col('k'), col('v'), col('a')
    w_incl, w_prev, w_inv, w_tail, w_last = col('w_incl'), col('w_prev'), col('w_inv'), col('w_tail'), col('w_last')
    kks = [x * lax.rsqrt(jnp.sum(x * x, axis=-1, keepdims=True) + L2_EPS) for x in col('kkv')]
    a_hat = [-(kks[i] * w_prev[i]) for i in n]
    kka = [kks[i] * a_[i] for i in n]
    b_hat = [kka[i] * w_inv[i] for i in n]
    k_hat = [k_[i] * w_inv[i] for i in n]
    r_hat = [r_[i] * w_incl[i] for i in n]
    cross = [_mm(jnp.concatenate([a_hat[i], r_hat[i]], axis=0), jnp.concatenate([b_hat[i], k_hat[i]], axis=0), NT)
             for i in n]
    row2 = lax.broadcasted_iota(jnp.int32, (chunk, 2 * chunk), 0)
    col2 = lax.broadcasted_iota(jnp.int32, (chunk, 2 * chunk), 1)
    col2 = jnp.where(col2 >= chunk, col2 - chunk, col2)
    strict2, incl2 = col2 < row2, col2 <= row2
    m_top = [jnp.where(strict2, x[:chunk], 0.0) for x in cross]
    a_bot = [jnp.where(incl2, x[chunk:], 0.0) for x in cross]
    t_inv = _unit_lower_inverses([x[:, :chunk] for x in m_top], eye, chunk)
    w_hat = [_mm(t_inv[i], a_hat[i], NN) for i in n]
    mv = [_mm(m_top[i], jnp.concatenate([jnp.zeros_like(v_[i]), v_[i]], axis=0), NN) for i in n]
    u = [_mm(t_inv[i], mv[i], NN) for i in n]
    ss = [s_scr[bi, h] for bi, h in chains]
    p = [_mm(w_hat[i], ss[i], NT) + u[i] for i in n]
    pv = [jnp.concatenate([p[i], v_[i]], axis=0) for i in n]
    ys = [_mm(r_hat[i], ss[i], NT) + _mm(a_bot[i], pv[i], NN) for i in n]
    for i, (bi, h) in enumerate(chains):
        tails = jnp.concatenate([kka[i] * w_tail[i], k_[i] * w_tail[i]], axis=0)
        s_scr[bi, h] = ss[i] * w_last[i] + _mm(pv[i], tails, TN)
    for i, (bi, h) in enumerate(chains):
        sl = sls[h]
        y = ys[i]
        mean = jnp.mean(y, axis=-1, keepdims=True)
        yc = y - mean
        var = jnp.mean(yc * yc, axis=-1, keepdims=True)
        yn = yc * lax.rsqrt(var + GN_EPS) * lnw_ref[:, sl] + lnb_ref[:, sl]
        bonus = jnp.sum(r_[i] * k_[i] * rk_ref[:, sl], axis=-1, keepdims=True) * v_[i]
        y_ref[bi, :, sl] = ((yn + bonus) * pre[bi]['g'][:, sl]).astype(y_ref.dtype)

    @pl.when(c == n_chunks - 1)
    def _fin():
        sout_ref[...] = s_scr[...]


def _rwkv_mix(pr, shift_prev, s0, p, chunk, t_valid, bb):
    b, t, _ = pr.shape
    assert b % bb == 0 and t % chunk == 0
    n_chunks = t // chunk
    const = lambda shape: pl.BlockSpec(shape, lambda i, j: (0,) * len(shape))
    return pl.pallas_call(
        functools.partial(_rwkv_body, bb=bb, chunk=chunk, t_valid=t_valid, n_chunks=n_chunks),
        grid=(b // bb, n_chunks),
        in_specs=[
            pl.BlockSpec((bb, chunk, R_PROJ), lambda i, j: (i, j, 0)),
            pl.BlockSpec((bb, 1, R_PROJ), lambda i, j: (i, 0, 0)),
            pl.BlockSpec((bb, H_R, N_R, N_R), lambda i, j: (i, 0, 0, 0)),
            const((1, R_PROJ)), const((1, W_RWKV)), const((LANES, W_RWKV)), const((1, W_RWKV)),
            const((LANES, W_RWKV)), const((LORA_G, W_RWKV)), const((1, W_RWKV)), const((1, W_RWKV)),
            const((1, W_RWKV)), const((1, W_RWKV)), const((1, W_RWKV)),
        ],
        out_specs=[
            pl.BlockSpec((bb, chunk, W_RWKV), lambda i, j: (i, j, 0)),
            pl.BlockSpec((bb, H_R, N_R, N_R), lambda i, j: (i, 0, 0, 0)),
        ],
        out_shape=[jax.ShapeDtypeStruct((b, t, W_RWKV), BF16), jax.ShapeDtypeStruct((b, H_R, N_R, N_R), F32)],
        scratch_shapes=[pltpu.VMEM((bb, H_R, N_R, N_R), F32), pltpu.VMEM((bb, 1, R_PROJ), F32)],
        compiler_params=pltpu.CompilerParams(dimension_semantics=("arbitrary", "arbitrary"),
                                             vmem_limit_bytes=VMEM_LIMIT),
        name="rwkv_mix",
    )(pr, shift_prev[:, None, :], s0, p['mu'], p['w0'], p['w2'], p['a0'], p['a2'], p['g2'], p['k_k'], p['k_a'],
      p['r_k'], p['lnx_w'], p['lnx_b'])


def _gdn_body(pg_ref, cbuf_ref, s0_ref, cw_ref, alog_ref, dtb_ref, nw_ref, y_ref, sout_ref, s_scr, xp_scr,
              *, bb, chunk, t_valid, n_chunks):
    c = pl.program_id(1)

    @pl.when(c == 0)
    def _init():
        s_scr[...] = s0_ref[...]
        xp_scr[:, 0:SUBLANES, :] = cbuf_ref[...]

    incl, strict, eye = _tri_masks(chunk)
    incl_bf = incl.astype(BF16)
    lane = lax.broadcasted_iota(jnp.int32, (chunk, LANES), 1)
    sls = [slice(h * D_G, (h + 1) * D_G) for h in range(H_G)]

    def per_batch(bi):
        xp_scr[bi, SUBLANES:SUBLANES + chunk, :] = pg_ref[bi, :, :GDN_CONV_DIM]
        base = SUBLANES - (CONV_W - 1)
        conv = xp_scr[bi, base:base + chunk, :] * cw_ref[0:1, :]
        for j in range(1, CONV_W):
            conv = conv + xp_scr[bi, base + j:base + j + chunk, :] * cw_ref[j:j + 1, :]
        xp_scr[bi, 0:SUBLANES, :] = xp_scr[bi, chunk:chunk + SUBLANES, :]
        qkv = conv * _sigmoid(conv)
        z = pg_ref[bi, :, GDN_CONV_DIM:GDN_CONV_DIM + W_GDN]
        ba = pg_ref[bi, :, GDN_CONV_DIM + W_GDN:]
        beta_blk = _sigmoid(ba)
        g_blk = -jnp.exp(alog_ref[...]) * _softplus(ba + dtb_ref[...])
        if t_valid < chunk:
            valid = lax.broadcasted_iota(jnp.int32, (chunk, 1), 0) < t_valid
            beta_blk = jnp.where(valid, beta_blk, 0.0)
            g_blk = jnp.where(valid, g_blk, 0.0)
        gc_blk = _sel_mm(incl_bf, g_blk, NN)
        return dict(qkv=qkv, z=z, beta_blk=beta_blk, gc_blk=gc_blk)

    pre = [per_batch(bi) for bi in range(bb)]
    chains = [(bi, h) for bi in range(bb) for h in range(H_G)]
    n = range(len(chains))
    qs = [pre[bi]['qkv'][:, sls[h]] for bi, h in chains]
    qs = [x * lax.rsqrt(jnp.sum(x * x, axis=-1, keepdims=True) + L2_EPS) * (D_G ** -0.5) for x in qs]
    ks = [pre[bi]['qkv'][:, W_GDN + h * D_G:W_GDN + (h + 1) * D_G] for bi, h in chains]
    ks = [x * lax.rsqrt(jnp.sum(x * x, axis=-1, keepdims=True) + L2_EPS) for x in ks]
    vs = [pre[bi]['qkv'][:, 2 * W_GDN + h * D_G:2 * W_GDN + (h + 1) * D_G] for bi, h in chains]
    betas = [pre[bi]['beta_blk'][:, h:h + 1] for bi, h in chains]
    gcols = [pre[bi]['gc_blk'][:, H_G + h:H_G + h + 1] for bi, h in chains]
    grows = [_sel_mm((lane == H_G + h).astype(BF16), pre[bi]['gc_blk'], NT) for bi, h in chains]
    g_last = [gcols[i][chunk - 1:chunk, :] for i in n]
    decay = [jnp.where(incl, jnp.exp(jnp.where(incl, gcols[i] - grows[i], 0.0)), 0.0) for i in n]
    k_beta = [ks[i] * betas[i] for i in n]
    cross = [_mm(jnp.concatenate([k_beta[i], qs[i]], axis=0), ks[i], NT) for i in n]
    lmat = [jnp.where(strict, cross[i][:chunk] * decay[i], 0.0) for i in n]
    attn = [jnp.where(incl, cross[i][chunk:] * decay[i], 0.0) for i in n]
    t_inv = _unit_lower_inverses([-x for x in lmat], eye, chunk)
    e_gc = [jnp.exp(gcols[i]) for i in n]
    uw = [_mm(t_inv[i], jnp.concatenate([vs[i] * betas[i], k_beta[i] * e_gc[i]], axis=1), NN) for i in n]
    ss = [s_scr[bi, h] for bi, h in chains]
    v_new = [uw[i][:, :D_G] - _mm(uw[i][:, D_G:], ss[i], NN) for i in n]
    os_ = [_mm(jnp.concatenate([qs[i] * e_gc[i], attn[i]], axis=1), jnp.concatenate([ss[i], v_new[i]], axis=0), NN)
           for i in n]
    for i, (bi, h) in enumerate(chains):
        s_scr[bi, h] = ss[i] * jnp.exp(g_last[i]) + _mm(ks[i] * jnp.exp(g_last[i] - gcols[i]), v_new[i], TN)
    for i, (bi, h) in enumerate(chains):
        o = os_[i]
        o = o * lax.rsqrt(jnp.mean(o * o, axis=-1, keepdims=True) + RMS_EPS) * nw_ref[...]
        z_h = pre[bi]['z'][:, sls[h]]
        y_ref[bi, :, sls[h]] = (o * (z_h * _sigmoid(z_h))).astype(y_ref.dtype)

    @pl.when(c == n_chunks - 1)
    def _fin():
        sout_ref[...] = s_scr[...]


def _gdn_mix(pg, conv_buf, s0, p, chunk, t_valid, bb):
    b, t, _ = pg.shape
    assert b % bb == 0 and t % chunk == 0
    n_chunks = t // chunk
    cbuf = jnp.pad(conv_buf, ((0, 0), (SUBLANES - (CONV_W - 1), 0), (0, 0)))
    const = lambda shape: pl.BlockSpec(shape, lambda i, j: (0,) * len(shape))
    return pl.pallas_call(
        functools.partial(_gdn_body, bb=bb, chunk=chunk, t_valid=t_valid, n_chunks=n_chunks),
        grid=(b // bb, n_chunks),
        in_specs=[
            pl.BlockSpec((bb, chunk, G_PROJ_PAD), lambda i, j: (i, j, 0)),
            pl.BlockSpec((bb, SUBLANES, GDN_CONV_DIM), lambda i, j: (i, 0, 0)),
            pl.BlockSpec((bb, H_G, D_G, D_G), lambda i, j: (i, 0, 0, 0)),
            const((CONV_W, GDN_CONV_DIM)), const((1, LANES)), const((1, LANES)), const((1, D_G)),
        ],
        out_specs=[
            pl.BlockSpec((bb, chunk, W_GDN), lambda i, j: (i, j, 0)),
            pl.BlockSpec((bb, H_G, D_G, D_G), lambda i, j: (i, 0, 0, 0)),
        ],
        out_shape=[jax.ShapeDtypeStruct((b, t, W_GDN), BF16), jax.ShapeDtypeStruct((b, H_G, D_G, D_G), F32)],
        scratch_shapes=[pltpu.VMEM((bb, H_G, D_G, D_G), F32),
                        pltpu.VMEM((bb, SUBLANES + chunk, GDN_CONV_DIM), F32)],
        compiler_params=pltpu.CompilerParams(dimension_semantics=("arbitrary", "arbitrary"),
                                             vmem_limit_bytes=VMEM_LIMIT),
        name="gdn_mix",
    )(pg, cbuf, s0, p['conv_w'], p['a_log'], p['dt_bias'], p['gdn_norm_w'])


def _out_q_body(x_ref, yr_ref, yg_ref, wor_ref, wog_ref, gn_ref, wq_ref, x1_ref, q_ref):
    x1 = x_ref[...] + _dot(yr_ref[...], wor_ref[...]) + _dot(yg_ref[...], wog_ref[...])
    x1_ref[...] = x1
    q_ref[...] = _dot(_rmsnorm(x1, gn_ref[...]).astype(BF16), wq_ref[...]).astype(q_ref.dtype)


def _out_q(x, yr, yg, p, tm):
    n, d = x.shape
    assert n % tm == 0
    row = lambda w: pl.BlockSpec((tm, w), lambda i: (i, 0))
    const = lambda shape: pl.BlockSpec(shape, lambda i: (0, 0))
    return pl.pallas_call(
        _out_q_body,
        grid=(n // tm,),
        in_specs=[row(d), row(W_RWKV), row(W_GDN), const((W_RWKV, d)), const((W_GDN, d)), const((1, d)),
                  const((d, d))],
        out_specs=[row(d), row(d)],
        out_shape=[jax.ShapeDtypeStruct((n, d), F32), jax.ShapeDtypeStruct((n, d), BF16)],
        compiler_params=pltpu.CompilerParams(dimension_semantics=("parallel",), vmem_limit_bytes=VMEM_LIMIT),
        name="out_q",
    )(x, yr, yg, p['w_out_r'], p['w_out_g'], p['norm_cross'], p['wq'])


def _attn_body(q_ref, x_ref, mk_ref, mv_ref, wo_ref, gn_ref, rw_ref, rb_ref, x2_ref, h_ref, ti_ref, gt_ref,
               *, bb, tq, head_rows):
    d_tiles = D_X // LANES
    mem_rows = N_MEM * d_tiles * H_X

    def head_mem(ref, i, hh):
        if not head_rows:
            return ref[i, :, hh * D_X:(hh + 1) * D_X]
        return jnp.concatenate(
            [ref[pl.ds(i * mem_rows + dt * H_X + hh, N_MEM, stride=d_tiles * H_X), :] for dt in range(d_tiles)],
            axis=-1)

    os_ = []
    for i in range(bb):
        q = q_ref[i]
        heads = []
        for hh in range(H_X):
            sl = slice(hh * D_X, (hh + 1) * D_X)
            s = _dot_nt(q[:, sl], head_mem(mk_ref, i, hh).astype(BF16)) * (D_X ** -0.5)
            e = jnp.exp(s - jnp.max(s, axis=-1, keepdims=True))
            prob = e / jnp.sum(e, axis=-1, keepdims=True)
            heads.append(_dot(prob.astype(BF16), head_mem(mv_ref, i, hh).astype(BF16)))
        os_.append(jnp.concatenate(heads, axis=-1))
    rows = bb * tq
    o = (os_[0] if bb == 1 else jnp.concatenate(os_, axis=0)).astype(BF16)
    x1 = x_ref[0] if bb == 1 else jnp.concatenate([x_ref[i] for i in range(bb)], axis=0)
    x2 = x1 + _dot(o, wo_ref[...])
    h = _rmsnorm(x2, gn_ref[...])
    for sub in range(SUBLANES):
        h_ref[pl.ds(sub, rows, stride=SUBLANES), :] = h[:, sub * LANES:(sub + 1) * LANES]
    h_hi, h_lo = _split2(h)
    pieces = _dot(jnp.concatenate([h_hi, h_lo], axis=0), rw_ref[...])
    logits = ((pieces[:rows, :LANES] + pieces[:rows, LANES:]) + (pieces[rows:, :LANES] + pieces[rows:, LANES:])
              + rb_ref[...])
    lane = lax.broadcasted_iota(jnp.int32, logits.shape, 1)
    vals, idxs = [], []
    for _ in range(TOP_K):
        m = jnp.max(logits, axis=-1, keepdims=True)
        first = jnp.min(jnp.where(logits == m, lane, LANES), axis=-1, keepdims=True)
        vals.append(m)
        idxs.append(first)
        logits = jnp.where(lane == first, -jnp.inf, logits)
    es = [jnp.exp(vv - vals[0]) for vv in vals]
    den = es[0] + es[1] + es[2] + es[3]
    ti = jnp.zeros(lane.shape, jnp.int32)
    gt = jnp.zeros(lane.shape, F32)
    for j in range(TOP_K):
        ti = jnp.where(lane == j, idxs[j], ti)
        gt = jnp.where(lane == j, es[j] / den, gt)
    for i in range(bb):
        x2_ref[i] = x2[i * tq:(i + 1) * tq]
        ti_ref[i] = ti[i * tq:(i + 1) * tq]
        gt_ref[i] = gt[i * tq:(i + 1) * tq]


def _attn_route(q, x1, mk, mv, p, bb, tq):
    b, t, d = x1.shape
    assert d == SUBLANES * LANES
    rows = bb * tq
    assert b % bb == 0 and t % tq == 0 and (bb == 1 or tq == t)
    n_tq = t // tq
    blk = lambda w: pl.BlockSpec((bb, tq, w), lambda i, j: (i, j, 0))
    head_rows = mk.ndim == 2
    mem = (pl.BlockSpec((bb * (mk.shape[0] // b), LANES), lambda i, j: (i, 0)) if head_rows
           else pl.BlockSpec((bb, N_MEM, d), lambda i, j: (i, 0, 0)))
    const = lambda shape: pl.BlockSpec(shape, lambda i, j: (0, 0))
    return pl.pallas_call(
        functools.partial(_attn_body, bb=bb, tq=tq, head_rows=head_rows),
        grid=(b // bb, n_tq),
        in_specs=[blk(d), blk(d), mem, mem, const((d, d)), const((1, d)), const((d, 2 * LANES)), const((1, LANES))],
        out_specs=[blk(d), pl.BlockSpec((rows * SUBLANES, LANES), lambda i, j: (i * n_tq + j, 0)), blk(LANES),
                   blk(LANES)],
        out_shape=[jax.ShapeDtypeStruct((b, t, d), F32), jax.ShapeDtypeStruct((b * t * SUBLANES, LANES), F32),
                   jax.ShapeDtypeStruct((b, t, LANES), jnp.int32), jax.ShapeDtypeStruct((b, t, LANES), F32)],
        compiler_params=pltpu.CompilerParams(dimension_semantics=("parallel", "parallel"),
                                             vmem_limit_bytes=VMEM_LIMIT),
        name="attn_route",
    )(q, x1, mk, mv, p['wo'], p['norm_ffn'], p['router_w'], p['router_b'])


def _moe_body(be_ref, live_ref, gnext_ref, sprev_ref, h_hbm, w1_ref, b1_ref, w2_ref, b2_ref, out_hbm,
              xbuf, ybuf, w1b, w2b, gsem, ssem, *, tm, n_steps):
    k = pl.program_id(0)

    def row(ref, slot, at):
        return ref.at[slot, pl.ds(at, SUBLANES)]

    def gather_all(slot):
        return pltpu.make_async_copy(h_hbm.at[pl.ds(0, tm * SUBLANES)], xbuf.at[slot], gsem.at[slot])

    def scatter_all(slot):
        return pltpu.make_async_copy(ybuf.at[slot], out_hbm.at[pl.ds(0, tm * SUBLANES)], ssem.at[slot])

    live = lambda blk: live_ref[jnp.clip(blk, 0, n_steps - 1)] > 0

    @pl.when(k == 0)
    def _init_spare():
        ybuf[1] = jnp.zeros(ybuf.shape[1:], F32)
        spare = pltpu.make_async_copy(ybuf.at[1], out_hbm.at[pl.ds(out_hbm.shape[0] - tm * SUBLANES, tm * SUBLANES)],
                                      ssem.at[1])
        spare.start()
        spare.wait()

    @pl.when(jnp.logical_and(k == 0, live(0)))
    def _prologue():
        for r in range(tm):
            pltpu.make_async_copy(h_hbm.at[pl.ds(0, SUBLANES)], row(xbuf, 0, r * SUBLANES), gsem.at[0]).start()

    @pl.when(jnp.logical_or(k == 0, be_ref[k] != be_ref[jnp.maximum(k - 1, 0)]))
    def _cast():
        w1b[...] = w1_ref[0].astype(BF16)
        w2b[...] = w2_ref[0].astype(BF16)

    def phase(cur):
        nxt = 1 - cur

        @pl.when(live(k))
        def _():
            gather_all(cur).wait()

        @pl.when(jnp.logical_and(k >= 2, live(k - 2)))
        def _():
            scatter_all(cur).wait()

        @pl.when(jnp.logical_and(k + 1 < n_steps, live(k + 1)))
        def _():
            for r in range(tm):
                src = pl.multiple_of(gnext_ref[0, 0, r], SUBLANES)
                pltpu.make_async_copy(h_hbm.at[pl.ds(src, SUBLANES)], row(xbuf, nxt, r * SUBLANES),
                                      gsem.at[nxt]).start()

        @pl.when(jnp.logical_and(k >= 1, live(k - 1)))
        def _():
            for r in range(tm):
                dst = pl.multiple_of(sprev_ref[0, 0, r], SUBLANES)
                pltpu.make_async_copy(row(ybuf, nxt, r * SUBLANES), out_hbm.at[pl.ds(dst, SUBLANES)],
                                      ssem.at[nxt]).start()

        @pl.when(live(k))
        def _compute():
            x = jnp.concatenate([xbuf[cur, pl.ds(sub, tm, stride=SUBLANES), :] for sub in range(SUBLANES)],
                                axis=-1)
            hc = _dot(x.astype(BF16), w1b[...]) + b1_ref[0]
            hg = jnp.minimum(hc[:, :D_FF], SWIGLU_LIMIT)
            hl = jnp.clip(hc[:, D_FF:], -SWIGLU_LIMIT, SWIGLU_LIMIT)
            act = hg * _sigmoid(SWIGLU_ALPHA * hg) * (hl + 1.0)
            y = _dot(act.astype(BF16), w2b[...]) + b2_ref[0]
            for sub in range(SUBLANES):
                ybuf[cur, pl.ds(sub, tm, stride=SUBLANES), :] = y[:, sub * LANES:(sub + 1) * LANES]

        @pl.when(jnp.logical_and(k == n_steps - 1, live(k - 1)))
        def _epilogue():
            scatter_all(nxt).wait()

    for parity in range(2):
        pl.when(lax.rem(k, 2) == parity)(functools.partial(phase, parity))


def _moe(h, gtok, sdst, block_e, block_live, p, tm):
    n, d = h.shape[0] // SUBLANES, D_MODEL
    n_steps = block_e.shape[0]
    grid_spec = pltpu.PrefetchScalarGridSpec(
        num_scalar_prefetch=2,
        grid=(n_steps,),
        in_specs=[
            pl.BlockSpec((1, 1, tm), lambda k, be, live:(jnp.minimum(k + 1, n_steps - 1), 0, 0),
                         memory_space=pltpu.SMEM),
            pl.BlockSpec((1, 1, tm), lambda k, be, live:(jnp.maximum(k - 1, 0), 0, 0), memory_space=pltpu.SMEM),
            pl.BlockSpec(memory_space=pl.ANY),
            pl.BlockSpec((1, d, 2 * D_FF), lambda k, be, live:(be[k], 0, 0)),
            pl.BlockSpec((1, 1, 2 * D_FF), lambda k, be, live:(be[k], 0, 0)),
            pl.BlockSpec((1, D_FF, d), lambda k, be, live:(be[k], 0, 0)),
            pl.BlockSpec((1, 1, d), lambda k, be, live:(be[k], 0, 0)),
        ],
        out_specs=pl.BlockSpec(memory_space=pl.ANY),
        scratch_shapes=[
            pltpu.VMEM((2, tm * SUBLANES, LANES), F32), pltpu.VMEM((2, tm * SUBLANES, LANES), F32),
            pltpu.VMEM((d, 2 * D_FF), BF16), pltpu.VMEM((D_FF, d), BF16),
            pltpu.SemaphoreType.DMA((2,)), pltpu.SemaphoreType.DMA((2,)),
        ],
    )
    return pl.pallas_call(
        functools.partial(_moe_body, tm=tm, n_steps=n_steps),
        grid_spec=grid_spec,
        out_shape=jax.ShapeDtypeStruct(((n * TOP_K + tm) * SUBLANES, LANES), F32),
        compiler_params=pltpu.CompilerParams(dimension_semantics=("arbitrary",), vmem_limit_bytes=VMEM_LIMIT),
        name="moe_experts",
    )(block_e, block_live, gtok.reshape(n_steps, 1, tm), sdst.reshape(n_steps, 1, tm), h, p['w1_e'], p['b1_e'], p['w2_e'],
      p['b2_e'])


def _route_plan(top_i, tm):
    n = top_i.shape[0]
    na = n * TOP_K
    flat_e = top_i.reshape(na)
    order = jnp.argsort(flat_e).astype(jnp.int32)
    counts = jnp.sum((flat_e[:, None] == jnp.arange(N_EXPERTS, dtype=jnp.int32)[None, :]).astype(jnp.int32), axis=0)
    padded = (counts + tm - 1) // tm * tm
    starts = jnp.cumsum(counts) - counts
    pends = jnp.cumsum(padded)
    pstarts = pends - padded
    n_steps = -(-(na + N_EXPERTS * (tm - 1)) // tm) + 2
    blk_start = (jnp.arange(n_steps, dtype=jnp.int32) - 1) * tm
    block_e = jnp.sum((blk_start[:, None] >= pends[None, :]).astype(jnp.int32), axis=1)
    block_e = jnp.minimum(block_e, N_EXPERTS - 1)
    block_e = jnp.where(blk_start < 0, block_e[1], block_e)
    is_e = block_e[:, None] == jnp.arange(N_EXPERTS, dtype=jnp.int32)[None, :]
    pick = lambda table: jnp.sum(jnp.where(is_e, table[None, :], 0), axis=1)
    lane = jnp.arange(tm, dtype=jnp.int32)[None, :]
    local = blk_start[:, None] + lane - pick(pstarts)[:, None]
    valid = (local >= 0) & (local < pick(counts)[:, None]) & (blk_start[:, None] >= 0)
    asg = order[jnp.clip(pick(starts)[:, None] + local, 0, na - 1)]
    gtok = jnp.where(valid, asg // TOP_K, 0)
    sdst = jnp.where(valid, (asg % TOP_K) * n + asg // TOP_K, TOP_K * n + lane)
    block_live = valid[:, 0].astype(jnp.int32)
    return gtok * SUBLANES, sdst * SUBLANES, block_e, block_live


def _combine_body(x_ref, gt_ref, gn_ref, *refs, final):
    slot_refs, y_ref = refs[:TOP_K], refs[TOP_K]
    x = x_ref[...]
    gt = gt_ref[...]
    tm = x.shape[0]
    for j in range(TOP_K):
        slot = jnp.concatenate([slot_refs[j][pl.ds(sub, tm, stride=SUBLANES), :] for sub in range(SUBLANES)],
                               axis=-1)
        x = x + gt[:, j:j + 1] * slot
    y_ref[...] = _rmsnorm(x, gn_ref[...]) if final else x


def _combine(x2, gates, slots, gn, tm, n_all, row0, final):
    n, d = x2.shape
    assert n % tm == 0 and row0 % tm == 0 and n_all % tm == 0
    slot_spec = lambda j: pl.BlockSpec((tm * SUBLANES, LANES), lambda i: ((j * n_all + row0) // tm + i, 0))
    return pl.pallas_call(
        functools.partial(_combine_body, final=final),
        grid=(n // tm,),
        in_specs=[pl.BlockSpec((tm, d), lambda i: (i, 0)), pl.BlockSpec((tm, LANES), lambda i: (i, 0)),
                  pl.BlockSpec((1, d), lambda i: (0, 0))] + [slot_spec(j) for j in range(TOP_K)],
        out_specs=pl.BlockSpec((tm, d), lambda i: (i, 0)),
        out_shape=jax.ShapeDtypeStruct((n, d), F32),
        compiler_params=pltpu.CompilerParams(dimension_semantics=("parallel",), vmem_limit_bytes=VMEM_LIMIT),
        name="combine",
    )(x2, gates, gn, *([slots] * TOP_K))


def _layer_params(l, norm_mix, w_in, mu_shift, w0, w2_decay, a0, a2_iclr, g2_gate, k_k, k_a, r_k, lnx_w, lnx_b,
                  conv_w, a_log, dt_bias, gdn_norm_w, w_out, norm_cross, norm_mem, wq_x, wk_x, wv_x, wo_x,
                  norm_ffn, router_w, router_b, w1_e, b1_e, w2_e, b2_e):
    row = lambda z: z.reshape(1, -1).astype(F32)
    lane_pad = lambda z, at: jnp.zeros((1, LANES), F32).at[0, at:at + z.shape[0]].set(z)
    return {
        'norm_mix': row(norm_mix[l]),
        'w_in_r': w_in[l][:, :R_PROJ].astype(BF16),
        'w_in_g': jnp.pad(w_in[l][:, R_PROJ:], ((0, 0), (0, G_PROJ_PAD - G_PROJ))).astype(BF16),
        'mu': row(mu_shift[l]), 'w0': row(w0[l]), 'a0': row(a0[l]),
        'w2': jnp.pad(w2_decay[l], ((0, LORA_A), (0, 0))).astype(BF16),
        'a2': jnp.pad(a2_iclr[l], ((LORA_W, 0), (0, 0))).astype(BF16),
        'g2': g2_gate[l].astype(BF16),
        'k_k': row(k_k[l]), 'k_a': row(k_a[l]), 'r_k': row(r_k[l]), 'lnx_w': row(lnx_w[l]), 'lnx_b': row(lnx_b[l]),
        'conv_w': conv_w[l].astype(F32),
        'a_log': lane_pad(a_log[l], H_G), 'dt_bias': lane_pad(dt_bias[l], H_G),
        'gdn_norm_w': row(gdn_norm_w[l]),
        'w_out_r': w_out[l][:W_RWKV].astype(BF16), 'w_out_g': w_out[l][W_RWKV:].astype(BF16),
        'norm_cross': row(norm_cross[l]), 'norm_mem': row(norm_mem[l]),
        'wq': wq_x[l].astype(BF16), 'wk': wk_x[l].astype(BF16), 'wv': wv_x[l].astype(BF16),
        'wo': wo_x[l].astype(BF16),
        'norm_ffn': row(norm_ffn[l]),
        'router_w': jnp.concatenate(_split2(jnp.pad(router_w[l].astype(F32), ((0, 0), (0, LANES - N_EXPERTS)))),
                                    axis=1),
        'router_b': jnp.full((1, LANES), NEG_BIG, F32).at[0, :N_EXPERTS].set(router_b[l].astype(F32)),
        'w1_e': w1_e[l], 'b1_e': b1_e[l][:, None, :], 'w2_e': w2_e[l], 'b2_e': b2_e[l][:, None, :],
    }


def _mix_and_attend(x, mk, mv, shift_prev, s_r, conv_buf, s_g, p, *, chunk, rwkv_bb, gdn_bb, tm, bb, tq):
    b, t, d = x.shape
    assert t >= CONV_W - 1
    pr, pg = _norm_proj(x.reshape(b * t, d), p['norm_mix'], [p['w_in_r'], p['w_in_g']], [F32, F32], tm)
    pr = pr.reshape(b, t, R_PROJ)
    pg = pg.reshape(b, t, G_PROJ_PAD)
    shift_new = pr[:, t - 1]
    conv_new = pg[:, t - (CONV_W - 1):, :GDN_CONV_DIM]
    t_pad = -(-t // chunk) * chunk
    if t_pad != t:
        pr = jnp.pad(pr, ((0, 0), (0, t_pad - t), (0, 0)))
        pg = jnp.pad(pg, ((0, 0), (0, t_pad - t), (0, 0)))
    t_valid = chunk if t_pad == t else t
    y_r, s_r_new = _rwkv_mix(pr, shift_prev, s_r, p, chunk, t_valid, rwkv_bb)
    y_g, s_g_new = _gdn_mix(pg, conv_buf, s_g, p, chunk, t_valid, gdn_bb)
    if t_pad != t:
        y_r, y_g = y_r[:, :t], y_g[:, :t]
    x1, q = _out_q(x.reshape(b * t, d), y_r.reshape(b * t, W_RWKV), y_g.reshape(b * t, W_GDN), p, tm)
    x2, h, top_i, gates = _attn_route(q.reshape(b, t, d), x1.reshape(b, t, d), mk, mv, p, bb, tq)
    return (x2.reshape(b * t, d), h, top_i.reshape(b * t, LANES), gates.reshape(b * t, LANES),
            shift_new, s_r_new, conv_new, s_g_new)


def kernel(x_prompt, x_sample, mem_prompt, state_rwkv, state_rwkv_shift, state_gdn, state_gdn_conv, cache_mem_k, cache_mem_v, norm_mix, w_in, mu_shift, w0, w2_decay, a0, a2_iclr, g2_gate, k_k, k_a, r_k, lnx_w, lnx_b, conv_w, a_log, dt_bias, gdn_norm_w, w_out, norm_cross, norm_mem, wq_x, wk_x, wv_x, wo_x, norm_ffn, router_w, router_b, w1_e, b1_e, w2_e, b2_e, final_norm):
    bp, tp, d = x_prompt.shape
    bs, ts, _ = x_sample.shape
    depth = w_in.shape[0]
    np_, ns = bp * tp, bs * ts
    xp, xs = x_prompt, x_sample
    outs = [[] for _ in range(10)]
    for l in range(depth):
        p = _layer_params(l, norm_mix, w_in, mu_shift, w0, w2_decay, a0, a2_iclr, g2_gate, k_k, k_a, r_k, lnx_w,
                          lnx_b, conv_w, a_log, dt_bias, gdn_norm_w, w_out, norm_cross, norm_mem, wq_x, wk_x, wv_x,
                          wo_x, norm_ffn, router_w, router_b, w1_e, b1_e, w2_e, b2_e)
        n_mem = mem_prompt.shape[1]
        mk, mv = _norm_proj(mem_prompt.reshape(bp * n_mem, d), p['norm_mem'], [p['wk'], p['wv']], [F32, F32], 256)
        mk, mv = mk.reshape(bp, n_mem, d), mv.reshape(bp, n_mem, d)
        res_p = _mix_and_attend(
            xp, mk, mv, jnp.zeros((bp, R_PROJ), F32), jnp.zeros((bp, H_R, N_R, N_R), F32),
            jnp.zeros((bp, CONV_W - 1, GDN_CONV_DIM), F32), jnp.zeros((bp, H_G, D_G, D_G), F32), p,
            chunk=MIX_CHUNK, rwkv_bb=4, gdn_bb=8, tm=512, bb=1, tq=1024)
        head_rows = lambda c: c.reshape(bs, n_mem, H_X, D_X // LANES, LANES).transpose(0, 1, 3, 2, 4).reshape(-1, LANES)
        mk_s, mv_s = head_rows(cache_mem_k[l]), head_rows(cache_mem_v[l])
        res_s = _mix_and_attend(
            xs, mk_s, mv_s, state_rwkv_shift[l], state_rwkv[l], state_gdn_conv[l],
            state_gdn[l], p, chunk=SUBLANES, rwkv_bb=8, gdn_bb=8, tm=256, bb=8, tq=ts)
        h = jnp.concatenate([res_p[1], res_s[1]], axis=0)
        top_i = jnp.concatenate([res_p[2], res_s[2]], axis=0)[:, :TOP_K]
        gtok, sdst, block_e, block_live = _route_plan(top_i, MOE_ROWS)
        slots = _moe(h, gtok, sdst, block_e, block_live, p, MOE_ROWS)
        gn = final_norm.reshape(1, d).astype(F32)
        last = l == depth - 1
        xp = _combine(res_p[0], res_p[3], slots, gn, 512, np_ + ns, 0, last).reshape(bp, tp, d)
        xs = _combine(res_s[0], res_s[3], slots, gn, 512, np_ + ns, np_, last).reshape(bs, ts, d)
        new = [res_p[5], res_p[4], res_p[7], res_p[6], mk.reshape(bp, n_mem, H_X, D_X),
               mv.reshape(bp, n_mem, H_X, D_X), res_s[5], res_s[4], res_s[7], res_s[6]]
        for acc, val in zip(outs, new):
            acc.append(val)
    return (xp, xs) + tuple(jnp.stack(o) for o in outs)
```

```python
import functools

import jax
import jax.numpy as jnp
from jax import lax
from jax.experimental import pallas as pl
from jax.experimental.pallas import tpu as pltpu

F32 = jnp.float32
BF16 = jnp.bfloat16
DEFAULT = lax.Precision.DEFAULT

D_MODEL = 1024
W_RWKV = 512
N_R = 64
H_R = W_RWKV // N_R
LORA_W = 64
LORA_A = 64
LORA_G = 128
R_PROJ = 3 * W_RWKV + LORA_W + LORA_A + LORA_G
GN_EPS = 64e-5
W_GDN = 512
D_G = 128
H_G = W_GDN // D_G
GDN_CONV_DIM = 3 * W_GDN
CONV_W = 4
G_PROJ = GDN_CONV_DIM + W_GDN + 2 * H_G
LANES = 128
SUBLANES = 8
G_PROJ_PAD = GDN_CONV_DIM + W_GDN + LANES
MIX_CHUNK = 64
N_MEM = 256
H_X = 4
D_X = D_MODEL // H_X
N_EXPERTS = 32
TOP_K = 4
D_FF = D_MODEL
SWIGLU_LIMIT = 7.0
SWIGLU_ALPHA = 1.702
MOE_ROWS = 256
RMS_EPS = 1e-6
L2_EPS = 1e-6
NEG_BIG = -1e30
VMEM_LIMIT = 56 * 1024 * 1024


def _dot(a, b, precision=DEFAULT):
    return jnp.dot(a, b, preferred_element_type=F32, precision=precision)


def _dot_nt(a, b, precision=DEFAULT):
    return lax.dot_general(a, b, (((1,), (1,)), ((), ())), preferred_element_type=F32, precision=precision)


NN = (((1,), (0,)), ((), ()))
NT = (((1,), (1,)), ((), ()))
TN = (((0,), (0,)), ((), ()))


def _split2(x):
    hi = x.astype(BF16)
    return hi, (x - hi.astype(F32)).astype(BF16)


def _mm(a, b, dims):
    return lax.dot_general(a.astype(BF16), b.astype(BF16), dims, preferred_element_type=F32)


def _sel_mm(sel, x, dims):
    dg = lambda y: lax.dot_general(sel, y, dims, preferred_element_type=F32)
    hi = x.astype(BF16)
    r1 = x - hi.astype(F32)
    mid = r1.astype(BF16)
    lo = (r1 - mid.astype(F32)).astype(BF16)
    return dg(hi) + (dg(mid) + dg(lo))


def _sigmoid(x):
    return 1.0 / (1.0 + jnp.exp(-x))


def _softplus(x):
    return jnp.maximum(x, 0.0) + jnp.log(1.0 + jnp.exp(-jnp.abs(x)))


def _rmsnorm(x, g):
    return x * lax.rsqrt(jnp.mean(x * x, axis=-1, keepdims=True) + RMS_EPS) * g


def _tri_masks(c):
    row = lax.broadcasted_iota(jnp.int32, (c, c), 0)
    col = lax.broadcasted_iota(jnp.int32, (c, c), 1)
    return col <= row, col < row, (col == row).astype(F32)


def _unit_lower_inverses(ms, eye, c):
    ts = [eye + m for m in ms]
    ps = list(ms)
    covered = 2
    while covered < c:
        ps = [_mm(p, p, NN) for p in ps]
        ts = [t + _mm(t, p, NN) for t, p in zip(ts, ps)]
        covered *= 2
    return ts


def _norm_proj_body(x_ref, g_ref, *refs, n_out):
    w_refs, o_refs = refs[:n_out], refs[n_out:]
    hb = _rmsnorm(x_ref[...], g_ref[...]).astype(BF16)
    for w_ref, o_ref in zip(w_refs, o_refs):
        o_ref[...] = _dot(hb, w_ref[...]).astype(o_ref.dtype)


def _norm_proj(x, g, ws, out_dtypes, tm):
    n, d = x.shape
    assert n % tm == 0
    in_specs = [pl.BlockSpec((tm, d), lambda i: (i, 0)), pl.BlockSpec((1, d), lambda i: (0, 0))]
    in_specs += [pl.BlockSpec(w.shape, lambda i: (0, 0)) for w in ws]
    return pl.pallas_call(
        functools.partial(_norm_proj_body, n_out=len(ws)),
        grid=(n // tm,),
        in_specs=in_specs,
        out_specs=[pl.BlockSpec((tm, w.shape[1]), lambda i: (i, 0)) for w in ws],
        out_shape=[jax.ShapeDtypeStruct((n, w.shape[1]), dt) for w, dt in zip(ws, out_dtypes)],
        compiler_params=pltpu.CompilerParams(dimension_semantics=("parallel",), vmem_limit_bytes=VMEM_LIMIT),
        name="norm_proj",
    )(x, g, *ws)


def _rwkv_body(pr_ref, shift_ref, s0_ref, mu_ref, w0_ref, w2_ref, a0_ref, a2_ref, g2_ref, kk_ref, ka_ref,
               rk_ref, lnw_ref, lnb_ref, y_ref, sout_ref, s_scr, prev_scr, *, bb, chunk, t_valid, n_chunks):
    c = pl.program_id(1)

    @pl.when(c == 0)
    def _init():
        s_scr[...] = s0_ref[...]
        prev_scr[...] = shift_ref[...]

    incl, strict, eye = _tri_masks(chunk)
    incl_bf = incl.astype(BF16)
    rows = lax.broadcasted_iota(jnp.int32, (chunk, 1), 0)
    sls = [slice(h * N_R, (h + 1) * N_R) for h in range(H_R)]

    def per_batch(bi):
        pr = pr_ref[bi]
        prev = jnp.where(rows == 0, prev_scr[bi], pltpu.roll(pr, 1, 0))
        prev_scr[bi] = pr[chunk - 1:chunk, :]
        xm = pr + (prev - pr) * mu_ref[...]
        r = xm[:, :W_RWKV]
        k = xm[:, W_RWKV:2 * W_RWKV]
        v = xm[:, 2 * W_RWKV:3 * W_RWKV]
        lo = xm[:, 3 * W_RWKV:3 * W_RWKV + LORA_W + LORA_A]
        g_lo = xm[:, 3 * W_RWKV + LORA_W + LORA_A:]
        logw = -_softplus(-(w0_ref[...] + _dot(jnp.tanh(lo).astype(BF16), w2_ref[...]))) - 0.5
        wl = -jnp.exp(logw)
        a = _sigmoid(a0_ref[...] + _dot(lo.astype(BF16), a2_ref[...]))
        g = _dot(_sigmoid(g_lo).astype(BF16), g2_ref[...])
        kkv = k * kk_ref[...]
        k = k * (1.0 + (a - 1.0) * ka_ref[...])
        if t_valid < chunk:
            valid = rows < t_valid
            wl = jnp.where(valid, wl, 0.0)
            kkv = jnp.where(valid, kkv, 0.0)
            k = jnp.where(valid, k, 0.0)
            v = jnp.where(valid, v, 0.0)
        cum = _sel_mm(incl_bf, wl, NN)
        cum_last = cum[chunk - 1:chunk, :]
        return dict(r=r, k=k, v=v, a=a, g=g, kkv=kkv, w_incl=jnp.exp(cum), w_prev=jnp.exp(cum - wl),
                    w_inv=jnp.exp(-cum), w_tail=jnp.exp(cum_last - cum), w_last=jnp.exp(cum_last))

    pre = [per_batch(bi) for bi in range(bb)]
    chains = [(bi, h) for bi in range(bb) for h in range(H_R)]
    col = lambda name: [pre[bi][name][:, sls[h]] for bi, h in chains]
    n = range(len(chains))
    r_, k_, v_, a_ = col('r'), col('k'), col('v'), col('a')
    w_incl, w_prev, w_inv, w_tail, w_last = col('w_incl'), col('w_prev'), col('w_inv'), col('w_tail'), col('w_last')
    kks = [x * lax.rsqrt(jnp.sum(x * x, axis=-1, keepdims=True) + L2_EPS) for x in col('kkv')]
    a_hat = [-(kks[i] * w_prev[i]) for i in n]
    kka = [kks[i] * a_[i] for i in n]
    b_hat = [kka[i] * w_inv[i] for i in n]
    k_hat = [k_[i] * w_inv[i] for i in n]
    r_hat = [r_[i] * w_incl[i] for i in n]
    cross = [_mm(jnp.concatenate([a_hat[i], r_hat[i]], axis=0), jnp.concatenate([b_hat[i], k_hat[i]], axis=0), NT)
             for i in n]
    row2 = lax.broadcasted_iota(jnp.int32, (chunk, 2 * chunk), 0)
    col2 = lax.broadcasted_iota(jnp.int32, (chunk, 2 * chunk), 1)
    col2 = jnp.where(col2 >= chunk, col2 - chunk, col2)
    strict2, incl2 = col2 < row2, col2 <= row2
    m_top = [jnp.where(strict2, x[:chunk], 0.0) for x in cross]
    a_bot = [jnp.where(incl2, x[chunk:], 0.0) for x in cross]
    t_inv = _unit_lower_inverses([x[:, :chunk] for x in m_top], eye, chunk)
    w_hat = [_mm(t_inv[i], a_hat[i], NN) for i in n]
    mv = [_mm(m_top[i], jnp.concatenate([jnp.zeros_like(v_[i]), v_[i]], axis=0), NN) for i in n]
    u = [_mm(t_inv[i], mv[i], NN) for i in n]
    ss = [s_scr[bi, h] for bi, h in chains]
    p = [_mm(w_hat[i], ss[i], NT) + u[i] for i in n]
    pv = [jnp.concatenate([p[i], v_[i]], axis=0) for i in n]
    ys = [_mm(r_hat[i], ss[i], NT) + _mm(a_bot[i], pv[i], NN) for i in n]
    for i, (bi, h) in enumerate(chains):
        tails = jnp.concatenate([kka[i] * w_tail[i], k_[i] * w_tail[i]], axis=0)
        s_scr[bi, h] = ss[i] * w_last[i] + _mm(pv[i], tails, TN)
    for i, (bi, h) in enumerate(chains):
        sl = sls[h]
        y = ys[i]
        mean = jnp.mean(y, axis=-1, keepdims=True)
        yc = y - mean
        var = jnp.mean(yc * yc, axis=-1, keepdims=True)
        yn = yc * lax.rsqrt(var + GN_EPS) * lnw_ref[:, sl] + lnb_ref[:, sl]
        bonus = jnp.sum(r_[i] * k_[i] * rk_ref[:, sl], axis=-1, keepdims=True) * v_[i]
        y_ref[bi, :, sl] = ((yn + bonus) * pre[bi]['g'][:, sl]).astype(y_ref.dtype)

    @pl.when(c == n_chunks - 1)
    def _fin():
        sout_ref[...] = s_scr[...]


def _rwkv_mix(pr, shift_prev, s0, p, chunk, t_valid, bb):
    b, t, _ = pr.shape
    assert b % bb == 0 and t % chunk == 0
    n_chunks = t // chunk
    const = lambda shape: pl.BlockSpec(shape, lambda i, j: (0,) * len(shape))
    return pl.pallas_call(
        functools.partial(_rwkv_body, bb=bb, chunk=chunk, t_valid=t_valid, n_chunks=n_chunks),
        grid=(b // bb, n_chunks),
        in_specs=[
            pl.BlockSpec((bb, chunk, R_PROJ), lambda i, j: (i, j, 0)),
            pl.BlockSpec((bb, 1, R_PROJ), lambda i, j: (i, 0, 0)),
            pl.BlockSpec((bb, H_R, N_R, N_R), lambda i, j: (i, 0, 0, 0)),
            const((1, R_PROJ)), const((1, W_RWKV)), const((LANES, W_RWKV)), const((1, W_RWKV)),
            const((LANES, W_RWKV)), const((LORA_G, W_RWKV)), const((1, W_RWKV)), const((1, W_RWKV)),
            const((1, W_RWKV)), const((1, W_RWKV)), const((1, W_RWKV)),
        ],
        out_specs=[
            pl.BlockSpec((bb, chunk, W_RWKV), lambda i, j: (i, j, 0)),
            pl.BlockSpec((bb, H_R, N_R, N_R), lambda i, j: (i, 0, 0, 0)),
        ],
        out_shape=[jax.ShapeDtypeStruct((b, t, W_RWKV), BF16), jax.ShapeDtypeStruct((b, H_R, N_R, N_R), F32)],
        scratch_shapes=[pltpu.VMEM((bb, H_R, N_R, N_R), F32), pltpu.VMEM((bb, 1, R_PROJ), F32)],
        compiler_params=pltpu.CompilerParams(dimension_semantics=("arbitrary", "arbitrary"),
                                             vmem_limit_bytes=VMEM_LIMIT),
        name="rwkv_mix",
    )(pr, shift_prev[:, None, :], s0, p['mu'], p['w0'], p['w2'], p['a0'], p['a2'], p['g2'], p['k_k'], p['k_a'],
      p['r_k'], p['lnx_w'], p['lnx_b'])


def _gdn_body(pg_ref, cbuf_ref, s0_ref, cw_ref, alog_ref, dtb_ref, nw_ref, y_ref, sout_ref, s_scr, xp_scr,
              *, bb, chunk, t_valid, n_chunks):
    c = pl.program_id(1)

    @pl.when(c == 0)
    def _init():
        s_scr[...] = s0_ref[...]
        xp_scr[:, 0:SUBLANES, :] = cbuf_ref[...]

    incl, strict, eye = _tri_masks(chunk)
    incl_bf = incl.astype(BF16)
    lane = lax.broadcasted_iota(jnp.int32, (chunk, LANES), 1)
    sls = [slice(h * D_G, (h + 1) * D_G) for h in range(H_G)]

    def per_batch(bi):
        xp_scr[bi, SUBLANES:SUBLANES + chunk, :] = pg_ref[bi, :, :GDN_CONV_DIM]
        base = SUBLANES - (CONV_W - 1)
        conv = xp_scr[bi, base:base + chunk, :] * cw_ref[0:1, :]
        for j in range(1, CONV_W):
            conv = conv + xp_scr[bi, base + j:base + j + chunk, :] * cw_ref[j:j + 1, :]
        xp_scr[bi, 0:SUBLANES, :] = xp_scr[bi, chunk:chunk + SUBLANES, :]
        qkv = conv * _sigmoid(conv)
        z = pg_ref[bi, :, GDN_CONV_DIM:GDN_CONV_DIM + W_GDN]
        ba = pg_ref[bi, :, GDN_CONV_DIM + W_GDN:]
        beta_blk = _sigmoid(ba)
        g_blk = -jnp.exp(alog_ref[...]) * _softplus(ba + dtb_ref[...])
        if t_valid < chunk:
            valid = lax.broadcasted_iota(jnp.int32, (chunk, 1), 0) < t_valid
            beta_blk = jnp.where(valid, beta_blk, 0.0)
            g_blk = jnp.where(valid, g_blk, 0.0)
        gc_blk = _sel_mm(incl_bf, g_blk, NN)
        return dict(qkv=qkv, z=z, beta_blk=beta_blk, gc_blk=gc_blk)

    pre = [per_batch(bi) for bi in range(bb)]
    chains = [(bi, h) for bi in range(bb) for h in range(H_G)]
    n = range(len(chains))
    qs = [pre[bi]['qkv'][:, sls[h]] for bi, h in chains]
    qs = [x * lax.rsqrt(jnp.sum(x * x, axis=-1, keepdims=True) + L2_EPS) * (D_G ** -0.5) for x in qs]
    ks = [pre[bi]['qkv'][:, W_GDN + h * D_G:W_GDN + (h + 1) * D_G] for bi, h in chains]
    ks = [x * lax.rsqrt(jnp.sum(x * x, axis=-1, keepdims=True) + L2_EPS) for x in ks]
    vs = [pre[bi]['qkv'][:, 2 * W_GDN + h * D_G:2 * W_GDN + (h + 1) * D_G] for bi, h in chains]
    betas = [pre[bi]['beta_blk'][:, h:h + 1] for bi, h in chains]
    gcols = [pre[bi]['gc_blk'][:, H_G + h:H_G + h + 1] for bi, h in chains]
    grows = [_sel_mm((lane == H_G + h).astype(BF16), pre[bi]['gc_blk'], NT) for bi, h in chains]
    g_last = [gcols[i][chunk - 1:chunk, :] for i in n]
    decay = [jnp.where(incl, jnp.exp(jnp.where(incl, gcols[i] - grows[i], 0.0)), 0.0) for i in n]
    k_beta = [ks[i] * betas[i] for i in n]
    cross = [_mm(jnp.concatenate([k_beta[i], qs[i]], axis=0), ks[i], NT) for i in n]
    lmat = [jnp.where(strict, cross[i][:chunk] * decay[i], 0.0) for i in n]
    attn = [jnp.where(incl, cross[i][chunk:] * decay[i], 0.0) for i in n]
    t_inv = _unit_lower_inverses([-x for x in lmat], eye, chunk)
    e_gc = [jnp.exp(gcols[i]) for i in n]
    uw = [_mm(t_inv[i], jnp.concatenate([vs[i] * betas[i], k_beta[i] * e_gc[i]], axis=1), NN) for i in n]
    ss = [s_scr[bi, h] for bi, h in chains]
    v_new = [uw[i][:, :D_G] - _mm(uw[i][:, D_G:], ss[i], NN) for i in n]
    os_ = [_mm(jnp.concatenate([qs[i] * e_gc[i], attn[i]], axis=1), jnp.concatenate([ss[i], v_new[i]], axis=0), NN)
           for i in n]
    for i, (bi, h) in enumerate(chains):
        s_scr[bi, h] = ss[i] * jnp.exp(g_last[i]) + _mm(ks[i] * jnp.exp(g_last[i] - gcols[i]), v_new[i], TN)
    for i, (bi, h) in enumerate(chains):
        o = os_[i]
        o = o * lax.rsqrt(jnp.mean(o * o, axis=-1, keepdims=True) + RMS_EPS) * nw_ref[...]
        z_h = pre[bi]['z'][:, sls[h]]
        y_ref[bi, :, sls[h]] = (o * (z_h * _sigmoid(z_h))).astype(y_ref.dtype)

    @pl.when(c == n_chunks - 1)
    def _fin():
        sout_ref[...] = s_scr[...]


def _gdn_mix(pg, conv_buf, s0, p, chunk, t_valid, bb):
    b, t, _ = pg.shape
    assert b % bb == 0 and t % chunk == 0
    n_chunks = t // chunk
    cbuf = jnp.pad(conv_buf, ((0, 0), (SUBLANES - (CONV_W - 1), 0), (0, 0)))
    const = lambda shape: pl.BlockSpec(shape, lambda i, j: (0,) * len(shape))
    return pl.pallas_call(
        functools.partial(_gdn_body, bb=bb, chunk=chunk, t_valid=t_valid, n_chunks=n_chunks),
        grid=(b // bb, n_chunks),
        in_specs=[
            pl.BlockSpec((bb, chunk, G_PROJ_PAD), lambda i, j: (i, j, 0)),
            pl.BlockSpec((bb, SUBLANES, GDN_CONV_DIM), lambda i, j: (i, 0, 0)),
            pl.BlockSpec((bb, H_G, D_G, D_G), lambda i, j: (i, 0, 0, 0)),
            const((CONV_W, GDN_CONV_DIM)), const((1, LANES)), const((1, LANES)), const((1, D_G)),
        ],
        out_specs=[
            pl.BlockSpec((bb, chunk, W_GDN), lambda i, j: (i, j, 0)),
            pl.BlockSpec((bb, H_G, D_G, D_G), lambda i, j: (i, 0, 0, 0)),
        ],
        out_shape=[jax.ShapeDtypeStruct((b, t, W_GDN), BF16), jax.ShapeDtypeStruct((b, H_G, D_G, D_G), F32)],
        scratch_shapes=[pltpu.VMEM((bb, H_G, D_G, D_G), F32),
                        pltpu.VMEM((bb, SUBLANES + chunk, GDN_CONV_DIM), F32)],
        compiler_params=pltpu.CompilerParams(dimension_semantics=("arbitrary", "arbitrary"),
                                             vmem_limit_bytes=VMEM_LIMIT),
        name="gdn_mix",
    )(pg, cbuf, s0, p['conv_w'], p['a_log'], p['dt_bias'], p['gdn_norm_w'])


def _out_q_body(x_ref, yr_ref, yg_ref, wor_ref, wog_ref, gn_ref, wq_ref, x1_ref, q_ref):
    x1 = x_ref[...] + _dot(yr_ref[...], wor_ref[...]) + _dot(yg_ref[...], wog_ref[...])
    x1_ref[...] = x1
    q_ref[...] = _dot(_rmsnorm(x1, gn_ref[...]).astype(BF16), wq_ref[...]).astype(q_ref.dtype)


def _out_q(x, yr, yg, p, tm):
    n, d = x.shape
    assert n % tm == 0
    row = lambda w: pl.BlockSpec((tm, w), lambda i: (i, 0))
    const = lambda shape: pl.BlockSpec(shape, lambda i: (0, 0))
    return pl.pallas_call(
        _out_q_body,
        grid=(n // tm,),
        in_specs=[row(d), row(W_RWKV), row(W_GDN), const((W_RWKV, d)), const((W_GDN, d)), const((1, d)),
                  const((d, d))],
        out_specs=[row(d), row(d)],
        out_shape=[jax.ShapeDtypeStruct((n, d), F32), jax.ShapeDtypeStruct((n, d), BF16)],
        compiler_params=pltpu.CompilerParams(dimension_semantics=("parallel",), vmem_limit_bytes=VMEM_LIMIT),
        name="out_q",
    )(x, yr, yg, p['w_out_r'], p['w_out_g'], p['norm_cross'], p['wq'])


def _attn_body(q_ref, x_ref, mk_ref, mv_ref, wo_ref, gn_ref, rw_ref, rb_ref, x2_ref, h_ref, ti_ref, gt_ref,
               *, bb, tq, head_rows):
    d_tiles = D_X // LANES
    mem_rows = N_MEM * d_tiles * H_X

    def head_mem(ref, i, hh):
        if not head_rows:
            return ref[i, :, hh * D_X:(hh + 1) * D_X]
        return jnp.concatenate(
            [ref[pl.ds(i * mem_rows + dt * H_X + hh, N_MEM, stride=d_tiles * H_X), :] for dt in range(d_tiles)],
            axis=-1)

    os_ = []
    for i in range(bb):
        q = q_ref[i]
        heads = []
        for hh in range(H_X):
            sl = slice(hh * D_X, (hh + 1) * D_X)
            s = _dot_nt(q[:, sl], head_mem(mk_ref, i, hh).astype(BF16)) * (D_X ** -0.5)
            e = jnp.exp(s - jnp.max(s, axis=-1, keepdims=True))
            prob = e / jnp.sum(e, axis=-1, keepdims=True)
            heads.append(_dot(prob.astype(BF16), head_mem(mv_ref, i, hh).astype(BF16)))
        os_.append(jnp.concatenate(heads, axis=-1))
    rows = bb * tq
    o = (os_[0] if bb == 1 else jnp.concatenate(os_, axis=0)).astype(BF16)
    x1 = x_ref[0] if bb == 1 else jnp.concatenate([x_ref[i] for i in range(bb)], axis=0)
    x2 = x1 + _dot(o, wo_ref[...])
    h = _rmsnorm(x2, gn_ref[...])
    for sub in range(SUBLANES):
        h_ref[pl.ds(sub, rows, stride=SUBLANES), :] = h[:, sub * LANES:(sub + 1) * LANES]
    h_hi, h_lo = _split2(h)
    pieces = _dot(jnp.concatenate([h_hi, h_lo], axis=0), rw_ref[...])
    logits = ((pieces[:rows, :LANES] + pieces[:rows, LANES:]) + (pieces[rows:, :LANES] + pieces[rows:, LANES:])
              + rb_ref[...])
    lane = lax.broadcasted_iota(jnp.int32, logits.shape, 1)
    vals, idxs = [], []
    for _ in range(TOP_K):
        m = jnp.max(logits, axis=-1, keepdims=True)
        first = jnp.min(jnp.where(logits == m, lane, LANES), axis=-1, keepdims=True)
        vals.append(m)
        idxs.append(first)
        logits = jnp.where(lane == first, -jnp.inf, logits)
    es = [jnp.exp(vv - vals[0]) for vv in vals]
    den = es[0] + es[1] + es[2] + es[3]
    ti = jnp.zeros(lane.shape, jnp.int32)
    gt = jnp.zeros(lane.shape, F32)
    for j in range(TOP_K):
        ti = jnp.where(lane == j, idxs[j], ti)
        gt = jnp.where(lane == j, es[j] / den, gt)
    for i in range(bb):
        x2_ref[i] = x2[i * tq:(i + 1) * tq]
        ti_ref[i] = ti[i * tq:(i + 1) * tq]
        gt_ref[i] = gt[i * tq:(i + 1) * tq]


def _attn_route(q, x1, mk, mv, p, bb, tq):
    b, t, d = x1.shape
    assert d == SUBLANES * LANES
    rows = bb * tq
    assert b % bb == 0 and t % tq == 0 and (bb == 1 or tq == t)
    n_tq = t // tq
    blk = lambda w: pl.BlockSpec((bb, tq, w), lambda i, j: (i, j, 0))
    head_rows = mk.ndim == 2
    mem = (pl.BlockSpec((bb * (mk.shape[0] // b), LANES), lambda i, j: (i, 0)) if head_rows
           else pl.BlockSpec((bb, N_MEM, d), lambda i, j: (i, 0, 0)))
    const = lambda shape: pl.BlockSpec(shape, lambda i, j: (0, 0))
    return pl.pallas_call(
        functools.partial(_attn_body, bb=bb, tq=tq, head_rows=head_rows),
        grid=(b // bb, n_tq),
        in_specs=[blk(d), blk(d), mem, mem, const((d, d)), const((1, d)), const((d, 2 * LANES)), const((1, LANES))],
        out_specs=[blk(d), pl.BlockSpec((rows * SUBLANES, LANES), lambda i, j: (i * n_tq + j, 0)), blk(LANES),
                   blk(LANES)],
        out_shape=[jax.ShapeDtypeStruct((b, t, d), F32), jax.ShapeDtypeStruct((b * t * SUBLANES, LANES), F32),
                   jax.ShapeDtypeStruct((b, t, LANES), jnp.int32), jax.ShapeDtypeStruct((b, t, LANES), F32)],
        compiler_params=pltpu.CompilerParams(dimension_semantics=("parallel", "parallel"),
                                             vmem_limit_bytes=VMEM_LIMIT),
        name="attn_route",
    )(q, x1, mk, mv, p['wo'], p['norm_ffn'], p['router_w'], p['router_b'])


def _moe_body(be_ref, live_ref, gnext_ref, sprev_ref, h_hbm, w1_ref, b1_ref, w2_ref, b2_ref, out_hbm,
              xbuf, ybuf, w1b, w2b, gsem, ssem, *, tm, n_steps):
    k = pl.program_id(0)

    def row(ref, slot, at):
        return ref.at[slot, pl.ds(at, SUBLANES)]

    def gather_all(slot):
        return pltpu.make_async_copy(h_hbm.at[pl.ds(0, tm * SUBLANES)], xbuf.at[slot], gsem.at[slot])

    def scatter_all(slot):
        return pltpu.make_async_copy(ybuf.at[slot], out_hbm.at[pl.ds(0, tm * SUBLANES)], ssem.at[slot])

    live = lambda blk: live_ref[jnp.clip(blk, 0, n_steps - 1)] > 0

    @pl.when(k == 0)
    def _init_spare():
        ybuf[1] = jnp.zeros(ybuf.shape[1:], F32)
        spare = pltpu.make_async_copy(ybuf.at[1], out_hbm.at[pl.ds(out_hbm.shape[0] - tm * SUBLANES, tm * SUBLANES)],
                                      ssem.at[1])
        spare.start()
        spare.wait()

    @pl.when(jnp.logical_and(k == 0, live(0)))
    def _prologue():
        for r in range(tm):
            pltpu.make_async_copy(h_hbm.at[pl.ds(0, SUBLANES)], row(xbuf, 0, r * SUBLANES), gsem.at[0]).start()

    @pl.when(jnp.logical_or(k == 0, be_ref[k] != be_ref[jnp.maximum(k - 1, 0)]))
    def _cast():
        w1b[...] = w1_ref[0].astype(BF16)
        w2b[...] = w2_ref[0].astype(BF16)

    def phase(cur):
        nxt = 1 - cur

        @pl.when(live(k))
        def _():
            gather_all(cur).wait()

        @pl.when(jnp.logical_and(k >= 2, live(k - 2)))
        def _():
            scatter_all(cur).wait()

        @pl.when(jnp.logical_and(k + 1 < n_steps, live(k + 1)))
        def _():
            for r in range(tm):
                src = pl.multiple_of(gnext_ref[0, 0, r], SUBLANES)
                pltpu.make_async_copy(h_hbm.at[pl.ds(src, SUBLANES)], row(xbuf, nxt, r * SUBLANES),
                                      gsem.at[nxt]).start()

        @pl.when(jnp.logical_and(k >= 1, live(k - 1)))
        def _():
            for r in range(tm):
                dst = pl.multiple_of(sprev_ref[0, 0, r], SUBLANES)
                pltpu.make_async_copy(row(ybuf, nxt, r * SUBLANES), out_hbm.at[pl.ds(dst, SUBLANES)],
                                      ssem.at[nxt]).start()

        @pl.when(live(k))
        def _compute():
            x = jnp.concatenate([xbuf[cur, pl.ds(sub, tm, stride=SUBLANES), :] for sub in range(SUBLANES)],
                                axis=-1)
            hc = _dot(x.astype(BF16), w1b[...]) + b1_ref[0]
            hg = jnp.minimum(hc[:, :D_FF], SWIGLU_LIMIT)
            hl = jnp.clip(hc[:, D_FF:], -SWIGLU_LIMIT, SWIGLU_LIMIT)
            act = hg * _sigmoid(SWIGLU_ALPHA * hg) * (hl + 1.0)
            y = _dot(act.astype(BF16), w2b[...]) + b2_ref[0]
            for sub in range(SUBLANES):
                ybuf[cur, pl.ds(sub, tm, stride=SUBLANES), :] = y[:, sub * LANES:(sub + 1) * LANES]

        @pl.when(jnp.logical_and(k == n_steps - 1, live(k - 1)))
        def _epilogue():
            scatter_all(nxt).wait()

    for parity in range(2):
        pl.when(lax.rem(k, 2) == parity)(functools.partial(phase, parity))


def _moe(h, gtok, sdst, block_e, block_live, p, tm):
    n, d = h.shape[0] // SUBLANES, D_MODEL
    n_steps = block_e.shape[0]
    grid_spec = pltpu.PrefetchScalarGridSpec(
        num_scalar_prefetch=2,
        grid=(n_steps,),
        in_specs=[
            pl.BlockSpec((1, 1, tm), lambda k, be, live:(jnp.minimum(k + 1, n_steps - 1), 0, 0),
                         memory_space=pltpu.SMEM),
            pl.BlockSpec((1, 1, tm), lambda k, be, live:(jnp.maximum(k - 1, 0), 0, 0), memory_space=pltpu.SMEM),
            pl.BlockSpec(memory_space=pl.ANY),
            pl.BlockSpec((1, d, 2 * D_FF), lambda k, be, live:(be[k], 0, 0)),
            pl.BlockSpec((1, 1, 2 * D_FF), lambda k, be, live:(be[k], 0, 0)),
            pl.BlockSpec((1, D_FF, d), lambda k, be, live:(be[k], 0, 0)),
            pl.BlockSpec((1, 1, d), lambda k, be, live:(be[k], 0, 0)),
        ],
        out_specs=pl.BlockSpec(memory_space=pl.ANY),
        scratch_shapes=[
            pltpu.VMEM((2, tm * SUBLANES, LANES), F32), pltpu.VMEM((2, tm * SUBLANES, LANES), F32),
            pltpu.VMEM((d, 2 * D_FF), BF16), pltpu.VMEM((D_FF, d), BF16),
            pltpu.SemaphoreType.DMA((2,)), pltpu.SemaphoreType.DMA((2,)),
        ],
    )
    return pl.pallas_call(
        functools.partial(_moe_body, tm=tm, n_steps=n_steps),
        grid_spec=grid_spec,
        out_shape=jax.ShapeDtypeStruct(((n * TOP_K + tm) * SUBLANES, LANES), F32),
        compiler_params=pltpu.CompilerParams(dimension_semantics=("arbitrary",), vmem_limit_bytes=VMEM_LIMIT),
        name="moe_experts",
    )(block_e, block_live, gtok.reshape(n_steps, 1, tm), sdst.reshape(n_steps, 1, tm), h, p['w1_e'], p['b1_e'], p['w2_e'],
      p['b2_e'])


def _route_plan(top_i, tm):
    n = top_i.shape[0]
    na = n * TOP_K
    flat_e = top_i.reshape(na)
    order = jnp.argsort(flat_e).astype(jnp.int32)
    counts = jnp.sum((flat_e[:, None] == jnp.arange(N_EXPERTS, dtype=jnp.int32)[None, :]).astype(jnp.int32), axis=0)
    padded = (counts + tm - 1) // tm * tm
    starts = jnp.cumsum(counts) - counts
    pends = jnp.cumsum(padded)
    pstarts = pends - padded
    n_steps = -(-(na + N_EXPERTS * (tm - 1)) // tm) + 2
    blk_start = (jnp.arange(n_steps, dtype=jnp.int32) - 1) * tm
    block_e = jnp.sum((blk_start[:, None] >= pends[None, :]).astype(jnp.int32), axis=1)
    block_e = jnp.minimum(block_e, N_EXPERTS - 1)
    block_e = jnp.where(blk_start < 0, block_e[1], block_e)
    is_e = block_e[:, None] == jnp.arange(N_EXPERTS, dtype=jnp.int32)[None, :]
    pick = lambda table: jnp.sum(jnp.where(is_e, table[None, :], 0), axis=1)
    lane = jnp.arange(tm, dtype=jnp.int32)[None, :]
    local = blk_start[:, None] + lane - pick(pstarts)[:, None]
    valid = (local >= 0) & (local < pick(counts)[:, None]) & (blk_start[:, None] >= 0)
    asg = order[jnp.clip(pick(starts)[:, None] + local, 0, na - 1)]
    gtok = jnp.where(valid, asg // TOP_K, lane)
    sdst = jnp.where(valid, (asg % TOP_K) * n + asg // TOP_K, TOP_K * n + lane)
    block_live = valid[:, 0].astype(jnp.int32)
    return gtok * SUBLANES, sdst * SUBLANES, block_e, block_live


def _combine_body(x_ref, gt_ref, gn_ref, *refs, final):
    slot_refs, y_ref = refs[:TOP_K], refs[TOP_K]
    x = x_ref[...]
    gt = gt_ref[...]
    tm = x.shape[0]
    for j in range(TOP_K):
        slot = jnp.concatenate([slot_refs[j][pl.ds(sub, tm, stride=SUBLANES), :] for sub in range(SUBLANES)],
                               axis=-1)
        x = x + gt[:, j:j + 1] * slot
    y_ref[...] = _rmsnorm(x, gn_ref[...]) if final else x


def _combine(x2, gates, slots, gn, tm, n_all, row0, final):
    n, d = x2.shape
    assert n % tm == 0 and row0 % tm == 0 and n_all % tm == 0
    slot_spec = lambda j: pl.BlockSpec((tm * SUBLANES, LANES), lambda i: ((j * n_all + row0) // tm + i, 0))
    return pl.pallas_call(
        functools.partial(_combine_body, final=final),
        grid=(n // tm,),
        in_specs=[pl.BlockSpec((tm, d), lambda i: (i, 0)), pl.BlockSpec((tm, LANES), lambda i: (i, 0)),
                  pl.BlockSpec((1, d), lambda i: (0, 0))] + [slot_spec(j) for j in range(TOP_K)],
        out_specs=pl.BlockSpec((tm, d), lambda i: (i, 0)),
        out_shape=jax.ShapeDtypeStruct((n, d), F32),
        compiler_params=pltpu.CompilerParams(dimension_semantics=("parallel",), vmem_limit_bytes=VMEM_LIMIT),
        name="combine",
    )(x2, gates, gn, *([slots] * TOP_K))


def _layer_params(l, norm_mix, w_in, mu_shift, w0, w2_decay, a0, a2_iclr, g2_gate, k_k, k_a, r_k, lnx_w, lnx_b,
                  conv_w, a_log, dt_bias, gdn_norm_w, w_out, norm_cross, norm_mem, wq_x, wk_x, wv_x, wo_x,
                  norm_ffn, router_w, router_b, w1_e, b1_e, w2_e, b2_e):
    row = lambda z: z.reshape(1, -1).astype(F32)
    lane_pad = lambda z, at: jnp.zeros((1, LANES), F32).at[0, at:at + z.shape[0]].set(z)
    return {
        'norm_mix': row(norm_mix[l]),
        'w_in_r': w_in[l][:, :R_PROJ].astype(BF16),
        'w_in_g': jnp.pad(w_in[l][:, R_PROJ:], ((0, 0), (0, G_PROJ_PAD - G_PROJ))).astype(BF16),
        'mu': row(mu_shift[l]), 'w0': row(w0[l]), 'a0': row(a0[l]),
        'w2': jnp.pad(w2_decay[l], ((0, LORA_A), (0, 0))).astype(BF16),
        'a2': jnp.pad(a2_iclr[l], ((LORA_W, 0), (0, 0))).astype(BF16),
        'g2': g2_gate[l].astype(BF16),
        'k_k': row(k_k[l]), 'k_a': row(k_a[l]), 'r_k': row(r_k[l]), 'lnx_w': row(lnx_w[l]), 'lnx_b': row(lnx_b[l]),
        'conv_w': conv_w[l].astype(F32),
        'a_log': lane_pad(a_log[l], H_G), 'dt_bias': lane_pad(dt_bias[l], H_G),
        'gdn_norm_w': row(gdn_norm_w[l]),
        'w_out_r': w_out[l][:W_RWKV].astype(BF16), 'w_out_g': w_out[l][W_RWKV:].astype(BF16),
        'norm_cross': row(norm_cross[l]), 'norm_mem': row(norm_mem[l]),
        'wq': wq_x[l].astype(BF16), 'wk': wk_x[l].astype(BF16), 'wv': wv_x[l].astype(BF16),
        'wo': wo_x[l].astype(BF16),
        'norm_ffn': row(norm_ffn[l]),
        'router_w': jnp.concatenate(_split2(jnp.pad(router_w[l].astype(F32), ((0, 0), (0, LANES - N_EXPERTS)))),
                                    axis=1),
        'router_b': jnp.full((1, LANES), NEG_BIG, F32).at[0, :N_EXPERTS].set(router_b[l].astype(F32)),
        'w1_e': w1_e[l], 'b1_e': b1_e[l][:, None, :], 'w2_e': w2_e[l], 'b2_e': b2_e[l][:, None, :],
    }


def _mix_and_attend(x, mk, mv, shift_prev, s_r, conv_buf, s_g, p, *, chunk, rwkv_bb, gdn_bb, tm, bb, tq):
    b, t, d = x.shape
    assert t >= CONV_W - 1
    pr, pg = _norm_proj(x.reshape(b * t, d), p['norm_mix'], [p['w_in_r'], p['w_in_g']], [F32, F32], tm)
    pr = pr.reshape(b, t, R_PROJ)
    pg = pg.reshape(b, t, G_PROJ_PAD)
    shift_new = pr[:, t - 1]
    conv_new = pg[:, t - (CONV_W - 1):, :GDN_CONV_DIM]
    t_pad = -(-t // chunk) * chunk
    if t_pad != t:
        pr = jnp.pad(pr, ((0, 0), (0, t_pad - t), (0, 0)))
        pg = jnp.pad(pg, ((0, 0), (0, t_pad - t), (0, 0)))
    t_valid = chunk if t_pad == t else t
    y_r, s_r_new = _rwkv_mix(pr, shift_prev, s_r, p, chunk, t_valid, rwkv_bb)
    y_g, s_g_new = _gdn_mix(pg, conv_buf, s_g, p, chunk, t_valid, gdn_bb)
    if t_pad != t:
        y_r, y_g = y_r[:, :t], y_g[:, :t]
    x1, q = _out_q(x.reshape(b * t, d), y_r.reshape(b * t, W_RWKV), y_g.reshape(b * t, W_GDN), p, tm)
    x2, h, top_i, gates = _attn_route(q.reshape(b, t, d), x1.reshape(b, t, d), mk, mv, p, bb, tq)
    return (x2.reshape(b * t, d), h, top_i.reshape(b * t, LANES), gates.reshape(b * t, LANES),
            shift_new, s_r_new, conv_new, s_g_new)


def kernel(x_prompt, x_sample, mem_prompt, state_rwkv, state_rwkv_shift, state_gdn, state_gdn_conv, cache_mem_k, cache_mem_v, norm_mix, w_in, mu_shift, w0, w2_decay, a0, a2_iclr, g2_gate, k_k, k_a, r_k, lnx_w, lnx_b, conv_w, a_log, dt_bias, gdn_norm_w, w_out, norm_cross, norm_mem, wq_x, wk_x, wv_x, wo_x, norm_ffn, router_w, router_b, w1_e, b1_e, w2_e, b2_e, final_norm):
    bp, tp, d = x_prompt.shape
    bs, ts, _ = x_sample.shape
    depth = w_in.shape[0]
    np_, ns = bp * tp, bs * ts
    xp, xs = x_prompt, x_sample
    outs = [[] for _ in range(10)]
    for l in range(depth):
        p = _layer_params(l, norm_mix, w_in, mu_shift, w0, w2_decay, a0, a2_iclr, g2_gate, k_k, k_a, r_k, lnx_w,
                          lnx_b, conv_w, a_log, dt_bias, gdn_norm_w, w_out, norm_cross, norm_mem, wq_x, wk_x, wv_x,
                          wo_x, norm_ffn, router_w, router_b, w1_e, b1_e, w2_e, b2_e)
        n_mem = mem_prompt.shape[1]
        mk, mv = _norm_proj(mem_prompt.reshape(bp * n_mem, d), p['norm_mem'], [p['wk'], p['wv']], [F32, F32], 256)
        mk, mv = mk.reshape(bp, n_mem, d), mv.reshape(bp, n_mem, d)
        res_p = _mix_and_attend(
            xp, mk, mv, jnp.zeros((bp, R_PROJ), F32), jnp.zeros((bp, H_R, N_R, N_R), F32),
            jnp.zeros((bp, CONV_W - 1, GDN_CONV_DIM), F32), jnp.zeros((bp, H_G, D_G, D_G), F32), p,
            chunk=MIX_CHUNK, rwkv_bb=4, gdn_bb=8, tm=512, bb=1, tq=1024)
        head_rows = lambda c: c.reshape(bs, n_mem, H_X, D_X // LANES, LANES).transpose(0, 1, 3, 2, 4).reshape(-1, LANES)
        mk_s, mv_s = head_rows(cache_mem_k[l]), head_rows(cache_mem_v[l])
        res_s = _mix_and_attend(
            xs, mk_s, mv_s, state_rwkv_shift[l], state_rwkv[l], state_gdn_conv[l],
            state_gdn[l], p, chunk=SUBLANES, rwkv_bb=8, gdn_bb=8, tm=256, bb=8, tq=ts)
        h = jnp.concatenate([res_p[1], res_s[1]], axis=0)
        top_i = jnp.concatenate([res_p[2], res_s[2]], axis=0)[:, :TOP_K]
        gtok, sdst, block_e, block_live = _route_plan(top_i, MOE_ROWS)
        slots = _moe(h, gtok, sdst, block_e, block_live, p, MOE_ROWS)
        gn = final_norm.reshape(1, d).astype(F32)
        last = l == depth - 1
        xp = _combine(res_p[0], res_p[3], slots, gn, 512, np_ + ns, 0, last).reshape(bp, tp, d)
        xs = _combine(res_s[0], res_s[3], slots, gn, 512, np_ + ns, np_, last).reshape(bs, ts, d)
        new = [res_p[5], res_p[4], res_p[7], res_p[6], mk.reshape(bp, n_mem, H_X, D_X),
               mv.reshape(bp, n_mem, H_X, D_X), res_s[5], res_s[4], res_s[7], res_s[6]]
        for acc, val in zip(outs, new):
            acc.append(val)
    return (xp, xs) + tuple(jnp.stack(o) for o in outs)
```
